```python
import jax, jax.numpy as jnp
from jax import lax
import numpy as np

D_MODEL = 1024
BATCH = 2
SEQ = 8192
DEPTH = 1

D_MIX = D_MODEL
HEAD_DIM = 64
D_RWKV = D_MIX // 2
D_NSA = D_MIX - D_RWKV
RWKV_HEADS = D_RWKV // HEAD_DIM
NSA_Q_HEADS = D_NSA // HEAD_DIM
NSA_KV_HEADS = 2
NSA_GROUP = NSA_Q_HEADS // NSA_KV_HEADS
D_KV = NSA_KV_HEADS * HEAD_DIM
LORA_W = 64
LORA_A = 64
LORA_G = 128
RWKV_SIZES = (D_RWKV, D_RWKV, D_RWKV, LORA_W, LORA_A, LORA_G)
N_RWKV_COLS = sum(RWKV_SIZES)
CMP_BLOCK = 32
CMP_STRIDE = 16
CMP_HIDDEN = 256
SEL_BLOCK = 64
SEL_TOPK = 16
WINDOW = 512
Q_BLOCK = 128
NSA_SIZES = (D_NSA, D_KV, D_KV, D_KV, D_KV, D_KV, D_KV, 3 * NSA_Q_HEADS)
N_NSA_COLS = sum(NSA_SIZES)
N_IN = N_RWKV_COLS + N_NSA_COLS
D_FF = -(-8 * D_MODEL // (3 * 256)) * 256
NORM_EPS = 1e-6
GN_EPS = 64e-5
NEG = -1e30
BIG = 1e30

kernel_name = 'hybrid_rwkv7_nsa_block'


def _split(z, sizes):
    return jnp.split(z, [int(c) for c in np.cumsum(sizes)[:-1]], axis=-1)


def rms_norm(x, g):
    xf = x.astype(jnp.float32)
    y = xf * lax.rsqrt(jnp.mean(xf * xf, axis=-1, keepdims=True) + NORM_EPS)
    return (y * g).astype(x.dtype)


def alibi_slopes(n_heads):
    return 2.0 ** (-8.0 * jnp.arange(1, n_heads + 1, dtype=jnp.float32) / n_heads)


def swiglu(x, w_gate, w_up, w_down):
    return (jax.nn.silu(x @ w_gate) * (x @ w_up)) @ w_down


def rwkv7_time_mix(p, mu, w0, w2, a0, a2, g2, k_k, k_a, r_k, lnx_w, lnx_b):
    B, T, _ = p.shape
    H, N = RWKV_HEADS, HEAD_DIM
    p = p.astype(jnp.float32)
    p_prev = jnp.pad(p, ((0, 0), (1, 0), (0, 0)))[:, :-1]
    p = p + mu * (p_prev - p)
    r, k, v, dw, da, dg = _split(p, RWKV_SIZES)
    w_log = -jax.nn.softplus(-(w0 + jnp.tanh(dw) @ w2)) - 0.5
    decay = jnp.exp(-jnp.exp(w_log))
    a = jax.nn.sigmoid(a0 + da @ a2)
    g = jax.nn.sigmoid(dg) @ g2
    heads = lambda z: z.reshape(B, T, H, N)
    kk = heads(k * k_k)
    kk = kk / jnp.maximum(jnp.sqrt(jnp.sum(kk * kk, axis=-1, keepdims=True)), 1e-12)
    k = k * (1.0 + (a - 1.0) * k_a)
    r, k, v, decay, a = map(heads, (r, k, v, decay, a))
    xs = tuple(jnp.moveaxis(z, 1, 0) for z in (r, decay, k, v, -kk, kk * a))

    def step(S, inp):
        r_t, w_t, k_t, v_t, a_t, b_t = inp
        sa = jnp.einsum('bhvk,bhk->bhv', S, a_t)
        S = S * w_t[:, :, None, :] + sa[..., None] * b_t[:, :, None, :] + v_t[..., None] * k_t[:, :, None, :]
        return S, jnp.einsum('bhvk,bhk->bhv', S, r_t)

    S0 = jnp.zeros((B, H, N, N), jnp.float32)
    _, y = lax.scan(step, S0, xs)
    y = jnp.moveaxis(y, 0, 1)
    mean = jnp.mean(y, axis=-1, keepdims=True)
    var = jnp.mean(jnp.square(y - mean), axis=-1, keepdims=True)
    y = ((y - mean) * lax.rsqrt(var + GN_EPS)).reshape(B, T, D_RWKV) * lnx_w + lnx_b
    bonus = jnp.sum(r * k * r_k, axis=-1, keepdims=True) * v
    return (y + bonus.reshape(B, T, D_RWKV)) * g


def nsa_attention(p, pe_k, pe_v, ck1, ck2, cv1, cv2):
    B, T, _ = p.shape
    Hk, G, dk = NSA_KV_HEADS, NSA_GROUP, HEAD_DIM
    p = p.astype(jnp.float32)
    q, kc, vc, ks, vs, kw, vw, gl = _split(p, NSA_SIZES)
    q = q.reshape(B, T, Hk, G, dk) * (dk ** -0.5)
    gates = jax.nn.sigmoid(gl).reshape(B, T, Hk, G, 3)
    kv_heads = lambda z: z.reshape(B, T, Hk, dk)

    n_cmp = (T - CMP_BLOCK) // CMP_STRIDE + 1
    cmp_idx = np.arange(n_cmp)[:, None] * CMP_STRIDE + np.arange(CMP_BLOCK)[None, :]

    def compress(z, pe, w1, w2):
        blk = kv_heads(z)[:, cmp_idx] + pe[None, None, :, None, :]
        blk = jnp.transpose(blk, (0, 3, 1, 2, 4)).reshape(B, Hk, n_cmp, CMP_BLOCK * dk)
        return jax.nn.gelu(blk @ w1) @ w2

    k_cmp = compress(kc, pe_k, ck1, ck2)
    v_cmp = compress(vc, pe_v, cv1, cv2)
    cmp_start = np.arange(n_cmp) * CMP_STRIDE
    cmp_end = jnp.asarray(cmp_start + CMP_BLOCK - 1)

    n_sel = T // SEL_BLOCK
    top_k = min(SEL_TOPK, n_sel)
    sel_start = np.arange(n_sel) * SEL_BLOCK
    ov = np.clip(np.minimum(cmp_start[:, None] + CMP_BLOCK, sel_start[None, :] + SEL_BLOCK)
                 - np.maximum(cmp_start[:, None], sel_start[None, :]), 0, None) / CMP_STRIDE
    ov = jnp.asarray(ov, jnp.float32)
    k_sel = kv_heads(ks).reshape(B, n_sel, SEL_BLOCK, Hk, dk).transpose(0, 3, 1, 2, 4)
    v_sel = kv_heads(vs).reshape(B, n_sel, SEL_BLOCK, Hk, dk).transpose(0, 3, 1, 2, 4)

    pad = ((0, 0), (WINDOW, 0), (0, 0), (0, 0))
    k_win = jnp.pad(kv_heads(kw), pad)
    v_win = jnp.pad(kv_heads(vw), pad)

    slopes = alibi_slopes(NSA_Q_HEADS).reshape(Hk, G)
    b_ix = jnp.arange(B)[:, None, None, None]
    h_ix = jnp.arange(Hk)[None, :, None, None]
    blk_ids = jnp.arange(n_sel)

    def query_block(qi):
        q0 = qi * Q_BLOCK
        t = q0 + jnp.arange(Q_BLOCK)
        qb = lax.dynamic_slice_in_dim(q, q0, Q_BLOCK, 1)
        gb = lax.dynamic_slice_in_dim(gates, q0, Q_BLOCK, 1)

        s = jnp.einsum('bqhgd,bhcd->bhgqc', qb, k_cmp)
        s = s - slopes[None, :, :, None, None] * jnp.abs(t[:, None] - cmp_end[None, :])
        ok_c = cmp_end[None, :] <= t[:, None]
        p_c = jax.nn.softmax(jnp.where(ok_c, s, NEG), axis=-1) * jnp.any(ok_c, axis=-1)[:, None]
        o_c = jnp.einsum('bhgqc,bhcd->bqhgd', p_c, v_cmp)

        imp = jnp.einsum('bhgqc,cn->bhqn', p_c, ov)
        cur = t // SEL_BLOCK
        valid = blk_ids[None, :] <= cur[:, None]
        forced = (blk_ids[None, :] == 0) | (blk_ids[None, :] == cur[:, None]) | (blk_ids[None, :] == cur[:, None] - 1)
        imp = jnp.where(valid, jnp.where(forced, BIG, imp), NEG)
        top_v, top_i = lax.top_k(imp, top_k)
        kg = k_sel[b_ix, h_ix, top_i]
        vg = v_sel[b_ix, h_ix, top_i]
        pos = top_i[..., None] * SEL_BLOCK + jnp.arange(SEL_BLOCK)
        s = jnp.einsum('bqhgd,bhqnjd->bhgqnj', qb, kg)
        s = s - slopes[None, :, :, None, None, None] * jnp.abs(t[:, None, None] - pos)[:, :, None]
        ok_s = ((top_v > 0.5 * NEG)[..., None] & (pos <= t[:, None, None]))[:, :, None]
        s = jnp.where(ok_s, s, NEG).reshape(B, Hk, G, Q_BLOCK, top_k * SEL_BLOCK)
        p_s = jax.nn.softmax(s, axis=-1)
        o_s = jnp.einsum('bhgqm,bhqmd->bqhgd', p_s, vg.reshape(B, Hk, Q_BLOCK, top_k * SEL_BLOCK, dk))

        kwb = lax.dynamic_slice_in_dim(k_win, q0, Q_BLOCK + WINDOW, 1)
        vwb = lax.dynamic_slice_in_dim(v_win, q0, Q_BLOCK + WINDOW, 1)
        pos_w = q0 - WINDOW + jnp.arange(Q_BLOCK + WINDOW)
        dist = t[:, None] - pos_w[None, :]
        s = jnp.einsum('bqhgd,bkhd->bhgqk', qb, kwb) - slopes[None, :, :, None, None] * jnp.abs(dist)
        ok_w = (pos_w[None, :] >= 0) & (dist >= 0) & (dist < WINDOW)
        p_w = jax.nn.softmax(jnp.where(ok_w, s, NEG), axis=-1)
        o_w = jnp.einsum('bhgqk,bkhd->bqhgd', p_w, vwb)

        return gb[..., 0:1] * o_c + gb[..., 1:2] * o_s + gb[..., 2:3] * o_w

    out = lax.map(query_block, jnp.arange(T // Q_BLOCK))
    return jnp.moveaxis(out, 0, 1).reshape(B, T, D_NSA)


def setup_inputs(seed: int = 0) -> dict:
    key = jax.random.key(seed)
    ks = jax.random.split(key, 26)
    f32 = jnp.float32
    L = DEPTH

    def nrm(k, shape, scale):
        return jax.random.normal(k, shape, f32) * scale

    def gain(k, shape):
        return 1.0 + 0.02 * jax.random.normal(k, shape, f32)

    return {
        'x': nrm(ks[0], (BATCH, SEQ, D_MODEL), 1.0),
        'norm1_g': gain(ks[1], (L, D_MODEL)),
        'w_in': nrm(ks[2], (L, D_MODEL, N_IN), D_MODEL ** -0.5),
        'mu_shift': jax.random.uniform(ks[3], (L, N_RWKV_COLS), f32),
        'rwkv_w0': jax.random.uniform(ks[4], (L, D_RWKV), f32, -4.0, 1.0),
        'rwkv_w2': nrm(ks[5], (L, LORA_W, D_RWKV), 0.1),
        'rwkv_a0': nrm(ks[6], (L, D_RWKV), 0.5),
        'rwkv_a2': nrm(ks[7], (L, LORA_A, D_RWKV), 0.1),
        'rwkv_g2': nrm(ks[8], (L, LORA_G, D_RWKV), LORA_G ** -0.5),
        'rwkv_k_k': 0.85 + 0.05 * jax.random.normal(ks[9], (L, D_RWKV), f32),
        'rwkv_k_a': gain(ks[10], (L, D_RWKV)),
        'rwkv_r_k': nrm(ks[11], (L, RWKV_HEADS, HEAD_DIM), 0.1),
        'rwkv_lnx_w': gain(ks[12], (L, D_RWKV)),
        'rwkv_lnx_b': nrm(ks[13], (L, D_RWKV), 0.01),
        'nsa_pe_k': nrm(ks[14], (L, CMP_BLOCK, HEAD_DIM), 0.1),
        'nsa_pe_v': nrm(ks[15], (L, CMP_BLOCK, HEAD_DIM), 0.1),
        'nsa_cmp_k_w1': nrm(ks[16], (L, CMP_BLOCK * HEAD_DIM, CMP_HIDDEN), (CMP_BLOCK * HEAD_DIM) ** -0.5),
        'nsa_cmp_k_w2': nrm(ks[17], (L, CMP_HIDDEN, HEAD_DIM), CMP_HIDDEN ** -0.5),
        'nsa_cmp_v_w1': nrm(ks[18], (L, CMP_BLOCK * HEAD_DIM, CMP_HIDDEN), (CMP_BLOCK * HEAD_DIM) ** -0.5),
        'nsa_cmp_v_w2': nrm(ks[19], (L, CMP_HIDDEN, HEAD_DIM), CMP_HIDDEN ** -0.5),
        'w_out': nrm(ks[20], (L, D_MIX, D_MODEL), D_MIX ** -0.5),
        'norm2_g': gain(ks[21], (L, D_MODEL)),
        'ffn_w_gate': nrm(ks[22], (L, D_MODEL, D_FF), D_MODEL ** -0.5),
        'ffn_w_up': nrm(ks[23], (L, D_MODEL, D_FF), D_MODEL ** -0.5),
        'ffn_w_down': nrm(ks[24], (L, D_FF, D_MODEL), D_FF ** -0.5),
        'norm_f_g': gain(ks[25], (D_MODEL,)),
    }


def reference(x, norm1_g, w_in, mu_shift, rwkv_w0, rwkv_w2, rwkv_a0, rwkv_a2, rwkv_g2,
              rwkv_k_k, rwkv_k_a, rwkv_r_k, rwkv_lnx_w, rwkv_lnx_b, nsa_pe_k, nsa_pe_v,
              nsa_cmp_k_w1, nsa_cmp_k_w2, nsa_cmp_v_w1, nsa_cmp_v_w2, w_out, norm2_g,
              ffn_w_gate, ffn_w_up, ffn_w_down, norm_f_g):
    h = x
    for i in range(DEPTH):
        u = rms_norm(h, norm1_g[i])
        p_rwkv, p_nsa = jnp.split(u @ w_in[i], [N_RWKV_COLS], axis=-1)
        y_rwkv = rwkv7_time_mix(p_rwkv, mu_shift[i], rwkv_w0[i], rwkv_w2[i], rwkv_a0[i], rwkv_a2[i],
                                rwkv_g2[i], rwkv_k_k[i], rwkv_k_a[i], rwkv_r_k[i],
                                rwkv_lnx_w[i], rwkv_lnx_b[i])
        y_nsa = nsa_attention(p_nsa, nsa_pe_k[i], nsa_pe_v[i], nsa_cmp_k_w1[i], nsa_cmp_k_w2[i],
                              nsa_cmp_v_w1[i], nsa_cmp_v_w2[i])
        y = jnp.concatenate([y_rwkv, y_nsa], axis=-1).astype(h.dtype)
        h = h + y @ w_out[i]
        h = h + swiglu(rms_norm(h, norm2_g[i]), ffn_w_gate[i], ffn_w_up[i], ffn_w_down[i])
    return rms_norm(h, norm_f_g)
```

```python
import functools

import numpy as np
import jax
import jax.numpy as jnp
from jax import lax
from jax.experimental import pallas as pl
from jax.experimental.pallas import tpu as pltpu

F32 = jnp.float32
BF16 = jnp.bfloat16

HEAD_DIM = 64
RWKV_HEADS = 8
D_RWKV = RWKV_HEADS * HEAD_DIM
NSA_Q_HEADS = 8
NSA_KV_HEADS = 2
NSA_GROUP = NSA_Q_HEADS // NSA_KV_HEADS
D_NSA = NSA_Q_HEADS * HEAD_DIM
D_KV = NSA_KV_HEADS * HEAD_DIM
LORA_W, LORA_A, LORA_G = 64, 64, 128
N_RWKV_COLS = 3 * D_RWKV + LORA_W + LORA_A + LORA_G
N_NSA_COLS = D_NSA + 6 * D_KV + 3 * NSA_Q_HEADS
CMP_BLOCK, CMP_STRIDE = 32, 16
SEL_BLOCK, SEL_TOPK = 64, 16
WINDOW = 512
Q_BLOCK = 128
NORM_EPS = 1e-6
GN_EPS = 64e-5
NEG = -1e30
BIG = 1e30

LANES = 128
RWKV_CHUNK = 64
RWKV_PREP_ROWS = 8 * RWKV_CHUNK
SEL_KEY_TILE = 512
VMEM_LIMIT = 56 * 1024 * 1024


def _round_up(n, m):
    return -(-n // m) * m


def _dot(a, b):
    return jnp.dot(a, b, preferred_element_type=F32)


def _dot_nt(a, b):
    return lax.dot_general(a, b, (((1,), (1,)), ((), ())), preferred_element_type=F32)


def _dot_tn(a, b):
    return lax.dot_general(a, b, (((0,), (0,)), ((), ())), preferred_element_type=F32)


def _dot_f32(a, b):
    return jnp.dot(a, b, preferred_element_type=F32, precision=lax.Precision.HIGHEST)


def _rms(x, g):
    return x * lax.rsqrt(jnp.mean(x * x, axis=-1, keepdims=True) + NORM_EPS) * g


def _params(*sem):
    return pltpu.CompilerParams(dimension_semantics=sem, vmem_limit_bytes=VMEM_LIMIT)


def _inproj_kernel(x_ref, g_ref, wr_ref, wn_ref, pr_ref, pn_ref):
    xb = _rms(x_ref[...], g_ref[...]).astype(BF16)
    pr_ref[...] = _dot(xb, wr_ref[...])
    pn_ref[...] = _dot(xb, wn_ref[...])


def _inproj(x2, g, w_r, w_n, tm=256):
    m, d = x2.shape
    nr, nn = w_r.shape[1], w_n.shape[1]
    return pl.pallas_call(
        _inproj_kernel,
        grid=(m // tm,),
        in_specs=[pl.BlockSpec((tm, d), lambda i: (i, 0)),
                  pl.BlockSpec((1, d), lambda i: (0, 0)),
                  pl.BlockSpec((d, nr), lambda i: (0, 0)),
                  pl.BlockSpec((d, nn), lambda i: (0, 0))],
        out_specs=[pl.BlockSpec((tm, nr), lambda i: (i, 0)),
                   pl.BlockSpec((tm, nn), lambda i: (i, 0))],
        out_shape=[jax.ShapeDtypeStruct((m, nr), F32), jax.ShapeDtypeStruct((m, nn), F32)],
        compiler_params=_params("arbitrary"),
        name="inproj",
    )(x2, g, w_r, w_n)


def _rwkv_prep_kernel(p_ref, mu_ref, w0_ref, w2_ref, a0_ref, a2_ref, g2_ref, kk_ref, ka_ref, rk_ref,
                      bd_ref, tri_ref,
                      rt_ref, kt_ref, bt_ref, at_ref, v_ref, wc_ref, bonus_ref, g_ref, carry_ref):
    rows = p_ref.shape[0]
    c = RWKV_CHUNK

    @pl.when(pl.program_id(1) == 0)
    def _():
        carry_ref[...] = jnp.zeros_like(carry_ref)

    p = p_ref[...]
    row = lax.broadcasted_iota(jnp.int32, p.shape, 0)
    p_prev = jnp.where(row == 0, carry_ref[7:8, :], pltpu.roll(p, 1, axis=0))
    carry_ref[...] = p[rows - 8:, :]
    ps = p + mu_ref[...] * (p_prev - p)

    d = D_RWKV
    r, k, v = ps[:, 0:d], ps[:, d:2 * d], ps[:, 2 * d:3 * d]
    dw = ps[:, 3 * d:3 * d + LORA_W]
    da = ps[:, 3 * d + LORA_W:3 * d + LORA_W + LORA_A]
    dg = ps[:, 3 * d + LORA_W + LORA_A:]

    z = -(w0_ref[...] + _dot(jnp.tanh(dw).astype(BF16), w2_ref[...]))
    softplus = jnp.maximum(z, 0.0) + jnp.log1p(jnp.exp(-jnp.abs(z)))
    logw = -jnp.exp(-softplus - 0.5)
    a = jax.nn.sigmoid(a0_ref[...] + _dot(da.astype(BF16), a2_ref[...]))
    g_ref[...] = _dot(jax.nn.sigmoid(dg).astype(BF16), g2_ref[...])

    bd = bd_ref[...]
    kk = k * kk_ref[...]
    kk = kk / jnp.maximum(jnp.sqrt(_dot_f32(kk * kk, bd)), 1e-12)
    kp = k * (1.0 + (a - 1.0) * ka_ref[...])
    bonus_ref[...] = _dot_f32(r * kp * rk_ref[...], bd) * v
    v_ref[...] = v

    tri = tri_ref[...]
    wc_rows = []
    for ci in range(rows // c):
        sl = slice(ci * c, (ci + 1) * c)
        cum = _dot_f32(tri, logw[sl])
        e_pos, e_neg = jnp.exp(cum), jnp.exp(-cum)
        rt_ref[sl, :] = r[sl] * e_pos
        kt_ref[sl, :] = kp[sl] * e_neg
        bt_ref[sl, :] = kk[sl] * a[sl] * e_neg
        at_ref[sl, :] = -kk[sl] * jnp.exp(cum - logw[sl])
        wc_rows.append(e_pos[c - 1:c, :])
    wc_ref[...] = jnp.concatenate(wc_rows, axis=0)


def _rwkv_prep(p_r, batch, mu, w0, w2, a0, a2, g2, k_k, k_a, r_k):
    m = p_r.shape[0]
    t = m // batch
    rows = RWKV_PREP_ROWS
    nt = t // rows
    d = D_RWKV
    head = np.arange(d) // HEAD_DIM
    bd = jnp.asarray(head[:, None] == head[None, :], F32)
    tri = jnp.asarray(np.tril(np.ones((RWKV_CHUNK, RWKV_CHUNK))), F32)
    vec = lambda n: pl.BlockSpec((1, n), lambda b, i: (0, 0))
    full = lambda a: pl.BlockSpec(a.shape, lambda b, i: (0, 0))
    tile = pl.BlockSpec((rows, d), lambda b, i: (b * nt + i, 0))
    big = jax.ShapeDtypeStruct((m, d), F32)
    outs = pl.pallas_call(
        _rwkv_prep_kernel,
        grid=(batch, nt),
        in_specs=[pl.BlockSpec((rows, N_RWKV_COLS), lambda b, i: (b * nt + i, 0)),
                  vec(N_RWKV_COLS), vec(d), full(w2), vec(d), full(a2), full(g2), vec(d), vec(d), vec(d),
                  full(bd), full(tri)],
        out_specs=[tile, tile, tile, tile, tile,
                   pl.BlockSpec((rows // RWKV_CHUNK, d), lambda b, i: (b * nt + i, 0)),
                   tile, tile],
        out_shape=[big, big, big, big, big,
                   jax.ShapeDtypeStruct((m // RWKV_CHUNK, d), F32), big, big],
        scratch_shapes=[pltpu.VMEM((8, N_RWKV_COLS), F32)],
        compiler_params=_params("arbitrary", "arbitrary"),
        name="rwkv_prep",
    )(p_r, mu, w0, w2, a0, a2, g2, k_k, k_a, r_k, bd, tri)
    return outs


def _rwkv_scan_kernel(rt_ref, kt_ref, bt_ref, at_ref, v_ref, wc_ref, bonus_ref, g_ref, lnw_ref, lnb_ref,
                      y_ref, s_ref):
    c = rt_ref.shape[0]

    @pl.when(pl.program_id(1) == 0)
    def _():
        s_ref[...] = jnp.zeros_like(s_ref)

    row = lax.broadcasted_iota(jnp.int32, (c, c), 0)
    col = lax.broadcasted_iota(jnp.int32, (c, c), 1)
    strict = col < row
    incl = col <= row
    eye = (row == col).astype(F32)
    n_double = int(np.log2(c)) - 1

    for h in range(RWKV_HEADS):
        sl = slice(h * HEAD_DIM, (h + 1) * HEAD_DIM)
        vv = v_ref[:, sl]
        vb = vv.astype(BF16)
        ar = jnp.concatenate([at_ref[:, sl], rt_ref[:, sl]], axis=0).astype(BF16)
        bk = jnp.concatenate([bt_ref[:, sl], kt_ref[:, sl]], axis=0).astype(BF16)
        amat = _dot_nt(ar, bk)
        n_ab = jnp.where(strict, amat[:c, :c], 0.0)
        a_ak = jnp.where(strict, amat[:c, c:], 0.0)
        a_rb = jnp.where(incl, amat[c:, :c], 0.0)
        a_rk = jnp.where(incl, amat[c:, c:], 0.0)
        s0 = s_ref[h]
        ar_s = _dot_nt(ar, s0.astype(BF16))
        rhs = ar_s[:c] + _dot(a_ak.astype(BF16), vb)
        inv = eye + n_ab
        pw = n_ab
        for _ in range(n_double):
            pwb = pw.astype(BF16)
            pw = _dot(pwb, pwb)
            inv = inv + _dot(pw.astype(BF16), inv.astype(BF16))
        u = _dot(inv.astype(BF16), rhs.astype(BF16))
        ub = u.astype(BF16)
        y = ar_s[c:] + _dot(a_rb.astype(BF16), ub) + _dot(a_rk.astype(BF16), vb)
        uv = jnp.concatenate([ub, vb], axis=0)
        s_ref[h] = (s0 + _dot_tn(uv, bk)) * wc_ref[0, :, sl]

        mean = jnp.mean(y, axis=-1, keepdims=True)
        var = jnp.mean(jnp.square(y - mean), axis=-1, keepdims=True)
        yn = (y - mean) * lax.rsqrt(var + GN_EPS)
        y_ref[:, sl] = (yn * lnw_ref[:, sl] + lnb_ref[:, sl] + bonus_ref[:, sl]) * g_ref[:, sl]


def _rwkv_scan(rt, kt, bt, at, v, wc, bonus, g, lnw, lnb, batch):
    m, d = rt.shape
    c = RWKV_CHUNK
    nc = m // batch // c
    tile = pl.BlockSpec((c, d), lambda b, i: (b * nc + i, 0))
    vec = pl.BlockSpec((1, d), lambda b, i: (0, 0))
    return pl.pallas_call(
        _rwkv_scan_kernel,
        grid=(batch, nc),
        in_specs=[tile, tile, tile, tile, tile,
                  pl.BlockSpec((1, 1, d), lambda b, i: (b * nc + i, 0, 0)),
                  tile, tile, vec, vec],
        out_specs=tile,
        out_shape=jax.ShapeDtypeStruct((m, d), F32),
        scratch_shapes=[pltpu.VMEM((RWKV_HEADS, HEAD_DIM, HEAD_DIM), F32)],
        compiler_params=_params("arbitrary", "arbitrary"),
        name="rwkv_scan",
    )(rt, kt, bt, at, v, wc.reshape(m // c, 1, d), bonus, g, lnw, lnb)


def _compress_kernel(zk_ref, zv_ref, pek_ref, pev_ref, k1_ref, k2_ref, v1_ref, v2_ref, ko_ref, vo_ref):
    def one(z_ref, pe_ref, w1_ref, w2_ref):
        z = z_ref[0].astype(BF16)
        n16, half = z.shape
        first = _dot(z, w1_ref[:half, :])
        second = _dot(z, w1_ref[half:, :])
        bias = _dot(pe_ref[...].astype(BF16), w1_ref[...])[0:1, :]
        hidden = first + pltpu.roll(second, n16 - 1, axis=0) + bias
        return _dot(jax.nn.gelu(hidden).astype(BF16), w2_ref[...])

    ko_ref[0] = one(zk_ref, pek_ref, k1_ref, k2_ref)
    vo_ref[0] = one(zv_ref, pev_ref, v1_ref, v2_ref)


def _compress(zk, zv, pek, pev, k1, k2, v1, v2):
    bh, n16, wide = zk.shape
    zspec = pl.BlockSpec((1, n16, wide), lambda i: (i, 0, 0))
    full = lambda a: pl.BlockSpec(a.shape, lambda i: (0,) * a.ndim)
    ospec = pl.BlockSpec((1, n16, HEAD_DIM), lambda i: (i, 0, 0))
    oshape = jax.ShapeDtypeStruct((bh, n16, HEAD_DIM), F32)
    return pl.pallas_call(
        _compress_kernel,
        grid=(bh,),
        in_specs=[zspec, zspec, full(pek), full(pev), full(k1), full(k2), full(v1), full(v2)],
        out_specs=[ospec, ospec],
        out_shape=[oshape, oshape],
        compiler_params=_params("arbitrary"),
        name="nsa_compress",
    )(zk, zv, pek, pev, k1, k2, v1, v2)


def _cmp_select_kernel(slopes_ref, q_ref, kc_ref, vc_ref, ov_ref, oc_ref, sel_ref):
    hk = pl.program_id(1)
    q0 = pl.program_id(2) * Q_BLOCK
    ncp = kc_ref.shape[1]
    kc = kc_ref[0].astype(BF16)
    vc = vc_ref[0].astype(BF16)

    t = q0 + lax.broadcasted_iota(jnp.int32, (Q_BLOCK, ncp), 0)
    cmp_end = lax.broadcasted_iota(jnp.int32, (Q_BLOCK, ncp), 1) * CMP_STRIDE + (CMP_BLOCK - 1)
    ok = cmp_end <= t
    dist = jnp.abs(t - cmp_end).astype(F32)
    any_ok = (q0 + lax.broadcasted_iota(jnp.int32, (Q_BLOCK, 1), 0) >= CMP_BLOCK - 1).astype(F32)

    p_sum = jnp.zeros((Q_BLOCK, ncp), F32)
    for g in range(NSA_GROUP):
        slope = slopes_ref[hk * NSA_GROUP + g]
        qg = (q_ref[0, 0, g] * (HEAD_DIM ** -0.5)).astype(BF16)
        s = _dot_nt(qg, kc) - slope * dist
        s = jnp.where(ok, s, NEG)
        e = jnp.exp(s - jnp.max(s, axis=-1, keepdims=True))
        p = e / jnp.sum(e, axis=-1, keepdims=True) * any_ok
        oc_ref[0, 0, g] = _dot(p.astype(BF16), vc)
        p_sum = p_sum + p

    ov = ov_ref[...]
    hi = p_sum.astype(BF16)
    r1 = p_sum - hi.astype(F32)
    mid = r1.astype(BF16)
    lo = (r1 - mid.astype(F32)).astype(BF16)
    imp = _dot(hi, ov) + _dot(mid, ov) + _dot(lo, ov)

    blk = lax.broadcasted_iota(jnp.int32, (Q_BLOCK, LANES), 1)
    cur = (q0 + lax.broadcasted_iota(jnp.int32, (Q_BLOCK, LANES), 0)) // SEL_BLOCK
    valid = blk <= cur
    forced = (blk == 0) | (blk == cur) | (blk == cur - 1)
    x = jnp.where(valid, jnp.where(forced, BIG, imp), NEG)
    lane = blk.astype(F32)
    chosen = jnp.zeros((Q_BLOCK, LANES), jnp.bool_)
    for _ in range(SEL_TOPK):
        m = jnp.max(x, axis=-1, keepdims=True)
        first = jnp.min(jnp.where(x == m, lane, float(LANES)), axis=-1, keepdims=True)
        hit = lane == first
        chosen = chosen | hit
        x = jnp.where(hit, -jnp.inf, x)
    sel_ref[0, 0] = (chosen & valid).astype(BF16)


def _cmp_select(slopes, q_t, k_cmp, v_cmp, ov, batch):
    _, hk, grp, t, dk = q_t.shape
    ncp = k_cmp.shape[1]
    nq = t // Q_BLOCK
    qspec = pl.BlockSpec((1, 1, grp, Q_BLOCK, dk), lambda b, h, i: (b, h, 0, i, 0))
    cspec = pl.BlockSpec((1, ncp, dk), lambda b, h, i: (b * hk + h, 0, 0))
    return pl.pallas_call(
        _cmp_select_kernel,
        grid=(batch, hk, nq),
        in_specs=[pl.BlockSpec(memory_space=pltpu.SMEM), qspec, cspec, cspec,
                  pl.BlockSpec(ov.shape, lambda b, h, i: (0, 0))],
        out_specs=[qspec, pl.BlockSpec((1, 1, Q_BLOCK, LANES), lambda b, h, i: (b, h, i, 0))],
        out_shape=[jax.ShapeDtypeStruct(q_t.shape, F32),
                   jax.ShapeDtypeStruct((batch, hk, t, LANES), BF16)],
        compiler_params=_params("arbitrary", "arbitrary", "arbitrary"),
        name="nsa_cmp_select",
    )(slopes, q_t, k_cmp, v_cmp, ov)


def _sel_win_kernel(slopes_ref, q_ref, sel_ref, ex_ref, ks_ref, vs_ref, kw_ref, vw_ref, oc_ref, gl_ref,
                    y_ref, m_ref, l_ref, acc_ref):
    hk = pl.program_id(1)
    q0 = pl.program_id(2) * Q_BLOCK
    kt = SEL_KEY_TILE
    scale = HEAD_DIM ** -0.5
    sel = sel_ref[0, 0]
    gates = jax.nn.sigmoid(gl_ref[0, 0])

    m_ref[...] = jnp.full_like(m_ref, NEG)
    l_ref[...] = jnp.zeros_like(l_ref)
    acc_ref[...] = jnp.zeros_like(acc_ref)
    t_s = q0 + lax.broadcasted_iota(jnp.int32, (Q_BLOCK, kt), 0)
    j_s = lax.broadcasted_iota(jnp.int32, (Q_BLOCK, kt), 1)

    def tile_step(j, carry):
        k0 = pl.multiple_of(j * kt, kt)
        kb = ks_ref[0, 0, pl.ds(k0, kt), :]
        vb = vs_ref[0, 0, pl.ds(k0, kt), :]
        picked = _dot(sel, ex_ref[:, pl.ds(k0, kt)])
        pos = k0 + j_s
        mask = (picked > 0.5) & (pos <= t_s)
        dist = (t_s - pos).astype(F32)
        for g in range(NSA_GROUP):
            slope = slopes_ref[hk * NSA_GROUP + g]
            qg = (q_ref[0, 0, g] * scale).astype(BF16)
            s = jnp.where(mask, _dot_nt(qg, kb) - slope * dist, NEG)
            m_old = m_ref[g]
            m_new = jnp.maximum(m_old, jnp.max(s, axis=-1, keepdims=True))
            p = jnp.where(mask, jnp.exp(s - m_new), 0.0)
            alpha = jnp.exp(m_old - m_new)
            l_ref[g] = alpha * l_ref[g] + jnp.sum(p, axis=-1, keepdims=True)
            acc_ref[g] = alpha * acc_ref[g] + _dot(p.astype(BF16), vb)
            m_ref[g] = m_new
        return carry

    lax.fori_loop(0, (q0 + Q_BLOCK + kt - 1) // kt, tile_step, 0)

    span = WINDOW + Q_BLOCK
    w0 = pl.multiple_of(jnp.maximum(q0 - WINDOW, 0), Q_BLOCK)
    kwb = kw_ref[0, 0, pl.ds(w0, span), :]
    vwb = vw_ref[0, 0, pl.ds(w0, span), :]
    t_w = q0 + lax.broadcasted_iota(jnp.int32, (Q_BLOCK, span), 0)
    dist_i = t_w - (w0 + lax.broadcasted_iota(jnp.int32, (Q_BLOCK, span), 1))
    ok_w = (dist_i >= 0) & (dist_i < WINDOW)
    dist_w = jnp.abs(dist_i).astype(F32)

    outs = []
    for g in range(NSA_GROUP):
        slope = slopes_ref[hk * NSA_GROUP + g]
        qg = (q_ref[0, 0, g] * scale).astype(BF16)
        s = jnp.where(ok_w, _dot_nt(qg, kwb) - slope * dist_w, NEG)
        e = jnp.exp(s - jnp.max(s, axis=-1, keepdims=True))
        p = e / jnp.sum(e, axis=-1, keepdims=True)
        o_w = _dot(p.astype(BF16), vwb)
        o_s = acc_ref[g] / l_ref[g]
        o_c = oc_ref[0, 0, g]
        outs.append(gates[:, 3 * g:3 * g + 1] * o_c + gates[:, 3 * g + 1:3 * g + 2] * o_s
                    + gates[:, 3 * g + 2:3 * g + 3] * o_w)
    y_ref[...] = jnp.concatenate(outs, axis=-1)


def _sel_win(slopes, q_t, sel, expand, ks, vs, kw, vw, o_c, gl, batch):
    _, hk, grp, t, dk = q_t.shape
    nq = t // Q_BLOCK
    qspec = pl.BlockSpec((1, 1, grp, Q_BLOCK, dk), lambda b, h, i: (b, h, 0, i, 0))
    kvspec = pl.BlockSpec((1, 1, t, dk), lambda b, h, i: (b, h, 0, 0))
    return pl.pallas_call(
        _sel_win_kernel,
        grid=(batch, hk, nq),
        in_specs=[pl.BlockSpec(memory_space=pltpu.SMEM), qspec,
                  pl.BlockSpec((1, 1, Q_BLOCK, LANES), lambda b, h, i: (b, h, i, 0)),
                  pl.BlockSpec(expand.shape, lambda b, h, i: (0, 0)),
                  kvspec, kvspec, kvspec, kvspec, qspec,
                  pl.BlockSpec((1, 1, Q_BLOCK, 3 * grp), lambda b, h, i: (b, h, i, 0))],
        out_specs=pl.BlockSpec((Q_BLOCK, grp * dk), lambda b, h, i: (b * nq + i, h)),
        out_shape=jax.ShapeDtypeStruct((batch * t, hk * grp * dk), F32),
        scratch_shapes=[pltpu.VMEM((grp, Q_BLOCK, 1), F32), pltpu.VMEM((grp, Q_BLOCK, 1), F32),
                        pltpu.VMEM((grp, Q_BLOCK, dk), F32)],
        compiler_params=_params("arbitrary", "arbitrary", "arbitrary"),
        name="nsa_sel_win",
    )(slopes, q_t, sel, expand, ks, vs, kw, vw, o_c, gl)


def _ffn_kernel(x_ref, yr_ref, yn_ref, wo1_ref, wo2_ref, g2_ref, wg_ref, wu_ref, wd_ref, gf_ref,
                o_ref, h1_ref, hn_ref, acc_ref):
    j = pl.program_id(1)

    @pl.when(j == 0)
    def _():
        h1 = (x_ref[...] + _dot(yr_ref[...].astype(BF16), wo1_ref[...])
              + _dot(yn_ref[...].astype(BF16), wo2_ref[...]))
        h1_ref[...] = h1
        hn_ref[...] = _rms(h1, g2_ref[...]).astype(BF16)
        acc_ref[...] = jnp.zeros_like(acc_ref)

    hn = hn_ref[...]
    gate = _dot(hn, wg_ref[...])
    up = _dot(hn, wu_ref[...])
    act = gate * jax.nn.sigmoid(gate) * up
    acc_ref[...] += _dot(act.astype(BF16), wd_ref[...])

    @pl.when(j == pl.num_programs(1) - 1)
    def _():
        o_ref[...] = _rms(h1_ref[...] + acc_ref[...], gf_ref[...])


def _ffn(x2, yr, yn, wo1, wo2, g2, wg, wu, wd, gf, tm=512, nf=2):
    m, d = x2.shape
    dff = wg.shape[1]
    tf = dff // nf
    row = lambda n: pl.BlockSpec((tm, n), lambda i, j: (i, 0))
    const = lambda a: pl.BlockSpec(a.shape, lambda i, j: (0, 0))
    return pl.pallas_call(
        _ffn_kernel,
        grid=(m // tm, nf),
        in_specs=[row(d), row(yr.shape[1]), row(yn.shape[1]), const(wo1), const(wo2), const(g2),
                  pl.BlockSpec((d, tf), lambda i, j: (0, j)), pl.BlockSpec((d, tf), lambda i, j: (0, j)),
                  pl.BlockSpec((tf, d), lambda i, j: (j, 0)), const(gf)],
        out_specs=row(d),
        out_shape=jax.ShapeDtypeStruct((m, d), F32),
        scratch_shapes=[pltpu.VMEM((tm, d), F32), pltpu.VMEM((tm, d), BF16), pltpu.VMEM((tm, d), F32)],
        compiler_params=_params("arbitrary", "arbitrary"),
        name="outproj_ffn",
    )(x2, yr, yn, wo1, wo2, g2, wg, wu, wd, gf)


def _overlap_matrix(t):
    n16 = t // CMP_STRIDE
    n_cmp = (t - CMP_BLOCK) // CMP_STRIDE + 1
    n_sel = t // SEL_BLOCK
    cmp_start = np.arange(n_cmp) * CMP_STRIDE
    sel_start = np.arange(n_sel) * SEL_BLOCK
    ov = np.clip(np.minimum(cmp_start[:, None] + CMP_BLOCK, sel_start[None, :] + SEL_BLOCK)
                 - np.maximum(cmp_start[:, None], sel_start[None, :]), 0, None) / CMP_STRIDE
    full = np.zeros((n16, LANES), np.float32)
    full[:n_cmp, :n_sel] = ov
    return jnp.asarray(full, BF16)


def _expand_matrix(t):
    blk = np.arange(t) // SEL_BLOCK
    return jnp.asarray(np.arange(LANES)[:, None] == blk[None, :], BF16)


def kernel(x, norm1_g, w_in, mu_shift, rwkv_w0, rwkv_w2, rwkv_a0, rwkv_a2, rwkv_g2, rwkv_k_k, rwkv_k_a,
           rwkv_r_k, rwkv_lnx_w, rwkv_lnx_b, nsa_pe_k, nsa_pe_v, nsa_cmp_k_w1, nsa_cmp_k_w2, nsa_cmp_v_w1,
           nsa_cmp_v_w2, w_out, norm2_g, ffn_w_gate, ffn_w_up, ffn_w_down, norm_f_g):
    batch, t, d_model = x.shape
    depth = w_in.shape[0]
    assert t % RWKV_PREP_ROWS == 0 and t >= WINDOW + Q_BLOCK and t // SEL_BLOCK <= LANES
    hk, grp, dk = NSA_KV_HEADS, NSA_GROUP, HEAD_DIM
    slopes = 2.0 ** (-8.0 * jnp.arange(1, NSA_Q_HEADS + 1, dtype=F32) / NSA_Q_HEADS)
    ov = _overlap_matrix(t)
    expand = _expand_matrix(t)
    nsa_pad = _round_up(N_NSA_COLS, LANES)
    row = lambda a: a.reshape(1, -1)

    h = x.reshape(batch * t, d_model)
    for i in range(depth):
        w_r = w_in[i][:, :N_RWKV_COLS].astype(BF16)
        w_n = jnp.pad(w_in[i][:, N_RWKV_COLS:], ((0, 0), (0, nsa_pad - N_NSA_COLS))).astype(BF16)
        p_r, p_n = _inproj(h, row(norm1_g[i]), w_r, w_n)

        rt, kt, bt, at, v, wc, bonus, g = _rwkv_prep(
            p_r, batch, row(mu_shift[i]), row(rwkv_w0[i]), rwkv_w2[i].astype(BF16), row(rwkv_a0[i]),
            rwkv_a2[i].astype(BF16), rwkv_g2[i].astype(BF16), row(rwkv_k_k[i]), row(rwkv_k_a[i]),
            row(rwkv_r_k[i]))
        y_rwkv = _rwkv_scan(rt, kt, bt, at, v, wc, bonus, g, row(rwkv_lnx_w[i]), row(rwkv_lnx_b[i]), batch)

        pn = p_n.reshape(batch, t, nsa_pad)
        q_t = pn[..., :D_NSA].reshape(batch, t, hk, grp, dk).transpose(0, 2, 3, 1, 4)
        kv = pn[..., D_NSA:D_NSA + 6 * D_KV].reshape(batch, t, 6, hk, dk).transpose(2, 0, 3, 1, 4)
        n16 = t // CMP_STRIDE
        zk = kv[0].reshape(batch * hk, n16, CMP_STRIDE * dk)
        zv = kv[1].reshape(batch * hk, n16, CMP_STRIDE * dk)
        ks, vs, kw, vw = (kv[j].astype(BF16) for j in range(2, 6))
        gl = pn[..., D_NSA + 6 * D_KV:N_NSA_COLS].reshape(batch, t, hk, 3 * grp).transpose(0, 2, 1, 3)

        pe_rows = lambda pe: jnp.broadcast_to(pe.reshape(1, -1), (8, CMP_BLOCK * dk))
        k_cmp, v_cmp = _compress(zk, zv, pe_rows(nsa_pe_k[i]), pe_rows(nsa_pe_v[i]),
                                 nsa_cmp_k_w1[i].astype(BF16), nsa_cmp_k_w2[i].astype(BF16),
                                 nsa_cmp_v_w1[i].astype(BF16), nsa_cmp_v_w2[i].astype(BF16))
        o_c, sel = _cmp_select(slopes, q_t, k_cmp, v_cmp, ov, batch)
        y_nsa = _sel_win(slopes, q_t, sel, expand, ks, vs, kw, vw, o_c, gl, batch)

        last = i == depth - 1
        gf = norm_f_g if last else jnp.ones_like(norm_f_g)
        assert last, "the final RMSNorm is fused into the last layer's FFN kernel"
        h = _ffn(h, y_rwkv, y_nsa, w_out[i][:D_RWKV].astype(BF16), w_out[i][D_RWKV:].astype(BF16),
                 row(norm2_g[i]), ffn_w_gate[i].astype(BF16), ffn_w_up[i].astype(BF16),
                 ffn_w_down[i].astype(BF16), row(gf))
    return h.reshape(batch, t, d_model)
```

```python
import functools

import numpy as np
import jax
import jax.numpy as jnp
from jax import lax
from jax.experimental import pallas as pl
from jax.experimental.pallas import tpu as pltpu

F32 = jnp.float32
BF16 = jnp.bfloat16

HEAD_DIM = 64
RWKV_HEADS = 8
D_RWKV = RWKV_HEADS * HEAD_DIM
NSA_Q_HEADS = 8
NSA_KV_HEADS = 2
NSA_GROUP = NSA_Q_HEADS // NSA_KV_HEADS
D_NSA = NSA_Q_HEADS * HEAD_DIM
D_KV = NSA_KV_HEADS * HEAD_DIM
LORA_W, LORA_A, LORA_G = 64, 64, 128
N_RWKV_COLS = 3 * D_RWKV + LORA_W + LORA_A + LORA_G
N_NSA_COLS = D_NSA + 6 * D_KV + 3 * NSA_Q_HEADS
CMP_BLOCK, CMP_STRIDE = 32, 16
SEL_BLOCK, SEL_TOPK = 64, 16
WINDOW = 512
Q_BLOCK = 128
NORM_EPS = 1e-6
GN_EPS = 64e-5
NEG = -1e30
BIG = 1e30

LANES = 128
RWKV_CHUNK = 64
RWKV_PREP_ROWS = 8 * RWKV_CHUNK
SEL_KEY_TILE = 512
VMEM_LIMIT = 56 * 1024 * 1024


def _round_up(n, m):
    return -(-n // m) * m


def _dot(a, b):
    return jnp.dot(a, b, preferred_element_type=F32)


def _dot_nt(a, b):
    return lax.dot_general(a, b, (((1,), (1,)), ((), ())), preferred_element_type=F32)


def _dot_tn(a, b):
    return lax.dot_general(a, b, (((0,), (0,)), ((), ())), preferred_element_type=F32)


def _dot_f32(a, b):
    return jnp.dot(a, b, preferred_element_type=F32, precision=lax.Precision.HIGHEST)


def _rms(x, g):
    return x * lax.rsqrt(jnp.mean(x * x, axis=-1, keepdims=True) + NORM_EPS) * g


def _params(*sem):
    return pltpu.CompilerParams(dimension_semantics=sem, vmem_limit_bytes=VMEM_LIMIT)


def _inproj_kernel(x_ref, g_ref, wr_ref, wn_ref, pr_ref, pn_ref):
    xb = _rms(x_ref[...], g_ref[...]).astype(BF16)
    pr_ref[...] = _dot(xb, wr_ref[...])
    pn_ref[...] = _dot(xb, wn_ref[...])


def _inproj(x2, g, w_r, w_n, tm=256):
    m, d = x2.shape
    nr, nn = w_r.shape[1], w_n.shape[1]
    return pl.pallas_call(
        _inproj_kernel,
        grid=(m // tm,),
        in_specs=[pl.BlockSpec((tm, d), lambda i: (i, 0)),
                  pl.BlockSpec((1, d), lambda i: (0, 0)),
                  pl.BlockSpec((d, nr), lambda i: (0, 0)),
                  pl.BlockSpec((d, nn), lambda i: (0, 0))],
        out_specs=[pl.BlockSpec((tm, nr), lambda i: (i, 0)),
                   pl.BlockSpec((tm, nn), lambda i: (i, 0))],
        out_shape=[jax.ShapeDtypeStruct((m, nr), F32), jax.ShapeDtypeStruct((m, nn), F32)],
        compiler_params=_params("arbitrary"),
        name="inproj",
    )(x2, g, w_r, w_n)


def _rwkv_prep_kernel(p_ref, mu_ref, w0_ref, w2_ref, a0_ref, a2_ref, g2_ref, kk_ref, ka_ref, rk_ref,
                      bd_ref, tri_ref,
                      rt_ref, kt_ref, bt_ref, at_ref, v_ref, wc_ref, bonus_ref, g_ref, carry_ref):
    rows = p_ref.shape[0]
    c = RWKV_CHUNK

    @pl.when(pl.program_id(1) == 0)
    def _():
        carry_ref[...] = jnp.zeros_like(carry_ref)

    p = p_ref[...]
    row = lax.broadcasted_iota(jnp.int32, p.shape, 0)
    p_prev = jnp.where(row == 0, carry_ref[7:8, :], pltpu.roll(p, 1, axis=0))
    carry_ref[...] = p[rows - 8:, :]
    ps = p + mu_ref[...] * (p_prev - p)

    d = D_RWKV
    r, k, v = ps[:, 0:d], ps[:, d:2 * d], ps[:, 2 * d:3 * d]
    dw = ps[:, 3 * d:3 * d + LORA_W]
    da = ps[:, 3 * d + LORA_W:3 * d + LORA_W + LORA_A]
    dg = ps[:, 3 * d + LORA_W + LORA_A:]

    z = -(w0_ref[...] + _dot(jnp.tanh(dw).astype(BF16), w2_ref[...]))
    softplus = jnp.maximum(z, 0.0) + jnp.log1p(jnp.exp(-jnp.abs(z)))
    logw = -jnp.exp(-softplus - 0.5)
    a = jax.nn.sigmoid(a0_ref[...] + _dot(da.astype(BF16), a2_ref[...]))
    g_ref[0] = _dot(jax.nn.sigmoid(dg).astype(BF16), g2_ref[...])

    bd = bd_ref[...]
    kk = k * kk_ref[...]
    kk = kk / jnp.maximum(jnp.sqrt(_dot_f32(kk * kk, bd)), 1e-12)
    kp = k * (1.0 + (a - 1.0) * ka_ref[...])
    bonus_ref[0] = _dot_f32(r * kp * rk_ref[...], bd) * v

    tri = tri_ref[...]
    cums = [_dot_f32(tri, logw[ci * c:(ci + 1) * c]) for ci in range(rows // c)]
    cum = jnp.concatenate(cums, axis=0)
    e_pos, e_neg = jnp.exp(cum), jnp.exp(-cum)
    outs = ((rt_ref, r * e_pos), (kt_ref, kp * e_neg), (bt_ref, kk * a * e_neg),
            (at_ref, -kk * jnp.exp(cum - logw)), (v_ref, v))
    for h in range(RWKV_HEADS):
        sl = slice(h * HEAD_DIM, (h + 1) * HEAD_DIM)
        for ref, val in outs:
            ref[0, h] = val[:, sl]
        for ci in range(rows // c):
            wc_ref[0, ci, h:h + 1, :] = e_pos[(ci + 1) * c - 1:(ci + 1) * c, sl]


def _rwkv_prep(p_r, batch, mu, w0, w2, a0, a2, g2, k_k, k_a, r_k):
    m = p_r.shape[0]
    t = m // batch
    rows = RWKV_PREP_ROWS
    nt = t // rows
    d = D_RWKV
    head = np.arange(d) // HEAD_DIM
    bd = jnp.asarray(head[:, None] == head[None, :], F32)
    tri = jnp.asarray(np.tril(np.ones((RWKV_CHUNK, RWKV_CHUNK))), F32)
    vec = lambda n: pl.BlockSpec((1, n), lambda b, i: (0, 0))
    full = lambda a: pl.BlockSpec(a.shape, lambda b, i: (0, 0))
    wide = pl.BlockSpec((1, rows, d), lambda b, i: (b, i, 0))
    heads = pl.BlockSpec((1, RWKV_HEADS, rows, HEAD_DIM), lambda b, i: (b, 0, i, 0))
    wide_shape = jax.ShapeDtypeStruct((batch, t, d), F32)
    heads_shape = jax.ShapeDtypeStruct((batch, RWKV_HEADS, t, HEAD_DIM), F32)
    outs = pl.pallas_call(
        _rwkv_prep_kernel,
        grid=(batch, nt),
        in_specs=[pl.BlockSpec((rows, N_RWKV_COLS), lambda b, i: (b * nt + i, 0)),
                  vec(N_RWKV_COLS), vec(d), full(w2), vec(d), full(a2), full(g2), vec(d), vec(d), vec(d),
                  full(bd), full(tri)],
        out_specs=[heads, heads, heads, heads, heads,
                   pl.BlockSpec((1, rows // RWKV_CHUNK, RWKV_HEADS, HEAD_DIM), lambda b, i: (b, i, 0, 0)),
                   wide, wide],
        out_shape=[heads_shape, heads_shape, heads_shape, heads_shape, heads_shape,
                   jax.ShapeDtypeStruct((batch, t // RWKV_CHUNK, RWKV_HEADS, HEAD_DIM), F32),
                   wide_shape, wide_shape],
        scratch_shapes=[pltpu.VMEM((8, N_RWKV_COLS), F32)],
        compiler_params=_params("arbitrary", "arbitrary"),
        name="rwkv_prep",
    )(p_r, mu, w0, w2, a0, a2, g2, k_k, k_a, r_k, bd, tri)
    return outs


def _bdot(spec, a, b):
    return jnp.einsum(spec, a, b, preferred_element_type=F32)


def _rwkv_scan_kernel(rt_ref, kt_ref, bt_ref, at_ref, v_ref, wc_ref, bonus_ref, g_ref, lnw_ref, lnb_ref,
                      y_ref, s_ref):
    nb, nh, c, dk = rt_ref.shape
    n = nb * nh

    @pl.when(pl.program_id(0) == 0)
    def _():
        s_ref[...] = jnp.zeros_like(s_ref)

    row = lax.broadcasted_iota(jnp.int32, (1, c, c), 1)
    col = lax.broadcasted_iota(jnp.int32, (1, c, c), 2)
    strict = col < row
    incl = col <= row
    eye = (row == col).astype(F32)

    load = lambda ref: ref[...].reshape(n, c, dk)
    vb = load(v_ref).astype(BF16)
    ar = jnp.concatenate([load(at_ref), load(rt_ref)], axis=1).astype(BF16)
    bk = jnp.concatenate([load(bt_ref), load(kt_ref)], axis=1).astype(BF16)
    amat = _bdot("nik,njk->nij", ar, bk)
    n_ab = jnp.where(strict, amat[:, :c, :c], 0.0)
    a_ak = jnp.where(strict, amat[:, :c, c:], 0.0)
    a_rb = jnp.where(incl, amat[:, c:, :c], 0.0)
    a_rk = jnp.where(incl, amat[:, c:, c:], 0.0)
    s0 = s_ref[...]
    ar_s = _bdot("nik,nvk->niv", ar, s0.astype(BF16))
    rhs = ar_s[:, :c] + _bdot("nij,njv->niv", a_ak.astype(BF16), vb)
    inv = eye + n_ab
    pw = n_ab
    for _ in range(int(np.log2(c)) - 1):
        pwb = pw.astype(BF16)
        pw = _bdot("nij,njk->nik", pwb, pwb)
        inv = inv + _bdot("nij,njk->nik", pw.astype(BF16), inv.astype(BF16))
    ub = _bdot("nij,njv->niv", inv.astype(BF16), rhs.astype(BF16)).astype(BF16)
    y = ar_s[:, c:] + _bdot("nij,njv->niv", a_rb.astype(BF16), ub) + _bdot("nij,njv->niv", a_rk.astype(BF16), vb)
    uv_t = jnp.swapaxes(jnp.concatenate([ub, vb], axis=1), 1, 2)
    wc = wc_ref[...].reshape(n, 1, dk)
    s_ref[...] = (s0 + _bdot("nvi,nik->nvk", uv_t, bk)) * wc

    mean = jnp.mean(y, axis=-1, keepdims=True)
    var = jnp.mean(jnp.square(y - mean), axis=-1, keepdims=True)
    yn = (y - mean) * lax.rsqrt(var + GN_EPS)
    for b in range(nb):
        wide = jnp.concatenate([yn[b * nh + h] for h in range(nh)], axis=-1)
        y_ref[b] = (wide * lnw_ref[...] + lnb_ref[...] + bonus_ref[b]) * g_ref[b]


def _rwkv_scan(rt, kt, bt, at, v, wc, bonus, g, lnw, lnb):
    batch, nh, t, dk = rt.shape
    c = RWKV_CHUNK
    d = nh * dk
    heads = pl.BlockSpec((batch, nh, c, dk), lambda i: (0, 0, i, 0))
    wide = pl.BlockSpec((batch, c, d), lambda i: (0, i, 0))
    vec = pl.BlockSpec((1, d), lambda i: (0, 0))
    return pl.pallas_call(
        _rwkv_scan_kernel,
        grid=(t // c,),
        in_specs=[heads, heads, heads, heads, heads,
                  pl.BlockSpec((batch, 1, nh, dk), lambda i: (0, i, 0, 0)),
                  wide, wide, vec, vec],
        out_specs=wide,
        out_shape=jax.ShapeDtypeStruct((batch, t, d), F32),
        scratch_shapes=[pltpu.VMEM((batch * nh, dk, dk), F32)],
        compiler_params=_params("arbitrary"),
        name="rwkv_scan",
    )(rt, kt, bt, at, v, wc, bonus, g, lnw, lnb)


def _compress_kernel(zk_ref, zv_ref, pek_ref, pev_ref, k1_ref, k2_ref, v1_ref, v2_ref, ko_ref, vo_ref):
    def one(z_ref, pe_ref, w1_ref, w2_ref):
        z = z_ref[0].astype(BF16)
        n16, half = z.shape
        first = _dot(z, w1_ref[:half, :])
        second = _dot(z, w1_ref[half:, :])
        bias = _dot(pe_ref[...].astype(BF16), w1_ref[...])[0:1, :]
        hidden = first + pltpu.roll(second, n16 - 1, axis=0) + bias
        return _dot(jax.nn.gelu(hidden).astype(BF16), w2_ref[...])

    ko_ref[0] = one(zk_ref, pek_ref, k1_ref, k2_ref)
    vo_ref[0] = one(zv_ref, pev_ref, v1_ref, v2_ref)


def _compress(zk, zv, pek, pev, k1, k2, v1, v2):
    bh, n16, wide = zk.shape
    zspec = pl.BlockSpec((1, n16, wide), lambda i: (i, 0, 0))
    full = lambda a: pl.BlockSpec(a.shape, lambda i: (0,) * a.ndim)
    ospec = pl.BlockSpec((1, n16, HEAD_DIM), lambda i: (i, 0, 0))
    oshape = jax.ShapeDtypeStruct((bh, n16, HEAD_DIM), F32)
    return pl.pallas_call(
        _compress_kernel,
        grid=(bh,),
        in_specs=[zspec, zspec, full(pek), full(pev), full(k1), full(k2), full(v1), full(v2)],
        out_specs=[ospec, ospec],
        out_shape=[oshape, oshape],
        compiler_params=_params("arbitrary"),
        name="nsa_compress",
    )(zk, zv, pek, pev, k1, k2, v1, v2)


def _cmp_select_kernel(slopes_ref, q_ref, kc_ref, vc_ref, ov_ref, oc_ref, sel_ref):
    hk = pl.program_id(1)
    q0 = pl.program_id(2) * Q_BLOCK
    ncp = kc_ref.shape[1]
    kc = kc_ref[0].astype(BF16)
    vc = vc_ref[0].astype(BF16)

    t = q0 + lax.broadcasted_iota(jnp.int32, (Q_BLOCK, ncp), 0)
    cmp_end = lax.broadcasted_iota(jnp.int32, (Q_BLOCK, ncp), 1) * CMP_STRIDE + (CMP_BLOCK - 1)
    ok = cmp_end <= t
    dist = jnp.abs(t - cmp_end).astype(F32)
    any_ok = (q0 + lax.broadcasted_iota(jnp.int32, (Q_BLOCK, 1), 0) >= CMP_BLOCK - 1).astype(F32)

    p_sum = jnp.zeros((Q_BLOCK, ncp), F32)
    for g in range(NSA_GROUP):
        slope = slopes_ref[hk * NSA_GROUP + g]
        qg = (q_ref[0, 0, g] * (HEAD_DIM ** -0.5)).astype(BF16)
        s = _dot_nt(qg, kc) - slope * dist
        s = jnp.where(ok, s, NEG)
        e = jnp.exp(s - jnp.max(s, axis=-1, keepdims=True))
        p = e / jnp.sum(e, axis=-1, keepdims=True) * any_ok
        oc_ref[0, 0, g] = _dot(p.astype(BF16), vc)
        p_sum = p_sum + p

    ov = ov_ref[...]
    hi = p_sum.astype(BF16)
    r1 = p_sum - hi.astype(F32)
    mid = r1.astype(BF16)
    lo = (r1 - mid.astype(F32)).astype(BF16)
    imp = _dot(hi, ov) + _dot(mid, ov) + _dot(lo, ov)

    blk = lax.broadcasted_iota(jnp.int32, (Q_BLOCK, LANES), 1)
    cur = (q0 + lax.broadcasted_iota(jnp.int32, (Q_BLOCK, LANES), 0)) // SEL_BLOCK
    valid = blk <= cur
    forced = (blk == 0) | (blk == cur) | (blk == cur - 1)
    x = jnp.where(valid, jnp.where(forced, BIG, imp), NEG)
    lane = blk.astype(F32)
    chosen = jnp.zeros((Q_BLOCK, LANES), jnp.bool_)
    for _ in range(SEL_TOPK):
        m = jnp.max(x, axis=-1, keepdims=True)
        first = jnp.min(jnp.where(x == m, lane, float(LANES)), axis=-1, keepdims=True)
        hit = lane == first
        chosen = chosen | hit
        x = jnp.where(hit, -jnp.inf, x)
    sel_ref[0, 0] = (chosen & valid).astype(BF16)


def _cmp_select(slopes, q_t, k_cmp, v_cmp, ov, batch):
    _, hk, grp, t, dk = q_t.shape
    ncp = k_cmp.shape[1]
    nq = t // Q_BLOCK
    qspec = pl.BlockSpec((1, 1, grp, Q_BLOCK, dk), lambda b, h, i: (b, h, 0, i, 0))
    cspec = pl.BlockSpec((1, ncp, dk), lambda b, h, i: (b * hk + h, 0, 0))
    return pl.pallas_call(
        _cmp_select_kernel,
        grid=(batch, hk, nq),
        in_specs=[pl.BlockSpec(memory_space=pltpu.SMEM), qspec, cspec, cspec,
                  pl.BlockSpec(ov.shape, lambda b, h, i: (0, 0))],
        out_specs=[qspec, pl.BlockSpec((1, 1, Q_BLOCK, LANES), lambda b, h, i: (b, h, i, 0))],
        out_shape=[jax.ShapeDtypeStruct(q_t.shape, F32),
                   jax.ShapeDtypeStruct((batch, hk, t, LANES), BF16)],
        compiler_params=_params("arbitrary", "arbitrary", "arbitrary"),
        name="nsa_cmp_select",
    )(slopes, q_t, k_cmp, v_cmp, ov)


def _sel_win_kernel(slopes_ref, q_ref, sel_ref, ex_ref, ks_ref, vs_ref, kw_ref, vw_ref, oc_ref, gl_ref,
                    y_ref, m_ref, l_ref, acc_ref):
    hk = pl.program_id(1)
    q0 = pl.program_id(2) * Q_BLOCK
    kt = SEL_KEY_TILE
    scale = HEAD_DIM ** -0.5
    sel = sel_ref[0, 0]
    gates = jax.nn.sigmoid(gl_ref[0, 0])

    m_ref[...] = jnp.full_like(m_ref, NEG)
    l_ref[...] = jnp.zeros_like(l_ref)
    acc_ref[...] = jnp.zeros_like(acc_ref)
    t_s = q0 + lax.broadcasted_iota(jnp.int32, (Q_BLOCK, kt), 0)
    j_s = lax.broadcasted_iota(jnp.int32, (Q_BLOCK, kt), 1)

    def tile_step(j, carry):
        k0 = pl.multiple_of(j * kt, kt)
        kb = ks_ref[0, 0, pl.ds(k0, kt), :]
        vb = vs_ref[0, 0, pl.ds(k0, kt), :]
        picked = _dot(sel, ex_ref[:, pl.ds(k0, kt)])
        pos = k0 + j_s
        mask = (picked > 0.5) & (pos <= t_s)
        dist = (t_s - pos).astype(F32)
        for g in range(NSA_GROUP):
            slope = slopes_ref[hk * NSA_GROUP + g]
            qg = (q_ref[0, 0, g] * scale).astype(BF16)
            s = jnp.where(mask, _dot_nt(qg, kb) - slope * dist, NEG)
            m_old = m_ref[g]
            m_new = jnp.maximum(m_old, jnp.max(s, axis=-1, keepdims=True))
            p = jnp.where(mask, jnp.exp(s - m_new), 0.0)
            alpha = jnp.exp(m_old - m_new)
            l_ref[g] = alpha * l_ref[g] + jnp.sum(p, axis=-1, keepdims=True)
            acc_ref[g] = alpha * acc_ref[g] + _dot(p.astype(BF16), vb)
            m_ref[g] = m_new
        return carry

    lax.fori_loop(0, (q0 + Q_BLOCK + kt - 1) // kt, tile_step, 0)

    span = WINDOW + Q_BLOCK
    w0 = pl.multiple_of(jnp.maximum(q0 - WINDOW, 0), Q_BLOCK)
    kwb = kw_ref[0, 0, pl.ds(w0, span), :]
    vwb = vw_ref[0, 0, pl.ds(w0, span), :]
    t_w = q0 + lax.broadcasted_iota(jnp.int32, (Q_BLOCK, span), 0)
    dist_i = t_w - (w0 + lax.broadcasted_iota(jnp.int32, (Q_BLOCK, span), 1))
    ok_w = (dist_i >= 0) & (dist_i < WINDOW)
    dist_w = jnp.abs(dist_i).astype(F32)

    outs = []
    for g in range(NSA_GROUP):
        slope = slopes_ref[hk * NSA_GROUP + g]
        qg = (q_ref[0, 0, g] * scale).astype(BF16)
        s = jnp.where(ok_w, _dot_nt(qg, kwb) - slope * dist_w, NEG)
        e = jnp.exp(s - jnp.max(s, axis=-1, keepdims=True))
        p = e / jnp.sum(e, axis=-1, keepdims=True)
        o_w = _dot(p.astype(BF16), vwb)
        o_s = acc_ref[g] / l_ref[g]
        o_c = oc_ref[0, 0, g]
        outs.append(gates[:, 3 * g:3 * g + 1] * o_c + gates[:, 3 * g + 1:3 * g + 2] * o_s
                    + gates[:, 3 * g + 2:3 * g + 3] * o_w)
    y_ref[...] = jnp.concatenate(outs, axis=-1)


def _sel_win(slopes, q_t, sel, expand, ks, vs, kw, vw, o_c, gl, batch):
    _, hk, grp, t, dk = q_t.shape
    nq = t // Q_BLOCK
    qspec = pl.BlockSpec((1, 1, grp, Q_BLOCK, dk), lambda b, h, i: (b, h, 0, i, 0))
    kvspec = pl.BlockSpec((1, 1, t, dk), lambda b, h, i: (b, h, 0, 0))
    return pl.pallas_call(
        _sel_win_kernel,
        grid=(batch, hk, nq),
        in_specs=[pl.BlockSpec(memory_space=pltpu.SMEM), qspec,
                  pl.BlockSpec((1, 1, Q_BLOCK, LANES), lambda b, h, i: (b, h, i, 0)),
                  pl.BlockSpec(expand.shape, lambda b, h, i: (0, 0)),
                  kvspec, kvspec, kvspec, kvspec, qspec,
                  pl.BlockSpec((1, 1, Q_BLOCK, 3 * grp), lambda b, h, i: (b, h, i, 0))],
        out_specs=pl.BlockSpec((Q_BLOCK, grp * dk), lambda b, h, i: (b * nq + i, h)),
        out_shape=jax.ShapeDtypeStruct((batch * t, hk * grp * dk), F32),
        scratch_shapes=[pltpu.VMEM((grp, Q_BLOCK, 1), F32), pltpu.VMEM((grp, Q_BLOCK, 1), F32),
                        pltpu.VMEM((grp, Q_BLOCK, dk), F32)],
        compiler_params=_params("arbitrary", "arbitrary", "arbitrary"),
        name="nsa_sel_win",
    )(slopes, q_t, sel, expand, ks, vs, kw, vw, o_c, gl)


def _ffn_kernel(x_ref, yr_ref, yn_ref, wo1_ref, wo2_ref, g2_ref, wg_ref, wu_ref, wd_ref, gf_ref,
                o_ref, h1_ref, hn_ref, acc_ref):
    j = pl.program_id(1)

    @pl.when(j == 0)
    def _():
        h1 = (x_ref[...] + _dot(yr_ref[...].astype(BF16), wo1_ref[...])
              + _dot(yn_ref[...].astype(BF16), wo2_ref[...]))
        h1_ref[...] = h1
        hn_ref[...] = _rms(h1, g2_ref[...]).astype(BF16)
        acc_ref[...] = jnp.zeros_like(acc_ref)

    hn = hn_ref[...]
    gate = _dot(hn, wg_ref[...])
    up = _dot(hn, wu_ref[...])
    act = gate * jax.nn.sigmoid(gate) * up
    acc_ref[...] += _dot(act.astype(BF16), wd_ref[...])

    @pl.when(j == pl.num_programs(1) - 1)
    def _():
        o_ref[...] = _rms(h1_ref[...] + acc_ref[...], gf_ref[...])


def _ffn(x2, yr, yn, wo1, wo2, g2, wg, wu, wd, gf, tm=512, nf=2):
    m, d = x2.shape
    dff = wg.shape[1]
    tf = dff // nf
    row = lambda n: pl.BlockSpec((tm, n), lambda i, j: (i, 0))
    const = lambda a: pl.BlockSpec(a.shape, lambda i, j: (0, 0))
    return pl.pallas_call(
        _ffn_kernel,
        grid=(m // tm, nf),
        in_specs=[row(d), row(yr.shape[1]), row(yn.shape[1]), const(wo1), const(wo2), const(g2),
                  pl.BlockSpec((d, tf), lambda i, j: (0, j)), pl.BlockSpec((d, tf), lambda i, j: (0, j)),
                  pl.BlockSpec((tf, d), lambda i, j: (j, 0)), const(gf)],
        out_specs=row(d),
        out_shape=jax.ShapeDtypeStruct((m, d), F32),
        scratch_shapes=[pltpu.VMEM((tm, d), F32), pltpu.VMEM((tm, d), BF16), pltpu.VMEM((tm, d), F32)],
        compiler_params=_params("arbitrary", "arbitrary"),
        name="outproj_ffn",
    )(x2, yr, yn, wo1, wo2, g2, wg, wu, wd, gf)


def _overlap_matrix(t):
    n16 = t // CMP_STRIDE
    n_cmp = (t - CMP_BLOCK) // CMP_STRIDE + 1
    n_sel = t // SEL_BLOCK
    cmp_start = np.arange(n_cmp) * CMP_STRIDE
    sel_start = np.arange(n_sel) * SEL_BLOCK
    ov = np.clip(np.minimum(cmp_start[:, None] + CMP_BLOCK, sel_start[None, :] + SEL_BLOCK)
                 - np.maximum(cmp_start[:, None], sel_start[None, :]), 0, None) / CMP_STRIDE
    full = np.zeros((n16, LANES), np.float32)
    full[:n_cmp, :n_sel] = ov
    return jnp.asarray(full, BF16)


def _expand_matrix(t):
    blk = np.arange(t) // SEL_BLOCK
    return jnp.asarray(np.arange(LANES)[:, None] == blk[None, :], BF16)


def kernel(x, norm1_g, w_in, mu_shift, rwkv_w0, rwkv_w2, rwkv_a0, rwkv_a2, rwkv_g2, rwkv_k_k, rwkv_k_a,
           rwkv_r_k, rwkv_lnx_w, rwkv_lnx_b, nsa_pe_k, nsa_pe_v, nsa_cmp_k_w1, nsa_cmp_k_w2, nsa_cmp_v_w1,
           nsa_cmp_v_w2, w_out, norm2_g, ffn_w_gate, ffn_w_up, ffn_w_down, norm_f_g):
    batch, t, d_model = x.shape
    depth = w_in.shape[0]
    assert t % RWKV_PREP_ROWS == 0 and t >= WINDOW + Q_BLOCK and t // SEL_BLOCK <= LANES
    hk, grp, dk = NSA_KV_HEADS, NSA_GROUP, HEAD_DIM
    slopes = 2.0 ** (-8.0 * jnp.arange(1, NSA_Q_HEADS + 1, dtype=F32) / NSA_Q_HEADS)
    ov = _overlap_matrix(t)
    expand = _expand_matrix(t)
    nsa_pad = _round_up(N_NSA_COLS, LANES)
    row = lambda a: a.reshape(1, -1)

    h = x.reshape(batch * t, d_model)
    for i in range(depth):
        w_r = w_in[i][:, :N_RWKV_COLS].astype(BF16)
        w_n = jnp.pad(w_in[i][:, N_RWKV_COLS:], ((0, 0), (0, nsa_pad - N_NSA_COLS))).astype(BF16)
        p_r, p_n = _inproj(h, row(norm1_g[i]), w_r, w_n)

        rt, kt, bt, at, v, wc, bonus, g = _rwkv_prep(
            p_r, batch, row(mu_shift[i]), row(rwkv_w0[i]), rwkv_w2[i].astype(BF16), row(rwkv_a0[i]),
            rwkv_a2[i].astype(BF16), rwkv_g2[i].astype(BF16), row(rwkv_k_k[i]), row(rwkv_k_a[i]),
            row(rwkv_r_k[i]))
        y_rwkv = _rwkv_scan(rt, kt, bt, at, v, wc, bonus, g, row(rwkv_lnx_w[i]), row(rwkv_lnx_b[i]))
        y_rwkv = y_rwkv.reshape(batch * t, D_RWKV)

        pn = p_n.reshape(batch, t, nsa_pad)
        q_t = pn[..., :D_NSA].reshape(batch, t, hk, grp, dk).transpose(0, 2, 3, 1, 4)
        kv = pn[..., D_NSA:D_NSA + 6 * D_KV].reshape(batch, t, 6, hk, dk).transpose(2, 0, 3, 1, 4)
        n16 = t // CMP_STRIDE
        zk = kv[0].reshape(batch * hk, n16, CMP_STRIDE * dk)
        zv = kv[1].reshape(batch * hk, n16, CMP_STRIDE * dk)
        ks, vs, kw, vw = (kv[j].astype(BF16) for j in range(2, 6))
        gl = pn[..., D_NSA + 6 * D_KV:N_NSA_COLS].reshape(batch, t, hk, 3 * grp).transpose(0, 2, 1, 3)

        pe_rows = lambda pe: jnp.broadcast_to(pe.reshape(1, -1), (8, CMP_BLOCK * dk))
        k_cmp, v_cmp = _compress(zk, zv, pe_rows(nsa_pe_k[i]), pe_rows(nsa_pe_v[i]),
                                 nsa_cmp_k_w1[i].astype(BF16), nsa_cmp_k_w2[i].astype(BF16),
                                 nsa_cmp_v_w1[i].astype(BF16), nsa_cmp_v_w2[i].astype(BF16))
        o_c, sel = _cmp_select(slopes, q_t, k_cmp, v_cmp, ov, batch)
        y_nsa = _sel_win(slopes, q_t, sel, expand, ks, vs, kw, vw, o_c, gl, batch)

        last = i == depth - 1
        gf = norm_f_g if last else jnp.ones_like(norm_f_g)
        assert last, "the final RMSNorm is fused into the last layer's FFN kernel"
        h = _ffn(h, y_rwkv, y_nsa, w_out[i][:D_RWKV].astype(BF16), w_out[i][D_RWKV:].astype(BF16),
                 row(norm2_g[i]), ffn_w_gate[i].astype(BF16), ffn_w_up[i].astype(BF16),
                 ffn_w_down[i].astype(BF16), row(gf))
    return h.reshape(batch, t, d_model)
```

```python
import numpy as np
import jax
import jax.numpy as jnp
from jax import lax
from jax.experimental import pallas as pl
from jax.experimental.pallas import tpu as pltpu

F32 = jnp.float32
BF16 = jnp.bfloat16

HEAD_DIM = 64
RWKV_HEADS = 8
D_RWKV = RWKV_HEADS * HEAD_DIM
NSA_Q_HEADS = 8
NSA_KV_HEADS = 2
NSA_GROUP = NSA_Q_HEADS // NSA_KV_HEADS
D_NSA = NSA_Q_HEADS * HEAD_DIM
D_KV = NSA_KV_HEADS * HEAD_DIM
LORA_W, LORA_A, LORA_G = 64, 64, 128
N_RWKV_COLS = 3 * D_RWKV + LORA_W + LORA_A + LORA_G
N_NSA_COLS = D_NSA + 6 * D_KV + 3 * NSA_Q_HEADS
CMP_BLOCK, CMP_STRIDE = 32, 16
SEL_BLOCK, SEL_TOPK = 64, 16
WINDOW = 512
Q_BLOCK = 128
NORM_EPS = 1e-6
GN_EPS = 64e-5
NEG = -1e30
BIG = 1e30

LANES = 128
RWKV_CHUNK = 64
RWKV_PREP_ROWS = 8 * RWKV_CHUNK
KEY_TILE = 256
MAX_KEY_TILES = LANES * SEL_BLOCK // KEY_TILE
WIN_TILES = (WINDOW + Q_BLOCK) // KEY_TILE + 1
V_ROWS = HEAD_DIM + 16
SOFTMAX_FLOOR = -1e20
QG = NSA_GROUP * Q_BLOCK
VMEM_LIMIT = 56 * 1024 * 1024


def _round_up(n, m):
    return -(-n // m) * m


def _dot(a, b):
    return jnp.dot(a, b, preferred_element_type=F32)


def _dot_tn(a, b):
    return lax.dot_general(a, b, (((0,), (0,)), ((), ())), preferred_element_type=F32)


def _dot_f32(a, b):
    return jnp.dot(a, b, preferred_element_type=F32, precision=lax.Precision.HIGHEST)


def _bdot(spec, a, b):
    return jnp.einsum(spec, a, b, preferred_element_type=F32)


def _rms(x, g):
    return x * lax.rsqrt(jnp.mean(x * x, axis=-1, keepdims=True) + NORM_EPS) * g


def _params(*sem):
    return pltpu.CompilerParams(dimension_semantics=sem, vmem_limit_bytes=VMEM_LIMIT)


def _inproj_kernel(x_ref, g_ref, wr_ref, wn_ref, pr_ref, pn_ref):
    xb = _rms(x_ref[...], g_ref[...]).astype(BF16)
    pr_ref[...] = _dot(xb, wr_ref[...])
    pn_ref[...] = _dot(xb, wn_ref[...])


def _inproj(x2, g, w_r, w_n, tm=256):
    m, d = x2.shape
    nr, nn = w_r.shape[1], w_n.shape[1]
    return pl.pallas_call(
        _inproj_kernel,
        grid=(m // tm,),
        in_specs=[pl.BlockSpec((tm, d), lambda i: (i, 0)),
                  pl.BlockSpec((1, d), lambda i: (0, 0)),
                  pl.BlockSpec((d, nr), lambda i: (0, 0)),
                  pl.BlockSpec((d, nn), lambda i: (0, 0))],
        out_specs=[pl.BlockSpec((tm, nr), lambda i: (i, 0)),
                   pl.BlockSpec((tm, nn), lambda i: (i, 0))],
        out_shape=[jax.ShapeDtypeStruct((m, nr), F32), jax.ShapeDtypeStruct((m, nn), F32)],
        compiler_params=_params("arbitrary"),
        name="inproj",
    )(x2, g, w_r, w_n)


def _rwkv_prep_kernel(p_ref, mu_ref, w0_ref, w2_ref, a0_ref, a2_ref, g2_ref, kk_ref, ka_ref, rk_ref,
                      bd_ref, tri_ref,
                      rt_ref, kt_ref, bt_ref, at_ref, v_ref, wc_ref, bonus_ref, g_ref, carry_ref):
    rows = p_ref.shape[0]
    c = RWKV_CHUNK

    @pl.when(pl.program_id(1) == 0)
    def _():
        carry_ref[...] = jnp.zeros_like(carry_ref)

    p = p_ref[...]
    row = lax.broadcasted_iota(jnp.int32, p.shape, 0)
    p_prev = jnp.where(row == 0, carry_ref[7:8, :], pltpu.roll(p, 1, axis=0))
    carry_ref[...] = p[rows - 8:, :]
    ps = p + mu_ref[...] * (p_prev - p)

    d = D_RWKV
    r, k, v = ps[:, 0:d], ps[:, d:2 * d], ps[:, 2 * d:3 * d]
    dw = ps[:, 3 * d:3 * d + LORA_W]
    da = ps[:, 3 * d + LORA_W:3 * d + LORA_W + LORA_A]
    dg = ps[:, 3 * d + LORA_W + LORA_A:]

    z = -(w0_ref[...] + _dot(jnp.tanh(dw).astype(BF16), w2_ref[...]))
    softplus = jnp.maximum(z, 0.0) + jnp.log1p(jnp.exp(-jnp.abs(z)))
    logw = -jnp.exp(-softplus - 0.5)
    a = jax.nn.sigmoid(a0_ref[...] + _dot(da.astype(BF16), a2_ref[...]))
    g_ref[0] = _dot(jax.nn.sigmoid(dg).astype(BF16), g2_ref[...])

    bd = bd_ref[...]
    kk = k * kk_ref[...]
    kk = kk / jnp.maximum(jnp.sqrt(_dot_f32(kk * kk, bd)), 1e-12)
    kp = k * (1.0 + (a - 1.0) * ka_ref[...])
    bonus_ref[0] = _dot_f32(r * kp * rk_ref[...], bd) * v

    tri = tri_ref[...]
    cums = [_dot_f32(tri, logw[ci * c:(ci + 1) * c]) for ci in range(rows // c)]
    cum = jnp.concatenate(cums, axis=0)
    e_pos, e_neg = jnp.exp(cum), jnp.exp(-cum)
    outs = ((rt_ref, r * e_pos), (kt_ref, kp * e_neg), (bt_ref, kk * a * e_neg),
            (at_ref, -kk * jnp.exp(cum - logw)), (v_ref, v))
    for h in range(RWKV_HEADS):
        sl = slice(h * HEAD_DIM, (h + 1) * HEAD_DIM)
        for ref, val in outs:
            ref[0, h] = val[:, sl]
        for ci in range(rows // c):
            wc_ref[0, ci, h:h + 1, :] = e_pos[(ci + 1) * c - 1:(ci + 1) * c, sl]


def _rwkv_prep(p_r, batch, mu, w0, w2, a0, a2, g2, k_k, k_a, r_k):
    m = p_r.shape[0]
    t = m // batch
    rows = RWKV_PREP_ROWS
    nt = t // rows
    d = D_RWKV
    head = np.arange(d) // HEAD_DIM
    bd = jnp.asarray(head[:, None] == head[None, :], F32)
    tri = jnp.asarray(np.tril(np.ones((RWKV_CHUNK, RWKV_CHUNK))), F32)
    vec = lambda n: pl.BlockSpec((1, n), lambda b, i: (0, 0))
    full = lambda a: pl.BlockSpec(a.shape, lambda b, i: (0, 0))
    wide = pl.BlockSpec((1, rows, d), lambda b, i: (b, i, 0))
    heads = pl.BlockSpec((1, RWKV_HEADS, rows, HEAD_DIM), lambda b, i: (b, 0, i, 0))
    wide_shape = jax.ShapeDtypeStruct((batch, t, d), F32)
    heads_shape = jax.ShapeDtypeStruct((batch, RWKV_HEADS, t, HEAD_DIM), F32)
    outs = pl.pallas_call(
        _rwkv_prep_kernel,
        grid=(batch, nt),
        in_specs=[pl.BlockSpec((rows, N_RWKV_COLS), lambda b, i: (b * nt + i, 0)),
                  vec(N_RWKV_COLS), vec(d), full(w2), vec(d), full(a2), full(g2), vec(d), vec(d), vec(d),
                  full(bd), full(tri)],
        out_specs=[heads, heads, heads, heads, heads,
                   pl.BlockSpec((1, rows // RWKV_CHUNK, RWKV_HEADS, HEAD_DIM), lambda b, i: (b, i, 0, 0)),
                   wide, wide],
        out_shape=[heads_shape, heads_shape, heads_shape, heads_shape, heads_shape,
                   jax.ShapeDtypeStruct((batch, t // RWKV_CHUNK, RWKV_HEADS, HEAD_DIM), F32),
                   wide_shape, wide_shape],
        scratch_shapes=[pltpu.VMEM((8, N_RWKV_COLS), F32)],
        compiler_params=_params("arbitrary", "arbitrary"),
        name="rwkv_prep",
    )(p_r, mu, w0, w2, a0, a2, g2, k_k, k_a, r_k, bd, tri)
    return outs


def _rwkv_scan_kernel(rt_ref, kt_ref, bt_ref, at_ref, v_ref, wc_ref, bonus_ref, g_ref, lnw_ref, lnb_ref,
                      y_ref, s_ref):
    nb, nh, c, dk = rt_ref.shape
    n = nb * nh

    @pl.when(pl.program_id(0) == 0)
    def _():
        s_ref[...] = jnp.zeros_like(s_ref)

    row = lax.broadcasted_iota(jnp.int32, (1, c, c), 1)
    col = lax.broadcasted_iota(jnp.int32, (1, c, c), 2)
    strict = col < row
    incl = col <= row
    eye = (row == col).astype(F32)

    load = lambda ref: ref[...].reshape(n, c, dk)
    vb = load(v_ref).astype(BF16)
    ar = jnp.concatenate([load(at_ref), load(rt_ref)], axis=1).astype(BF16)
    bk = jnp.concatenate([load(bt_ref), load(kt_ref)], axis=1).astype(BF16)
    amat = _bdot("nik,njk->nij", ar, bk)
    n_ab = jnp.where(strict, amat[:, :c, :c], 0.0)
    a_ak = jnp.where(strict, amat[:, :c, c:], 0.0)
    a_rb = jnp.where(incl, amat[:, c:, :c], 0.0)
    a_rk = jnp.where(incl, amat[:, c:, c:], 0.0)
    s0 = s_ref[...]
    ar_s = _bdot("nik,nvk->niv", ar, s0.astype(BF16))
    rhs = ar_s[:, :c] + _bdot("nij,njv->niv", a_ak.astype(BF16), vb)
    inv = eye + n_ab
    pw = n_ab
    for _ in range(int(np.log2(c)) - 1):
        pwb = pw.astype(BF16)
        pw = _bdot("nij,njk->nik", pwb, pwb)
        inv = inv + _bdot("nij,njk->nik", pw.astype(BF16), inv.astype(BF16))
    ub = _bdot("nij,njv->niv", inv.astype(BF16), rhs.astype(BF16)).astype(BF16)
    y = ar_s[:, c:] + _bdot("nij,njv->niv", a_rb.astype(BF16), ub) + _bdot("nij,njv->niv", a_rk.astype(BF16), vb)
    uv_t = jnp.swapaxes(jnp.concatenate([ub, vb], axis=1), 1, 2)
    wc = wc_ref[...].reshape(n, 1, dk)
    s_ref[...] = (s0 + _bdot("nvi,nik->nvk", uv_t, bk)) * wc

    mean = jnp.mean(y, axis=-1, keepdims=True)
    var = jnp.mean(jnp.square(y - mean), axis=-1, keepdims=True)
    yn = (y - mean) * lax.rsqrt(var + GN_EPS)
    for b in range(nb):
        wide = jnp.concatenate([yn[b * nh + h] for h in range(nh)], axis=-1)
        y_ref[b] = (wide * lnw_ref[...] + lnb_ref[...] + bonus_ref[b]) * g_ref[b]


def _rwkv_scan(rt, kt, bt, at, v, wc, bonus, g, lnw, lnb):
    batch, nh, t, dk = rt.shape
    c = RWKV_CHUNK
    d = nh * dk
    heads = pl.BlockSpec((batch, nh, c, dk), lambda i: (0, 0, i, 0))
    wide = pl.BlockSpec((batch, c, d), lambda i: (0, i, 0))
    vec = pl.BlockSpec((1, d), lambda i: (0, 0))
    return pl.pallas_call(
        _rwkv_scan_kernel,
        grid=(t // c,),
        in_specs=[heads, heads, heads, heads, heads,
                  pl.BlockSpec((batch, 1, nh, dk), lambda i: (0, i, 0, 0)),
                  wide, wide, vec, vec],
        out_specs=wide,
        out_shape=jax.ShapeDtypeStruct((batch, t, d), F32),
        scratch_shapes=[pltpu.VMEM((batch * nh, dk, dk), F32)],
        compiler_params=_params("arbitrary"),
        name="rwkv_scan",
    )(rt, kt, bt, at, v, wc, bonus, g, lnw, lnb)


def _compress_kernel(zk_ref, zv_ref, pek_ref, pev_ref, k1_ref, k2_ref, v1_ref, v2_ref, ko_ref, vo_ref):
    def one(z_ref, pe_ref, w1_ref, w2_ref):
        z = z_ref[0].astype(BF16)
        n16, half = z.shape
        first = _dot(z, w1_ref[:half, :])
        second = _dot(z, w1_ref[half:, :])
        bias = _dot(pe_ref[...].astype(BF16), w1_ref[...])[0:1, :]
        hidden = first + pltpu.roll(second, n16 - 1, axis=0) + bias
        return _dot(jax.nn.gelu(hidden).astype(BF16), w2_ref[...])

    kc = one(zk_ref, pek_ref, k1_ref, k2_ref)
    blk = lax.broadcasted_iota(jnp.int32, kc.shape, 0)
    col = lax.broadcasted_iota(jnp.int32, kc.shape, 1)
    per = SEL_BLOCK // CMP_STRIDE
    hi = blk // per
    lo = CMP_STRIDE * (blk % per) + (CMP_BLOCK - 1)
    feat = jnp.where(col == 0, hi, jnp.where(col == 1, lo, 0)).astype(F32)
    ko_ref[0] = jnp.concatenate([kc, feat], axis=1).astype(BF16)
    vc = one(zv_ref, pev_ref, v1_ref, v2_ref)
    vc_t = jnp.transpose(jnp.concatenate([vc, jnp.zeros_like(vc)], axis=1))
    vo_ref[0] = vc_t[:HEAD_DIM].astype(BF16)


def _compress(zk, zv, pek, pev, k1, k2, v1, v2):
    bh, n16, wide = zk.shape
    zspec = pl.BlockSpec((1, n16, wide), lambda i: (i, 0, 0))
    full = lambda a: pl.BlockSpec(a.shape, lambda i: (0,) * a.ndim)
    return pl.pallas_call(
        _compress_kernel,
        grid=(bh,),
        in_specs=[zspec, zspec, full(pek), full(pev), full(k1), full(k2), full(v1), full(v2)],
        out_specs=[pl.BlockSpec((1, n16, 2 * HEAD_DIM), lambda i: (i, 0, 0)),
                   pl.BlockSpec((1, HEAD_DIM, n16), lambda i: (i, 0, 0))],
        out_shape=[jax.ShapeDtypeStruct((bh, n16, 2 * HEAD_DIM), BF16),
                   jax.ShapeDtypeStruct((bh, HEAD_DIM, n16), BF16)],
        compiler_params=_params("arbitrary"),
        name="nsa_compress",
    )(zk, zv, pek, pev, k1, k2, v1, v2)


def _query_features(qt, slopes_ref, hk):
    lane_g = lax.broadcasted_iota(jnp.int32, (1, QG), 1) // Q_BLOCK
    slope = jnp.zeros((1, QG), F32)
    for g in range(NSA_GROUP):
        slope = jnp.where(lane_g == g, slopes_ref[hk * NSA_GROUP + g], slope)
    row = lax.broadcasted_iota(jnp.int32, (HEAD_DIM, QG), 0)
    extra = jnp.where(row == 0, SEL_BLOCK * slope, jnp.where(row == 1, slope, 0.0))
    return jnp.concatenate([qt * (HEAD_DIM ** -0.5), extra], axis=0).astype(BF16)


def _cmp_select_kernel(slopes_ref, qt_ref, kc_ref, vct_ref, ovt_ref, oct_ref, selt_ref):
    hk = pl.program_id(1)
    q0 = pl.program_id(2) * Q_BLOCK
    q_aug = _query_features(qt_ref[0, 0, 0], slopes_ref, hk)
    s = _dot(kc_ref[0], q_aug)
    ncp = s.shape[0]
    cmp_end = lax.broadcasted_iota(jnp.int32, (ncp, Q_BLOCK), 0) * CMP_STRIDE + (CMP_BLOCK - 1)
    t_q = q0 + lax.broadcasted_iota(jnp.int32, (ncp, Q_BLOCK), 1)
    ok = cmp_end <= t_q
    any_ok = (q0 + lax.broadcasted_iota(jnp.int32, (1, Q_BLOCK), 1) >= CMP_BLOCK - 1).astype(F32)

    p_sum = jnp.zeros((ncp, Q_BLOCK), F32)
    probs = []
    for g in range(NSA_GROUP):
        sg = jnp.where(ok, s[:, g * Q_BLOCK:(g + 1) * Q_BLOCK], NEG)
        e = jnp.exp(sg - jnp.max(sg, axis=0, keepdims=True))
        p = e * (any_ok / jnp.sum(e, axis=0, keepdims=True))
        p_sum = p_sum + p
        probs.append(p.astype(BF16))
    oct_ref[0, 0, 0] = _dot(vct_ref[0], jnp.concatenate(probs, axis=1))

    ovt = ovt_ref[...]
    hi = p_sum.astype(BF16)
    r1 = p_sum - hi.astype(F32)
    mid = r1.astype(BF16)
    lo = (r1 - mid.astype(F32)).astype(BF16)
    imp = _dot(ovt, hi) + _dot(ovt, mid) + _dot(ovt, lo)

    blk = lax.broadcasted_iota(jnp.int32, (LANES, Q_BLOCK), 0)
    cur = (q0 + lax.broadcasted_iota(jnp.int32, (LANES, Q_BLOCK), 1)) // SEL_BLOCK
    valid = blk <= cur
    forced = (blk == 0) | (blk == cur) | (blk == cur - 1)
    x = jnp.where(valid, jnp.where(forced, BIG, imp), NEG)
    blk_f = blk.astype(F32)
    chosen = jnp.zeros((LANES, Q_BLOCK), jnp.bool_)
    for _ in range(SEL_TOPK):
        m = jnp.max(x, axis=0, keepdims=True)
        first = jnp.min(jnp.where(x == m, blk_f, float(LANES)), axis=0, keepdims=True)
        hit = blk_f == first
        chosen = chosen | hit
        x = jnp.where(hit, -jnp.inf, x)
    selt_ref[0, 0, 0] = (chosen & valid).astype(BF16)


def _cmp_select(slopes, q_t, kc_aug, vc_t, ov_t):
    batch, hk, nq, dk, qg = q_t.shape
    ncp = kc_aug.shape[1]
    return pl.pallas_call(
        _cmp_select_kernel,
        grid=(batch, hk, nq),
        in_specs=[pl.BlockSpec(memory_space=pltpu.SMEM),
                  pl.BlockSpec((1, 1, 1, dk, qg), lambda b, h, i: (b, h, i, 0, 0)),
                  pl.BlockSpec((1, ncp, 2 * dk), lambda b, h, i: (b * hk + h, 0, 0)),
                  pl.BlockSpec((1, dk, ncp), lambda b, h, i: (b * hk + h, 0, 0)),
                  pl.BlockSpec(ov_t.shape, lambda b, h, i: (0, 0))],
        out_specs=[pl.BlockSpec((1, 1, 1, dk, qg), lambda b, h, i: (b, h, i, 0, 0)),
                   pl.BlockSpec((1, 1, 1, LANES, Q_BLOCK), lambda b, h, i: (b, h, i, 0, 0))],
        out_shape=[jax.ShapeDtypeStruct(q_t.shape, F32),
                   jax.ShapeDtypeStruct((batch, hk, nq, LANES, Q_BLOCK), BF16)],
        compiler_params=_params("arbitrary", "arbitrary", "arbitrary"),
        name="nsa_cmp_select",
    )(slopes, q_t, kc_aug, vc_t, ov_t)


def _sel_win_kernel(ids_ref, cnt_ref, slopes_ref, qt_ref, selt_ref, et_ref, ks_ref, vst_ref, kw_ref, vwt_ref,
                    oct_ref, glt_ref, yt_ref, m_ref, acc_ref):
    b, hk, i = pl.program_id(0), pl.program_id(1), pl.program_id(2)
    step = (b * pl.num_programs(1) + hk) * pl.num_programs(2) + i
    q0 = i * Q_BLOCK
    kt = KEY_TILE
    q_aug = _query_features(qt_ref[0, 0, 0], slopes_ref, hk)
    selt = selt_ref[0, 0, 0]
    lane_minus_row = (lax.broadcasted_iota(jnp.int32, (kt, Q_BLOCK), 1)
                      - lax.broadcasted_iota(jnp.int32, (kt, Q_BLOCK), 0))

    def reset():
        m_ref[...] = jnp.full_like(m_ref, SOFTMAX_FLOOR)
        acc_ref[...] = jnp.zeros_like(acc_ref)

    def attend(k_tile, vt_tile, mask):
        s = _dot(k_tile, q_aug)
        m_old = m_ref[...]
        probs, maxes = [], []
        for g in range(NSA_GROUP):
            gsl = slice(g * Q_BLOCK, (g + 1) * Q_BLOCK)
            sg = jnp.where(mask, s[:, gsl], NEG)
            mg = jnp.maximum(m_old[:, gsl], jnp.max(sg, axis=0, keepdims=True))
            probs.append(jnp.exp(sg - mg).astype(BF16))
            maxes.append(mg)
        m_new = jnp.concatenate(maxes, axis=1)
        acc_ref[...] = jnp.exp(m_old - m_new) * acc_ref[...] + _dot(vt_tile, jnp.concatenate(probs, axis=1))
        m_ref[...] = m_new

    def result():
        acc = acc_ref[...]
        return acc[:HEAD_DIM] / acc[HEAD_DIM:HEAD_DIM + 1]

    reset()

    def sel_step(n, carry):
        j = ids_ref[step * MAX_KEY_TILES + n]
        k0 = pl.multiple_of(j * kt, kt)
        picked = _dot(et_ref[pl.ds(k0, kt), :], selt)
        mask = (picked > 0.5) & (lane_minus_row >= k0 - q0)
        attend(ks_ref[0, 0, pl.ds(k0, kt), :], vst_ref[0, 0, j], mask)
        return carry

    lax.fori_loop(0, cnt_ref[step], sel_step, 0)
    o_s = result()

    reset()
    last = (q0 + Q_BLOCK - 1) // kt

    def win_step(j, carry):
        k0 = pl.multiple_of(j * kt, kt)
        dist = lane_minus_row + (q0 - k0)
        attend(kw_ref[0, 0, pl.ds(k0, kt), :], vwt_ref[0, 0, j], (dist >= 0) & (dist < WINDOW))
        return carry

    lax.fori_loop(jnp.maximum(last - (WIN_TILES - 1), 0), last + 1, win_step, 0)
    o_w = result()

    gates = jax.nn.sigmoid(glt_ref[0, 0])
    o_c = oct_ref[0, 0, 0]
    outs = []
    for g in range(NSA_GROUP):
        gsl = slice(g * Q_BLOCK, (g + 1) * Q_BLOCK)
        outs.append(gates[3 * g:3 * g + 1] * o_c[:, gsl] + gates[3 * g + 1:3 * g + 2] * o_s[:, gsl]
                    + gates[3 * g + 2:3 * g + 3] * o_w[:, gsl])
    yt_ref[0] = jnp.concatenate(outs, axis=0)


def _sel_win(tile_ids, tile_cnt, slopes, q_t, sel_t, expand_t, ks, vs_t, kw, vw_t, oc_t, gl_t):
    batch, hk, nq, dk, qg = q_t.shape
    t = nq * Q_BLOCK
    nkt = t // KEY_TILE
    grp = qg // Q_BLOCK
    qspec = pl.BlockSpec((1, 1, 1, dk, qg), lambda b, h, i, *_: (b, h, i, 0, 0))
    kspec = pl.BlockSpec((1, 1, t, 2 * dk), lambda b, h, i, *_: (b, h, 0, 0))
    vspec = pl.BlockSpec((1, 1, nkt, V_ROWS, KEY_TILE), lambda b, h, i, *_: (b, h, 0, 0, 0))
    grid_spec = pltpu.PrefetchScalarGridSpec(
        num_scalar_prefetch=2,
        grid=(batch, hk, nq),
        in_specs=[pl.BlockSpec(memory_space=pltpu.SMEM), qspec,
                  pl.BlockSpec((1, 1, 1, LANES, Q_BLOCK), lambda b, h, i, *_: (b, h, i, 0, 0)),
                  pl.BlockSpec(expand_t.shape, lambda b, h, i, *_: (0, 0)),
                  kspec, vspec, kspec, vspec, qspec,
                  pl.BlockSpec((1, 1, 3 * grp, Q_BLOCK), lambda b, h, i, *_: (b, h, 0, i))],
        out_specs=pl.BlockSpec((1, grp * dk, Q_BLOCK), lambda b, h, i, *_: (b, h, i)),
        scratch_shapes=[pltpu.VMEM((1, qg), F32), pltpu.VMEM((V_ROWS, qg), F32)],
    )
    return pl.pallas_call(
        _sel_win_kernel,
        grid_spec=grid_spec,
        out_shape=jax.ShapeDtypeStruct((batch, hk * grp * dk, t), F32),
        compiler_params=_params("arbitrary", "arbitrary", "arbitrary"),
        name="nsa_sel_win",
    )(tile_ids, tile_cnt, slopes, q_t, sel_t, expand_t, ks, vs_t, kw, vw_t, oc_t, gl_t)


def _ffn_kernel(x_ref, yr_ref, ynt_ref, wo1_ref, wo2_ref, g2_ref, wg_ref, wu_ref, wd_ref, gf_ref,
                o_ref, h1_ref, hn_ref, acc_ref):
    j = pl.program_id(1)

    @pl.when(j == 0)
    def _():
        h1 = (x_ref[...] + _dot(yr_ref[...].astype(BF16), wo1_ref[...])
              + _dot_tn(ynt_ref[0].astype(BF16), wo2_ref[...]))
        h1_ref[...] = h1
        hn_ref[...] = _rms(h1, g2_ref[...]).astype(BF16)
        acc_ref[...] = jnp.zeros_like(acc_ref)

    hn = hn_ref[...]
    gate = _dot(hn, wg_ref[...])
    up = _dot(hn, wu_ref[...])
    act = gate * jax.nn.sigmoid(gate) * up
    acc_ref[...] += _dot(act.astype(BF16), wd_ref[...])

    @pl.when(j == pl.num_programs(1) - 1)
    def _():
        o_ref[...] = _rms(h1_ref[...] + acc_ref[...], gf_ref[...])


def _ffn(x2, yr, yn_t, wo1, wo2, g2, wg, wu, wd, gf, tm=512, nf=2):
    m, d = x2.shape
    _, dn, t = yn_t.shape
    per_seq = t // tm
    dff = wg.shape[1]
    tf = dff // nf
    row = lambda n: pl.BlockSpec((tm, n), lambda i, j: (i, 0))
    const = lambda a: pl.BlockSpec(a.shape, lambda i, j: (0, 0))
    return pl.pallas_call(
        _ffn_kernel,
        grid=(m // tm, nf),
        in_specs=[row(d), row(yr.shape[1]),
                  pl.BlockSpec((1, dn, tm), lambda i, j: (i // per_seq, 0, i % per_seq)),
                  const(wo1), const(wo2), const(g2),
                  pl.BlockSpec((d, tf), lambda i, j: (0, j)), pl.BlockSpec((d, tf), lambda i, j: (0, j)),
                  pl.BlockSpec((tf, d), lambda i, j: (j, 0)), const(gf)],
        out_specs=row(d),
        out_shape=jax.ShapeDtypeStruct((m, d), F32),
        scratch_shapes=[pltpu.VMEM((tm, d), F32), pltpu.VMEM((tm, d), BF16), pltpu.VMEM((tm, d), F32)],
        compiler_params=_params("arbitrary", "arbitrary"),
        name="outproj_ffn",
    )(x2, yr, yn_t, wo1, wo2, g2, wg, wu, wd, gf)


def _overlap_matrix_t(t):
    n16 = t // CMP_STRIDE
    n_cmp = (t - CMP_BLOCK) // CMP_STRIDE + 1
    n_sel = t // SEL_BLOCK
    cmp_start = np.arange(n_cmp) * CMP_STRIDE
    sel_start = np.arange(n_sel) * SEL_BLOCK
    ov = np.clip(np.minimum(cmp_start[:, None] + CMP_BLOCK, sel_start[None, :] + SEL_BLOCK)
                 - np.maximum(cmp_start[:, None], sel_start[None, :]), 0, None) / CMP_STRIDE
    full = np.zeros((LANES, n16), np.float32)
    full[:n_sel, :n_cmp] = ov.T
    return jnp.asarray(full, BF16)


def _expand_matrix_t(t):
    blk = np.arange(t) // SEL_BLOCK
    return jnp.asarray(blk[:, None] == np.arange(LANES)[None, :], BF16)


def _position_columns(t):
    pos = np.arange(t)
    cols = np.zeros((t, HEAD_DIM), np.float32)
    cols[:, 0] = pos // SEL_BLOCK
    cols[:, 1] = pos % SEL_BLOCK
    return jnp.asarray(cols)


def kernel(x, norm1_g, w_in, mu_shift, rwkv_w0, rwkv_w2, rwkv_a0, rwkv_a2, rwkv_g2, rwkv_k_k, rwkv_k_a,
           rwkv_r_k, rwkv_lnx_w, rwkv_lnx_b, nsa_pe_k, nsa_pe_v, nsa_cmp_k_w1, nsa_cmp_k_w2, nsa_cmp_v_w1,
           nsa_cmp_v_w2, w_out, norm2_g, ffn_w_gate, ffn_w_up, ffn_w_down, norm_f_g):
    batch, t, d_model = x.shape
    assert w_in.shape[0] == 1, "the final RMSNorm is fused into the (single) layer's FFN kernel"
    assert t % RWKV_PREP_ROWS == 0 and t % KEY_TILE == 0 and t // SEL_BLOCK <= LANES
    hk, grp, dk = NSA_KV_HEADS, NSA_GROUP, HEAD_DIM
    nq = t // Q_BLOCK
    slopes = 2.0 ** (-8.0 * jnp.arange(1, NSA_Q_HEADS + 1, dtype=F32) / NSA_Q_HEADS)
    nsa_pad = _round_up(N_NSA_COLS, LANES)
    row = lambda a: a.reshape(1, -1)
    i = 0

    h = x.reshape(batch * t, d_model)
    w_r = w_in[i][:, :N_RWKV_COLS].astype(BF16)
    w_n = jnp.pad(w_in[i][:, N_RWKV_COLS:], ((0, 0), (0, nsa_pad - N_NSA_COLS))).astype(BF16)
    p_r, p_n = _inproj(h, row(norm1_g[i]), w_r, w_n)

    rt, kt, bt, at, v, wc, bonus, g = _rwkv_prep(
        p_r, batch, row(mu_shift[i]), row(rwkv_w0[i]), rwkv_w2[i].astype(BF16), row(rwkv_a0[i]),
        rwkv_a2[i].astype(BF16), rwkv_g2[i].astype(BF16), row(rwkv_k_k[i]), row(rwkv_k_a[i]),
        row(rwkv_r_k[i]))
    y_rwkv = _rwkv_scan(rt, kt, bt, at, v, wc, bonus, g, row(rwkv_lnx_w[i]), row(rwkv_lnx_b[i]))
    y_rwkv = y_rwkv.reshape(batch * t, D_RWKV)

    pn = p_n.reshape(batch, t, nsa_pad)
    q_t = (pn[..., :D_NSA].reshape(batch, nq, Q_BLOCK, hk, grp, dk)
           .transpose(0, 3, 1, 5, 4, 2).reshape(batch, hk, nq, dk, QG))
    kv = pn[..., D_NSA:D_NSA + 6 * D_KV].reshape(batch, t, 6, hk, dk).transpose(2, 0, 3, 1, 4)
    n16 = t // CMP_STRIDE
    zk = kv[0].reshape(batch * hk, n16, CMP_STRIDE * dk)
    zv = kv[1].reshape(batch * hk, n16, CMP_STRIDE * dk)
    pos_cols = jnp.broadcast_to(_position_columns(t), (batch, hk, t, dk))
    with_pos = lambda k: jnp.concatenate([k, pos_cols], axis=-1).astype(BF16)
    ones_rows = jnp.zeros((batch, hk, V_ROWS - dk, t), F32).at[:, :, 0].set(1.0)

    def tiles_t(val):
        vt = jnp.concatenate([val.transpose(0, 1, 3, 2), ones_rows], axis=2)
        return vt.reshape(batch, hk, V_ROWS, t // KEY_TILE, KEY_TILE).transpose(0, 1, 3, 2, 4).astype(BF16)

    ks, vs_t, kw, vw_t = with_pos(kv[2]), tiles_t(kv[3]), with_pos(kv[4]), tiles_t(kv[5])
    gl_t = pn[..., D_NSA + 6 * D_KV:N_NSA_COLS].reshape(batch, t, hk, 3 * grp).transpose(0, 2, 3, 1)

    pe_rows = lambda pe: jnp.broadcast_to(pe.reshape(1, -1), (8, CMP_BLOCK * dk))
    kc_aug, vc_t = _compress(zk, zv, pe_rows(nsa_pe_k[i]), pe_rows(nsa_pe_v[i]),
                             nsa_cmp_k_w1[i].astype(BF16), nsa_cmp_k_w2[i].astype(BF16),
                             nsa_cmp_v_w1[i].astype(BF16), nsa_cmp_v_w2[i].astype(BF16))
    oc_t, sel_t = _cmp_select(slopes, q_t, kc_aug, vc_t, _overlap_matrix_t(t))

    blocks_per_tile = KEY_TILE // SEL_BLOCK
    active = sel_t.reshape(batch, hk, nq, MAX_KEY_TILES, blocks_per_tile * Q_BLOCK).max(axis=-1) > 0
    tile_ids = jnp.argsort(jnp.logical_not(active), axis=-1, stable=True).astype(jnp.int32).reshape(-1)
    tile_cnt = active.sum(axis=-1).astype(jnp.int32).reshape(-1)
    y_nsa_t = _sel_win(tile_ids, tile_cnt, slopes, q_t, sel_t, _expand_matrix_t(t), ks, vs_t, kw, vw_t,
                       oc_t, gl_t)

    out = _ffn(h, y_rwkv, y_nsa_t, w_out[i][:D_RWKV].astype(BF16), w_out[i][D_RWKV:].astype(BF16),
               row(norm2_g[i]), ffn_w_gate[i].astype(BF16), ffn_w_up[i].astype(BF16),
               ffn_w_down[i].astype(BF16), row(norm_f_g))
    return out.reshape(batch, t, d_model)
```

```python
import numpy as np
import jax
import jax.numpy as jnp
from jax import lax
from jax.experimental import pallas as pl
from jax.experimental.pallas import tpu as pltpu

F32 = jnp.float32
BF16 = jnp.bfloat16

HEAD_DIM = 64
RWKV_HEADS = 8
D_RWKV = RWKV_HEADS * HEAD_DIM
NSA_Q_HEADS = 8
NSA_KV_HEADS = 2
NSA_GROUP = NSA_Q_HEADS // NSA_KV_HEADS
D_NSA = NSA_Q_HEADS * HEAD_DIM
D_KV = NSA_KV_HEADS * HEAD_DIM
LORA_W, LORA_A, LORA_G = 64, 64, 128
N_RWKV_COLS = 3 * D_RWKV + LORA_W + LORA_A + LORA_G
N_NSA_COLS = D_NSA + 6 * D_KV + 3 * NSA_Q_HEADS
CMP_BLOCK, CMP_STRIDE = 32, 16
SEL_BLOCK, SEL_TOPK = 64, 16
WINDOW = 512
Q_BLOCK = 128
NORM_EPS = 1e-6
GN_EPS = 64e-5
NEG = -1e30
BIG = 1e30

LANES = 128
RWKV_CHUNK = 64
RWKV_PREP_ROWS = 8 * RWKV_CHUNK
KEY_TILE = 256
MAX_KEY_TILES = LANES * SEL_BLOCK // KEY_TILE
WIN_TILES = (WINDOW + Q_BLOCK) // KEY_TILE + 1
SEL_GROUP = 3
V_ROWS = HEAD_DIM + 16
SOFTMAX_FLOOR = -1e20
QG = NSA_GROUP * Q_BLOCK
VMEM_LIMIT = 56 * 1024 * 1024


def _round_up(n, m):
    return -(-n // m) * m


def _dot(a, b):
    return jnp.dot(a, b, preferred_element_type=F32)


def _dot_tn(a, b):
    return lax.dot_general(a, b, (((0,), (0,)), ((), ())), preferred_element_type=F32)


def _dot_f32(a, b):
    return jnp.dot(a, b, preferred_element_type=F32, precision=lax.Precision.HIGHEST)


def _bdot(spec, a, b):
    return jnp.einsum(spec, a, b, preferred_element_type=F32)


def _rms(x, g):
    return x * lax.rsqrt(jnp.mean(x * x, axis=-1, keepdims=True) + NORM_EPS) * g


def _params(*sem):
    return pltpu.CompilerParams(dimension_semantics=sem, vmem_limit_bytes=VMEM_LIMIT)


def _inproj_kernel(x_ref, g_ref, wr_ref, wn_ref, pr_ref, qt_ref, ks_ref, kw_ref, vst_ref, vwt_ref,
                   kc_ref, vc_ref, glt_ref):
    xb = _rms(x_ref[...], g_ref[...]).astype(BF16)
    pr_ref[...] = _dot(xb, wr_ref[...])
    pn = _dot(xb, wn_ref[...])
    rows = pn.shape[0]
    dk, hk_n = HEAD_DIM, NSA_KV_HEADS
    group = lambda j: pn[:, D_NSA + j * D_KV:D_NSA + (j + 1) * D_KV]

    for half in range(rows // Q_BLOCK):
        q_tr = jnp.transpose(pn[half * Q_BLOCK:(half + 1) * Q_BLOCK, :D_NSA])
        for hk in range(hk_n):
            base = hk * NSA_GROUP * dk
            qt_ref[0, hk, half] = jnp.concatenate(
                [q_tr[base + g * dk:base + (g + 1) * dk, :] for g in range(NSA_GROUP)], axis=1)

    pos = pl.program_id(1) * rows + lax.broadcasted_iota(jnp.int32, (rows, dk), 0)
    col = lax.broadcasted_iota(jnp.int32, (rows, dk), 1)
    pos_cols = jnp.where(col == 0, pos // SEL_BLOCK, jnp.where(col == 1, pos % SEL_BLOCK, 0)).astype(F32)
    ones_rows = (lax.broadcasted_iota(jnp.int32, (V_ROWS - dk, rows), 0) == 0).astype(F32)
    kc, vc, ks, vs, kw, vw = (group(j) for j in range(6))
    vs_tr, vw_tr = jnp.transpose(vs), jnp.transpose(vw)
    gl_tr = jnp.transpose(pn[:, D_NSA + 6 * D_KV:])
    n_gate = 3 * NSA_GROUP
    for hk in range(hk_n):
        sl = slice(hk * dk, (hk + 1) * dk)
        kc_ref[0, hk] = kc[:, sl]
        vc_ref[0, hk] = vc[:, sl]
        ks_ref[0, hk] = jnp.concatenate([ks[:, sl], pos_cols], axis=1).astype(BF16)
        kw_ref[0, hk] = jnp.concatenate([kw[:, sl], pos_cols], axis=1).astype(BF16)
        vst_ref[0, hk, 0] = jnp.concatenate([vs_tr[sl, :], ones_rows], axis=0).astype(BF16)
        vwt_ref[0, hk, 0] = jnp.concatenate([vw_tr[sl, :], ones_rows], axis=0).astype(BF16)
        glt_ref[0, hk] = gl_tr[hk * n_gate:(hk + 1) * n_gate, :]


def _inproj(x2, batch, g, w_r, w_n):
    m, d = x2.shape
    t = m // batch
    tm = KEY_TILE
    nt = t // tm
    nr, nn = w_r.shape[1], w_n.shape[1]
    hk, dk = NSA_KV_HEADS, HEAD_DIM
    const = lambda a: pl.BlockSpec(a.shape, lambda b, i: (0, 0))
    keys = pl.BlockSpec((1, hk, tm, 2 * dk), lambda b, i: (b, 0, i, 0))
    vals = pl.BlockSpec((1, hk, 1, V_ROWS, tm), lambda b, i: (b, 0, i, 0, 0))
    cmp_in = pl.BlockSpec((1, hk, tm, dk), lambda b, i: (b, 0, i, 0))
    keys_shape = jax.ShapeDtypeStruct((batch, hk, t, 2 * dk), BF16)
    vals_shape = jax.ShapeDtypeStruct((batch, hk, nt, V_ROWS, tm), BF16)
    cmp_shape = jax.ShapeDtypeStruct((batch, hk, t, dk), F32)
    return pl.pallas_call(
        _inproj_kernel,
        grid=(batch, nt),
        in_specs=[pl.BlockSpec((tm, d), lambda b, i: (b * nt + i, 0)), const(g), const(w_r), const(w_n)],
        out_specs=[pl.BlockSpec((tm, nr), lambda b, i: (b * nt + i, 0)),
                   pl.BlockSpec((1, hk, tm // Q_BLOCK, dk, QG), lambda b, i: (b, 0, i, 0, 0)),
                   keys, keys, vals, vals, cmp_in, cmp_in,
                   pl.BlockSpec((1, hk, 3 * NSA_GROUP, tm), lambda b, i: (b, 0, 0, i))],
        out_shape=[jax.ShapeDtypeStruct((m, nr), F32),
                   jax.ShapeDtypeStruct((batch, hk, t // Q_BLOCK, dk, QG), F32),
                   keys_shape, keys_shape, vals_shape, vals_shape, cmp_shape, cmp_shape,
                   jax.ShapeDtypeStruct((batch, hk, 3 * NSA_GROUP, t), F32)],
        compiler_params=_params("arbitrary", "arbitrary"),
        name="inproj",
    )(x2, g, w_r, w_n)


def _rwkv_prep_kernel(p_ref, mu_ref, w0_ref, w2_ref, a0_ref, a2_ref, g2_ref, kk_ref, ka_ref, rk_ref,
                      bd_ref, tri_ref,
                      rt_ref, kt_ref, bt_ref, at_ref, v_ref, wc_ref, bonus_ref, g_ref, carry_ref):
    rows = p_ref.shape[0]
    c = RWKV_CHUNK

    @pl.when(pl.program_id(1) == 0)
    def _():
        carry_ref[...] = jnp.zeros_like(carry_ref)

    p = p_ref[...]
    row = lax.broadcasted_iota(jnp.int32, p.shape, 0)
    p_prev = jnp.where(row == 0, carry_ref[7:8, :], pltpu.roll(p, 1, axis=0))
    carry_ref[...] = p[rows - 8:, :]
    ps = p + mu_ref[...] * (p_prev - p)

    d = D_RWKV
    r, k, v = ps[:, 0:d], ps[:, d:2 * d], ps[:, 2 * d:3 * d]
    dw = ps[:, 3 * d:3 * d + LORA_W]
    da = ps[:, 3 * d + LORA_W:3 * d + LORA_W + LORA_A]
    dg = ps[:, 3 * d + LORA_W + LORA_A:]

    z = -(w0_ref[...] + _dot(jnp.tanh(dw).astype(BF16), w2_ref[...]))
    softplus = jnp.maximum(z, 0.0) + jnp.log1p(jnp.exp(-jnp.abs(z)))
    logw = -jnp.exp(-softplus - 0.5)
    a = jax.nn.sigmoid(a0_ref[...] + _dot(da.astype(BF16), a2_ref[...]))
    g_ref[0] = _dot(jax.nn.sigmoid(dg).astype(BF16), g2_ref[...])

    bd = bd_ref[...]
    kk = k * kk_ref[...]
    kk = kk / jnp.maximum(jnp.sqrt(_dot_f32(kk * kk, bd)), 1e-12)
    kp = k * (1.0 + (a - 1.0) * ka_ref[...])
    bonus_ref[0] = _dot_f32(r * kp * rk_ref[...], bd) * v

    tri = tri_ref[...]
    cums = [_dot_f32(tri, logw[ci * c:(ci + 1) * c]) for ci in range(rows // c)]
    cum = jnp.concatenate(cums, axis=0)
    e_pos, e_neg = jnp.exp(cum), jnp.exp(-cum)
    outs = ((rt_ref, r * e_pos), (kt_ref, kp * e_neg), (bt_ref, kk * a * e_neg),
            (at_ref, -kk * jnp.exp(cum - logw)), (v_ref, v))
    for h in range(RWKV_HEADS):
        sl = slice(h * HEAD_DIM, (h + 1) * HEAD_DIM)
        for ref, val in outs:
            ref[0, h] = val[:, sl]
        for ci in range(rows // c):
            wc_ref[0, ci, h:h + 1, :] = e_pos[(ci + 1) * c - 1:(ci + 1) * c, sl]


def _rwkv_prep(p_r, batch, mu, w0, w2, a0, a2, g2, k_k, k_a, r_k):
    m = p_r.shape[0]
    t = m // batch
    rows = RWKV_PREP_ROWS
    nt = t // rows
    d = D_RWKV
    head = np.arange(d) // HEAD_DIM
    bd = jnp.asarray(head[:, None] == head[None, :], F32)
    tri = jnp.asarray(np.tril(np.ones((RWKV_CHUNK, RWKV_CHUNK))), F32)
    vec = lambda n: pl.BlockSpec((1, n), lambda b, i: (0, 0))
    full = lambda a: pl.BlockSpec(a.shape, lambda b, i: (0, 0))
    wide = pl.BlockSpec((1, rows, d), lambda b, i: (b, i, 0))
    heads = pl.BlockSpec((1, RWKV_HEADS, rows, HEAD_DIM), lambda b, i: (b, 0, i, 0))
    wide_shape = jax.ShapeDtypeStruct((batch, t, d), F32)
    heads_shape = jax.ShapeDtypeStruct((batch, RWKV_HEADS, t, HEAD_DIM), F32)
    outs = pl.pallas_call(
        _rwkv_prep_kernel,
        grid=(batch, nt),
        in_specs=[pl.BlockSpec((rows, N_RWKV_COLS), lambda b, i: (b * nt + i, 0)),
                  vec(N_RWKV_COLS), vec(d), full(w2), vec(d), full(a2), full(g2), vec(d), vec(d), vec(d),
                  full(bd), full(tri)],
        out_specs=[heads, heads, heads, heads, heads,
                   pl.BlockSpec((1, rows // RWKV_CHUNK, RWKV_HEADS, HEAD_DIM), lambda b, i: (b, i, 0, 0)),
                   wide, wide],
        out_shape=[heads_shape, heads_shape, heads_shape, heads_shape, heads_shape,
                   jax.ShapeDtypeStruct((batch, t // RWKV_CHUNK, RWKV_HEADS, HEAD_DIM), F32),
                   wide_shape, wide_shape],
        scratch_shapes=[pltpu.VMEM((8, N_RWKV_COLS), F32)],
        compiler_params=_params("arbitrary", "arbitrary"),
        name="rwkv_prep",
    )(p_r, mu, w0, w2, a0, a2, g2, k_k, k_a, r_k, bd, tri)
    return outs


def _rwkv_scan_kernel(rt_ref, kt_ref, bt_ref, at_ref, v_ref, wc_ref, bonus_ref, g_ref, lnw_ref, lnb_ref,
                      y_ref, s_ref):
    nb, nh, c, dk = rt_ref.shape
    n = nb * nh

    @pl.when(pl.program_id(0) == 0)
    def _():
        s_ref[...] = jnp.zeros_like(s_ref)

    row = lax.broadcasted_iota(jnp.int32, (1, c, c), 1)
    col = lax.broadcasted_iota(jnp.int32, (1, c, c), 2)
    strict = col < row
    incl = col <= row
    eye = (row == col).astype(F32)

    load = lambda ref: ref[...].reshape(n, c, dk)
    vb = load(v_ref).astype(BF16)
    ar = jnp.concatenate([load(at_ref), load(rt_ref)], axis=1).astype(BF16)
    bk = jnp.concatenate([load(bt_ref), load(kt_ref)], axis=1).astype(BF16)
    amat = _bdot("nik,njk->nij", ar, bk)
    n_ab = jnp.where(strict, amat[:, :c, :c], 0.0)
    a_ak = jnp.where(strict, amat[:, :c, c:], 0.0)
    a_rb = jnp.where(incl, amat[:, c:, :c], 0.0)
    a_rk = jnp.where(incl, amat[:, c:, c:], 0.0)
    s0 = s_ref[...]
    ar_s = _bdot("nik,nvk->niv", ar, s0.astype(BF16))
    rhs = ar_s[:, :c] + _bdot("nij,njv->niv", a_ak.astype(BF16), vb)
    inv = eye + n_ab
    pw = n_ab
    for _ in range(int(np.log2(c)) - 1):
        pwb = pw.astype(BF16)
        pw = _bdot("nij,njk->nik", pwb, pwb)
        inv = inv + _bdot("nij,njk->nik", pw.astype(BF16), inv.astype(BF16))
    ub = _bdot("nij,njv->niv", inv.astype(BF16), rhs.astype(BF16)).astype(BF16)
    y = ar_s[:, c:] + _bdot("nij,njv->niv", a_rb.astype(BF16), ub) + _bdot("nij,njv->niv", a_rk.astype(BF16), vb)
    uv_t = jnp.swapaxes(jnp.concatenate([ub, vb], axis=1), 1, 2)
    wc = wc_ref[...].reshape(n, 1, dk)
    s_ref[...] = (s0 + _bdot("nvi,nik->nvk", uv_t, bk)) * wc

    mean = jnp.mean(y, axis=-1, keepdims=True)
    var = jnp.mean(jnp.square(y - mean), axis=-1, keepdims=True)
    yn = (y - mean) * lax.rsqrt(var + GN_EPS)
    for b in range(nb):
        wide = jnp.concatenate([yn[b * nh + h] for h in range(nh)], axis=-1)
        y_ref[b] = (wide * lnw_ref[...] + lnb_ref[...] + bonus_ref[b]) * g_ref[b]


def _rwkv_scan(rt, kt, bt, at, v, wc, bonus, g, lnw, lnb):
    batch, nh, t, dk = rt.shape
    c = RWKV_CHUNK
    d = nh * dk
    heads = pl.BlockSpec((batch, nh, c, dk), lambda i: (0, 0, i, 0))
    wide = pl.BlockSpec((batch, c, d), lambda i: (0, i, 0))
    vec = pl.BlockSpec((1, d), lambda i: (0, 0))
    return pl.pallas_call(
        _rwkv_scan_kernel,
        grid=(t // c,),
        in_specs=[heads, heads, heads, heads, heads,
                  pl.BlockSpec((batch, 1, nh, dk), lambda i: (0, i, 0, 0)),
                  wide, wide, vec, vec],
        out_specs=wide,
        out_shape=jax.ShapeDtypeStruct((batch, t, d), F32),
        scratch_shapes=[pltpu.VMEM((batch * nh, dk, dk), F32)],
        compiler_params=_params("arbitrary"),
        name="rwkv_scan",
    )(rt, kt, bt, at, v, wc, bonus, g, lnw, lnb)


def _compress_kernel(zk_ref, zv_ref, pek_ref, pev_ref, k1_ref, k2_ref, v1_ref, v2_ref, ko_ref, vo_ref):
    def one(z_ref, pe_ref, w1_ref, w2_ref):
        n16 = z_ref.shape[2] // CMP_STRIDE
        dk = HEAD_DIM
        first = second = None
        for l in range(CMP_STRIDE):
            z = z_ref[0, 0, pl.ds(l, n16, stride=CMP_STRIDE), :]
            lo = _dot((z + pe_ref[l:l + 1, :]).astype(BF16), w1_ref[l * dk:(l + 1) * dk, :])
            u = CMP_STRIDE + l
            hi = _dot((z + pe_ref[u:u + 1, :]).astype(BF16), w1_ref[u * dk:(u + 1) * dk, :])
            first = lo if first is None else first + lo
            second = hi if second is None else second + hi
        hidden = first + pltpu.roll(second, n16 - 1, axis=0)
        return _dot(jax.nn.gelu(hidden).astype(BF16), w2_ref[...])

    kc = one(zk_ref, pek_ref, k1_ref, k2_ref)
    blk = lax.broadcasted_iota(jnp.int32, kc.shape, 0)
    col = lax.broadcasted_iota(jnp.int32, kc.shape, 1)
    per = SEL_BLOCK // CMP_STRIDE
    hi = blk // per
    lo = CMP_STRIDE * (blk % per) + (CMP_BLOCK - 1)
    feat = jnp.where(col == 0, hi, jnp.where(col == 1, lo, 0)).astype(F32)
    ko_ref[0] = jnp.concatenate([kc, feat], axis=1).astype(BF16)
    vc = one(zv_ref, pev_ref, v1_ref, v2_ref)
    vc_t = jnp.transpose(jnp.concatenate([vc, jnp.zeros_like(vc)], axis=1))
    vo_ref[0] = vc_t[:HEAD_DIM].astype(BF16)


def _compress(zk, zv, pek, pev, k1, k2, v1, v2):
    batch, hk, t, dk = zk.shape
    n16 = t // CMP_STRIDE
    zspec = pl.BlockSpec((1, 1, t, dk), lambda b, h: (b, h, 0, 0))
    full = lambda a: pl.BlockSpec(a.shape, lambda b, h: (0,) * a.ndim)
    return pl.pallas_call(
        _compress_kernel,
        grid=(batch, hk),
        in_specs=[zspec, zspec, full(pek), full(pev), full(k1), full(k2), full(v1), full(v2)],
        out_specs=[pl.BlockSpec((1, n16, 2 * dk), lambda b, h: (b * hk + h, 0, 0)),
                   pl.BlockSpec((1, dk, n16), lambda b, h: (b * hk + h, 0, 0))],
        out_shape=[jax.ShapeDtypeStruct((batch * hk, n16, 2 * dk), BF16),
                   jax.ShapeDtypeStruct((batch * hk, dk, n16), BF16)],
        compiler_params=_params("arbitrary", "arbitrary"),
        name="nsa_compress",
    )(zk, zv, pek, pev, k1, k2, v1, v2)


def _query_features(qt, slopes_ref, hk):
    lane_g = lax.broadcasted_iota(jnp.int32, (1, QG), 1) // Q_BLOCK
    slope = jnp.zeros((1, QG), F32)
    for g in range(NSA_GROUP):
        slope = jnp.where(lane_g == g, slopes_ref[hk * NSA_GROUP + g], slope)
    row = lax.broadcasted_iota(jnp.int32, (HEAD_DIM, QG), 0)
    extra = jnp.where(row == 0, SEL_BLOCK * slope, jnp.where(row == 1, slope, 0.0))
    return jnp.concatenate([qt * (HEAD_DIM ** -0.5), extra], axis=0).astype(BF16)


def _cmp_select_kernel(slopes_ref, qt_ref, kc_ref, vct_ref, ovt_ref, oct_ref, selt_ref):
    hk = pl.program_id(1)
    q0 = pl.program_id(2) * Q_BLOCK
    q_aug = _query_features(qt_ref[0, 0, 0], slopes_ref, hk)
    s = _dot(kc_ref[0], q_aug)
    ncp = s.shape[0]
    cmp_end = lax.broadcasted_iota(jnp.int32, (ncp, Q_BLOCK), 0) * CMP_STRIDE + (CMP_BLOCK - 1)
    t_q = q0 + lax.broadcasted_iota(jnp.int32, (ncp, Q_BLOCK), 1)
    ok = cmp_end <= t_q
    any_ok = (q0 + lax.broadcasted_iota(jnp.int32, (1, Q_BLOCK), 1) >= CMP_BLOCK - 1).astype(F32)

    p_sum = jnp.zeros((ncp, Q_BLOCK), F32)
    probs = []
    for g in range(NSA_GROUP):
        sg = jnp.where(ok, s[:, g * Q_BLOCK:(g + 1) * Q_BLOCK], NEG)
        e = jnp.exp(sg - jnp.max(sg, axis=0, keepdims=True))
        p = e * (any_ok / jnp.sum(e, axis=0, keepdims=True))
        p_sum = p_sum + p
        probs.append(p.astype(BF16))
    oct_ref[0, 0, 0] = _dot(vct_ref[0], jnp.concatenate(probs, axis=1))

    ovt = ovt_ref[...]
    hi = p_sum.astype(BF16)
    r1 = p_sum - hi.astype(F32)
    mid = r1.astype(BF16)
    lo = (r1 - mid.astype(F32)).astype(BF16)
    imp = _dot(ovt, hi) + _dot(ovt, mid) + _dot(ovt, lo)

    blk = lax.broadcasted_iota(jnp.int32, (LANES, Q_BLOCK), 0)
    cur = (q0 + lax.broadcasted_iota(jnp.int32, (LANES, Q_BLOCK), 1)) // SEL_BLOCK
    valid = blk <= cur
    forced = (blk == 0) | (blk == cur) | (blk == cur - 1)
    x = jnp.where(valid, jnp.where(forced, BIG, imp), NEG)
    blk_f = blk.astype(F32)
    chosen = jnp.zeros((LANES, Q_BLOCK), jnp.bool_)
    for _ in range(SEL_TOPK):
        m = jnp.max(x, axis=0, keepdims=True)
        first = jnp.min(jnp.where(x == m, blk_f, float(LANES)), axis=0, keepdims=True)
        hit = blk_f == first
        chosen = chosen | hit
        x = jnp.where(hit, -jnp.inf, x)
    selt_ref[0, 0, 0] = (chosen & valid).astype(BF16)


def _cmp_select(slopes, q_t, kc_aug, vc_t, ov_t):
    batch, hk, nq, dk, qg = q_t.shape
    ncp = kc_aug.shape[1]
    return pl.pallas_call(
        _cmp_select_kernel,
        grid=(batch, hk, nq),
        in_specs=[pl.BlockSpec(memory_space=pltpu.SMEM),
                  pl.BlockSpec((1, 1, 1, dk, qg), lambda b, h, i: (b, h, i, 0, 0)),
                  pl.BlockSpec((1, ncp, 2 * dk), lambda b, h, i: (b * hk + h, 0, 0)),
                  pl.BlockSpec((1, dk, ncp), lambda b, h, i: (b * hk + h, 0, 0)),
                  pl.BlockSpec(ov_t.shape, lambda b, h, i: (0, 0))],
        out_specs=[pl.BlockSpec((1, 1, 1, dk, qg), lambda b, h, i: (b, h, i, 0, 0)),
                   pl.BlockSpec((1, 1, 1, LANES, Q_BLOCK), lambda b, h, i: (b, h, i, 0, 0))],
        out_shape=[jax.ShapeDtypeStruct(q_t.shape, F32),
                   jax.ShapeDtypeStruct((batch, hk, nq, LANES, Q_BLOCK), BF16)],
        compiler_params=_params("arbitrary", "arbitrary", "arbitrary"),
        name="nsa_cmp_select",
    )(slopes, q_t, kc_aug, vc_t, ov_t)


def _sel_win_kernel(ids_ref, cnt_ref, slopes_ref, qt_ref, selt_ref, et_ref, ks_ref, vst_ref, kw_ref, vwt_ref,
                    oct_ref, glt_ref, yt_ref, m_ref, acc_ref):
    b, hk, i = pl.program_id(0), pl.program_id(1), pl.program_id(2)
    step = (b * pl.num_programs(1) + hk) * pl.num_programs(2) + i
    q0 = i * Q_BLOCK
    kt = KEY_TILE
    q_aug = _query_features(qt_ref[0, 0, 0], slopes_ref, hk)
    selt = selt_ref[0, 0, 0]
    lane_minus_row = (lax.broadcasted_iota(jnp.int32, (kt, Q_BLOCK), 1)
                      - lax.broadcasted_iota(jnp.int32, (kt, Q_BLOCK), 0))

    def reset():
        m_ref[...] = jnp.full_like(m_ref, SOFTMAX_FLOOR)
        acc_ref[...] = jnp.zeros_like(acc_ref)

    def attend(tiles):
        scores = [_dot(k_tile, q_aug) for k_tile, _, _ in tiles]
        m_old = m_ref[...]
        probs, maxes = [[] for _ in tiles], []
        for g in range(NSA_GROUP):
            gsl = slice(g * Q_BLOCK, (g + 1) * Q_BLOCK)
            masked = [jnp.where(mask, s[:, gsl], NEG) for s, (_, _, mask) in zip(scores, tiles)]
            mg = m_old[:, gsl]
            for sg in masked:
                mg = jnp.maximum(mg, jnp.max(sg, axis=0, keepdims=True))
            for n, sg in enumerate(masked):
                probs[n].append(jnp.exp(sg - mg).astype(BF16))
            maxes.append(mg)
        m_new = jnp.concatenate(maxes, axis=1)
        update = _dot(tiles[0][1], jnp.concatenate(probs[0], axis=1))
        for n in range(1, len(tiles)):
            update = update + _dot(tiles[n][1], jnp.concatenate(probs[n], axis=1))
        acc_ref[...] = jnp.exp(m_old - m_new) * acc_ref[...] + update
        m_ref[...] = m_new

    def result():
        acc = acc_ref[...]
        return acc[:HEAD_DIM] / acc[HEAD_DIM:HEAD_DIM + 1]

    reset()
    count = cnt_ref[step]
    last_tile = ks_ref.shape[2] // kt - 1

    def sel_tile(n):
        j = jnp.minimum(ids_ref[step * MAX_KEY_TILES + jnp.minimum(n, MAX_KEY_TILES - 1)], last_tile)
        k0 = pl.multiple_of(j * kt, kt)
        picked = _dot(et_ref[pl.ds(k0, kt), :], selt)
        causal_from = jnp.where(n < count, k0 - q0, 1 << 30)
        mask = (picked > 0.5) & (lane_minus_row >= causal_from)
        return ks_ref[0, 0, pl.ds(k0, kt), :], vst_ref[0, 0, j], mask

    def sel_step(n, carry):
        attend([sel_tile(SEL_GROUP * n + u) for u in range(SEL_GROUP)])
        return carry

    lax.fori_loop(0, (count + SEL_GROUP - 1) // SEL_GROUP, sel_step, 0)
    o_s = result()

    reset()
    last = (q0 + Q_BLOCK - 1) // kt

    def win_tile(u):
        j = last - u
        jc = jnp.maximum(j, 0)
        k0 = pl.multiple_of(jc * kt, kt)
        dist = lane_minus_row + jnp.where(j >= 0, q0 - k0, -(1 << 30))
        mask = (dist >= 0) & (dist < WINDOW)
        return kw_ref[0, 0, pl.ds(k0, kt), :], vwt_ref[0, 0, jc], mask

    attend([win_tile(u) for u in range(WIN_TILES)])
    o_w = result()

    gates = jax.nn.sigmoid(glt_ref[0, 0])
    o_c = oct_ref[0, 0, 0]
    outs = []
    for g in range(NSA_GROUP):
        gsl = slice(g * Q_BLOCK, (g + 1) * Q_BLOCK)
        outs.append(gates[3 * g:3 * g + 1] * o_c[:, gsl] + gates[3 * g + 1:3 * g + 2] * o_s[:, gsl]
                    + gates[3 * g + 2:3 * g + 3] * o_w[:, gsl])
    yt_ref[0] = jnp.concatenate(outs, axis=0)


def _sel_win(tile_ids, tile_cnt, slopes, q_t, sel_t, expand_t, ks, vs_t, kw, vw_t, oc_t, gl_t):
    batch, hk, nq, dk, qg = q_t.shape
    t = nq * Q_BLOCK
    nkt = t // KEY_TILE
    grp = qg // Q_BLOCK
    qspec = pl.BlockSpec((1, 1, 1, dk, qg), lambda b, h, i, *_: (b, h, i, 0, 0))
    kspec = pl.BlockSpec((1, 1, t, 2 * dk), lambda b, h, i, *_: (b, h, 0, 0))
    vspec = pl.BlockSpec((1, 1, nkt, V_ROWS, KEY_TILE), lambda b, h, i, *_: (b, h, 0, 0, 0))
    grid_spec = pltpu.PrefetchScalarGridSpec(
        num_scalar_prefetch=2,
        grid=(batch, hk, nq),
        in_specs=[pl.BlockSpec(memory_space=pltpu.SMEM), qspec,
                  pl.BlockSpec((1, 1, 1, LANES, Q_BLOCK), lambda b, h, i, *_: (b, h, i, 0, 0)),
                  pl.BlockSpec(expand_t.shape, lambda b, h, i, *_: (0, 0)),
                  kspec, vspec, kspec, vspec, qspec,
                  pl.BlockSpec((1, 1, 3 * grp, Q_BLOCK), lambda b, h, i, *_: (b, h, 0, i))],
        out_specs=pl.BlockSpec((1, grp * dk, Q_BLOCK), lambda b, h, i, *_: (b, h, i)),
        scratch_shapes=[pltpu.VMEM((1, qg), F32), pltpu.VMEM((V_ROWS, qg), F32)],
    )
    return pl.pallas_call(
        _sel_win_kernel,
        grid_spec=grid_spec,
        out_shape=jax.ShapeDtypeStruct((batch, hk * grp * dk, t), F32),
        compiler_params=_params("arbitrary", "arbitrary", "arbitrary"),
        name="nsa_sel_win",
    )(tile_ids, tile_cnt, slopes, q_t, sel_t, expand_t, ks, vs_t, kw, vw_t, oc_t, gl_t)


def _ffn_kernel(x_ref, yr_ref, ynt_ref, wo1_ref, wo2_ref, g2_ref, wg_ref, wu_ref, wd_ref, gf_ref,
                o_ref, h1_ref, hn_ref, acc_ref):
    j = pl.program_id(1)

    @pl.when(j == 0)
    def _():
        h1 = (x_ref[...] + _dot(yr_ref[...].astype(BF16), wo1_ref[...])
              + _dot_tn(ynt_ref[0].astype(BF16), wo2_ref[...]))
        h1_ref[...] = h1
        hn_ref[...] = _rms(h1, g2_ref[...]).astype(BF16)
        acc_ref[...] = jnp.zeros_like(acc_ref)

    hn = hn_ref[...]
    gate = _dot(hn, wg_ref[...])
    up = _dot(hn, wu_ref[...])
    act = gate * jax.nn.sigmoid(gate) * up
    acc_ref[...] += _dot(act.astype(BF16), wd_ref[...])

    @pl.when(j == pl.num_programs(1) - 1)
    def _():
        o_ref[...] = _rms(h1_ref[...] + acc_ref[...], gf_ref[...])


def _ffn(x2, yr, yn_t, wo1, wo2, g2, wg, wu, wd, gf, tm=512, nf=2):
    m, d = x2.shape
    _, dn, t = yn_t.shape
    per_seq = t // tm
    dff = wg.shape[1]
    tf = dff // nf
    row = lambda n: pl.BlockSpec((tm, n), lambda i, j: (i, 0))
    const = lambda a: pl.BlockSpec(a.shape, lambda i, j: (0, 0))
    return pl.pallas_call(
        _ffn_kernel,
        grid=(m // tm, nf),
        in_specs=[row(d), row(yr.shape[1]),
                  pl.BlockSpec((1, dn, tm), lambda i, j: (i // per_seq, 0, i % per_seq)),
                  const(wo1), const(wo2), const(g2),
                  pl.BlockSpec((d, tf), lambda i, j: (0, j)), pl.BlockSpec((d, tf), lambda i, j: (0, j)),
                  pl.BlockSpec((tf, d), lambda i, j: (j, 0)), const(gf)],
        out_specs=row(d),
        out_shape=jax.ShapeDtypeStruct((m, d), F32),
        scratch_shapes=[pltpu.VMEM((tm, d), F32), pltpu.VMEM((tm, d), BF16), pltpu.VMEM((tm, d), F32)],
        compiler_params=_params("arbitrary", "arbitrary"),
        name="outproj_ffn",
    )(x2, yr, yn_t, wo1, wo2, g2, wg, wu, wd, gf)


def _overlap_matrix_t(t):
    n16 = t // CMP_STRIDE
    n_cmp = (t - CMP_BLOCK) // CMP_STRIDE + 1
    n_sel = t // SEL_BLOCK
    cmp_start = np.arange(n_cmp) * CMP_STRIDE
    sel_start = np.arange(n_sel) * SEL_BLOCK
    ov = np.clip(np.minimum(cmp_start[:, None] + CMP_BLOCK, sel_start[None, :] + SEL_BLOCK)
                 - np.maximum(cmp_start[:, None], sel_start[None, :]), 0, None) / CMP_STRIDE
    full = np.zeros((LANES, n16), np.float32)
    full[:n_sel, :n_cmp] = ov.T
    return jnp.asarray(full, BF16)


def _expand_matrix_t(t):
    blk = np.arange(t) // SEL_BLOCK
    return jnp.asarray(blk[:, None] == np.arange(LANES)[None, :], BF16)


def kernel(x, norm1_g, w_in, mu_shift, rwkv_w0, rwkv_w2, rwkv_a0, rwkv_a2, rwkv_g2, rwkv_k_k, rwkv_k_a,
           rwkv_r_k, rwkv_lnx_w, rwkv_lnx_b, nsa_pe_k, nsa_pe_v, nsa_cmp_k_w1, nsa_cmp_k_w2, nsa_cmp_v_w1,
           nsa_cmp_v_w2, w_out, norm2_g, ffn_w_gate, ffn_w_up, ffn_w_down, norm_f_g):
    batch, t, d_model = x.shape
    assert w_in.shape[0] == 1, "the final RMSNorm is fused into the (single) layer's FFN kernel"
    assert t % RWKV_PREP_ROWS == 0 and t % KEY_TILE == 0 and t // SEL_BLOCK <= LANES
    hk, grp, dk = NSA_KV_HEADS, NSA_GROUP, HEAD_DIM
    nq = t // Q_BLOCK
    slopes = 2.0 ** (-8.0 * jnp.arange(1, NSA_Q_HEADS + 1, dtype=F32) / NSA_Q_HEADS)
    nsa_pad = _round_up(N_NSA_COLS, LANES)
    row = lambda a: a.reshape(1, -1)
    i = 0

    h = x.reshape(batch * t, d_model)
    w_r = w_in[i][:, :N_RWKV_COLS].astype(BF16)
    w_n = jnp.pad(w_in[i][:, N_RWKV_COLS:], ((0, 0), (0, nsa_pad - N_NSA_COLS))).astype(BF16)
    p_r, q_t, ks, kw, vs_t, vw_t, zk, zv, gl_t = _inproj(h, batch, row(norm1_g[i]), w_r, w_n)

    rt, kt, bt, at, v, wc, bonus, g = _rwkv_prep(
        p_r, batch, row(mu_shift[i]), row(rwkv_w0[i]), rwkv_w2[i].astype(BF16), row(rwkv_a0[i]),
        rwkv_a2[i].astype(BF16), rwkv_g2[i].astype(BF16), row(rwkv_k_k[i]), row(rwkv_k_a[i]),
        row(rwkv_r_k[i]))
    y_rwkv = _rwkv_scan(rt, kt, bt, at, v, wc, bonus, g, row(rwkv_lnx_w[i]), row(rwkv_lnx_b[i]))
    y_rwkv = y_rwkv.reshape(batch * t, D_RWKV)

    kc_aug, vc_t = _compress(zk, zv, nsa_pe_k[i], nsa_pe_v[i],
                             nsa_cmp_k_w1[i].astype(BF16), nsa_cmp_k_w2[i].astype(BF16),
                             nsa_cmp_v_w1[i].astype(BF16), nsa_cmp_v_w2[i].astype(BF16))
    oc_t, sel_t = _cmp_select(slopes, q_t, kc_aug, vc_t, _overlap_matrix_t(t))

    blocks_per_tile = KEY_TILE // SEL_BLOCK
    active = sel_t.reshape(batch, hk, nq, MAX_KEY_TILES, blocks_per_tile * Q_BLOCK).max(axis=-1) > 0
    tile_ids = jnp.argsort(jnp.logical_not(active), axis=-1, stable=True).astype(jnp.int32).reshape(-1)
    tile_cnt = active.sum(axis=-1).astype(jnp.int32).reshape(-1)
    y_nsa_t = _sel_win(tile_ids, tile_cnt, slopes, q_t, sel_t, _expand_matrix_t(t), ks, vs_t, kw, vw_t,
                       oc_t, gl_t)

    out = _ffn(h, y_rwkv, y_nsa_t, w_out[i][:D_RWKV].astype(BF16), w_out[i][D_RWKV:].astype(BF16),
               row(norm2_g[i]), ffn_w_gate[i].astype(BF16), ffn_w_up[i].astype(BF16),
               ffn_w_down[i].astype(BF16), row(norm_f_g))
    return out.reshape(batch, t, d_model)
```

```python
import numpy as np
import jax
import jax.numpy as jnp
from jax import lax
from jax.experimental import pallas as pl
from jax.experimental.pallas import tpu as pltpu

F32 = jnp.float32
BF16 = jnp.bfloat16

HEAD_DIM = 64
RWKV_HEADS = 8
D_RWKV = RWKV_HEADS * HEAD_DIM
NSA_Q_HEADS = 8
NSA_KV_HEADS = 2
NSA_GROUP = NSA_Q_HEADS // NSA_KV_HEADS
D_NSA = NSA_Q_HEADS * HEAD_DIM
D_KV = NSA_KV_HEADS * HEAD_DIM
LORA_W, LORA_A, LORA_G = 64, 64, 128
N_RWKV_COLS = 3 * D_RWKV + LORA_W + LORA_A + LORA_G
N_NSA_COLS = D_NSA + 6 * D_KV + 3 * NSA_Q_HEADS
CMP_BLOCK, CMP_STRIDE = 32, 16
SEL_BLOCK, SEL_TOPK = 64, 16
WINDOW = 512
Q_BLOCK = 128
NORM_EPS = 1e-6
GN_EPS = 64e-5
NEG = -1e30
BIG = 1e30

LANES = 128
RWKV_CHUNK = 64
RWKV_PREP_ROWS = 8 * RWKV_CHUNK
KEY_TILE = 256
MAX_KEY_TILES = LANES * SEL_BLOCK // KEY_TILE
WIN_TILES = (WINDOW + Q_BLOCK) // KEY_TILE + 1
SEL_GROUP = 3
SEL_FIRST = 2 * SEL_GROUP
V_ROWS = HEAD_DIM + 16
SOFTMAX_FLOOR = -1e20
QG = NSA_GROUP * Q_BLOCK
VMEM_LIMIT = 56 * 1024 * 1024


def _round_up(n, m):
    return -(-n // m) * m


def _dot(a, b):
    return jnp.dot(a, b, preferred_element_type=F32)


def _dot_tn(a, b):
    return lax.dot_general(a, b, (((0,), (0,)), ((), ())), preferred_element_type=F32)


def _split3(a):
    hi = a.astype(BF16)
    rest = a - hi.astype(F32)
    mid = rest.astype(BF16)
    return hi, mid, (rest - mid.astype(F32)).astype(BF16)


def _dot_exact_lhs(a, b):
    hi, mid, lo = _split3(b)
    return _dot(a, hi) + _dot(a, mid) + _dot(a, lo)


def _dot_exact_rhs(a, b):
    hi, mid, lo = _split3(a)
    return _dot(hi, b) + _dot(mid, b) + _dot(lo, b)


def _bdot(spec, a, b):
    return jnp.einsum(spec, a, b, preferred_element_type=F32)


def _rms(x, g):
    return x * lax.rsqrt(jnp.mean(x * x, axis=-1, keepdims=True) + NORM_EPS) * g


def _params(*sem):
    return pltpu.CompilerParams(dimension_semantics=sem, vmem_limit_bytes=VMEM_LIMIT)


def _inproj_kernel(x_ref, g_ref, wr_ref, wn_ref, pr_ref, qt_ref, ks_ref, kw_ref, vst_ref, vwt_ref,
                   kc_ref, vc_ref, glt_ref):
    xb = _rms(x_ref[...], g_ref[...]).astype(BF16)
    pr_ref[...] = _dot(xb, wr_ref[...])
    pn = _dot(xb, wn_ref[...])
    rows = pn.shape[0]
    dk, hk_n = HEAD_DIM, NSA_KV_HEADS
    group = lambda j: pn[:, D_NSA + j * D_KV:D_NSA + (j + 1) * D_KV]

    for half in range(rows // Q_BLOCK):
        q_tr = jnp.transpose(pn[half * Q_BLOCK:(half + 1) * Q_BLOCK, :D_NSA])
        for hk in range(hk_n):
            base = hk * NSA_GROUP * dk
            qt_ref[0, hk, half] = jnp.concatenate(
                [q_tr[base + g * dk:base + (g + 1) * dk, :] for g in range(NSA_GROUP)], axis=1)

    pos = pl.program_id(1) * rows + lax.broadcasted_iota(jnp.int32, (rows, dk), 0)
    col = lax.broadcasted_iota(jnp.int32, (rows, dk), 1)
    pos_cols = jnp.where(col == 0, pos // SEL_BLOCK, jnp.where(col == 1, pos % SEL_BLOCK, 0)).astype(F32)
    ones_rows = (lax.broadcasted_iota(jnp.int32, (V_ROWS - dk, rows), 0) == 0).astype(F32)
    kc, vc, ks, vs, kw, vw = (group(j) for j in range(6))
    vs_tr, vw_tr = jnp.transpose(vs), jnp.transpose(vw)
    gl_tr = jnp.transpose(pn[:, D_NSA + 6 * D_KV:])
    n_gate = 3 * NSA_GROUP
    for hk in range(hk_n):
        sl = slice(hk * dk, (hk + 1) * dk)
        kc_ref[0, hk] = kc[:, sl]
        vc_ref[0, hk] = vc[:, sl]
        ks_ref[0, hk] = jnp.concatenate([ks[:, sl], pos_cols], axis=1).astype(BF16)
        kw_ref[0, hk] = jnp.concatenate([kw[:, sl], pos_cols], axis=1).astype(BF16)
        vst_ref[0, hk, 0] = jnp.concatenate([vs_tr[sl, :], ones_rows], axis=0).astype(BF16)
        vwt_ref[0, hk, 0] = jnp.concatenate([vw_tr[sl, :], ones_rows], axis=0).astype(BF16)
        glt_ref[0, hk] = gl_tr[hk * n_gate:(hk + 1) * n_gate, :]


def _inproj(x2, batch, g, w_r, w_n):
    m, d = x2.shape
    t = m // batch
    tm = KEY_TILE
    nt = t // tm
    nr, nn = w_r.shape[1], w_n.shape[1]
    hk, dk = NSA_KV_HEADS, HEAD_DIM
    const = lambda a: pl.BlockSpec(a.shape, lambda b, i: (0, 0))
    keys = pl.BlockSpec((1, hk, tm, 2 * dk), lambda b, i: (b, 0, i, 0))
    vals = pl.BlockSpec((1, hk, 1, V_ROWS, tm), lambda b, i: (b, 0, i, 0, 0))
    cmp_in = pl.BlockSpec((1, hk, tm, dk), lambda b, i: (b, 0, i, 0))
    keys_shape = jax.ShapeDtypeStruct((batch, hk, t, 2 * dk), BF16)
    vals_shape = jax.ShapeDtypeStruct((batch, hk, nt, V_ROWS, tm), BF16)
    cmp_shape = jax.ShapeDtypeStruct((batch, hk, t, dk), F32)
    return pl.pallas_call(
        _inproj_kernel,
        grid=(batch, nt),
        in_specs=[pl.BlockSpec((tm, d), lambda b, i: (b * nt + i, 0)), const(g), const(w_r), const(w_n)],
        out_specs=[pl.BlockSpec((tm, nr), lambda b, i: (b * nt + i, 0)),
                   pl.BlockSpec((1, hk, tm // Q_BLOCK, dk, QG), lambda b, i: (b, 0, i, 0, 0)),
                   keys, keys, vals, vals, cmp_in, cmp_in,
                   pl.BlockSpec((1, hk, 3 * NSA_GROUP, tm), lambda b, i: (b, 0, 0, i))],
        out_shape=[jax.ShapeDtypeStruct((m, nr), F32),
                   jax.ShapeDtypeStruct((batch, hk, t // Q_BLOCK, dk, QG), F32),
                   keys_shape, keys_shape, vals_shape, vals_shape, cmp_shape, cmp_shape,
                   jax.ShapeDtypeStruct((batch, hk, 3 * NSA_GROUP, t), F32)],
        compiler_params=_params("arbitrary", "arbitrary"),
        name="inproj",
    )(x2, g, w_r, w_n)


def _rwkv_prep_kernel(p_ref, mu_ref, w0_ref, w2_ref, a0_ref, a2_ref, g2_ref, kk_ref, ka_ref, rk_ref,
                      bd_ref, tri_ref,
                      rt_ref, kt_ref, bt_ref, at_ref, v_ref, wc_ref, bonus_ref, g_ref, carry_ref):
    rows = p_ref.shape[0]
    c = RWKV_CHUNK

    @pl.when(pl.program_id(1) == 0)
    def _():
        carry_ref[...] = jnp.zeros_like(carry_ref)

    p = p_ref[...]
    row = lax.broadcasted_iota(jnp.int32, p.shape, 0)
    p_prev = jnp.where(row == 0, carry_ref[7:8, :], pltpu.roll(p, 1, axis=0))
    carry_ref[...] = p[rows - 8:, :]
    ps = p + mu_ref[...] * (p_prev - p)

    d = D_RWKV
    r, k, v = ps[:, 0:d], ps[:, d:2 * d], ps[:, 2 * d:3 * d]
    dw = ps[:, 3 * d:3 * d + LORA_W]
    da = ps[:, 3 * d + LORA_W:3 * d + LORA_W + LORA_A]
    dg = ps[:, 3 * d + LORA_W + LORA_A:]

    z = -(w0_ref[...] + _dot(jnp.tanh(dw).astype(BF16), w2_ref[...]))
    softplus = jnp.maximum(z, 0.0) + jnp.log1p(jnp.exp(-jnp.abs(z)))
    logw = -jnp.exp(-softplus - 0.5)
    a = jax.nn.sigmoid(a0_ref[...] + _dot(da.astype(BF16), a2_ref[...]))
    g_ref[0] = _dot(jax.nn.sigmoid(dg).astype(BF16), g2_ref[...])

    bd = bd_ref[...]
    kk = k * kk_ref[...]
    kk = kk / jnp.maximum(jnp.sqrt(_dot_exact_rhs(kk * kk, bd)), 1e-12)
    kp = k * (1.0 + (a - 1.0) * ka_ref[...])
    bonus_ref[0] = _dot_exact_rhs(r * kp * rk_ref[...], bd) * v

    tri = tri_ref[...]
    cums = [_dot_exact_lhs(tri, logw[ci * c:(ci + 1) * c]) for ci in range(rows // c)]
    cum = jnp.concatenate(cums, axis=0)
    e_pos, e_neg = jnp.exp(cum), jnp.exp(-cum)
    outs = ((rt_ref, r * e_pos), (kt_ref, kp * e_neg), (bt_ref, kk * a * e_neg),
            (at_ref, -kk * jnp.exp(cum - logw)), (v_ref, v))
    for h in range(RWKV_HEADS):
        sl = slice(h * HEAD_DIM, (h + 1) * HEAD_DIM)
        for ref, val in outs:
            ref[0, h] = val[:, sl]
        for ci in range(rows // c):
            wc_ref[0, ci, h:h + 1, :] = e_pos[(ci + 1) * c - 1:(ci + 1) * c, sl]


def _rwkv_prep(p_r, batch, mu, w0, w2, a0, a2, g2, k_k, k_a, r_k):
    m = p_r.shape[0]
    t = m // batch
    rows = RWKV_PREP_ROWS
    nt = t // rows
    d = D_RWKV
    head = np.arange(d) // HEAD_DIM
    bd = jnp.asarray(head[:, None] == head[None, :], BF16)
    tri = jnp.asarray(np.tril(np.ones((RWKV_CHUNK, RWKV_CHUNK))), BF16)
    vec = lambda n: pl.BlockSpec((1, n), lambda b, i: (0, 0))
    full = lambda a: pl.BlockSpec(a.shape, lambda b, i: (0, 0))
    wide = pl.BlockSpec((1, rows, d), lambda b, i: (b, i, 0))
    heads = pl.BlockSpec((1, RWKV_HEADS, rows, HEAD_DIM), lambda b, i: (b, 0, i, 0))
    wide_shape = jax.ShapeDtypeStruct((batch, t, d), F32)
    heads_shape = jax.ShapeDtypeStruct((batch, RWKV_HEADS, t, HEAD_DIM), F32)
    outs = pl.pallas_call(
        _rwkv_prep_kernel,
        grid=(batch, nt),
        in_specs=[pl.BlockSpec((rows, N_RWKV_COLS), lambda b, i: (b * nt + i, 0)),
                  vec(N_RWKV_COLS), vec(d), full(w2), vec(d), full(a2), full(g2), vec(d), vec(d), vec(d),
                  full(bd), full(tri)],
        out_specs=[heads, heads, heads, heads, heads,
                   pl.BlockSpec((1, rows // RWKV_CHUNK, RWKV_HEADS, HEAD_DIM), lambda b, i: (b, i, 0, 0)),
                   wide, wide],
        out_shape=[heads_shape, heads_shape, heads_shape, heads_shape, heads_shape,
                   jax.ShapeDtypeStruct((batch, t // RWKV_CHUNK, RWKV_HEADS, HEAD_DIM), F32),
                   wide_shape, wide_shape],
        scratch_shapes=[pltpu.VMEM((8, N_RWKV_COLS), F32)],
        compiler_params=_params("arbitrary", "arbitrary"),
        name="rwkv_prep",
    )(p_r, mu, w0, w2, a0, a2, g2, k_k, k_a, r_k, bd, tri)
    return outs


def _rwkv_scan_kernel(rt_ref, kt_ref, bt_ref, at_ref, v_ref, wc_ref, bonus_ref, g_ref, lnw_ref, lnb_ref,
                      y_ref, s_ref):
    nb, nh, c, dk = rt_ref.shape
    n = nb * nh

    @pl.when(pl.program_id(0) == 0)
    def _():
        s_ref[...] = jnp.zeros_like(s_ref)

    row = lax.broadcasted_iota(jnp.int32, (1, c, c), 1)
    col = lax.broadcasted_iota(jnp.int32, (1, c, c), 2)
    strict = col < row
    incl = col <= row
    eye = (row == col).astype(F32)

    load = lambda ref: ref[...].reshape(n, c, dk)
    vb = load(v_ref).astype(BF16)
    ar = jnp.concatenate([load(at_ref), load(rt_ref)], axis=1).astype(BF16)
    bk = jnp.concatenate([load(bt_ref), load(kt_ref)], axis=1).astype(BF16)
    amat = _bdot("nik,njk->nij", ar, bk)
    n_ab = jnp.where(strict, amat[:, :c, :c], 0.0)
    a_ak = jnp.where(strict, amat[:, :c, c:], 0.0)
    a_rb = jnp.where(incl, amat[:, c:, :c], 0.0)
    a_rk = jnp.where(incl, amat[:, c:, c:], 0.0)
    s0 = s_ref[...]
    ar_s = _bdot("nik,nvk->niv", ar, s0.astype(BF16))
    rhs = ar_s[:, :c] + _bdot("nij,njv->niv", a_ak.astype(BF16), vb)
    inv = eye + n_ab
    pw = n_ab
    for _ in range(int(np.log2(c)) - 1):
        pwb = pw.astype(BF16)
        pw = _bdot("nij,njk->nik", pwb, pwb)
        inv = inv + _bdot("nij,njk->nik", pw.astype(BF16), inv.astype(BF16))
    ub = _bdot("nij,njv->niv", inv.astype(BF16), rhs.astype(BF16)).astype(BF16)
    y = ar_s[:, c:] + _bdot("nij,njv->niv", a_rb.astype(BF16), ub) + _bdot("nij,njv->niv", a_rk.astype(BF16), vb)
    uv_t = jnp.swapaxes(jnp.concatenate([ub, vb], axis=1), 1, 2)
    wc = wc_ref[...].reshape(n, 1, dk)
    s_ref[...] = (s0 + _bdot("nvi,nik->nvk", uv_t, bk)) * wc

    mean = jnp.mean(y, axis=-1, keepdims=True)
    var = jnp.mean(jnp.square(y - mean), axis=-1, keepdims=True)
    yn = (y - mean) * lax.rsqrt(var + GN_EPS)
    for b in range(nb):
        wide = jnp.concatenate([yn[b * nh + h] for h in range(nh)], axis=-1)
        y_ref[b] = (wide * lnw_ref[...] + lnb_ref[...] + bonus_ref[b]) * g_ref[b]


def _rwkv_scan(rt, kt, bt, at, v, wc, bonus, g, lnw, lnb):
    batch, nh, t, dk = rt.shape
    c = RWKV_CHUNK
    d = nh * dk
    heads = pl.BlockSpec((batch, nh, c, dk), lambda i: (0, 0, i, 0))
    wide = pl.BlockSpec((batch, c, d), lambda i: (0, i, 0))
    vec = pl.BlockSpec((1, d), lambda i: (0, 0))
    return pl.pallas_call(
        _rwkv_scan_kernel,
        grid=(t // c,),
        in_specs=[heads, heads, heads, heads, heads,
                  pl.BlockSpec((batch, 1, nh, dk), lambda i: (0, i, 0, 0)),
                  wide, wide, vec, vec],
        out_specs=wide,
        out_shape=jax.ShapeDtypeStruct((batch, t, d), F32),
        scratch_shapes=[pltpu.VMEM((batch * nh, dk, dk), F32)],
        compiler_params=_params("arbitrary"),
        name="rwkv_scan",
    )(rt, kt, bt, at, v, wc, bonus, g, lnw, lnb)


def _compress_kernel(zk_ref, zv_ref, pek_ref, pev_ref, k1_ref, k2_ref, v1_ref, v2_ref, ko_ref, vo_ref):
    def one(z_ref, pe_ref, w1_ref, w2_ref):
        n16 = z_ref.shape[2] // CMP_STRIDE
        dk = HEAD_DIM
        first = second = None
        for l in range(CMP_STRIDE):
            z = z_ref[0, 0, pl.ds(l, n16, stride=CMP_STRIDE), :]
            lo = _dot((z + pe_ref[l:l + 1, :]).astype(BF16), w1_ref[l * dk:(l + 1) * dk, :])
            u = CMP_STRIDE + l
            hi = _dot((z + pe_ref[u:u + 1, :]).astype(BF16), w1_ref[u * dk:(u + 1) * dk, :])
            first = lo if first is None else first + lo
            second = hi if second is None else second + hi
        hidden = first + pltpu.roll(second, n16 - 1, axis=0)
        return _dot(jax.nn.gelu(hidden).astype(BF16), w2_ref[...])

    kc = one(zk_ref, pek_ref, k1_ref, k2_ref)
    blk = lax.broadcasted_iota(jnp.int32, kc.shape, 0)
    col = lax.broadcasted_iota(jnp.int32, kc.shape, 1)
    per = SEL_BLOCK // CMP_STRIDE
    hi = blk // per
    lo = CMP_STRIDE * (blk % per) + (CMP_BLOCK - 1)
    feat = jnp.where(col == 0, hi, jnp.where(col == 1, lo, 0)).astype(F32)
    ko_ref[0] = jnp.concatenate([kc, feat], axis=1).astype(BF16)
    vc = one(zv_ref, pev_ref, v1_ref, v2_ref)
    vc_t = jnp.transpose(jnp.concatenate([vc, jnp.zeros_like(vc)], axis=1))
    vo_ref[0] = vc_t[:HEAD_DIM].astype(BF16)


def _compress(zk, zv, pek, pev, k1, k2, v1, v2):
    batch, hk, t, dk = zk.shape
    n16 = t // CMP_STRIDE
    zspec = pl.BlockSpec((1, 1, t, dk), lambda b, h: (b, h, 0, 0))
    full = lambda a: pl.BlockSpec(a.shape, lambda b, h: (0,) * a.ndim)
    return pl.pallas_call(
        _compress_kernel,
        grid=(batch, hk),
        in_specs=[zspec, zspec, full(pek), full(pev), full(k1), full(k2), full(v1), full(v2)],
        out_specs=[pl.BlockSpec((1, n16, 2 * dk), lambda b, h: (b * hk + h, 0, 0)),
                   pl.BlockSpec((1, dk, n16), lambda b, h: (b * hk + h, 0, 0))],
        out_shape=[jax.ShapeDtypeStruct((batch * hk, n16, 2 * dk), BF16),
                   jax.ShapeDtypeStruct((batch * hk, dk, n16), BF16)],
        compiler_params=_params("arbitrary", "arbitrary"),
        name="nsa_compress",
    )(zk, zv, pek, pev, k1, k2, v1, v2)


def _query_features(qt, slopes_ref, hk):
    lane_g = lax.broadcasted_iota(jnp.int32, (1, QG), 1) // Q_BLOCK
    slope = jnp.zeros((1, QG), F32)
    for g in range(NSA_GROUP):
        slope = jnp.where(lane_g == g, slopes_ref[hk * NSA_GROUP + g], slope)
    row = lax.broadcasted_iota(jnp.int32, (HEAD_DIM, QG), 0)
    extra = jnp.where(row == 0, SEL_BLOCK * slope, jnp.where(row == 1, slope, 0.0))
    return jnp.concatenate([qt * (HEAD_DIM ** -0.5), extra], axis=0).astype(BF16)


def _cmp_select_kernel(slopes_ref, qt_ref, kc_ref, vct_ref, ovt_ref, oct_ref, selt_ref):
    hk = pl.program_id(1)
    q0 = pl.program_id(2) * Q_BLOCK
    q_aug = _query_features(qt_ref[0, 0, 0], slopes_ref, hk)
    s = _dot(kc_ref[0], q_aug)
    ncp = s.shape[0]
    cmp_end = lax.broadcasted_iota(jnp.int32, (ncp, Q_BLOCK), 0) * CMP_STRIDE + (CMP_BLOCK - 1)
    t_q = q0 + lax.broadcasted_iota(jnp.int32, (ncp, Q_BLOCK), 1)
    ok = cmp_end <= t_q
    any_ok = (q0 + lax.broadcasted_iota(jnp.int32, (1, Q_BLOCK), 1) >= CMP_BLOCK - 1).astype(F32)

    p_sum = jnp.zeros((ncp, Q_BLOCK), F32)
    probs = []
    for g in range(NSA_GROUP):
        sg = jnp.where(ok, s[:, g * Q_BLOCK:(g + 1) * Q_BLOCK], NEG)
        e = jnp.exp(sg - jnp.max(sg, axis=0, keepdims=True))
        p = e * (any_ok / jnp.sum(e, axis=0, keepdims=True))
        p_sum = p_sum + p
        probs.append(p.astype(BF16))
    oct_ref[0, 0, 0] = _dot(vct_ref[0], jnp.concatenate(probs, axis=1))

    imp = _dot_exact_lhs(ovt_ref[...], p_sum)

    blk = lax.broadcasted_iota(jnp.int32, (LANES, Q_BLOCK), 0)
    cur = (q0 + lax.broadcasted_iota(jnp.int32, (LANES, Q_BLOCK), 1)) // SEL_BLOCK
    valid = blk <= cur
    forced = (blk == 0) | (blk == cur) | (blk == cur - 1)
    x = jnp.where(valid, jnp.where(forced, BIG, imp), NEG)
    blk_f = blk.astype(F32)
    chosen = jnp.zeros((LANES, Q_BLOCK), jnp.bool_)
    for _ in range(SEL_TOPK):
        m = jnp.max(x, axis=0, keepdims=True)
        first = jnp.min(jnp.where(x == m, blk_f, float(LANES)), axis=0, keepdims=True)
        hit = blk_f == first
        chosen = chosen | hit
        x = jnp.where(hit, -jnp.inf, x)
    selt_ref[0, 0, 0] = (chosen & valid).astype(BF16)


def _cmp_select(slopes, q_t, kc_aug, vc_t, ov_t):
    batch, hk, nq, dk, qg = q_t.shape
    ncp = kc_aug.shape[1]
    return pl.pallas_call(
        _cmp_select_kernel,
        grid=(batch, hk, nq),
        in_specs=[pl.BlockSpec(memory_space=pltpu.SMEM),
                  pl.BlockSpec((1, 1, 1, dk, qg), lambda b, h, i: (b, h, i, 0, 0)),
                  pl.BlockSpec((1, ncp, 2 * dk), lambda b, h, i: (b * hk + h, 0, 0)),
                  pl.BlockSpec((1, dk, ncp), lambda b, h, i: (b * hk + h, 0, 0)),
                  pl.BlockSpec(ov_t.shape, lambda b, h, i: (0, 0))],
        out_specs=[pl.BlockSpec((1, 1, 1, dk, qg), lambda b, h, i: (b, h, i, 0, 0)),
                   pl.BlockSpec((1, 1, 1, LANES, Q_BLOCK), lambda b, h, i: (b, h, i, 0, 0))],
        out_shape=[jax.ShapeDtypeStruct(q_t.shape, F32),
                   jax.ShapeDtypeStruct((batch, hk, nq, LANES, Q_BLOCK), BF16)],
        compiler_params=_params("arbitrary", "arbitrary", "arbitrary"),
        name="nsa_cmp_select",
    )(slopes, q_t, kc_aug, vc_t, ov_t)


def _sel_win_kernel(ids_ref, cnt_ref, slopes_ref, qt_ref, selt_ref, et_ref, ks_ref, vst_ref, kw_ref, vwt_ref,
                    oct_ref, glt_ref, yt_ref):
    b, hk, i = pl.program_id(0), pl.program_id(1), pl.program_id(2)
    step = (b * pl.num_programs(1) + hk) * pl.num_programs(2) + i
    q0 = i * Q_BLOCK
    kt = KEY_TILE
    q_aug = _query_features(qt_ref[0, 0, 0], slopes_ref, hk)
    selt = selt_ref[0, 0, 0]
    lane_minus_row = (lax.broadcasted_iota(jnp.int32, (kt, Q_BLOCK), 1)
                      - lax.broadcasted_iota(jnp.int32, (kt, Q_BLOCK), 0))

    def attend(state, tiles):
        m_old, acc_old = state
        scores = [_dot(k_tile, q_aug) for k_tile, _, _ in tiles]
        probs, maxes = [[] for _ in tiles], []
        for g in range(NSA_GROUP):
            gsl = slice(g * Q_BLOCK, (g + 1) * Q_BLOCK)
            masked = [jnp.where(mask, s[:, gsl], NEG) for s, (_, _, mask) in zip(scores, tiles)]
            mg = m_old[:, gsl]
            for sg in masked:
                mg = jnp.maximum(mg, jnp.max(sg, axis=0, keepdims=True))
            for n, sg in enumerate(masked):
                probs[n].append(jnp.exp(sg - mg).astype(BF16))
            maxes.append(mg)
        m_new = jnp.concatenate(maxes, axis=1)
        update = _dot(tiles[0][1], jnp.concatenate(probs[0], axis=1))
        for n in range(1, len(tiles)):
            update = update + _dot(tiles[n][1], jnp.concatenate(probs[n], axis=1))
        return m_new, jnp.exp(m_old - m_new) * acc_old + update

    start = (jnp.full((1, QG), SOFTMAX_FLOOR, F32), jnp.zeros((V_ROWS, QG), F32))
    result = lambda state: state[1][:HEAD_DIM] / state[1][HEAD_DIM:HEAD_DIM + 1]

    count = cnt_ref[step]
    last_tile = ks_ref.shape[2] // kt - 1

    def sel_tile(n):
        j = jnp.minimum(ids_ref[step * MAX_KEY_TILES + jnp.minimum(n, MAX_KEY_TILES - 1)], last_tile)
        k0 = pl.multiple_of(j * kt, kt)
        picked = _dot(et_ref[pl.ds(k0, kt), :], selt)
        causal_from = jnp.where(n < count, k0 - q0, 1 << 30)
        mask = (picked > 0.5) & (lane_minus_row >= causal_from)
        return ks_ref[0, 0, pl.ds(k0, kt), :], vst_ref[0, 0, j], mask

    last = (q0 + Q_BLOCK - 1) // kt

    def win_tile(u):
        j = last - u
        jc = jnp.maximum(j, 0)
        k0 = pl.multiple_of(jc * kt, kt)
        dist = lane_minus_row + jnp.where(j >= 0, q0 - k0, -(1 << 30))
        mask = (dist >= 0) & (dist < WINDOW)
        return kw_ref[0, 0, pl.ds(k0, kt), :], vwt_ref[0, 0, jc], mask

    state_w = attend(start, [win_tile(u) for u in range(WIN_TILES)])
    state_s = attend(start, [sel_tile(n) for n in range(SEL_FIRST)])
    groups_done = SEL_FIRST // SEL_GROUP
    state_s = lax.fori_loop(
        groups_done, (count + SEL_GROUP - 1) // SEL_GROUP,
        lambda n, st: attend(st, [sel_tile(SEL_GROUP * n + u) for u in range(SEL_GROUP)]), state_s)
    o_s, o_w = result(state_s), result(state_w)

    gates = jax.nn.sigmoid(glt_ref[0, 0])
    o_c = oct_ref[0, 0, 0]
    outs = []
    for g in range(NSA_GROUP):
        gsl = slice(g * Q_BLOCK, (g + 1) * Q_BLOCK)
        outs.append(gates[3 * g:3 * g + 1] * o_c[:, gsl] + gates[3 * g + 1:3 * g + 2] * o_s[:, gsl]
                    + gates[3 * g + 2:3 * g + 3] * o_w[:, gsl])
    yt_ref[0] = jnp.concatenate(outs, axis=0)


def _sel_win(tile_ids, tile_cnt, slopes, q_t, sel_t, expand_t, ks, vs_t, kw, vw_t, oc_t, gl_t):
    batch, hk, nq, dk, qg = q_t.shape
    t = nq * Q_BLOCK
    nkt = t // KEY_TILE
    grp = qg // Q_BLOCK
    qspec = pl.BlockSpec((1, 1, 1, dk, qg), lambda b, h, i, *_: (b, h, i, 0, 0))
    kspec = pl.BlockSpec((1, 1, t, 2 * dk), lambda b, h, i, *_: (b, h, 0, 0))
    vspec = pl.BlockSpec((1, 1, nkt, V_ROWS, KEY_TILE), lambda b, h, i, *_: (b, h, 0, 0, 0))
    grid_spec = pltpu.PrefetchScalarGridSpec(
        num_scalar_prefetch=2,
        grid=(batch, hk, nq),
        in_specs=[pl.BlockSpec(memory_space=pltpu.SMEM), qspec,
                  pl.BlockSpec((1, 1, 1, LANES, Q_BLOCK), lambda b, h, i, *_: (b, h, i, 0, 0)),
                  pl.BlockSpec(expand_t.shape, lambda b, h, i, *_: (0, 0)),
                  kspec, vspec, kspec, vspec, qspec,
                  pl.BlockSpec((1, 1, 3 * grp, Q_BLOCK), lambda b, h, i, *_: (b, h, 0, i))],
        out_specs=pl.BlockSpec((1, grp * dk, Q_BLOCK), lambda b, h, i, *_: (b, h, i)),
    )
    return pl.pallas_call(
        _sel_win_kernel,
        grid_spec=grid_spec,
        out_shape=jax.ShapeDtypeStruct((batch, hk * grp * dk, t), F32),
        compiler_params=_params("arbitrary", "arbitrary", "arbitrary"),
        name="nsa_sel_win",
    )(tile_ids, tile_cnt, slopes, q_t, sel_t, expand_t, ks, vs_t, kw, vw_t, oc_t, gl_t)


def _ffn_kernel(x_ref, yr_ref, ynt_ref, wo1_ref, wo2_ref, g2_ref, wg_ref, wu_ref, wd_ref, gf_ref, o_ref):
    h1 = (x_ref[...] + _dot(yr_ref[...].astype(BF16), wo1_ref[...])
          + _dot_tn(ynt_ref[0].astype(BF16), wo2_ref[...]))
    hn = _rms(h1, g2_ref[...]).astype(BF16)
    gate = _dot(hn, wg_ref[...])
    up = _dot(hn, wu_ref[...])
    act = gate * jax.nn.sigmoid(gate) * up
    o_ref[...] = _rms(h1 + _dot(act.astype(BF16), wd_ref[...]), gf_ref[...])


def _ffn(x2, yr, yn_t, wo1, wo2, g2, wg, wu, wd, gf, tm=512):
    m, d = x2.shape
    _, dn, t = yn_t.shape
    per_seq = t // tm
    row = lambda n: pl.BlockSpec((tm, n), lambda i: (i, 0))
    const = lambda a: pl.BlockSpec(a.shape, lambda i: (0, 0), pipeline_mode=pl.Buffered(1))
    return pl.pallas_call(
        _ffn_kernel,
        grid=(m // tm,),
        in_specs=[row(d), row(yr.shape[1]),
                  pl.BlockSpec((1, dn, tm), lambda i: (i // per_seq, 0, i % per_seq)),
                  const(wo1), const(wo2), const(g2), const(wg), const(wu), const(wd), const(gf)],
        out_specs=row(d),
        out_shape=jax.ShapeDtypeStruct((m, d), F32),
        compiler_params=_params("arbitrary"),
        name="outproj_ffn",
    )(x2, yr, yn_t, wo1, wo2, g2, wg, wu, wd, gf)


def _overlap_matrix_t(t):
    n16 = t // CMP_STRIDE
    n_cmp = (t - CMP_BLOCK) // CMP_STRIDE + 1
    n_sel = t // SEL_BLOCK
    cmp_start = np.arange(n_cmp) * CMP_STRIDE
    sel_start = np.arange(n_sel) * SEL_BLOCK
    ov = np.clip(np.minimum(cmp_start[:, None] + CMP_BLOCK, sel_start[None, :] + SEL_BLOCK)
                 - np.maximum(cmp_start[:, None], sel_start[None, :]), 0, None) / CMP_STRIDE
    full = np.zeros((LANES, n16), np.float32)
    full[:n_sel, :n_cmp] = ov.T
    return jnp.asarray(full, BF16)


def _expand_matrix_t(t):
    blk = np.arange(t) // SEL_BLOCK
    return jnp.asarray(blk[:, None] == np.arange(LANES)[None, :], BF16)


def kernel(x, norm1_g, w_in, mu_shift, rwkv_w0, rwkv_w2, rwkv_a0, rwkv_a2, rwkv_g2, rwkv_k_k, rwkv_k_a,
           rwkv_r_k, rwkv_lnx_w, rwkv_lnx_b, nsa_pe_k, nsa_pe_v, nsa_cmp_k_w1, nsa_cmp_k_w2, nsa_cmp_v_w1,
           nsa_cmp_v_w2, w_out, norm2_g, ffn_w_gate, ffn_w_up, ffn_w_down, norm_f_g):
    batch, t, d_model = x.shape
    assert w_in.shape[0] == 1, "the final RMSNorm is fused into the (single) layer's FFN kernel"
    assert t % RWKV_PREP_ROWS == 0 and t % KEY_TILE == 0 and t // SEL_BLOCK <= LANES
    hk, grp, dk = NSA_KV_HEADS, NSA_GROUP, HEAD_DIM
    nq = t // Q_BLOCK
    slopes = 2.0 ** (-8.0 * jnp.arange(1, NSA_Q_HEADS + 1, dtype=F32) / NSA_Q_HEADS)
    nsa_pad = _round_up(N_NSA_COLS, LANES)
    row = lambda a: a.reshape(1, -1)
    i = 0

    h = x.reshape(batch * t, d_model)
    w_r = w_in[i][:, :N_RWKV_COLS].astype(BF16)
    w_n = jnp.pad(w_in[i][:, N_RWKV_COLS:], ((0, 0), (0, nsa_pad - N_NSA_COLS))).astype(BF16)
    p_r, q_t, ks, kw, vs_t, vw_t, zk, zv, gl_t = _inproj(h, batch, row(norm1_g[i]), w_r, w_n)

    rt, kt, bt, at, v, wc, bonus, g = _rwkv_prep(
        p_r, batch, row(mu_shift[i]), row(rwkv_w0[i]), rwkv_w2[i].astype(BF16), row(rwkv_a0[i]),
        rwkv_a2[i].astype(BF16), rwkv_g2[i].astype(BF16), row(rwkv_k_k[i]), row(rwkv_k_a[i]),
        row(rwkv_r_k[i]))
    y_rwkv = _rwkv_scan(rt, kt, bt, at, v, wc, bonus, g, row(rwkv_lnx_w[i]), row(rwkv_lnx_b[i]))
    y_rwkv = y_rwkv.reshape(batch * t, D_RWKV)

    kc_aug, vc_t = _compress(zk, zv, nsa_pe_k[i], nsa_pe_v[i],
                             nsa_cmp_k_w1[i].astype(BF16), nsa_cmp_k_w2[i].astype(BF16),
                             nsa_cmp_v_w1[i].astype(BF16), nsa_cmp_v_w2[i].astype(BF16))
    oc_t, sel_t = _cmp_select(slopes, q_t, kc_aug, vc_t, _overlap_matrix_t(t))

    blocks_per_tile = KEY_TILE // SEL_BLOCK
    active = sel_t.reshape(batch, hk, nq, MAX_KEY_TILES, blocks_per_tile * Q_BLOCK).max(axis=-1) > 0
    tile_ids = jnp.argsort(jnp.logical_not(active), axis=-1, stable=True).astype(jnp.int32).reshape(-1)
    tile_cnt = active.sum(axis=-1).astype(jnp.int32).reshape(-1)
    y_nsa_t = _sel_win(tile_ids, tile_cnt, slopes, q_t, sel_t, _expand_matrix_t(t), ks, vs_t, kw, vw_t,
                       oc_t, gl_t)

    out = _ffn(h, y_rwkv, y_nsa_t, w_out[i][:D_RWKV].astype(BF16), w_out[i][D_RWKV:].astype(BF16),
               row(norm2_g[i]), ffn_w_gate[i].astype(BF16), ffn_w_up[i].astype(BF16),
               ffn_w_down[i].astype(BF16), row(norm_f_g))
    return out.reshape(batch, t, d_model)
```

```python
import numpy as np
import jax
import jax.numpy as jnp
from jax import lax
from jax.experimental import pallas as pl
from jax.experimental.pallas import tpu as pltpu

F32 = jnp.float32
BF16 = jnp.bfloat16

HEAD_DIM = 64
RWKV_HEADS = 8
D_RWKV = RWKV_HEADS * HEAD_DIM
NSA_Q_HEADS = 8
NSA_KV_HEADS = 2
NSA_GROUP = NSA_Q_HEADS // NSA_KV_HEADS
D_NSA = NSA_Q_HEADS * HEAD_DIM
D_KV = NSA_KV_HEADS * HEAD_DIM
LORA_W, LORA_A, LORA_G = 64, 64, 128
N_RWKV_COLS = 3 * D_RWKV + LORA_W + LORA_A + LORA_G
N_NSA_COLS = D_NSA + 6 * D_KV + 3 * NSA_Q_HEADS
CMP_BLOCK, CMP_STRIDE = 32, 16
SEL_BLOCK, SEL_TOPK = 64, 16
WINDOW = 512
Q_BLOCK = 128
NORM_EPS = 1e-6
GN_EPS = 64e-5
NEG = -1e30
BIG = 1e30

LANES = 128
RWKV_CHUNK = 64
RWKV_PREP_ROWS = 8 * RWKV_CHUNK
KEY_TILE = 256
MAX_KEY_TILES = LANES * SEL_BLOCK // KEY_TILE
WIN_TILES = (WINDOW + Q_BLOCK) // KEY_TILE + 1
SEL_GROUP = 3
SEL_FIRST = 2 * SEL_GROUP
CMP_Q_BLOCKS = 2
V_ROWS = HEAD_DIM + 16
SOFTMAX_FLOOR = -1e20
QG = NSA_GROUP * Q_BLOCK
VMEM_LIMIT = 56 * 1024 * 1024


def _round_up(n, m):
    return -(-n // m) * m


def _dot(a, b):
    return jnp.dot(a, b, preferred_element_type=F32)


def _dot_tn(a, b):
    return lax.dot_general(a, b, (((0,), (0,)), ((), ())), preferred_element_type=F32)


def _split3(a):
    hi = a.astype(BF16)
    rest = a - hi.astype(F32)
    mid = rest.astype(BF16)
    return hi, mid, (rest - mid.astype(F32)).astype(BF16)


def _dot_exact_lhs(a, b):
    hi, mid, lo = _split3(b)
    return _dot(a, hi) + _dot(a, mid) + _dot(a, lo)


def _dot_exact_rhs(a, b):
    hi, mid, lo = _split3(a)
    return _dot(hi, b) + _dot(mid, b) + _dot(lo, b)


def _bdot(spec, a, b):
    return jnp.einsum(spec, a, b, preferred_element_type=F32)


def _rms(x, g):
    return x * lax.rsqrt(jnp.mean(x * x, axis=-1, keepdims=True) + NORM_EPS) * g


def _params(*sem):
    return pltpu.CompilerParams(dimension_semantics=sem, vmem_limit_bytes=VMEM_LIMIT)


def _inproj_kernel(x_ref, g_ref, wr_ref, wn_ref, pr_ref, qt_ref, ks_ref, kw_ref, vst_ref, vwt_ref,
                   kc_ref, vc_ref, glt_ref):
    xb = _rms(x_ref[...], g_ref[...]).astype(BF16)
    pr_ref[...] = _dot(xb, wr_ref[...])
    pn = _dot(xb, wn_ref[...])
    rows = pn.shape[0]
    dk, hk_n = HEAD_DIM, NSA_KV_HEADS
    group = lambda j: pn[:, D_NSA + j * D_KV:D_NSA + (j + 1) * D_KV]

    for half in range(rows // Q_BLOCK):
        q_tr = jnp.transpose(pn[half * Q_BLOCK:(half + 1) * Q_BLOCK, :D_NSA])
        for hk in range(hk_n):
            base = hk * NSA_GROUP * dk
            qt_ref[0, hk, half] = jnp.concatenate(
                [q_tr[base + g * dk:base + (g + 1) * dk, :] for g in range(NSA_GROUP)], axis=1)

    pos = pl.program_id(1) * rows + lax.broadcasted_iota(jnp.int32, (rows, dk), 0)
    col = lax.broadcasted_iota(jnp.int32, (rows, dk), 1)
    pos_cols = jnp.where(col == 0, pos // SEL_BLOCK, jnp.where(col == 1, pos % SEL_BLOCK, 0)).astype(F32)
    ones_rows = (lax.broadcasted_iota(jnp.int32, (V_ROWS - dk, rows), 0) == 0).astype(F32)
    kc, vc, ks, vs, kw, vw = (group(j) for j in range(6))
    vs_tr, vw_tr = jnp.transpose(vs), jnp.transpose(vw)
    gl_tr = jnp.transpose(pn[:, D_NSA + 6 * D_KV:])
    n_gate = 3 * NSA_GROUP
    for hk in range(hk_n):
        sl = slice(hk * dk, (hk + 1) * dk)
        kc_ref[0, hk] = kc[:, sl]
        vc_ref[0, hk] = vc[:, sl]
        ks_ref[0, hk] = jnp.concatenate([ks[:, sl], pos_cols], axis=1).astype(BF16)
        kw_ref[0, hk] = jnp.concatenate([kw[:, sl], pos_cols], axis=1).astype(BF16)
        vst_ref[0, hk, 0] = jnp.concatenate([vs_tr[sl, :], ones_rows], axis=0).astype(BF16)
        vwt_ref[0, hk, 0] = jnp.concatenate([vw_tr[sl, :], ones_rows], axis=0).astype(BF16)
        glt_ref[0, hk] = gl_tr[hk * n_gate:(hk + 1) * n_gate, :]


def _inproj(x2, batch, g, w_r, w_n):
    m, d = x2.shape
    t = m // batch
    tm = KEY_TILE
    nt = t // tm
    nr, nn = w_r.shape[1], w_n.shape[1]
    hk, dk = NSA_KV_HEADS, HEAD_DIM
    const = lambda a: pl.BlockSpec(a.shape, lambda b, i: (0, 0))
    keys = pl.BlockSpec((1, hk, tm, 2 * dk), lambda b, i: (b, 0, i, 0))
    vals = pl.BlockSpec((1, hk, 1, V_ROWS, tm), lambda b, i: (b, 0, i, 0, 0))
    cmp_in = pl.BlockSpec((1, hk, tm, dk), lambda b, i: (b, 0, i, 0))
    keys_shape = jax.ShapeDtypeStruct((batch, hk, t, 2 * dk), BF16)
    vals_shape = jax.ShapeDtypeStruct((batch, hk, nt, V_ROWS, tm), BF16)
    cmp_shape = jax.ShapeDtypeStruct((batch, hk, t, dk), F32)
    return pl.pallas_call(
        _inproj_kernel,
        grid=(batch, nt),
        in_specs=[pl.BlockSpec((tm, d), lambda b, i: (b * nt + i, 0)), const(g), const(w_r), const(w_n)],
        out_specs=[pl.BlockSpec((tm, nr), lambda b, i: (b * nt + i, 0)),
                   pl.BlockSpec((1, hk, tm // Q_BLOCK, dk, QG), lambda b, i: (b, 0, i, 0, 0)),
                   keys, keys, vals, vals, cmp_in, cmp_in,
                   pl.BlockSpec((1, hk, 3 * NSA_GROUP, tm), lambda b, i: (b, 0, 0, i))],
        out_shape=[jax.ShapeDtypeStruct((m, nr), F32),
                   jax.ShapeDtypeStruct((batch, hk, t // Q_BLOCK, dk, QG), F32),
                   keys_shape, keys_shape, vals_shape, vals_shape, cmp_shape, cmp_shape,
                   jax.ShapeDtypeStruct((batch, hk, 3 * NSA_GROUP, t), F32)],
        compiler_params=_params("arbitrary", "arbitrary"),
        name="inproj",
    )(x2, g, w_r, w_n)


def _rwkv_prep_kernel(p_ref, mu_ref, w0_ref, w2_ref, a0_ref, a2_ref, g2_ref, kk_ref, ka_ref, rk_ref,
                      bd_ref, tri_ref,
                      rt_ref, kt_ref, bt_ref, at_ref, v_ref, wc_ref, bonus_ref, g_ref, carry_ref):
    rows = p_ref.shape[0]
    c = RWKV_CHUNK

    @pl.when(pl.program_id(1) == 0)
    def _():
        carry_ref[...] = jnp.zeros_like(carry_ref)

    p = p_ref[...]
    row = lax.broadcasted_iota(jnp.int32, p.shape, 0)
    p_prev = jnp.where(row == 0, carry_ref[7:8, :], pltpu.roll(p, 1, axis=0))
    carry_ref[...] = p[rows - 8:, :]
    ps = p + mu_ref[...] * (p_prev - p)

    d = D_RWKV
    r, k, v = ps[:, 0:d], ps[:, d:2 * d], ps[:, 2 * d:3 * d]
    dw = ps[:, 3 * d:3 * d + LORA_W]
    da = ps[:, 3 * d + LORA_W:3 * d + LORA_W + LORA_A]
    dg = ps[:, 3 * d + LORA_W + LORA_A:]

    z = -(w0_ref[...] + _dot(jnp.tanh(dw).astype(BF16), w2_ref[...]))
    softplus = jnp.maximum(z, 0.0) + jnp.log1p(jnp.exp(-jnp.abs(z)))
    logw = -jnp.exp(-softplus - 0.5)
    a = jax.nn.sigmoid(a0_ref[...] + _dot(da.astype(BF16), a2_ref[...]))
    g_ref[0] = _dot(jax.nn.sigmoid(dg).astype(BF16), g2_ref[...])

    bd = bd_ref[...]
    kk = k * kk_ref[...]
    kk = kk / jnp.maximum(jnp.sqrt(_dot_exact_rhs(kk * kk, bd)), 1e-12)
    kp = k * (1.0 + (a - 1.0) * ka_ref[...])
    bonus_ref[0] = _dot_exact_rhs(r * kp * rk_ref[...], bd) * v

    tri = tri_ref[...]
    cums = [_dot_exact_lhs(tri, logw[ci * c:(ci + 1) * c]) for ci in range(rows // c)]
    cum = jnp.concatenate(cums, axis=0)
    e_pos, e_neg = jnp.exp(cum), jnp.exp(-cum)
    rt_ref[0] = (r * e_pos).astype(BF16)
    kt_ref[0] = (kp * e_neg).astype(BF16)
    bt_ref[0] = (kk * a * e_neg).astype(BF16)
    at_ref[0] = (-kk * jnp.exp(cum - logw)).astype(BF16)
    v_ref[0] = v.astype(BF16)
    wc_ref[0] = jnp.concatenate([e_pos[(ci + 1) * c - 1:(ci + 1) * c, :] for ci in range(rows // c)], axis=0)


def _rwkv_prep(p_r, batch, mu, w0, w2, a0, a2, g2, k_k, k_a, r_k):
    m = p_r.shape[0]
    t = m // batch
    rows = RWKV_PREP_ROWS
    nt = t // rows
    d = D_RWKV
    head = np.arange(d) // HEAD_DIM
    bd = jnp.asarray(head[:, None] == head[None, :], BF16)
    tri = jnp.asarray(np.tril(np.ones((RWKV_CHUNK, RWKV_CHUNK))), BF16)
    vec = lambda n: pl.BlockSpec((1, n), lambda b, i: (0, 0))
    full = lambda a: pl.BlockSpec(a.shape, lambda b, i: (0, 0))
    wide = pl.BlockSpec((1, rows, d), lambda b, i: (b, i, 0))
    wide_f32 = jax.ShapeDtypeStruct((batch, t, d), F32)
    wide_bf16 = jax.ShapeDtypeStruct((batch, t, d), BF16)
    outs = pl.pallas_call(
        _rwkv_prep_kernel,
        grid=(batch, nt),
        in_specs=[pl.BlockSpec((rows, N_RWKV_COLS), lambda b, i: (b * nt + i, 0)),
                  vec(N_RWKV_COLS), vec(d), full(w2), vec(d), full(a2), full(g2), vec(d), vec(d), vec(d),
                  full(bd), full(tri)],
        out_specs=[wide, wide, wide, wide, wide,
                   pl.BlockSpec((1, rows // RWKV_CHUNK, d), lambda b, i: (b, i, 0)),
                   wide, wide],
        out_shape=[wide_bf16, wide_bf16, wide_bf16, wide_bf16, wide_bf16,
                   jax.ShapeDtypeStruct((batch, t // RWKV_CHUNK, d), F32),
                   wide_f32, wide_f32],
        scratch_shapes=[pltpu.VMEM((8, N_RWKV_COLS), F32)],
        compiler_params=_params("arbitrary", "arbitrary"),
        name="rwkv_prep",
    )(p_r, mu, w0, w2, a0, a2, g2, k_k, k_a, r_k, bd, tri)
    return outs


def _rwkv_scan_kernel(rt_ref, kt_ref, bt_ref, at_ref, v_ref, wc_ref, bonus_ref, g_ref, lnw_ref, lnb_ref,
                      y_ref, s_ref):
    nb, c, _ = rt_ref.shape
    nh, dk = RWKV_HEADS, HEAD_DIM
    n = nb * nh

    @pl.when(pl.program_id(0) == 0)
    def _():
        s_ref[...] = jnp.zeros_like(s_ref)

    row = lax.broadcasted_iota(jnp.int32, (1, c, c), 1)
    col = lax.broadcasted_iota(jnp.int32, (1, c, c), 2)
    strict = col < row
    incl = col <= row
    eye = (row == col).astype(F32)

    def load(ref):
        x = ref[...]
        return jnp.stack([x[b, :, h * dk:(h + 1) * dk] for b in range(nb) for h in range(nh)], axis=0)

    vb = load(v_ref)
    ar = jnp.concatenate([load(at_ref), load(rt_ref)], axis=1)
    bk = jnp.concatenate([load(bt_ref), load(kt_ref)], axis=1)
    amat = _bdot("nik,njk->nij", ar, bk)
    n_ab = jnp.where(strict, amat[:, :c, :c], 0.0)
    a_ak = jnp.where(strict, amat[:, :c, c:], 0.0)
    a_rb = jnp.where(incl, amat[:, c:, :c], 0.0)
    a_rk = jnp.where(incl, amat[:, c:, c:], 0.0)
    s0 = s_ref[...]
    ar_s = _bdot("nik,nvk->niv", ar, s0.astype(BF16))
    rhs = ar_s[:, :c] + _bdot("nij,njv->niv", a_ak.astype(BF16), vb)
    inv = eye + n_ab
    pw = n_ab
    for _ in range(int(np.log2(c)) - 1):
        pwb = pw.astype(BF16)
        pw = _bdot("nij,njk->nik", pwb, pwb)
        inv = inv + _bdot("nij,njk->nik", pw.astype(BF16), inv.astype(BF16))
    ub = _bdot("nij,njv->niv", inv.astype(BF16), rhs.astype(BF16)).astype(BF16)
    y = ar_s[:, c:] + _bdot("nij,njv->niv", a_rb.astype(BF16), ub) + _bdot("nij,njv->niv", a_rk.astype(BF16), vb)
    uv_t = jnp.swapaxes(jnp.concatenate([ub, vb], axis=1), 1, 2)
    s_ref[...] = (s0 + _bdot("nvi,nik->nvk", uv_t, bk)) * load(wc_ref.at[:, 0])

    mean = jnp.mean(y, axis=-1, keepdims=True)
    var = jnp.mean(jnp.square(y - mean), axis=-1, keepdims=True)
    yn = (y - mean) * lax.rsqrt(var + GN_EPS)
    for b in range(nb):
        wide = jnp.concatenate([yn[b * nh + h] for h in range(nh)], axis=-1)
        y_ref[b] = ((wide * lnw_ref[...] + lnb_ref[...] + bonus_ref[b]) * g_ref[b]).astype(y_ref.dtype)


def _rwkv_scan(rt, kt, bt, at, v, wc, bonus, g, lnw, lnb):
    batch, t, d = rt.shape
    c = RWKV_CHUNK
    wide = pl.BlockSpec((batch, c, d), lambda i: (0, i, 0))
    vec = pl.BlockSpec((1, d), lambda i: (0, 0))
    return pl.pallas_call(
        _rwkv_scan_kernel,
        grid=(t // c,),
        in_specs=[wide, wide, wide, wide, wide,
                  pl.BlockSpec((batch, 1, 1, d), lambda i: (0, i, 0, 0)),
                  wide, wide, vec, vec],
        out_specs=wide,
        out_shape=jax.ShapeDtypeStruct((batch, t, d), BF16),
        scratch_shapes=[pltpu.VMEM((batch * RWKV_HEADS, HEAD_DIM, HEAD_DIM), F32)],
        compiler_params=_params("arbitrary"),
        name="rwkv_scan",
    )(rt, kt, bt, at, v, wc.reshape(batch, t // c, 1, d), bonus, g, lnw, lnb)


def _compress_kernel(zk_ref, zv_ref, pek_ref, pev_ref, k1_ref, k2_ref, v1_ref, v2_ref, ko_ref, vo_ref):
    def one(z_ref, pe_ref, w1_ref, w2_ref):
        n16 = z_ref.shape[2] // CMP_STRIDE
        dk = HEAD_DIM
        first = second = None
        for l in range(CMP_STRIDE):
            z = z_ref[0, 0, pl.ds(l, n16, stride=CMP_STRIDE), :]
            lo = _dot((z + pe_ref[l:l + 1, :]).astype(BF16), w1_ref[l * dk:(l + 1) * dk, :])
            u = CMP_STRIDE + l
            hi = _dot((z + pe_ref[u:u + 1, :]).astype(BF16), w1_ref[u * dk:(u + 1) * dk, :])
            first = lo if first is None else first + lo
            second = hi if second is None else second + hi
        hidden = first + pltpu.roll(second, n16 - 1, axis=0)
        return _dot(jax.nn.gelu(hidden).astype(BF16), w2_ref[...])

    kc = one(zk_ref, pek_ref, k1_ref, k2_ref)
    blk = lax.broadcasted_iota(jnp.int32, kc.shape, 0)
    col = lax.broadcasted_iota(jnp.int32, kc.shape, 1)
    per = SEL_BLOCK // CMP_STRIDE
    hi = blk // per
    lo = CMP_STRIDE * (blk % per) + (CMP_BLOCK - 1)
    feat = jnp.where(col == 0, hi, jnp.where(col == 1, lo, 0)).astype(F32)
    ko_ref[0] = jnp.concatenate([kc, feat], axis=1).astype(BF16)
    vc = one(zv_ref, pev_ref, v1_ref, v2_ref)
    vc_t = jnp.transpose(jnp.concatenate([vc, jnp.zeros_like(vc)], axis=1))
    vo_ref[0] = vc_t[:HEAD_DIM].astype(BF16)


def _compress(zk, zv, pek, pev, k1, k2, v1, v2):
    batch, hk, t, dk = zk.shape
    n16 = t // CMP_STRIDE
    zspec = pl.BlockSpec((1, 1, t, dk), lambda b, h: (b, h, 0, 0))
    full = lambda a: pl.BlockSpec(a.shape, lambda b, h: (0,) * a.ndim)
    return pl.pallas_call(
        _compress_kernel,
        grid=(batch, hk),
        in_specs=[zspec, zspec, full(pek), full(pev), full(k1), full(k2), full(v1), full(v2)],
        out_specs=[pl.BlockSpec((1, n16, 2 * dk), lambda b, h: (b * hk + h, 0, 0)),
                   pl.BlockSpec((1, dk, n16), lambda b, h: (b * hk + h, 0, 0))],
        out_shape=[jax.ShapeDtypeStruct((batch * hk, n16, 2 * dk), BF16),
                   jax.ShapeDtypeStruct((batch * hk, dk, n16), BF16)],
        compiler_params=_params("arbitrary", "arbitrary"),
        name="nsa_compress",
    )(zk, zv, pek, pev, k1, k2, v1, v2)


def _query_features(qt, slopes_ref, hk):
    lane_g = lax.broadcasted_iota(jnp.int32, (1, QG), 1) // Q_BLOCK
    slope = jnp.zeros((1, QG), F32)
    for g in range(NSA_GROUP):
        slope = jnp.where(lane_g == g, slopes_ref[hk * NSA_GROUP + g], slope)
    row = lax.broadcasted_iota(jnp.int32, (HEAD_DIM, QG), 0)
    extra = jnp.where(row == 0, SEL_BLOCK * slope, jnp.where(row == 1, slope, 0.0))
    return jnp.concatenate([qt * (HEAD_DIM ** -0.5), extra], axis=0).astype(BF16)


def _cmp_select_kernel(slopes_ref, qt_ref, kc_ref, vct_ref, ovt_ref, oct_ref, selt_ref):
    hk = pl.program_id(1)
    nblk = qt_ref.shape[2]
    width = nblk * Q_BLOCK
    ncp = kc_ref.shape[1]
    cmp_end = lax.broadcasted_iota(jnp.int32, (ncp, Q_BLOCK), 0) * CMP_STRIDE + (CMP_BLOCK - 1)
    imps = []
    for u in range(nblk):
        q0 = (pl.program_id(2) * nblk + u) * Q_BLOCK
        q_aug = _query_features(qt_ref[0, 0, u], slopes_ref, hk)
        s = _dot(kc_ref[0], q_aug)
        ok = cmp_end <= q0 + lax.broadcasted_iota(jnp.int32, (ncp, Q_BLOCK), 1)
        any_ok = (q0 + lax.broadcasted_iota(jnp.int32, (1, Q_BLOCK), 1) >= CMP_BLOCK - 1).astype(F32)
        p_sum = jnp.zeros((ncp, Q_BLOCK), F32)
        probs = []
        for g in range(NSA_GROUP):
            sg = jnp.where(ok, s[:, g * Q_BLOCK:(g + 1) * Q_BLOCK], NEG)
            e = jnp.exp(sg - jnp.max(sg, axis=0, keepdims=True))
            p = e * (any_ok / jnp.sum(e, axis=0, keepdims=True))
            p_sum = p_sum + p
            probs.append(p.astype(BF16))
        oct_ref[0, 0, u] = _dot(vct_ref[0], jnp.concatenate(probs, axis=1))
        imps.append(_dot_exact_lhs(ovt_ref[...], p_sum))

    imp = jnp.concatenate(imps, axis=1)
    blk = lax.broadcasted_iota(jnp.int32, (LANES, width), 0)
    cur = (pl.program_id(2) * width + lax.broadcasted_iota(jnp.int32, (LANES, width), 1)) // SEL_BLOCK
    valid = blk <= cur
    forced = (blk == 0) | (blk == cur) | (blk == cur - 1)
    x = jnp.where(valid, jnp.where(forced, BIG, imp), NEG)
    blk_f = blk.astype(F32)
    chosen = jnp.zeros((LANES, width), jnp.bool_)
    for _ in range(SEL_TOPK):
        m = jnp.max(x, axis=0, keepdims=True)
        first = jnp.min(jnp.where(x == m, blk_f, float(LANES)), axis=0, keepdims=True)
        hit = blk_f == first
        chosen = chosen | hit
        x = jnp.where(hit, -jnp.inf, x)
    sel = (chosen & valid).astype(BF16)
    for u in range(nblk):
        selt_ref[0, 0, u] = sel[:, u * Q_BLOCK:(u + 1) * Q_BLOCK]


def _cmp_select(slopes, q_t, kc_aug, vc_t, ov_t):
    batch, hk, nq, dk, qg = q_t.shape
    ncp = kc_aug.shape[1]
    nblk = CMP_Q_BLOCKS
    return pl.pallas_call(
        _cmp_select_kernel,
        grid=(batch, hk, nq // nblk),
        in_specs=[pl.BlockSpec(memory_space=pltpu.SMEM),
                  pl.BlockSpec((1, 1, nblk, dk, qg), lambda b, h, i: (b, h, i, 0, 0)),
                  pl.BlockSpec((1, ncp, 2 * dk), lambda b, h, i: (b * hk + h, 0, 0)),
                  pl.BlockSpec((1, dk, ncp), lambda b, h, i: (b * hk + h, 0, 0)),
                  pl.BlockSpec(ov_t.shape, lambda b, h, i: (0, 0))],
        out_specs=[pl.BlockSpec((1, 1, nblk, dk, qg), lambda b, h, i: (b, h, i, 0, 0)),
                   pl.BlockSpec((1, 1, nblk, LANES, Q_BLOCK), lambda b, h, i: (b, h, i, 0, 0))],
        out_shape=[jax.ShapeDtypeStruct(q_t.shape, F32),
                   jax.ShapeDtypeStruct((batch, hk, nq, LANES, Q_BLOCK), BF16)],
        compiler_params=_params("arbitrary", "arbitrary", "arbitrary"),
        name="nsa_cmp_select",
    )(slopes, q_t, kc_aug, vc_t, ov_t)


def _sel_win_kernel(ids_ref, cnt_ref, slopes_ref, qt_ref, selt_ref, et_ref, ks_ref, vst_ref, kw_ref, vwt_ref,
                    oct_ref, glt_ref, yt_ref):
    b, hk, i = pl.program_id(0), pl.program_id(1), pl.program_id(2)
    step = (b * pl.num_programs(1) + hk) * pl.num_programs(2) + i
    q0 = i * Q_BLOCK
    kt = KEY_TILE
    q_aug = _query_features(qt_ref[0, 0, 0], slopes_ref, hk)
    selt = selt_ref[0, 0, 0]
    lane_minus_row = (lax.broadcasted_iota(jnp.int32, (kt, Q_BLOCK), 1)
                      - lax.broadcasted_iota(jnp.int32, (kt, Q_BLOCK), 0))

    def attend(state, tiles):
        m_old, acc_old = state
        scores = [_dot(k_tile, q_aug) for k_tile, _, _ in tiles]
        probs, maxes = [[] for _ in tiles], []
        for g in range(NSA_GROUP):
            gsl = slice(g * Q_BLOCK, (g + 1) * Q_BLOCK)
            masked = [jnp.where(mask, s[:, gsl], NEG) for s, (_, _, mask) in zip(scores, tiles)]
            mg = m_old[:, gsl]
            for sg in masked:
                mg = jnp.maximum(mg, jnp.max(sg, axis=0, keepdims=True))
            for n, sg in enumerate(masked):
                probs[n].append(jnp.exp(sg - mg).astype(BF16))
            maxes.append(mg)
        m_new = jnp.concatenate(maxes, axis=1)
        update = _dot(tiles[0][1], jnp.concatenate(probs[0], axis=1))
        for n in range(1, len(tiles)):
            update = update + _dot(tiles[n][1], jnp.concatenate(probs[n], axis=1))
        return m_new, jnp.exp(m_old - m_new) * acc_old + update

    start = (jnp.full((1, QG), SOFTMAX_FLOOR, F32), jnp.zeros((V_ROWS, QG), F32))
    result = lambda state: state[1][:HEAD_DIM] / state[1][HEAD_DIM:HEAD_DIM + 1]

    count = cnt_ref[step]
    last_tile = ks_ref.shape[2] // kt - 1

    def sel_tile(n):
        j = jnp.minimum(ids_ref[step * MAX_KEY_TILES + jnp.minimum(n, MAX_KEY_TILES - 1)], last_tile)
        k0 = pl.multiple_of(j * kt, kt)
        picked = _dot(et_ref[pl.ds(k0, kt), :], selt)
        causal_from = jnp.where(n < count, k0 - q0, 1 << 30)
        mask = (picked > 0.5) & (lane_minus_row >= causal_from)
        return ks_ref[0, 0, pl.ds(k0, kt), :], vst_ref[0, 0, j], mask

    last = (q0 + Q_BLOCK - 1) // kt

    def win_tile(u):
        j = last - u
        jc = jnp.maximum(j, 0)
        k0 = pl.multiple_of(jc * kt, kt)
        dist = lane_minus_row + jnp.where(j >= 0, q0 - k0, -(1 << 30))
        mask = (dist >= 0) & (dist < WINDOW)
        return kw_ref[0, 0, pl.ds(k0, kt), :], vwt_ref[0, 0, jc], mask

    state_w = attend(start, [win_tile(u) for u in range(WIN_TILES)])
    state_s = attend(start, [sel_tile(n) for n in range(SEL_FIRST)])
    groups_done = SEL_FIRST // SEL_GROUP
    state_s = lax.fori_loop(
        groups_done, (count + SEL_GROUP - 1) // SEL_GROUP,
        lambda n, st: attend(st, [sel_tile(SEL_GROUP * n + u) for u in range(SEL_GROUP)]), state_s)
    o_s, o_w = result(state_s), result(state_w)

    gates = jax.nn.sigmoid(glt_ref[0, 0])
    o_c = oct_ref[0, 0, 0]
    outs = []
    for g in range(NSA_GROUP):
        gsl = slice(g * Q_BLOCK, (g + 1) * Q_BLOCK)
        outs.append(gates[3 * g:3 * g + 1] * o_c[:, gsl] + gates[3 * g + 1:3 * g + 2] * o_s[:, gsl]
                    + gates[3 * g + 2:3 * g + 3] * o_w[:, gsl])
    yt_ref[0] = jnp.concatenate(outs, axis=0).astype(yt_ref.dtype)


def _sel_win(tile_ids, tile_cnt, slopes, q_t, sel_t, expand_t, ks, vs_t, kw, vw_t, oc_t, gl_t):
    batch, hk, nq, dk, qg = q_t.shape
    t = nq * Q_BLOCK
    nkt = t // KEY_TILE
    grp = qg // Q_BLOCK
    qspec = pl.BlockSpec((1, 1, 1, dk, qg), lambda b, h, i, *_: (b, h, i, 0, 0))
    kspec = pl.BlockSpec((1, 1, t, 2 * dk), lambda b, h, i, *_: (b, h, 0, 0))
    vspec = pl.BlockSpec((1, 1, nkt, V_ROWS, KEY_TILE), lambda b, h, i, *_: (b, h, 0, 0, 0))
    grid_spec = pltpu.PrefetchScalarGridSpec(
        num_scalar_prefetch=2,
        grid=(batch, hk, nq),
        in_specs=[pl.BlockSpec(memory_space=pltpu.SMEM), qspec,
                  pl.BlockSpec((1, 1, 1, LANES, Q_BLOCK), lambda b, h, i, *_: (b, h, i, 0, 0)),
                  pl.BlockSpec(expand_t.shape, lambda b, h, i, *_: (0, 0)),
                  kspec, vspec, kspec, vspec, qspec,
                  pl.BlockSpec((1, 1, 3 * grp, Q_BLOCK), lambda b, h, i, *_: (b, h, 0, i))],
        out_specs=pl.BlockSpec((1, grp * dk, Q_BLOCK), lambda b, h, i, *_: (b, h, i)),
    )
    return pl.pallas_call(
        _sel_win_kernel,
        grid_spec=grid_spec,
        out_shape=jax.ShapeDtypeStruct((batch, hk * grp * dk, t), BF16),
        compiler_params=_params("arbitrary", "arbitrary", "arbitrary"),
        name="nsa_sel_win",
    )(tile_ids, tile_cnt, slopes, q_t, sel_t, expand_t, ks, vs_t, kw, vw_t, oc_t, gl_t)


def _ffn_kernel(x_ref, yr_ref, ynt_ref, wo1_ref, wo2_ref, g2_ref, wg_ref, wu_ref, wd_ref, gf_ref, o_ref):
    h1 = (x_ref[...] + _dot(yr_ref[...].astype(BF16), wo1_ref[...])
          + _dot_tn(ynt_ref[0].astype(BF16), wo2_ref[...]))
    hn = _rms(h1, g2_ref[...]).astype(BF16)
    gate = _dot(hn, wg_ref[...])
    up = _dot(hn, wu_ref[...])
    act = gate * jax.nn.sigmoid(gate) * up
    o_ref[...] = _rms(h1 + _dot(act.astype(BF16), wd_ref[...]), gf_ref[...])


def _ffn(x2, yr, yn_t, wo1, wo2, g2, wg, wu, wd, gf, tm=512):
    m, d = x2.shape
    _, dn, t = yn_t.shape
    per_seq = t // tm
    row = lambda n: pl.BlockSpec((tm, n), lambda i: (i, 0))
    const = lambda a: pl.BlockSpec(a.shape, lambda i: (0, 0), pipeline_mode=pl.Buffered(1))
    return pl.pallas_call(
        _ffn_kernel,
        grid=(m // tm,),
        in_specs=[row(d), row(yr.shape[1]),
                  pl.BlockSpec((1, dn, tm), lambda i: (i // per_seq, 0, i % per_seq)),
                  const(wo1), const(wo2), const(g2), const(wg), const(wu), const(wd), const(gf)],
        out_specs=row(d),
        out_shape=jax.ShapeDtypeStruct((m, d), F32),
        compiler_params=_params("arbitrary"),
        name="outproj_ffn",
    )(x2, yr, yn_t, wo1, wo2, g2, wg, wu, wd, gf)


def _overlap_matrix_t(t):
    n16 = t // CMP_STRIDE
    n_cmp = (t - CMP_BLOCK) // CMP_STRIDE + 1
    n_sel = t // SEL_BLOCK
    cmp_start = np.arange(n_cmp) * CMP_STRIDE
    sel_start = np.arange(n_sel) * SEL_BLOCK
    ov = np.clip(np.minimum(cmp_start[:, None] + CMP_BLOCK, sel_start[None, :] + SEL_BLOCK)
                 - np.maximum(cmp_start[:, None], sel_start[None, :]), 0, None) / CMP_STRIDE
    full = np.zeros((LANES, n16), np.float32)
    full[:n_sel, :n_cmp] = ov.T
    return jnp.asarray(full, BF16)


def _expand_matrix_t(t):
    blk = np.arange(t) // SEL_BLOCK
    return jnp.asarray(blk[:, None] == np.arange(LANES)[None, :], BF16)


def kernel(x, norm1_g, w_in, mu_shift, rwkv_w0, rwkv_w2, rwkv_a0, rwkv_a2, rwkv_g2, rwkv_k_k, rwkv_k_a,
           rwkv_r_k, rwkv_lnx_w, rwkv_lnx_b, nsa_pe_k, nsa_pe_v, nsa_cmp_k_w1, nsa_cmp_k_w2, nsa_cmp_v_w1,
           nsa_cmp_v_w2, w_out, norm2_g, ffn_w_gate, ffn_w_up, ffn_w_down, norm_f_g):
    batch, t, d_model = x.shape
    assert w_in.shape[0] == 1, "the final RMSNorm is fused into the (single) layer's FFN kernel"
    assert t % RWKV_PREP_ROWS == 0 and t % KEY_TILE == 0 and t // SEL_BLOCK <= LANES
    hk, grp, dk = NSA_KV_HEADS, NSA_GROUP, HEAD_DIM
    nq = t // Q_BLOCK
    slopes = 2.0 ** (-8.0 * jnp.arange(1, NSA_Q_HEADS + 1, dtype=F32) / NSA_Q_HEADS)
    nsa_pad = _round_up(N_NSA_COLS, LANES)
    row = lambda a: a.reshape(1, -1)
    i = 0

    h = x.reshape(batch * t, d_model)
    w_r = w_in[i][:, :N_RWKV_COLS].astype(BF16)
    w_n = jnp.pad(w_in[i][:, N_RWKV_COLS:], ((0, 0), (0, nsa_pad - N_NSA_COLS))).astype(BF16)
    p_r, q_t, ks, kw, vs_t, vw_t, zk, zv, gl_t = _inproj(h, batch, row(norm1_g[i]), w_r, w_n)

    rt, kt, bt, at, v, wc, bonus, g = _rwkv_prep(
        p_r, batch, row(mu_shift[i]), row(rwkv_w0[i]), rwkv_w2[i].astype(BF16), row(rwkv_a0[i]),
        rwkv_a2[i].astype(BF16), rwkv_g2[i].astype(BF16), row(rwkv_k_k[i]), row(rwkv_k_a[i]),
        row(rwkv_r_k[i]))
    y_rwkv = _rwkv_scan(rt, kt, bt, at, v, wc, bonus, g, row(rwkv_lnx_w[i]), row(rwkv_lnx_b[i]))
    y_rwkv = y_rwkv.reshape(batch * t, D_RWKV)

    kc_aug, vc_t = _compress(zk, zv, nsa_pe_k[i], nsa_pe_v[i],
                             nsa_cmp_k_w1[i].astype(BF16), nsa_cmp_k_w2[i].astype(BF16),
                             nsa_cmp_v_w1[i].astype(BF16), nsa_cmp_v_w2[i].astype(BF16))
    oc_t, sel_t = _cmp_select(slopes, q_t, kc_aug, vc_t, _overlap_matrix_t(t))

    blocks_per_tile = KEY_TILE // SEL_BLOCK
    active = sel_t.reshape(batch, hk, nq, MAX_KEY_TILES, blocks_per_tile * Q_BLOCK).max(axis=-1) > 0
    tile_ids = jnp.argsort(jnp.logical_not(active), axis=-1, stable=True).astype(jnp.int32).reshape(-1)
    tile_cnt = active.sum(axis=-1).astype(jnp.int32).reshape(-1)
    y_nsa_t = _sel_win(tile_ids, tile_cnt, slopes, q_t, sel_t, _expand_matrix_t(t), ks, vs_t, kw, vw_t,
                       oc_t, gl_t)

    out = _ffn(h, y_rwkv, y_nsa_t, w_out[i][:D_RWKV].astype(BF16), w_out[i][D_RWKV:].astype(BF16),
               row(norm2_g[i]), ffn_w_gate[i].astype(BF16), ffn_w_up[i].astype(BF16),
               ffn_w_down[i].astype(BF16), row(norm_f_g))
    return out.reshape(batch, t, d_model)
```

```python
import ml_dtypes
import numpy as np
import jax
import jax.numpy as jnp
from jax import lax
from jax.experimental import pallas as pl
from jax.experimental.pallas import tpu as pltpu

F32 = jnp.float32
BF16 = jnp.bfloat16

HEAD_DIM = 64
RWKV_HEADS = 8
D_RWKV = RWKV_HEADS * HEAD_DIM
NSA_Q_HEADS = 8
NSA_KV_HEADS = 2
NSA_GROUP = NSA_Q_HEADS // NSA_KV_HEADS
D_NSA = NSA_Q_HEADS * HEAD_DIM
D_KV = NSA_KV_HEADS * HEAD_DIM
LORA_W, LORA_A, LORA_G = 64, 64, 128
N_RWKV_COLS = 3 * D_RWKV + LORA_W + LORA_A + LORA_G
N_NSA_COLS = D_NSA + 6 * D_KV + 3 * NSA_Q_HEADS
CMP_BLOCK, CMP_STRIDE = 32, 16
SEL_BLOCK, SEL_TOPK = 64, 16
WINDOW = 512
Q_BLOCK = 128
NORM_EPS = 1e-6
GN_EPS = 64e-5
NEG = -1e30
BIG = 1e30

LANES = 128
RWKV_CHUNK = 64
RWKV_PREP_ROWS = 8 * RWKV_CHUNK
KEY_TILE = 128
INPROJ_ROWS = 256
MAX_KEY_TILES = LANES * SEL_BLOCK // KEY_TILE
WIN_TILES = (WINDOW + Q_BLOCK) // KEY_TILE
SEL_FIRST = 9
SEL_GROUP = 2
CMP_Q_BLOCKS = 2
V_ROWS = HEAD_DIM + 16
SOFTMAX_FLOOR = -1e20
QG = NSA_GROUP * Q_BLOCK
VMEM_LIMIT = 56 * 1024 * 1024


def _bf16_terms(x, count):
    terms = []
    for _ in range(count):
        terms.append(float(np.asarray(x, ml_dtypes.bfloat16)))
        x = x - terms[-1]
    return tuple(terms)


LOG2E = float(np.log2(np.e))
LOG2E_TERMS = _bf16_terms(LOG2E, 3)


def _round_up(n, m):
    return -(-n // m) * m


def _dot(a, b):
    return jnp.dot(a, b, preferred_element_type=F32)


def _dot_tn(a, b):
    return lax.dot_general(a, b, (((0,), (0,)), ((), ())), preferred_element_type=F32)


def _split3(a):
    hi = a.astype(BF16)
    rest = a - hi.astype(F32)
    mid = rest.astype(BF16)
    return hi, mid, (rest - mid.astype(F32)).astype(BF16)


def _dot_exact_lhs(a, b):
    hi, mid, lo = _split3(b)
    return _dot(a, hi) + _dot(a, mid) + _dot(a, lo)


def _dot_exact_rhs(a, b):
    hi, mid, lo = _split3(a)
    return _dot(hi, b) + _dot(mid, b) + _dot(lo, b)


def _bdot(spec, a, b):
    return jnp.einsum(spec, a, b, preferred_element_type=F32)


def _rms(x, g):
    return x * lax.rsqrt(jnp.mean(x * x, axis=-1, keepdims=True) + NORM_EPS) * g


def _params(*sem):
    return pltpu.CompilerParams(dimension_semantics=sem, vmem_limit_bytes=VMEM_LIMIT)


def _inproj_kernel(x_ref, g_ref, wr_ref, wn_ref, pr_ref, qt_ref, ks_ref, kw_ref, vst_ref, vwt_ref,
                   kc_ref, vc_ref, glt_ref):
    xb = _rms(x_ref[...], g_ref[...]).astype(BF16)
    pr_ref[...] = _dot(xb, wr_ref[...])
    pn = _dot(xb, wn_ref[...])
    rows = pn.shape[0]
    dk, hk_n = HEAD_DIM, NSA_KV_HEADS
    group = lambda j: pn[:, D_NSA + j * D_KV:D_NSA + (j + 1) * D_KV]

    for half in range(rows // Q_BLOCK):
        q_tr = jnp.transpose(pn[half * Q_BLOCK:(half + 1) * Q_BLOCK, :D_NSA])
        for hk in range(hk_n):
            base = hk * NSA_GROUP * dk
            qt_ref[0, hk, half] = jnp.concatenate(
                [q_tr[base + g * dk:base + (g + 1) * dk, :] for g in range(NSA_GROUP)], axis=1)

    pos = pl.program_id(1) * rows + lax.broadcasted_iota(jnp.int32, (rows, dk), 0)
    col = lax.broadcasted_iota(jnp.int32, (rows, dk), 1)
    pos_cols = _position_features(pos // SEL_BLOCK, pos % SEL_BLOCK, col)
    kt = KEY_TILE
    ones_rows = (lax.broadcasted_iota(jnp.int32, (V_ROWS - dk, kt), 0) == 0).astype(F32)
    kc, vc, ks, vs, kw, vw = (group(j) for j in range(6))
    vs_tr, vw_tr = jnp.transpose(vs), jnp.transpose(vw)
    gl_tr = jnp.transpose(pn[:, D_NSA + 6 * D_KV:])
    n_gate = 3 * NSA_GROUP
    for hk in range(hk_n):
        sl = slice(hk * dk, (hk + 1) * dk)
        kc_ref[0, hk] = kc[:, sl]
        vc_ref[0, hk] = vc[:, sl]
        ks_ref[0, hk] = jnp.concatenate([ks[:, sl], pos_cols], axis=1).astype(BF16)
        kw_ref[0, hk] = jnp.concatenate([kw[:, sl], pos_cols], axis=1).astype(BF16)
        for u in range(rows // kt):
            vst_ref[0, hk, u] = jnp.concatenate([vs_tr[sl, u * kt:(u + 1) * kt], ones_rows], axis=0).astype(BF16)
            vwt_ref[0, hk, u] = jnp.concatenate([vw_tr[sl, u * kt:(u + 1) * kt], ones_rows], axis=0).astype(BF16)
        glt_ref[0, hk] = gl_tr[hk * n_gate:(hk + 1) * n_gate, :]


def _inproj(x2, batch, g, w_r, w_n):
    m, d = x2.shape
    t = m // batch
    tm = INPROJ_ROWS
    nt = t // tm
    nr, nn = w_r.shape[1], w_n.shape[1]
    hk, dk = NSA_KV_HEADS, HEAD_DIM
    const = lambda a: pl.BlockSpec(a.shape, lambda b, i: (0, 0))
    keys = pl.BlockSpec((1, hk, tm, 2 * dk), lambda b, i: (b, 0, i, 0))
    vals = pl.BlockSpec((1, hk, tm // KEY_TILE, V_ROWS, KEY_TILE), lambda b, i: (b, 0, i, 0, 0))
    cmp_in = pl.BlockSpec((1, hk, tm, dk), lambda b, i: (b, 0, i, 0))
    keys_shape = jax.ShapeDtypeStruct((batch, hk, t, 2 * dk), BF16)
    vals_shape = jax.ShapeDtypeStruct((batch, hk, t // KEY_TILE, V_ROWS, KEY_TILE), BF16)
    cmp_shape = jax.ShapeDtypeStruct((batch, hk, t, dk), F32)
    return pl.pallas_call(
        _inproj_kernel,
        grid=(batch, nt),
        in_specs=[pl.BlockSpec((tm, d), lambda b, i: (b * nt + i, 0)), const(g), const(w_r), const(w_n)],
        out_specs=[pl.BlockSpec((tm, nr), lambda b, i: (b * nt + i, 0)),
                   pl.BlockSpec((1, hk, tm // Q_BLOCK, dk, QG), lambda b, i: (b, 0, i, 0, 0)),
                   keys, keys, vals, vals, cmp_in, cmp_in,
                   pl.BlockSpec((1, hk, 3 * NSA_GROUP, tm), lambda b, i: (b, 0, 0, i))],
        out_shape=[jax.ShapeDtypeStruct((m, nr), F32),
                   jax.ShapeDtypeStruct((batch, hk, t // Q_BLOCK, dk, QG), F32),
                   keys_shape, keys_shape, vals_shape, vals_shape, cmp_shape, cmp_shape,
                   jax.ShapeDtypeStruct((batch, hk, 3 * NSA_GROUP, t), F32)],
        compiler_params=_params("arbitrary", "arbitrary"),
        name="inproj",
    )(x2, g, w_r, w_n)


def _rwkv_prep_kernel(p_ref, mu_ref, w0_ref, w2_ref, a0_ref, a2_ref, g2_ref, kk_ref, ka_ref, rk_ref,
                      bd_ref, tri_ref,
                      rt_ref, kt_ref, bt_ref, at_ref, v_ref, wc_ref, bonus_ref, g_ref, carry_ref):
    rows = p_ref.shape[0]
    c = RWKV_CHUNK

    @pl.when(pl.program_id(1) == 0)
    def _():
        carry_ref[...] = jnp.zeros_like(carry_ref)

    p = p_ref[...]
    row = lax.broadcasted_iota(jnp.int32, p.shape, 0)
    p_prev = jnp.where(row == 0, carry_ref[7:8, :], pltpu.roll(p, 1, axis=0))
    carry_ref[...] = p[rows - 8:, :]
    ps = p + mu_ref[...] * (p_prev - p)

    d = D_RWKV
    r, k, v = ps[:, 0:d], ps[:, d:2 * d], ps[:, 2 * d:3 * d]
    dw = ps[:, 3 * d:3 * d + LORA_W]
    da = ps[:, 3 * d + LORA_W:3 * d + LORA_W + LORA_A]
    dg = ps[:, 3 * d + LORA_W + LORA_A:]

    z = -(w0_ref[...] + _dot(jnp.tanh(dw).astype(BF16), w2_ref[...]))
    softplus = jnp.maximum(z, 0.0) + jnp.log1p(jnp.exp(-jnp.abs(z)))
    logw = -jnp.exp(-softplus - 0.5)
    a = jax.nn.sigmoid(a0_ref[...] + _dot(da.astype(BF16), a2_ref[...]))
    g_ref[0] = _dot(jax.nn.sigmoid(dg).astype(BF16), g2_ref[...])

    bd = bd_ref[...]
    kk = k * kk_ref[...]
    kk = kk / jnp.maximum(jnp.sqrt(_dot_exact_rhs(kk * kk, bd)), 1e-12)
    kp = k * (1.0 + (a - 1.0) * ka_ref[...])
    bonus_ref[0] = _dot_exact_rhs(r * kp * rk_ref[...], bd) * v

    tri = tri_ref[...]
    cums = [_dot_exact_lhs(tri, logw[ci * c:(ci + 1) * c]) for ci in range(rows // c)]
    cum = jnp.concatenate(cums, axis=0)
    e_pos, e_neg = jnp.exp(cum), jnp.exp(-cum)
    rt_ref[0] = (r * e_pos).astype(BF16)
    kt_ref[0] = (kp * e_neg).astype(BF16)
    bt_ref[0] = (kk * a * e_neg).astype(BF16)
    at_ref[0] = (-kk * jnp.exp(cum - logw)).astype(BF16)
    v_ref[0] = v.astype(BF16)
    wc_ref[0] = jnp.concatenate([e_pos[(ci + 1) * c - 1:(ci + 1) * c, :] for ci in range(rows // c)], axis=0)


def _rwkv_prep(p_r, batch, mu, w0, w2, a0, a2, g2, k_k, k_a, r_k):
    m = p_r.shape[0]
    t = m // batch
    rows = RWKV_PREP_ROWS
    nt = t // rows
    d = D_RWKV
    head = np.arange(d) // HEAD_DIM
    bd = jnp.asarray(head[:, None] == head[None, :], BF16)
    tri = jnp.asarray(np.tril(np.ones((RWKV_CHUNK, RWKV_CHUNK))), BF16)
    vec = lambda n: pl.BlockSpec((1, n), lambda b, i: (0, 0))
    full = lambda a: pl.BlockSpec(a.shape, lambda b, i: (0, 0))
    wide = pl.BlockSpec((1, rows, d), lambda b, i: (b, i, 0))
    wide_f32 = jax.ShapeDtypeStruct((batch, t, d), F32)
    wide_bf16 = jax.ShapeDtypeStruct((batch, t, d), BF16)
    outs = pl.pallas_call(
        _rwkv_prep_kernel,
        grid=(batch, nt),
        in_specs=[pl.BlockSpec((rows, N_RWKV_COLS), lambda b, i: (b * nt + i, 0)),
                  vec(N_RWKV_COLS), vec(d), full(w2), vec(d), full(a2), full(g2), vec(d), vec(d), vec(d),
                  full(bd), full(tri)],
        out_specs=[wide, wide, wide, wide, wide,
                   pl.BlockSpec((1, rows // RWKV_CHUNK, d), lambda b, i: (b, i, 0)),
                   wide, wide],
        out_shape=[wide_bf16, wide_bf16, wide_bf16, wide_bf16, wide_bf16,
                   jax.ShapeDtypeStruct((batch, t // RWKV_CHUNK, d), F32),
                   wide_f32, wide_f32],
        scratch_shapes=[pltpu.VMEM((8, N_RWKV_COLS), F32)],
        compiler_params=_params("arbitrary", "arbitrary"),
        name="rwkv_prep",
    )(p_r, mu, w0, w2, a0, a2, g2, k_k, k_a, r_k, bd, tri)
    return outs


def _rwkv_scan_kernel(rt_ref, kt_ref, bt_ref, at_ref, v_ref, wc_ref, bonus_ref, g_ref, lnw_ref, lnb_ref,
                      y_ref, s_ref):
    nb, c, _ = rt_ref.shape
    nh, dk = RWKV_HEADS, HEAD_DIM
    n = nb * nh

    @pl.when(pl.program_id(0) == 0)
    def _():
        s_ref[...] = jnp.zeros_like(s_ref)

    row = lax.broadcasted_iota(jnp.int32, (1, c, c), 1)
    col = lax.broadcasted_iota(jnp.int32, (1, c, c), 2)
    strict = col < row
    incl = col <= row
    eye = (row == col).astype(F32)

    def load(ref):
        x = ref[...]
        return jnp.stack([x[b, :, h * dk:(h + 1) * dk] for b in range(nb) for h in range(nh)], axis=0)

    vb = load(v_ref)
    ar = jnp.concatenate([load(at_ref), load(rt_ref)], axis=1)
    bk = jnp.concatenate([load(bt_ref), load(kt_ref)], axis=1)
    amat = _bdot("nik,njk->nij", ar, bk)
    n_ab = jnp.where(strict, amat[:, :c, :c], 0.0)
    a_ak = jnp.where(strict, amat[:, :c, c:], 0.0)
    a_rb = jnp.where(incl, amat[:, c:, :c], 0.0)
    a_rk = jnp.where(incl, amat[:, c:, c:], 0.0)
    s0 = s_ref[...]
    ar_s = _bdot("nik,nvk->niv", ar, s0.astype(BF16))
    rhs = ar_s[:, :c] + _bdot("nij,njv->niv", a_ak.astype(BF16), vb)
    inv = eye + n_ab
    pw = n_ab
    for _ in range(int(np.log2(c)) - 1):
        pwb = pw.astype(BF16)
        pw = _bdot("nij,njk->nik", pwb, pwb)
        inv = inv + _bdot("nij,njk->nik", pw.astype(BF16), inv.astype(BF16))
    ub = _bdot("nij,njv->niv", inv.astype(BF16), rhs.astype(BF16)).astype(BF16)
    y = ar_s[:, c:] + _bdot("nij,njv->niv", a_rb.astype(BF16), ub) + _bdot("nij,njv->niv", a_rk.astype(BF16), vb)
    uv_t = jnp.swapaxes(jnp.concatenate([ub, vb], axis=1), 1, 2)
    s_ref[...] = (s0 + _bdot("nvi,nik->nvk", uv_t, bk)) * load(wc_ref.at[:, 0])

    mean = jnp.mean(y, axis=-1, keepdims=True)
    var = jnp.mean(jnp.square(y - mean), axis=-1, keepdims=True)
    yn = (y - mean) * lax.rsqrt(var + GN_EPS)
    for b in range(nb):
        wide = jnp.concatenate([yn[b * nh + h] for h in range(nh)], axis=-1)
        y_ref[b] = ((wide * lnw_ref[...] + lnb_ref[...] + bonus_ref[b]) * g_ref[b]).astype(y_ref.dtype)


def _rwkv_scan(rt, kt, bt, at, v, wc, bonus, g, lnw, lnb):
    batch, t, d = rt.shape
    c = RWKV_CHUNK
    wide = pl.BlockSpec((batch, c, d), lambda i: (0, i, 0))
    vec = pl.BlockSpec((1, d), lambda i: (0, 0))
    return pl.pallas_call(
        _rwkv_scan_kernel,
        grid=(t // c,),
        in_specs=[wide, wide, wide, wide, wide,
                  pl.BlockSpec((batch, 1, 1, d), lambda i: (0, i, 0, 0)),
                  wide, wide, vec, vec],
        out_specs=wide,
        out_shape=jax.ShapeDtypeStruct((batch, t, d), BF16),
        scratch_shapes=[pltpu.VMEM((batch * RWKV_HEADS, HEAD_DIM, HEAD_DIM), F32)],
        compiler_params=_params("arbitrary"),
        name="rwkv_scan",
    )(rt, kt, bt, at, v, wc.reshape(batch, t // c, 1, d), bonus, g, lnw, lnb)


def _compress_kernel(zk_ref, zv_ref, pek_ref, pev_ref, k1_ref, k2_ref, v1_ref, v2_ref, ko_ref, vo_ref):
    def one(z_ref, pe_ref, w1_ref, w2_ref):
        n16 = z_ref.shape[2] // CMP_STRIDE
        dk = HEAD_DIM
        first = second = None
        for l in range(CMP_STRIDE):
            z = z_ref[0, 0, pl.ds(l, n16, stride=CMP_STRIDE), :]
            lo = _dot((z + pe_ref[l:l + 1, :]).astype(BF16), w1_ref[l * dk:(l + 1) * dk, :])
            u = CMP_STRIDE + l
            hi = _dot((z + pe_ref[u:u + 1, :]).astype(BF16), w1_ref[u * dk:(u + 1) * dk, :])
            first = lo if first is None else first + lo
            second = hi if second is None else second + hi
        hidden = first + pltpu.roll(second, n16 - 1, axis=0)
        return _dot(jax.nn.gelu(hidden).astype(BF16), w2_ref[...])

    kc = one(zk_ref, pek_ref, k1_ref, k2_ref)
    blk = lax.broadcasted_iota(jnp.int32, kc.shape, 0)
    col = lax.broadcasted_iota(jnp.int32, kc.shape, 1)
    per = SEL_BLOCK // CMP_STRIDE
    hi = blk // per
    lo = CMP_STRIDE * (blk % per) + (CMP_BLOCK - 1)
    feat = _position_features(hi, lo, col)
    ko_ref[0] = jnp.concatenate([kc, feat], axis=1).astype(BF16)
    vc = one(zv_ref, pev_ref, v1_ref, v2_ref)
    vc_t = jnp.transpose(jnp.concatenate([vc, jnp.zeros_like(vc)], axis=1))
    vo_ref[0] = vc_t[:HEAD_DIM].astype(BF16)


def _compress(zk, zv, pek, pev, k1, k2, v1, v2):
    batch, hk, t, dk = zk.shape
    n16 = t // CMP_STRIDE
    zspec = pl.BlockSpec((1, 1, t, dk), lambda b, h: (b, h, 0, 0))
    full = lambda a: pl.BlockSpec(a.shape, lambda b, h: (0,) * a.ndim)
    return pl.pallas_call(
        _compress_kernel,
        grid=(batch, hk),
        in_specs=[zspec, zspec, full(pek), full(pev), full(k1), full(k2), full(v1), full(v2)],
        out_specs=[pl.BlockSpec((1, n16, 2 * dk), lambda b, h: (b * hk + h, 0, 0)),
                   pl.BlockSpec((1, dk, n16), lambda b, h: (b * hk + h, 0, 0))],
        out_shape=[jax.ShapeDtypeStruct((batch * hk, n16, 2 * dk), BF16),
                   jax.ShapeDtypeStruct((batch * hk, dk, n16), BF16)],
        compiler_params=_params("arbitrary", "arbitrary"),
        name="nsa_compress",
    )(zk, zv, pek, pev, k1, k2, v1, v2)


def _query_features(qt, slopes_ref, hk):
    lane_g = lax.broadcasted_iota(jnp.int32, (1, QG), 1) // Q_BLOCK
    slope = jnp.zeros((1, QG), F32)
    for g in range(NSA_GROUP):
        slope = jnp.where(lane_g == g, slopes_ref[hk * NSA_GROUP + g], slope)
    row = lax.broadcasted_iota(jnp.int32, (HEAD_DIM, QG), 0)
    extra = jnp.zeros((HEAD_DIM, QG), F32)
    for n, term in enumerate(LOG2E_TERMS):
        extra = jnp.where(row == 2 * n, (SEL_BLOCK * term) * slope, jnp.where(row == 2 * n + 1, term * slope, extra))
    return jnp.concatenate([qt * (HEAD_DIM ** -0.5 * LOG2E), extra], axis=0).astype(BF16)


def _position_features(hi, lo, col):
    return jnp.where(col < 2 * len(LOG2E_TERMS), jnp.where(col % 2 == 0, hi, lo), 0).astype(F32)


def _cmp_select_kernel(slopes_ref, qt_ref, kc_ref, vct_ref, ovt_ref, oct_ref, selt_ref):
    hk = pl.program_id(1)
    nblk = qt_ref.shape[2]
    width = nblk * Q_BLOCK
    ncp = kc_ref.shape[1]
    cmp_end = lax.broadcasted_iota(jnp.int32, (ncp, Q_BLOCK), 0) * CMP_STRIDE + (CMP_BLOCK - 1)
    imps = []
    for u in range(nblk):
        q0 = (pl.program_id(2) * nblk + u) * Q_BLOCK
        q_aug = _query_features(qt_ref[0, 0, u], slopes_ref, hk)
        s = _dot(kc_ref[0], q_aug)
        ok = cmp_end <= q0 + lax.broadcasted_iota(jnp.int32, (ncp, Q_BLOCK), 1)
        any_ok = (q0 + lax.broadcasted_iota(jnp.int32, (1, Q_BLOCK), 1) >= CMP_BLOCK - 1).astype(F32)
        p_sum = jnp.zeros((ncp, Q_BLOCK), F32)
        probs = []
        for g in range(NSA_GROUP):
            sg = jnp.where(ok, s[:, g * Q_BLOCK:(g + 1) * Q_BLOCK], NEG)
            e = jnp.exp2(sg - jnp.max(sg, axis=0, keepdims=True))
            p = e * (any_ok / jnp.sum(e, axis=0, keepdims=True))
            p_sum = p_sum + p
            probs.append(p.astype(BF16))
        oct_ref[0, 0, u] = _dot(vct_ref[0], jnp.concatenate(probs, axis=1))
        imps.append(_dot_exact_lhs(ovt_ref[...], p_sum))

    imp = jnp.concatenate(imps, axis=1)
    blk = lax.broadcasted_iota(jnp.int32, (LANES, width), 0)
    cur = (pl.program_id(2) * width + lax.broadcasted_iota(jnp.int32, (LANES, width), 1)) // SEL_BLOCK
    valid = blk <= cur
    forced = (blk == 0) | (blk == cur) | (blk == cur - 1)
    x = jnp.where(valid, jnp.where(forced, BIG, imp), NEG)
    blk_f = blk.astype(F32)
    chosen = jnp.zeros((LANES, width), jnp.bool_)
    for _ in range(SEL_TOPK):
        m = jnp.max(x, axis=0, keepdims=True)
        first = jnp.min(jnp.where(x == m, blk_f, float(LANES)), axis=0, keepdims=True)
        hit = blk_f == first
        chosen = chosen | hit
        x = jnp.where(hit, -jnp.inf, x)
    sel = (chosen & valid).astype(F32)
    for u in range(nblk):
        selt_ref[0, 0, u] = sel[:, u * Q_BLOCK:(u + 1) * Q_BLOCK]


def _cmp_select(slopes, q_t, kc_aug, vc_t, ov_t):
    batch, hk, nq, dk, qg = q_t.shape
    ncp = kc_aug.shape[1]
    nblk = CMP_Q_BLOCKS
    return pl.pallas_call(
        _cmp_select_kernel,
        grid=(batch, hk, nq // nblk),
        in_specs=[pl.BlockSpec(memory_space=pltpu.SMEM),
                  pl.BlockSpec((1, 1, nblk, dk, qg), lambda b, h, i: (b, h, i, 0, 0)),
                  pl.BlockSpec((1, ncp, 2 * dk), lambda b, h, i: (b * hk + h, 0, 0)),
                  pl.BlockSpec((1, dk, ncp), lambda b, h, i: (b * hk + h, 0, 0)),
                  pl.BlockSpec(ov_t.shape, lambda b, h, i: (0, 0))],
        out_specs=[pl.BlockSpec((1, 1, nblk, dk, qg), lambda b, h, i: (b, h, i, 0, 0)),
                   pl.BlockSpec((1, 1, nblk, LANES, Q_BLOCK), lambda b, h, i: (b, h, i, 0, 0))],
        out_shape=[jax.ShapeDtypeStruct(q_t.shape, F32),
                   jax.ShapeDtypeStruct((batch, hk, nq, LANES, Q_BLOCK), F32)],
        compiler_params=_params("arbitrary", "arbitrary", "arbitrary"),
        name="nsa_cmp_select",
    )(slopes, q_t, kc_aug, vc_t, ov_t)


def _sel_win_kernel(ids_ref, cnt_ref, slopes_ref, qt_ref, selt_ref, ks_ref, vst_ref, kw_ref, vwt_ref,
                    oct_ref, glt_ref, yt_ref):
    b, hk, i = pl.program_id(0), pl.program_id(1), pl.program_id(2)
    step = (b * pl.num_programs(1) + hk) * pl.num_programs(2) + i
    q0 = i * Q_BLOCK
    kt = KEY_TILE
    q_aug = _query_features(qt_ref[0, 0, 0], slopes_ref, hk)
    lane_minus_row = (lax.broadcasted_iota(jnp.int32, (kt, Q_BLOCK), 1)
                      - lax.broadcasted_iota(jnp.int32, (kt, Q_BLOCK), 0))

    def attend(state, tiles):
        m_old, acc_old = state
        scores = [_dot(k_tile, q_aug) for k_tile, _, _ in tiles]
        probs, maxes = [[] for _ in tiles], []
        for g in range(NSA_GROUP):
            gsl = slice(g * Q_BLOCK, (g + 1) * Q_BLOCK)
            masked = [jnp.where(mask, s[:, gsl], NEG) for s, (_, _, mask) in zip(scores, tiles)]
            top = jnp.max(masked[0].reshape(kt // 8, 8, Q_BLOCK), axis=0)
            for sg in masked[1:]:
                top = jnp.maximum(top, jnp.max(sg.reshape(kt // 8, 8, Q_BLOCK), axis=0))
            mg = jnp.maximum(m_old[:, gsl], jnp.max(top, axis=0, keepdims=True))
            for n, sg in enumerate(masked):
                probs[n].append(jnp.exp2(sg - mg).astype(BF16))
            maxes.append(mg)
        m_new = jnp.concatenate(maxes, axis=1)
        update = _dot(tiles[0][1], jnp.concatenate(probs[0], axis=1))
        for n in range(1, len(tiles)):
            update = update + _dot(tiles[n][1], jnp.concatenate(probs[n], axis=1))
        return m_new, jnp.exp2(m_old - m_new) * acc_old + update

    start = (jnp.full((1, QG), SOFTMAX_FLOOR, F32), jnp.zeros((V_ROWS, QG), F32))
    result = lambda state: state[1][:HEAD_DIM] / state[1][HEAD_DIM:HEAD_DIM + 1]

    count = cnt_ref[step]
    last_tile = ks_ref.shape[2] // kt - 1

    def sel_tile(n):
        j = jnp.minimum(ids_ref[step * MAX_KEY_TILES + jnp.minimum(n, MAX_KEY_TILES - 1)], last_tile)
        k0 = pl.multiple_of(j * kt, kt)
        per = kt // SEL_BLOCK
        picked = jnp.concatenate(
            [jnp.broadcast_to(selt_ref[0, 0, 0, pl.ds(j * per + u, 1), :], (SEL_BLOCK, Q_BLOCK)) for u in range(per)],
            axis=0)
        causal_from = jnp.where(n < count, k0 - q0, 1 << 30)
        mask = (picked > 0.5) & (lane_minus_row >= causal_from)
        return ks_ref[0, 0, pl.ds(k0, kt), :], vst_ref[0, 0, j], mask

    last = (q0 + Q_BLOCK - 1) // kt

    def win_tile(u):
        j = last - u
        jc = jnp.maximum(j, 0)
        k0 = pl.multiple_of(jc * kt, kt)
        dist = lane_minus_row + jnp.where(j >= 0, q0 - k0, -(1 << 30))
        mask = (dist >= 0) & (dist < WINDOW)
        return kw_ref[0, 0, pl.ds(k0, kt), :], vwt_ref[0, 0, jc], mask

    state_w = attend(start, [win_tile(u) for u in range(WIN_TILES)])
    state_s = attend(start, [sel_tile(n) for n in range(SEL_FIRST)])
    state_s = lax.fori_loop(
        0, (jnp.maximum(count - SEL_FIRST, 0) + SEL_GROUP - 1) // SEL_GROUP,
        lambda n, st: attend(st, [sel_tile(SEL_FIRST + SEL_GROUP * n + u) for u in range(SEL_GROUP)]), state_s)
    o_s, o_w = result(state_s), result(state_w)

    gates = jax.nn.sigmoid(glt_ref[0, 0])
    o_c = oct_ref[0, 0, 0]
    outs = []
    for g in range(NSA_GROUP):
        gsl = slice(g * Q_BLOCK, (g + 1) * Q_BLOCK)
        outs.append(gates[3 * g:3 * g + 1] * o_c[:, gsl] + gates[3 * g + 1:3 * g + 2] * o_s[:, gsl]
                    + gates[3 * g + 2:3 * g + 3] * o_w[:, gsl])
    yt_ref[0] = jnp.concatenate(outs, axis=0).astype(yt_ref.dtype)


def _sel_win(tile_ids, tile_cnt, slopes, q_t, sel_t, ks, vs_t, kw, vw_t, oc_t, gl_t):
    batch, hk, nq, dk, qg = q_t.shape
    t = nq * Q_BLOCK
    nkt = t // KEY_TILE
    grp = qg // Q_BLOCK
    qspec = pl.BlockSpec((1, 1, 1, dk, qg), lambda b, h, i, *_: (b, h, i, 0, 0))
    kspec = pl.BlockSpec((1, 1, t, 2 * dk), lambda b, h, i, *_: (b, h, 0, 0))
    vspec = pl.BlockSpec((1, 1, nkt, V_ROWS, KEY_TILE), lambda b, h, i, *_: (b, h, 0, 0, 0))
    grid_spec = pltpu.PrefetchScalarGridSpec(
        num_scalar_prefetch=2,
        grid=(batch, hk, nq),
        in_specs=[pl.BlockSpec(memory_space=pltpu.SMEM), qspec,
                  pl.BlockSpec((1, 1, 1, LANES, Q_BLOCK), lambda b, h, i, *_: (b, h, i, 0, 0)),
                  kspec, vspec, kspec, vspec, qspec,
                  pl.BlockSpec((1, 1, 3 * grp, Q_BLOCK), lambda b, h, i, *_: (b, h, 0, i))],
        out_specs=pl.BlockSpec((1, grp * dk, Q_BLOCK), lambda b, h, i, *_: (b, h, i)),
    )
    return pl.pallas_call(
        _sel_win_kernel,
        grid_spec=grid_spec,
        out_shape=jax.ShapeDtypeStruct((batch, hk * grp * dk, t), BF16),
        compiler_params=_params("arbitrary", "arbitrary", "arbitrary"),
        name="nsa_sel_win",
    )(tile_ids, tile_cnt, slopes, q_t, sel_t, ks, vs_t, kw, vw_t, oc_t, gl_t)


def _ffn_kernel(x_ref, yr_ref, ynt_ref, wo1_ref, wo2_ref, g2_ref, wg_ref, wu_ref, wd_ref, gf_ref, o_ref):
    h1 = (x_ref[...] + _dot(yr_ref[...].astype(BF16), wo1_ref[...])
          + _dot_tn(ynt_ref[0].astype(BF16), wo2_ref[...]))
    hn = _rms(h1, g2_ref[...]).astype(BF16)
    gate = _dot(hn, wg_ref[...])
    up = _dot(hn, wu_ref[...])
    act = gate * jax.nn.sigmoid(gate) * up
    o_ref[...] = _rms(h1 + _dot(act.astype(BF16), wd_ref[...]), gf_ref[...])


def _ffn(x2, yr, yn_t, wo1, wo2, g2, wg, wu, wd, gf, tm=512):
    m, d = x2.shape
    _, dn, t = yn_t.shape
    per_seq = t // tm
    row = lambda n: pl.BlockSpec((tm, n), lambda i: (i, 0))
    const = lambda a: pl.BlockSpec(a.shape, lambda i: (0, 0), pipeline_mode=pl.Buffered(1))
    return pl.pallas_call(
        _ffn_kernel,
        grid=(m // tm,),
        in_specs=[row(d), row(yr.shape[1]),
                  pl.BlockSpec((1, dn, tm), lambda i: (i // per_seq, 0, i % per_seq)),
                  const(wo1), const(wo2), const(g2), const(wg), const(wu), const(wd), const(gf)],
        out_specs=row(d),
        out_shape=jax.ShapeDtypeStruct((m, d), F32),
        compiler_params=_params("arbitrary"),
        name="outproj_ffn",
    )(x2, yr, yn_t, wo1, wo2, g2, wg, wu, wd, gf)


def _overlap_matrix_t(t):
    n16 = t // CMP_STRIDE
    n_cmp = (t - CMP_BLOCK) // CMP_STRIDE + 1
    n_sel = t // SEL_BLOCK
    cmp_start = np.arange(n_cmp) * CMP_STRIDE
    sel_start = np.arange(n_sel) * SEL_BLOCK
    ov = np.clip(np.minimum(cmp_start[:, None] + CMP_BLOCK, sel_start[None, :] + SEL_BLOCK)
                 - np.maximum(cmp_start[:, None], sel_start[None, :]), 0, None) / CMP_STRIDE
    full = np.zeros((LANES, n16), np.float32)
    full[:n_sel, :n_cmp] = ov.T
    return jnp.asarray(full, BF16)


def kernel(x, norm1_g, w_in, mu_shift, rwkv_w0, rwkv_w2, rwkv_a0, rwkv_a2, rwkv_g2, rwkv_k_k, rwkv_k_a,
           rwkv_r_k, rwkv_lnx_w, rwkv_lnx_b, nsa_pe_k, nsa_pe_v, nsa_cmp_k_w1, nsa_cmp_k_w2, nsa_cmp_v_w1,
           nsa_cmp_v_w2, w_out, norm2_g, ffn_w_gate, ffn_w_up, ffn_w_down, norm_f_g):
    batch, t, d_model = x.shape
    assert w_in.shape[0] == 1, "the final RMSNorm is fused into the (single) layer's FFN kernel"
    assert t % RWKV_PREP_ROWS == 0 and t % KEY_TILE == 0 and t // SEL_BLOCK <= LANES
    hk, grp, dk = NSA_KV_HEADS, NSA_GROUP, HEAD_DIM
    nq = t // Q_BLOCK
    slopes = 2.0 ** (-8.0 * jnp.arange(1, NSA_Q_HEADS + 1, dtype=F32) / NSA_Q_HEADS)
    nsa_pad = _round_up(N_NSA_COLS, LANES)
    row = lambda a: a.reshape(1, -1)
    i = 0

    h = x.reshape(batch * t, d_model)
    w_r = w_in[i][:, :N_RWKV_COLS].astype(BF16)
    w_n = jnp.pad(w_in[i][:, N_RWKV_COLS:], ((0, 0), (0, nsa_pad - N_NSA_COLS))).astype(BF16)
    p_r, q_t, ks, kw, vs_t, vw_t, zk, zv, gl_t = _inproj(h, batch, row(norm1_g[i]), w_r, w_n)

    rt, kt, bt, at, v, wc, bonus, g = _rwkv_prep(
        p_r, batch, row(mu_shift[i]), row(rwkv_w0[i]), rwkv_w2[i].astype(BF16), row(rwkv_a0[i]),
        rwkv_a2[i].astype(BF16), rwkv_g2[i].astype(BF16), row(rwkv_k_k[i]), row(rwkv_k_a[i]),
        row(rwkv_r_k[i]))
    y_rwkv = _rwkv_scan(rt, kt, bt, at, v, wc, bonus, g, row(rwkv_lnx_w[i]), row(rwkv_lnx_b[i]))
    y_rwkv = y_rwkv.reshape(batch * t, D_RWKV)

    kc_aug, vc_t = _compress(zk, zv, nsa_pe_k[i], nsa_pe_v[i],
                             nsa_cmp_k_w1[i].astype(BF16), nsa_cmp_k_w2[i].astype(BF16),
                             nsa_cmp_v_w1[i].astype(BF16), nsa_cmp_v_w2[i].astype(BF16))
    oc_t, sel_t = _cmp_select(slopes, q_t, kc_aug, vc_t, _overlap_matrix_t(t))

    blocks_per_tile = KEY_TILE // SEL_BLOCK
    active = sel_t.reshape(batch, hk, nq, MAX_KEY_TILES, blocks_per_tile * Q_BLOCK).max(axis=-1) > 0
    tile_ids = jnp.argsort(jnp.logical_not(active), axis=-1, stable=True).astype(jnp.int32).reshape(-1)
    tile_cnt = active.sum(axis=-1).astype(jnp.int32).reshape(-1)
    y_nsa_t = _sel_win(tile_ids, tile_cnt, slopes, q_t, sel_t, ks, vs_t, kw, vw_t,
                       oc_t, gl_t)

    out = _ffn(h, y_rwkv, y_nsa_t, w_out[i][:D_RWKV].astype(BF16), w_out[i][D_RWKV:].astype(BF16),
               row(norm2_g[i]), ffn_w_gate[i].astype(BF16), ffn_w_up[i].astype(BF16),
               ffn_w_down[i].astype(BF16), row(norm_f_g))
    return out.reshape(batch, t, d_model)
```

```python
import ml_dtypes
import numpy as np
import jax
import jax.numpy as jnp
from jax import lax
from jax.experimental import pallas as pl
from jax.experimental.pallas import tpu as pltpu

F32 = jnp.float32
BF16 = jnp.bfloat16

HEAD_DIM = 64
RWKV_HEADS = 8
D_RWKV = RWKV_HEADS * HEAD_DIM
NSA_Q_HEADS = 8
NSA_KV_HEADS = 2
NSA_GROUP = NSA_Q_HEADS // NSA_KV_HEADS
D_NSA = NSA_Q_HEADS * HEAD_DIM
D_KV = NSA_KV_HEADS * HEAD_DIM
LORA_W, LORA_A, LORA_G = 64, 64, 128
N_RWKV_COLS = 3 * D_RWKV + LORA_W + LORA_A + LORA_G
N_NSA_COLS = D_NSA + 6 * D_KV + 3 * NSA_Q_HEADS
CMP_BLOCK, CMP_STRIDE = 32, 16
SEL_BLOCK, SEL_TOPK = 64, 16
WINDOW = 512
Q_BLOCK = 128
NORM_EPS = 1e-6
GN_EPS = 64e-5
NEG = -1e30
BIG = 1e30

LANES = 128
RWKV_CHUNK = 64
KEY_TILE = 128
INPROJ_ROWS = 512
MAX_KEY_TILES = LANES * SEL_BLOCK // KEY_TILE
WIN_TILES = (WINDOW + Q_BLOCK) // KEY_TILE
SEL_FIRST = 9
SEL_GROUP = 2
CMP_Q_BLOCKS = 2
CMP_ROW_CHUNK = 128
V_ROWS = HEAD_DIM + 16
SOFTMAX_FLOOR = -1e20
QG = NSA_GROUP * Q_BLOCK
VMEM_LIMIT = 56 * 1024 * 1024


def _bf16_terms(x, count):
    terms = []
    for _ in range(count):
        terms.append(float(np.asarray(x, ml_dtypes.bfloat16)))
        x = x - terms[-1]
    return tuple(terms)


LOG2E = float(np.log2(np.e))
LOG2E_TERMS = _bf16_terms(LOG2E, 3)


def _round_up(n, m):
    return -(-n // m) * m


def _dot(a, b):
    return jnp.dot(a, b, preferred_element_type=F32)


def _dot_tn(a, b):
    return lax.dot_general(a, b, (((0,), (0,)), ((), ())), preferred_element_type=F32)


def _split3(a):
    hi = a.astype(BF16)
    rest = a - hi.astype(F32)
    mid = rest.astype(BF16)
    return hi, mid, (rest - mid.astype(F32)).astype(BF16)


def _dot_exact_lhs(a, b):
    hi, mid, lo = _split3(b)
    return _dot(a, hi) + _dot(a, mid) + _dot(a, lo)


def _dot_exact_rhs(a, b):
    hi, mid, lo = _split3(a)
    return _dot(hi, b) + _dot(mid, b) + _dot(lo, b)


def _bdot(spec, a, b):
    return jnp.einsum(spec, a, b, preferred_element_type=F32)


def _rms(x, g):
    return x * lax.rsqrt(jnp.mean(x * x, axis=-1, keepdims=True) + NORM_EPS) * g


def _params(*sem):
    return pltpu.CompilerParams(dimension_semantics=sem, vmem_limit_bytes=VMEM_LIMIT)


def _inproj_kernel(x_ref, g_ref, wr_ref, wn_ref, pr_ref, qt_ref, ks_ref, kw_ref, vst_ref, vwt_ref,
                   kc_ref, vc_ref, glt_ref):
    xb = _rms(x_ref[...], g_ref[...]).astype(BF16)
    pr_ref[...] = _dot(xb, wr_ref[...])
    pn = _dot(xb, wn_ref[...])
    rows = pn.shape[0]
    dk, hk_n = HEAD_DIM, NSA_KV_HEADS
    group = lambda j: pn[:, D_NSA + j * D_KV:D_NSA + (j + 1) * D_KV]

    for half in range(rows // Q_BLOCK):
        q_tr = jnp.transpose(pn[half * Q_BLOCK:(half + 1) * Q_BLOCK, :D_NSA])
        for hk in range(hk_n):
            base = hk * NSA_GROUP * dk
            qt_ref[0, hk, half] = jnp.concatenate(
                [q_tr[base + g * dk:base + (g + 1) * dk, :] for g in range(NSA_GROUP)], axis=1)

    pos = pl.program_id(1) * rows + lax.broadcasted_iota(jnp.int32, (rows, dk), 0)
    col = lax.broadcasted_iota(jnp.int32, (rows, dk), 1)
    pos_cols = _position_features(pos // SEL_BLOCK, pos % SEL_BLOCK, col)
    kt = KEY_TILE
    ones_rows = (lax.broadcasted_iota(jnp.int32, (V_ROWS - dk, kt), 0) == 0).astype(F32)
    kc, vc, ks, vs, kw, vw = (group(j) for j in range(6))
    vs_tr, vw_tr = jnp.transpose(vs), jnp.transpose(vw)
    gl_tr = jnp.transpose(pn[:, D_NSA + 6 * D_KV:])
    n_gate = 3 * NSA_GROUP
    for hk in range(hk_n):
        sl = slice(hk * dk, (hk + 1) * dk)
        kc_ref[0, hk] = kc[:, sl]
        vc_ref[0, hk] = vc[:, sl]
        ks_ref[0, hk] = jnp.concatenate([ks[:, sl], pos_cols], axis=1).astype(BF16)
        kw_ref[0, hk] = jnp.concatenate([kw[:, sl], pos_cols], axis=1).astype(BF16)
        for u in range(rows // kt):
            vst_ref[0, hk, u] = jnp.concatenate([vs_tr[sl, u * kt:(u + 1) * kt], ones_rows], axis=0).astype(BF16)
            vwt_ref[0, hk, u] = jnp.concatenate([vw_tr[sl, u * kt:(u + 1) * kt], ones_rows], axis=0).astype(BF16)
        glt_ref[0, hk] = gl_tr[hk * n_gate:(hk + 1) * n_gate, :]


def _inproj(x2, batch, g, w_r, w_n):
    m, d = x2.shape
    t = m // batch
    tm = INPROJ_ROWS
    nt = t // tm
    nr, nn = w_r.shape[1], w_n.shape[1]
    hk, dk = NSA_KV_HEADS, HEAD_DIM
    const = lambda a: pl.BlockSpec(a.shape, lambda b, i: (0, 0))
    keys = pl.BlockSpec((1, hk, tm, 2 * dk), lambda b, i: (b, 0, i, 0))
    vals = pl.BlockSpec((1, hk, tm // KEY_TILE, V_ROWS, KEY_TILE), lambda b, i: (b, 0, i, 0, 0))
    cmp_in = pl.BlockSpec((1, hk, tm, dk), lambda b, i: (b, 0, i, 0))
    keys_shape = jax.ShapeDtypeStruct((batch, hk, t, 2 * dk), BF16)
    vals_shape = jax.ShapeDtypeStruct((batch, hk, t // KEY_TILE, V_ROWS, KEY_TILE), BF16)
    cmp_shape = jax.ShapeDtypeStruct((batch, hk, t, dk), F32)
    return pl.pallas_call(
        _inproj_kernel,
        grid=(batch, nt),
        in_specs=[pl.BlockSpec((tm, d), lambda b, i: (b * nt + i, 0)), const(g), const(w_r), const(w_n)],
        out_specs=[pl.BlockSpec((tm, nr), lambda b, i: (b * nt + i, 0)),
                   pl.BlockSpec((1, hk, tm // Q_BLOCK, dk, QG), lambda b, i: (b, 0, i, 0, 0)),
                   keys, keys, vals, vals, cmp_in, cmp_in,
                   pl.BlockSpec((1, hk, 3 * NSA_GROUP, tm), lambda b, i: (b, 0, 0, i))],
        out_shape=[jax.ShapeDtypeStruct((m, nr), F32),
                   jax.ShapeDtypeStruct((batch, hk, t // Q_BLOCK, dk, QG), F32),
                   keys_shape, keys_shape, vals_shape, vals_shape, cmp_shape, cmp_shape,
                   jax.ShapeDtypeStruct((batch, hk, 3 * NSA_GROUP, t), F32)],
        compiler_params=_params("arbitrary", "arbitrary"),
        name="inproj",
    )(x2, g, w_r, w_n)


def _rwkv_operands(p, p_prev, mu, w0, w2, a0, a2, g2, k_k, k_a, r_k, bd, tri):
    c = RWKV_CHUNK
    ps = p + mu * (p_prev - p)
    d = D_RWKV
    r, k, v = ps[:, 0:d], ps[:, d:2 * d], ps[:, 2 * d:3 * d]
    dw = ps[:, 3 * d:3 * d + LORA_W]
    da = ps[:, 3 * d + LORA_W:3 * d + LORA_W + LORA_A]
    dg = ps[:, 3 * d + LORA_W + LORA_A:]

    z = -(w0 + _dot(jnp.tanh(dw).astype(BF16), w2))
    softplus = jnp.maximum(z, 0.0) + jnp.log1p(jnp.exp(-jnp.abs(z)))
    logw = -jnp.exp(-softplus - 0.5)
    a = jax.nn.sigmoid(a0 + _dot(da.astype(BF16), a2))
    g = _dot(jax.nn.sigmoid(dg).astype(BF16), g2)

    kk = k * k_k
    kk = kk / jnp.maximum(jnp.sqrt(_dot_exact_rhs(kk * kk, bd)), 1e-12)
    kp = k * (1.0 + (a - 1.0) * k_a)
    bonus = _dot_exact_rhs(r * kp * r_k, bd) * v

    n_chunks = p.shape[0] // c
    cum = jnp.concatenate([_dot_exact_lhs(tri, logw[ci * c:(ci + 1) * c]) for ci in range(n_chunks)], axis=0)
    e_pos, e_neg = jnp.exp(cum), jnp.exp(-cum)
    wc = jnp.concatenate([e_pos[(ci + 1) * c - 1:(ci + 1) * c, :] for ci in range(n_chunks)], axis=0)
    ops = (r * e_pos, kp * e_neg, kk * a * e_neg, -kk * jnp.exp(cum - logw), v)
    return tuple(o.astype(BF16) for o in ops), wc, bonus, g


def _rwkv_kernel(pc_ref, pn_ref, mu_ref, w0_ref, w2_ref, a0_ref, a2_ref, g2_ref, kk_ref, ka_ref, rk_ref,
                 bd_ref, tri_ref, lnw_ref, lnb_ref, y_ref, s_ref, ops_a, aux_a, wc_a, ops_b, aux_b, wc_b):
    i = pl.program_id(0)
    nb, c, width = pc_ref.shape
    nh, dk = RWKV_HEADS, HEAD_DIM
    n = nb * nh
    rows = nb * c

    def prepare(dst, p3, prev_last):
        ops_ref, aux_ref, wc_ref = dst
        p = p3.reshape(rows, width)
        row = lax.broadcasted_iota(jnp.int32, p.shape, 0)
        p_prev = pltpu.roll(p, 1, axis=0)
        for b in range(nb):
            p_prev = jnp.where(row == b * c, prev_last[b], p_prev)
        ops, wc, bonus, g = _rwkv_operands(
            p, p_prev, mu_ref[...], w0_ref[...], w2_ref[...], a0_ref[...], a2_ref[...], g2_ref[...],
            kk_ref[...], ka_ref[...], rk_ref[...], bd_ref[...], tri_ref[...])
        for idx, o in enumerate(ops):
            ops_ref[idx] = o
        aux_ref[0] = bonus
        aux_ref[1] = g
        wc_ref[0:nb] = wc

    slot_a, slot_b = (ops_a, aux_a, wc_a), (ops_b, aux_b, wc_b)
    step = lambda src, dst: _rwkv_step(src, dst, prepare, pc_ref, pn_ref, lnw_ref, lnb_ref, y_ref, s_ref)

    @pl.when(i == 0)
    def _():
        s_ref[...] = jnp.zeros_like(s_ref)
        prepare(slot_a, pc_ref[...], [jnp.zeros((1, width), F32)] * nb)

    @pl.when(i % 2 == 0)
    def _():
        step(slot_a, slot_b)

    @pl.when(i % 2 == 1)
    def _():
        step(slot_b, slot_a)


def _rwkv_step(src, dst, prepare, pc_ref, pn_ref, lnw_ref, lnb_ref, y_ref, s_ref):
    ops_ref, aux_ref, wc_ref = src
    nb, c, width = pc_ref.shape
    nh, dk = RWKV_HEADS, HEAD_DIM
    n = nb * nh
    cur = pc_ref[...]
    prepare(dst, pn_ref[...], [cur[b, c - 1:c, :] for b in range(nb)])

    row = lax.broadcasted_iota(jnp.int32, (1, c, c), 1)
    col = lax.broadcasted_iota(jnp.int32, (1, c, c), 2)
    strict = col < row
    incl = col <= row
    eye = (row == col).astype(F32)

    def heads(x, rows_per):
        return jnp.stack([x[b * rows_per:(b + 1) * rows_per, h * dk:(h + 1) * dk]
                          for b in range(nb) for h in range(nh)], axis=0)

    rt, kt, bt, at, vb = (heads(ops_ref[idx], c) for idx in range(5))
    ar = jnp.concatenate([at, rt], axis=1)
    bk = jnp.concatenate([bt, kt], axis=1)
    amat = _bdot("nik,njk->nij", ar, bk)
    n_ab = jnp.where(strict, amat[:, :c, :c], 0.0)
    a_ak = jnp.where(strict, amat[:, :c, c:], 0.0)
    a_rb = jnp.where(incl, amat[:, c:, :c], 0.0)
    a_rk = jnp.where(incl, amat[:, c:, c:], 0.0)
    s0 = s_ref[...]
    ar_s = _bdot("nik,nvk->niv", ar, s0.astype(BF16))
    rhs = ar_s[:, :c] + _bdot("nij,njv->niv", a_ak.astype(BF16), vb)
    inv = eye + n_ab
    pw = n_ab
    for _ in range(int(np.log2(c)) - 1):
        pwb = pw.astype(BF16)
        pw = _bdot("nij,njk->nik", pwb, pwb)
        inv = inv + _bdot("nij,njk->nik", pw.astype(BF16), inv.astype(BF16))
    ub = _bdot("nij,njv->niv", inv.astype(BF16), rhs.astype(BF16)).astype(BF16)
    y = ar_s[:, c:] + _bdot("nij,njv->niv", a_rb.astype(BF16), ub) + _bdot("nij,njv->niv", a_rk.astype(BF16), vb)
    uv_t = jnp.swapaxes(jnp.concatenate([ub, vb], axis=1), 1, 2)
    s_ref[...] = (s0 + _bdot("nvi,nik->nvk", uv_t, bk)) * heads(wc_ref[0:nb], 1)

    mean = jnp.mean(y, axis=-1, keepdims=True)
    var = jnp.mean(jnp.square(y - mean), axis=-1, keepdims=True)
    yn = (y - mean) * lax.rsqrt(var + GN_EPS)
    bonus, g = aux_ref[0], aux_ref[1]
    for b in range(nb):
        wide = jnp.concatenate([yn[b * nh + h] for h in range(nh)], axis=-1)
        rsl = slice(b * c, (b + 1) * c)
        y_ref[b] = ((wide * lnw_ref[...] + lnb_ref[...] + bonus[rsl]) * g[rsl]).astype(y_ref.dtype)


def _rwkv(p_r, mu, w0, w2, a0, a2, g2, k_k, k_a, r_k, lnw, lnb):
    batch, t, width = p_r.shape
    c = RWKV_CHUNK
    nc = t // c
    d = D_RWKV
    head = np.arange(d) // HEAD_DIM
    bd = jnp.asarray(head[:, None] == head[None, :], BF16)
    tri = jnp.asarray(np.tril(np.ones((c, c))), BF16)
    const = lambda a: pl.BlockSpec(a.shape, lambda i: (0, 0))
    consts = (mu, w0, w2, a0, a2, g2, k_k, k_a, r_k, bd, tri, lnw, lnb)
    return pl.pallas_call(
        _rwkv_kernel,
        grid=(nc,),
        in_specs=[pl.BlockSpec((batch, c, width), lambda i: (0, i, 0)),
                  pl.BlockSpec((batch, c, width), lambda i: (0, jnp.minimum(i + 1, nc - 1), 0))]
                 + [const(a) for a in consts],
        out_specs=pl.BlockSpec((batch, c, d), lambda i: (0, i, 0)),
        out_shape=jax.ShapeDtypeStruct((batch, t, d), BF16),
        scratch_shapes=[pltpu.VMEM((batch * RWKV_HEADS, HEAD_DIM, HEAD_DIM), F32),
                        *([pltpu.VMEM((5, batch * c, d), BF16), pltpu.VMEM((2, batch * c, d), F32),
                           pltpu.VMEM((8, d), F32)] * 2)],
        compiler_params=_params("arbitrary"),
        name="rwkv_mix",
    )(p_r, p_r, *consts)


def _compress_kernel(zk_ref, zv_ref, pek_ref, pev_ref, k1_ref, k2_ref, v1_ref, v2_ref, ko_ref, vo_ref):
    def one(z_ref, pe_ref, w1_ref, w2_ref):
        n16 = z_ref.shape[2] // CMP_STRIDE
        dk = HEAD_DIM
        first = second = None
        for l in range(CMP_STRIDE):
            z = z_ref[0, 0, pl.ds(l, n16, stride=CMP_STRIDE), :]
            lo = _dot((z + pe_ref[l:l + 1, :]).astype(BF16), w1_ref[l * dk:(l + 1) * dk, :])
            u = CMP_STRIDE + l
            hi = _dot((z + pe_ref[u:u + 1, :]).astype(BF16), w1_ref[u * dk:(u + 1) * dk, :])
            first = lo if first is None else first + lo
            second = hi if second is None else second + hi
        hidden = first + pltpu.roll(second, n16 - 1, axis=0)
        return _dot(jax.nn.gelu(hidden).astype(BF16), w2_ref[...])

    kc = one(zk_ref, pek_ref, k1_ref, k2_ref)
    blk = lax.broadcasted_iota(jnp.int32, kc.shape, 0)
    col = lax.broadcasted_iota(jnp.int32, kc.shape, 1)
    per = SEL_BLOCK // CMP_STRIDE
    hi = blk // per
    lo = CMP_STRIDE * (blk % per) + (CMP_BLOCK - 1)
    feat = _position_features(hi, lo, col)
    ko_ref[0] = jnp.concatenate([kc, feat], axis=1).astype(BF16)
    vc = one(zv_ref, pev_ref, v1_ref, v2_ref)
    vc_t = jnp.transpose(jnp.concatenate([vc, jnp.zeros_like(vc)], axis=1))
    vo_ref[0] = vc_t[:HEAD_DIM].astype(BF16)


def _compress(zk, zv, pek, pev, k1, k2, v1, v2):
    batch, hk, t, dk = zk.shape
    n16 = t // CMP_STRIDE
    zspec = pl.BlockSpec((1, 1, t, dk), lambda b, h: (b, h, 0, 0))
    full = lambda a: pl.BlockSpec(a.shape, lambda b, h: (0,) * a.ndim)
    return pl.pallas_call(
        _compress_kernel,
        grid=(batch, hk),
        in_specs=[zspec, zspec, full(pek), full(pev), full(k1), full(k2), full(v1), full(v2)],
        out_specs=[pl.BlockSpec((1, n16, 2 * dk), lambda b, h: (b * hk + h, 0, 0)),
                   pl.BlockSpec((1, dk, n16), lambda b, h: (b * hk + h, 0, 0))],
        out_shape=[jax.ShapeDtypeStruct((batch * hk, n16, 2 * dk), BF16),
                   jax.ShapeDtypeStruct((batch * hk, dk, n16), BF16)],
        compiler_params=_params("arbitrary", "arbitrary"),
        name="nsa_compress",
    )(zk, zv, pek, pev, k1, k2, v1, v2)


def _query_features(qt, slopes_ref, hk):
    lane_g = lax.broadcasted_iota(jnp.int32, (1, QG), 1) // Q_BLOCK
    slope = jnp.zeros((1, QG), F32)
    for g in range(NSA_GROUP):
        slope = jnp.where(lane_g == g, slopes_ref[hk * NSA_GROUP + g], slope)
    row = lax.broadcasted_iota(jnp.int32, (HEAD_DIM, QG), 0)
    extra = jnp.zeros((HEAD_DIM, QG), F32)
    for n, term in enumerate(LOG2E_TERMS):
        extra = jnp.where(row == 2 * n, (SEL_BLOCK * term) * slope, jnp.where(row == 2 * n + 1, term * slope, extra))
    return jnp.concatenate([qt * (HEAD_DIM ** -0.5 * LOG2E), extra], axis=0).astype(BF16)


def _position_features(hi, lo, col):
    return jnp.where(col < 2 * len(LOG2E_TERMS), jnp.where(col % 2 == 0, hi, lo), 0).astype(F32)


def _cmp_select_kernel(slopes_ref, qt_ref, kc_ref, vct_ref, ovt_ref, oct_ref, selt_ref, imp_ref):
    hk = pl.program_id(1)
    nblk = qt_ref.shape[2]
    width = nblk * Q_BLOCK
    ncp = kc_ref.shape[1]

    def attend(rows):
        cmp_end = lax.broadcasted_iota(jnp.int32, (rows, Q_BLOCK), 0) * CMP_STRIDE + (CMP_BLOCK - 1)
        for u in range(nblk):
            q0 = (pl.program_id(2) * nblk + u) * Q_BLOCK
            q_aug = _query_features(qt_ref[0, 0, u], slopes_ref, hk)
            s = _dot(kc_ref[0, 0:rows, :], q_aug)
            ok = cmp_end <= q0 + lax.broadcasted_iota(jnp.int32, (rows, Q_BLOCK), 1)
            any_ok = (q0 + lax.broadcasted_iota(jnp.int32, (1, Q_BLOCK), 1) >= CMP_BLOCK - 1).astype(F32)
            p_sum = jnp.zeros((rows, Q_BLOCK), F32)
            probs = []
            for g in range(NSA_GROUP):
                sg = jnp.where(ok, s[:, g * Q_BLOCK:(g + 1) * Q_BLOCK], NEG)
                e = jnp.exp2(sg - jnp.max(sg, axis=0, keepdims=True))
                p = e * (any_ok / jnp.sum(e, axis=0, keepdims=True))
                p_sum = p_sum + p
                probs.append(p.astype(BF16))
            oct_ref[0, 0, u] = _dot(vct_ref[0, :, 0:rows], jnp.concatenate(probs, axis=1))
            imp_ref[:, u * Q_BLOCK:(u + 1) * Q_BLOCK] = _dot_exact_lhs(ovt_ref[:, 0:rows], p_sum)

    chunk = min(CMP_ROW_CHUNK, ncp)
    n_chunks = ncp // chunk
    last_t = (pl.program_id(2) + 1) * width - 1
    needed = jnp.maximum((last_t - (CMP_BLOCK - 1)) // CMP_STRIDE + 1, 1)
    needed_chunks = jnp.minimum((needed + chunk - 1) // chunk, n_chunks)
    for nck in range(1, n_chunks + 1):
        pl.when(needed_chunks == nck)(lambda nck=nck: attend(nck * chunk))

    imp = imp_ref[...]
    blk = lax.broadcasted_iota(jnp.int32, (LANES, width), 0)
    cur = (pl.program_id(2) * width + lax.broadcasted_iota(jnp.int32, (LANES, width), 1)) // SEL_BLOCK
    valid = blk <= cur
    forced = (blk == 0) | (blk == cur) | (blk == cur - 1)
    x = jnp.where(valid, jnp.where(forced, BIG, imp), NEG)
    blk_f = blk.astype(F32)
    chosen = jnp.zeros((LANES, width), jnp.bool_)
    for _ in range(SEL_TOPK):
        m = jnp.max(x, axis=0, keepdims=True)
        first = jnp.min(jnp.where(x == m, blk_f, float(LANES)), axis=0, keepdims=True)
        hit = blk_f == first
        chosen = chosen | hit
        x = jnp.where(hit, -jnp.inf, x)
    sel = (chosen & valid).astype(F32)
    for u in range(nblk):
        selt_ref[0, 0, u] = sel[:, u * Q_BLOCK:(u + 1) * Q_BLOCK]


def _cmp_select(slopes, q_t, kc_aug, vc_t, ov_t):
    batch, hk, nq, dk, qg = q_t.shape
    ncp = kc_aug.shape[1]
    nblk = CMP_Q_BLOCKS
    return pl.pallas_call(
        _cmp_select_kernel,
        grid=(batch, hk, nq // nblk),
        in_specs=[pl.BlockSpec(memory_space=pltpu.SMEM),
                  pl.BlockSpec((1, 1, nblk, dk, qg), lambda b, h, i: (b, h, i, 0, 0)),
                  pl.BlockSpec((1, ncp, 2 * dk), lambda b, h, i: (b * hk + h, 0, 0)),
                  pl.BlockSpec((1, dk, ncp), lambda b, h, i: (b * hk + h, 0, 0)),
                  pl.BlockSpec(ov_t.shape, lambda b, h, i: (0, 0))],
        out_specs=[pl.BlockSpec((1, 1, nblk, dk, qg), lambda b, h, i: (b, h, i, 0, 0)),
                   pl.BlockSpec((1, 1, nblk, LANES, Q_BLOCK), lambda b, h, i: (b, h, i, 0, 0))],
        out_shape=[jax.ShapeDtypeStruct(q_t.shape, F32),
                   jax.ShapeDtypeStruct((batch, hk, nq, LANES, Q_BLOCK), F32)],
        scratch_shapes=[pltpu.VMEM((LANES, nblk * Q_BLOCK), F32)],
        compiler_params=_params("arbitrary", "arbitrary", "arbitrary"),
        name="nsa_cmp_select",
    )(slopes, q_t, kc_aug, vc_t, ov_t)


def _sel_win_kernel(ids_ref, cnt_ref, slopes_ref, qt_ref, selt_ref, ks_ref, vst_ref, kw_ref, vwt_ref,
                    oct_ref, glt_ref, yt_ref):
    b, hk, i = pl.program_id(0), pl.program_id(1), pl.program_id(2)
    step = (b * pl.num_programs(1) + hk) * pl.num_programs(2) + i
    q0 = i * Q_BLOCK
    kt = KEY_TILE
    q_aug = _query_features(qt_ref[0, 0, 0], slopes_ref, hk)
    lane_minus_row = (lax.broadcasted_iota(jnp.int32, (kt, Q_BLOCK), 1)
                      - lax.broadcasted_iota(jnp.int32, (kt, Q_BLOCK), 0))

    def attend(state, tiles):
        m_old, acc_old = state
        scores = [_dot(k_tile, q_aug) for k_tile, _, _ in tiles]
        probs, maxes = [[] for _ in tiles], []
        for g in range(NSA_GROUP):
            gsl = slice(g * Q_BLOCK, (g + 1) * Q_BLOCK)
            masked = [jnp.where(mask, s[:, gsl], NEG) for s, (_, _, mask) in zip(scores, tiles)]
            top = jnp.max(masked[0].reshape(kt // 8, 8, Q_BLOCK), axis=0)
            for sg in masked[1:]:
                top = jnp.maximum(top, jnp.max(sg.reshape(kt // 8, 8, Q_BLOCK), axis=0))
            mg = jnp.maximum(m_old[:, gsl], jnp.max(top, axis=0, keepdims=True))
            for n, sg in enumerate(masked):
                probs[n].append(jnp.exp2(sg - mg).astype(BF16))
            maxes.append(mg)
        m_new = jnp.concatenate(maxes, axis=1)
        update = _dot(tiles[0][1], jnp.concatenate(probs[0], axis=1))
        for n in range(1, len(tiles)):
            update = update + _dot(tiles[n][1], jnp.concatenate(probs[n], axis=1))
        return m_new, jnp.exp2(m_old - m_new) * acc_old + update

    start = (jnp.full((1, QG), SOFTMAX_FLOOR, F32), jnp.zeros((V_ROWS, QG), F32))
    result = lambda state: state[1][:HEAD_DIM] / state[1][HEAD_DIM:HEAD_DIM + 1]

    count = cnt_ref[step]
    last_tile = ks_ref.shape[2] // kt - 1

    def sel_tile(n):
        j = jnp.minimum(ids_ref[step * MAX_KEY_TILES + jnp.minimum(n, MAX_KEY_TILES - 1)], last_tile)
        k0 = pl.multiple_of(j * kt, kt)
        per = kt // SEL_BLOCK
        picked = jnp.concatenate(
            [jnp.broadcast_to(selt_ref[0, 0, 0, pl.ds(j * per + u, 1), :], (SEL_BLOCK, Q_BLOCK)) for u in range(per)],
            axis=0)
        causal_from = jnp.where(n < count, k0 - q0, 1 << 30)
        mask = (picked > 0.5) & (lane_minus_row >= causal_from)
        return ks_ref[0, 0, pl.ds(k0, kt), :], vst_ref[0, 0, j], mask

    last = (q0 + Q_BLOCK - 1) // kt

    def win_tile(u):
        j = last - u
        jc = jnp.maximum(j, 0)
        k0 = pl.multiple_of(jc * kt, kt)
        dist = lane_minus_row + jnp.where(j >= 0, q0 - k0, -(1 << 30))
        mask = (dist >= 0) & (dist < WINDOW)
        return kw_ref[0, 0, pl.ds(k0, kt), :], vwt_ref[0, 0, jc], mask

    state_w = attend(start, [win_tile(u) for u in range(WIN_TILES)])
    state_s = attend(start, [sel_tile(n) for n in range(SEL_FIRST)])
    state_s = lax.fori_loop(
        0, (jnp.maximum(count - SEL_FIRST, 0) + SEL_GROUP - 1) // SEL_GROUP,
        lambda n, st: attend(st, [sel_tile(SEL_FIRST + SEL_GROUP * n + u) for u in range(SEL_GROUP)]), state_s)
    o_s, o_w = result(state_s), result(state_w)

    gates = jax.nn.sigmoid(glt_ref[0, 0])
    o_c = oct_ref[0, 0, 0]
    outs = []
    for g in range(NSA_GROUP):
        gsl = slice(g * Q_BLOCK, (g + 1) * Q_BLOCK)
        outs.append(gates[3 * g:3 * g + 1] * o_c[:, gsl] + gates[3 * g + 1:3 * g + 2] * o_s[:, gsl]
                    + gates[3 * g + 2:3 * g + 3] * o_w[:, gsl])
    yt_ref[0] = jnp.concatenate(outs, axis=0).astype(yt_ref.dtype)


def _sel_win(tile_ids, tile_cnt, slopes, q_t, sel_t, ks, vs_t, kw, vw_t, oc_t, gl_t):
    batch, hk, nq, dk, qg = q_t.shape
    t = nq * Q_BLOCK
    nkt = t // KEY_TILE
    grp = qg // Q_BLOCK
    qspec = pl.BlockSpec((1, 1, 1, dk, qg), lambda b, h, i, *_: (b, h, i, 0, 0))
    kspec = pl.BlockSpec((1, 1, t, 2 * dk), lambda b, h, i, *_: (b, h, 0, 0))
    vspec = pl.BlockSpec((1, 1, nkt, V_ROWS, KEY_TILE), lambda b, h, i, *_: (b, h, 0, 0, 0))
    grid_spec = pltpu.PrefetchScalarGridSpec(
        num_scalar_prefetch=2,
        grid=(batch, hk, nq),
        in_specs=[pl.BlockSpec(memory_space=pltpu.SMEM), qspec,
                  pl.BlockSpec((1, 1, 1, LANES, Q_BLOCK), lambda b, h, i, *_: (b, h, i, 0, 0)),
                  kspec, vspec, kspec, vspec, qspec,
                  pl.BlockSpec((1, 1, 3 * grp, Q_BLOCK), lambda b, h, i, *_: (b, h, 0, i))],
        out_specs=pl.BlockSpec((1, grp * dk, Q_BLOCK), lambda b, h, i, *_: (b, h, i)),
    )
    return pl.pallas_call(
        _sel_win_kernel,
        grid_spec=grid_spec,
        out_shape=jax.ShapeDtypeStruct((batch, hk * grp * dk, t), BF16),
        compiler_params=_params("arbitrary", "arbitrary", "arbitrary"),
        name="nsa_sel_win",
    )(tile_ids, tile_cnt, slopes, q_t, sel_t, ks, vs_t, kw, vw_t, oc_t, gl_t)


def _ffn_kernel(x_ref, yr_ref, ynt_ref, wo1_ref, wo2_ref, g2_ref, wg_ref, wu_ref, wd_ref, gf_ref, o_ref):
    h1 = (x_ref[...] + _dot(yr_ref[...].astype(BF16), wo1_ref[...])
          + _dot_tn(ynt_ref[0].astype(BF16), wo2_ref[...]))
    hn = _rms(h1, g2_ref[...]).astype(BF16)
    gate = _dot(hn, wg_ref[...])
    up = _dot(hn, wu_ref[...])
    act = gate * jax.nn.sigmoid(gate) * up
    o_ref[...] = _rms(h1 + _dot(act.astype(BF16), wd_ref[...]), gf_ref[...])


def _ffn(x2, yr, yn_t, wo1, wo2, g2, wg, wu, wd, gf, tm=512):
    m, d = x2.shape
    _, dn, t = yn_t.shape
    per_seq = t // tm
    row = lambda n: pl.BlockSpec((tm, n), lambda i: (i, 0))
    const = lambda a: pl.BlockSpec(a.shape, lambda i: (0, 0), pipeline_mode=pl.Buffered(1))
    return pl.pallas_call(
        _ffn_kernel,
        grid=(m // tm,),
        in_specs=[row(d), row(yr.shape[1]),
                  pl.BlockSpec((1, dn, tm), lambda i: (i // per_seq, 0, i % per_seq)),
                  const(wo1), const(wo2), const(g2), const(wg), const(wu), const(wd), const(gf)],
        out_specs=row(d),
        out_shape=jax.ShapeDtypeStruct((m, d), F32),
        compiler_params=_params("arbitrary"),
        name="outproj_ffn",
    )(x2, yr, yn_t, wo1, wo2, g2, wg, wu, wd, gf)


def _overlap_matrix_t(t):
    n16 = t // CMP_STRIDE
    n_cmp = (t - CMP_BLOCK) // CMP_STRIDE + 1
    n_sel = t // SEL_BLOCK
    cmp_start = np.arange(n_cmp) * CMP_STRIDE
    sel_start = np.arange(n_sel) * SEL_BLOCK
    ov = np.clip(np.minimum(cmp_start[:, None] + CMP_BLOCK, sel_start[None, :] + SEL_BLOCK)
                 - np.maximum(cmp_start[:, None], sel_start[None, :]), 0, None) / CMP_STRIDE
    full = np.zeros((LANES, n16), np.float32)
    full[:n_sel, :n_cmp] = ov.T
    return jnp.asarray(full, BF16)


def kernel(x, norm1_g, w_in, mu_shift, rwkv_w0, rwkv_w2, rwkv_a0, rwkv_a2, rwkv_g2, rwkv_k_k, rwkv_k_a,
           rwkv_r_k, rwkv_lnx_w, rwkv_lnx_b, nsa_pe_k, nsa_pe_v, nsa_cmp_k_w1, nsa_cmp_k_w2, nsa_cmp_v_w1,
           nsa_cmp_v_w2, w_out, norm2_g, ffn_w_gate, ffn_w_up, ffn_w_down, norm_f_g):
    batch, t, d_model = x.shape
    assert w_in.shape[0] == 1, "the final RMSNorm is fused into the (single) layer's FFN kernel"
    assert t % INPROJ_ROWS == 0 and t // SEL_BLOCK <= LANES and batch <= 8
    hk, grp, dk = NSA_KV_HEADS, NSA_GROUP, HEAD_DIM
    nq = t // Q_BLOCK
    slopes = 2.0 ** (-8.0 * jnp.arange(1, NSA_Q_HEADS + 1, dtype=F32) / NSA_Q_HEADS)
    nsa_pad = _round_up(N_NSA_COLS, LANES)
    row = lambda a: a.reshape(1, -1)
    i = 0

    h = x.reshape(batch * t, d_model)
    w_r = w_in[i][:, :N_RWKV_COLS].astype(BF16)
    w_n = jnp.pad(w_in[i][:, N_RWKV_COLS:], ((0, 0), (0, nsa_pad - N_NSA_COLS))).astype(BF16)
    p_r, q_t, ks, kw, vs_t, vw_t, zk, zv, gl_t = _inproj(h, batch, row(norm1_g[i]), w_r, w_n)

    y_rwkv = _rwkv(p_r.reshape(batch, t, N_RWKV_COLS), row(mu_shift[i]), row(rwkv_w0[i]), rwkv_w2[i].astype(BF16),
                   row(rwkv_a0[i]), rwkv_a2[i].astype(BF16), rwkv_g2[i].astype(BF16), row(rwkv_k_k[i]),
                   row(rwkv_k_a[i]), row(rwkv_r_k[i]), row(rwkv_lnx_w[i]), row(rwkv_lnx_b[i]))
    y_rwkv = y_rwkv.reshape(batch * t, D_RWKV)

    kc_aug, vc_t = _compress(zk, zv, nsa_pe_k[i], nsa_pe_v[i],
                             nsa_cmp_k_w1[i].astype(BF16), nsa_cmp_k_w2[i].astype(BF16),
                             nsa_cmp_v_w1[i].astype(BF16), nsa_cmp_v_w2[i].astype(BF16))
    oc_t, sel_t = _cmp_select(slopes, q_t, kc_aug, vc_t, _overlap_matrix_t(t))

    blocks_per_tile = KEY_TILE // SEL_BLOCK
    active = sel_t.reshape(batch, hk, nq, MAX_KEY_TILES, blocks_per_tile * Q_BLOCK).max(axis=-1) > 0
    tile_ids = jnp.argsort(jnp.logical_not(active), axis=-1, stable=True).astype(jnp.int32).reshape(-1)
    tile_cnt = active.sum(axis=-1).astype(jnp.int32).reshape(-1)
    y_nsa_t = _sel_win(tile_ids, tile_cnt, slopes, q_t, sel_t, ks, vs_t, kw, vw_t, oc_t, gl_t)

    out = _ffn(h, y_rwkv, y_nsa_t, w_out[i][:D_RWKV].astype(BF16), w_out[i][D_RWKV:].astype(BF16),
               row(norm2_g[i]), ffn_w_gate[i].astype(BF16), ffn_w_up[i].astype(BF16),
               ffn_w_down[i].astype(BF16), row(norm_f_g))
    return out.reshape(batch, t, d_model)
```

```python
import ml_dtypes
import numpy as np
import jax
import jax.numpy as jnp
from jax import lax
from jax.experimental import pallas as pl
from jax.experimental.pallas import tpu as pltpu

F32 = jnp.float32
BF16 = jnp.bfloat16

HEAD_DIM = 64
RWKV_HEADS = 8
D_RWKV = RWKV_HEADS * HEAD_DIM
NSA_Q_HEADS = 8
NSA_KV_HEADS = 2
NSA_GROUP = NSA_Q_HEADS // NSA_KV_HEADS
D_NSA = NSA_Q_HEADS * HEAD_DIM
D_KV = NSA_KV_HEADS * HEAD_DIM
LORA_W, LORA_A, LORA_G = 64, 64, 128
N_RWKV_COLS = 3 * D_RWKV + LORA_W + LORA_A + LORA_G
N_NSA_COLS = D_NSA + 6 * D_KV + 3 * NSA_Q_HEADS
CMP_BLOCK, CMP_STRIDE = 32, 16
SEL_BLOCK, SEL_TOPK = 64, 16
WINDOW = 512
Q_BLOCK = 128
NORM_EPS = 1e-6
GN_EPS = 64e-5
NEG = -1e30
BIG = 1e30

LANES = 128
RWKV_CHUNK = 64
KEY_TILE = 128
INPROJ_ROWS = 512
MAX_KEY_TILES = LANES * SEL_BLOCK // KEY_TILE
WIN_TILES = (WINDOW + Q_BLOCK) // KEY_TILE
SEL_FIRST = 9
SEL_GROUP = 2
CMP_Q_BLOCKS = 4
CMP_ROW_CHUNK = 128
V_ROWS = HEAD_DIM + 16
SOFTMAX_FLOOR = -1e20
QG = NSA_GROUP * Q_BLOCK
VMEM_LIMIT = 56 * 1024 * 1024


def _bf16_terms(x, count):
    terms = []
    for _ in range(count):
        terms.append(float(np.asarray(x, ml_dtypes.bfloat16)))
        x = x - terms[-1]
    return tuple(terms)


LOG2E = float(np.log2(np.e))
LOG2E_TERMS = _bf16_terms(LOG2E, 3)


def _round_up(n, m):
    return -(-n // m) * m


def _dot(a, b):
    return jnp.dot(a, b, preferred_element_type=F32)


def _dot_tn(a, b):
    return lax.dot_general(a, b, (((0,), (0,)), ((), ())), preferred_element_type=F32)


def _split3(a):
    hi = a.astype(BF16)
    rest = a - hi.astype(F32)
    mid = rest.astype(BF16)
    return hi, mid, (rest - mid.astype(F32)).astype(BF16)


def _dot_exact_lhs(a, b):
    hi, mid, lo = _split3(b)
    return _dot(a, hi) + _dot(a, mid) + _dot(a, lo)


def _dot_exact_rhs(a, b):
    hi, mid, lo = _split3(a)
    return _dot(hi, b) + _dot(mid, b) + _dot(lo, b)


def _bdot(spec, a, b):
    return jnp.einsum(spec, a, b, preferred_element_type=F32)


def _rms(x, g):
    return x * lax.rsqrt(jnp.mean(x * x, axis=-1, keepdims=True) + NORM_EPS) * g


def _params(*sem):
    return pltpu.CompilerParams(dimension_semantics=sem, vmem_limit_bytes=VMEM_LIMIT)


def _inproj_kernel(x_ref, g_ref, wr_ref, wn_ref, pr_ref, qt_ref, ks_ref, kw_ref, vst_ref, vwt_ref,
                   kc_ref, vc_ref, glt_ref):
    xb = _rms(x_ref[...], g_ref[...]).astype(BF16)
    pr_ref[...] = _dot(xb, wr_ref[...])
    pn = _dot(xb, wn_ref[...])
    rows = pn.shape[0]
    dk, hk_n = HEAD_DIM, NSA_KV_HEADS
    group = lambda j: pn[:, D_NSA + j * D_KV:D_NSA + (j + 1) * D_KV]

    for half in range(rows // Q_BLOCK):
        q_tr = jnp.transpose(pn[half * Q_BLOCK:(half + 1) * Q_BLOCK, :D_NSA])
        for hk in range(hk_n):
            base = hk * NSA_GROUP * dk
            qt_ref[0, hk, half] = jnp.concatenate(
                [q_tr[base + g * dk:base + (g + 1) * dk, :] for g in range(NSA_GROUP)], axis=1)

    pos = pl.program_id(1) * rows + lax.broadcasted_iota(jnp.int32, (rows, dk), 0)
    col = lax.broadcasted_iota(jnp.int32, (rows, dk), 1)
    pos_cols = _position_features(pos // SEL_BLOCK, pos % SEL_BLOCK, col)
    kt = KEY_TILE
    ones_rows = (lax.broadcasted_iota(jnp.int32, (V_ROWS - dk, kt), 0) == 0).astype(F32)
    kc, vc, ks, vs, kw, vw = (group(j) for j in range(6))
    vs_tr, vw_tr = jnp.transpose(vs), jnp.transpose(vw)
    gl_tr = jnp.transpose(pn[:, D_NSA + 6 * D_KV:])
    n_gate = 3 * NSA_GROUP
    for hk in range(hk_n):
        sl = slice(hk * dk, (hk + 1) * dk)
        kc_ref[0, hk] = kc[:, sl]
        vc_ref[0, hk] = vc[:, sl]
        ks_ref[0, hk] = jnp.concatenate([ks[:, sl], pos_cols], axis=1).astype(BF16)
        kw_ref[0, hk] = jnp.concatenate([kw[:, sl], pos_cols], axis=1).astype(BF16)
        for u in range(rows // kt):
            vst_ref[0, hk, u] = jnp.concatenate([vs_tr[sl, u * kt:(u + 1) * kt], ones_rows], axis=0).astype(BF16)
            vwt_ref[0, hk, u] = jnp.concatenate([vw_tr[sl, u * kt:(u + 1) * kt], ones_rows], axis=0).astype(BF16)
        glt_ref[0, hk] = gl_tr[hk * n_gate:(hk + 1) * n_gate, :]


def _inproj(x2, batch, g, w_r, w_n):
    m, d = x2.shape
    t = m // batch
    tm = INPROJ_ROWS
    nt = t // tm
    nr, nn = w_r.shape[1], w_n.shape[1]
    hk, dk = NSA_KV_HEADS, HEAD_DIM
    const = lambda a: pl.BlockSpec(a.shape, lambda b, i: (0, 0))
    keys = pl.BlockSpec((1, hk, tm, 2 * dk), lambda b, i: (b, 0, i, 0))
    vals = pl.BlockSpec((1, hk, tm // KEY_TILE, V_ROWS, KEY_TILE), lambda b, i: (b, 0, i, 0, 0))
    cmp_in = pl.BlockSpec((1, hk, tm, dk), lambda b, i: (b, 0, i, 0))
    keys_shape = jax.ShapeDtypeStruct((batch, hk, t, 2 * dk), BF16)
    vals_shape = jax.ShapeDtypeStruct((batch, hk, t // KEY_TILE, V_ROWS, KEY_TILE), BF16)
    cmp_shape = jax.ShapeDtypeStruct((batch, hk, t, dk), F32)
    return pl.pallas_call(
        _inproj_kernel,
        grid=(batch, nt),
        in_specs=[pl.BlockSpec((tm, d), lambda b, i: (b * nt + i, 0)), const(g), const(w_r), const(w_n)],
        out_specs=[pl.BlockSpec((tm, nr), lambda b, i: (b * nt + i, 0)),
                   pl.BlockSpec((1, hk, tm // Q_BLOCK, dk, QG), lambda b, i: (b, 0, i, 0, 0)),
                   keys, keys, vals, vals, cmp_in, cmp_in,
                   pl.BlockSpec((1, hk, 3 * NSA_GROUP, tm), lambda b, i: (b, 0, 0, i))],
        out_shape=[jax.ShapeDtypeStruct((m, nr), F32),
                   jax.ShapeDtypeStruct((batch, hk, t // Q_BLOCK, dk, QG), F32),
                   keys_shape, keys_shape, vals_shape, vals_shape, cmp_shape, cmp_shape,
                   jax.ShapeDtypeStruct((batch, hk, 3 * NSA_GROUP, t), F32)],
        compiler_params=_params("arbitrary", "arbitrary"),
        name="inproj",
    )(x2, g, w_r, w_n)


def _rwkv_operands(p, p_prev, mu, w0, w2, a0, a2, g2, k_k, k_a, r_k, bd, tri):
    c = RWKV_CHUNK
    ps = p + mu * (p_prev - p)
    d = D_RWKV
    r, k, v = ps[:, 0:d], ps[:, d:2 * d], ps[:, 2 * d:3 * d]
    dw = ps[:, 3 * d:3 * d + LORA_W]
    da = ps[:, 3 * d + LORA_W:3 * d + LORA_W + LORA_A]
    dg = ps[:, 3 * d + LORA_W + LORA_A:]

    z = -(w0 + _dot(jnp.tanh(dw).astype(BF16), w2))
    softplus = jnp.maximum(z, 0.0) + jnp.log1p(jnp.exp(-jnp.abs(z)))
    logw = -jnp.exp(-softplus - 0.5)
    a = jax.nn.sigmoid(a0 + _dot(da.astype(BF16), a2))
    g = _dot(jax.nn.sigmoid(dg).astype(BF16), g2)

    kk = k * k_k
    kk = kk / jnp.maximum(jnp.sqrt(_dot_exact_rhs(kk * kk, bd)), 1e-12)
    kp = k * (1.0 + (a - 1.0) * k_a)
    bonus = _dot_exact_rhs(r * kp * r_k, bd) * v

    n_chunks = p.shape[0] // c
    cum = jnp.concatenate([_dot_exact_lhs(tri, logw[ci * c:(ci + 1) * c]) for ci in range(n_chunks)], axis=0)
    e_pos, e_neg = jnp.exp(cum), jnp.exp(-cum)
    wc = jnp.concatenate([e_pos[(ci + 1) * c - 1:(ci + 1) * c, :] for ci in range(n_chunks)], axis=0)
    ops = (r * e_pos, kp * e_neg, kk * a * e_neg, -kk * jnp.exp(cum - logw), v)
    return tuple(o.astype(BF16) for o in ops), wc, bonus, g


def _rwkv_kernel(pc_ref, pn_ref, mu_ref, w0_ref, w2_ref, a0_ref, a2_ref, g2_ref, kk_ref, ka_ref, rk_ref,
                 bd_ref, tri_ref, lnw_ref, lnb_ref, y_ref, s_ref, ops_a, aux_a, wc_a, ops_b, aux_b, wc_b):
    i = pl.program_id(0)
    nb, c, width = pc_ref.shape
    nh, dk = RWKV_HEADS, HEAD_DIM
    n = nb * nh
    rows = nb * c

    def prepare(dst, p3, prev_last):
        ops_ref, aux_ref, wc_ref = dst
        p = p3.reshape(rows, width)
        row = lax.broadcasted_iota(jnp.int32, p.shape, 0)
        p_prev = pltpu.roll(p, 1, axis=0)
        for b in range(nb):
            p_prev = jnp.where(row == b * c, prev_last[b], p_prev)
        ops, wc, bonus, g = _rwkv_operands(
            p, p_prev, mu_ref[...], w0_ref[...], w2_ref[...], a0_ref[...], a2_ref[...], g2_ref[...],
            kk_ref[...], ka_ref[...], rk_ref[...], bd_ref[...], tri_ref[...])
        for idx, o in enumerate(ops):
            ops_ref[idx] = o
        aux_ref[0] = bonus
        aux_ref[1] = g
        wc_ref[0:nb] = wc

    slot_a, slot_b = (ops_a, aux_a, wc_a), (ops_b, aux_b, wc_b)
    step = lambda src, dst: _rwkv_step(src, dst, prepare, pc_ref, pn_ref, lnw_ref, lnb_ref, y_ref, s_ref)

    @pl.when(i == 0)
    def _():
        s_ref[...] = jnp.zeros_like(s_ref)
        prepare(slot_a, pc_ref[...], [jnp.zeros((1, width), F32)] * nb)

    @pl.when(i % 2 == 0)
    def _():
        step(slot_a, slot_b)

    @pl.when(i % 2 == 1)
    def _():
        step(slot_b, slot_a)


def _rwkv_step(src, dst, prepare, pc_ref, pn_ref, lnw_ref, lnb_ref, y_ref, s_ref):
    ops_ref, aux_ref, wc_ref = src
    nb, c, width = pc_ref.shape
    nh, dk = RWKV_HEADS, HEAD_DIM
    n = nb * nh
    cur = pc_ref[...]
    prepare(dst, pn_ref[...], [cur[b, c - 1:c, :] for b in range(nb)])

    row = lax.broadcasted_iota(jnp.int32, (1, c, c), 1)
    col = lax.broadcasted_iota(jnp.int32, (1, c, c), 2)
    strict = col < row
    incl = col <= row
    eye = (row == col).astype(F32)

    def heads(x, rows_per):
        return jnp.stack([x[b * rows_per:(b + 1) * rows_per, h * dk:(h + 1) * dk]
                          for b in range(nb) for h in range(nh)], axis=0)

    rt, kt, bt, at, vb = (heads(ops_ref[idx], c) for idx in range(5))
    ar = jnp.concatenate([at, rt], axis=1)
    bk = jnp.concatenate([bt, kt], axis=1)
    amat = _bdot("nik,njk->nij", ar, bk)
    n_ab = jnp.where(strict, amat[:, :c, :c], 0.0)
    a_ak = jnp.where(strict, amat[:, :c, c:], 0.0)
    a_rb = jnp.where(incl, amat[:, c:, :c], 0.0)
    a_rk = jnp.where(incl, amat[:, c:, c:], 0.0)
    s0 = s_ref[...]
    ar_s = _bdot("nik,nvk->niv", ar, s0.astype(BF16))
    rhs = ar_s[:, :c] + _bdot("nij,njv->niv", a_ak.astype(BF16), vb)
    inv = eye + n_ab
    pw = n_ab
    for _ in range(int(np.log2(c)) - 1):
        pwb = pw.astype(BF16)
        pw = _bdot("nij,njk->nik", pwb, pwb)
        inv = inv + _bdot("nij,njk->nik", pw.astype(BF16), inv.astype(BF16))
    ub = _bdot("nij,njv->niv", inv.astype(BF16), rhs.astype(BF16)).astype(BF16)
    y = ar_s[:, c:] + _bdot("nij,njv->niv", a_rb.astype(BF16), ub) + _bdot("nij,njv->niv", a_rk.astype(BF16), vb)
    uv_t = jnp.swapaxes(jnp.concatenate([ub, vb], axis=1), 1, 2)
    s_ref[...] = (s0 + _bdot("nvi,nik->nvk", uv_t, bk)) * heads(wc_ref[0:nb], 1)

    mean = jnp.mean(y, axis=-1, keepdims=True)
    var = jnp.mean(jnp.square(y - mean), axis=-1, keepdims=True)
    yn = (y - mean) * lax.rsqrt(var + GN_EPS)
    bonus, g = aux_ref[0], aux_ref[1]
    for b in range(nb):
        wide = jnp.concatenate([yn[b * nh + h] for h in range(nh)], axis=-1)
        rsl = slice(b * c, (b + 1) * c)
        y_ref[b] = ((wide * lnw_ref[...] + lnb_ref[...] + bonus[rsl]) * g[rsl]).astype(y_ref.dtype)


def _rwkv(p_r, mu, w0, w2, a0, a2, g2, k_k, k_a, r_k, lnw, lnb):
    batch, t, width = p_r.shape
    c = RWKV_CHUNK
    nc = t // c
    d = D_RWKV
    head = np.arange(d) // HEAD_DIM
    bd = jnp.asarray(head[:, None] == head[None, :], BF16)
    tri = jnp.asarray(np.tril(np.ones((c, c))), BF16)
    const = lambda a: pl.BlockSpec(a.shape, lambda i: (0, 0))
    consts = (mu, w0, w2, a0, a2, g2, k_k, k_a, r_k, bd, tri, lnw, lnb)
    return pl.pallas_call(
        _rwkv_kernel,
        grid=(nc,),
        in_specs=[pl.BlockSpec((batch, c, width), lambda i: (0, i, 0)),
                  pl.BlockSpec((batch, c, width), lambda i: (0, jnp.minimum(i + 1, nc - 1), 0))]
                 + [const(a) for a in consts],
        out_specs=pl.BlockSpec((batch, c, d), lambda i: (0, i, 0)),
        out_shape=jax.ShapeDtypeStruct((batch, t, d), BF16),
        scratch_shapes=[pltpu.VMEM((batch * RWKV_HEADS, HEAD_DIM, HEAD_DIM), F32),
                        *([pltpu.VMEM((5, batch * c, d), BF16), pltpu.VMEM((2, batch * c, d), F32),
                           pltpu.VMEM((8, d), F32)] * 2)],
        compiler_params=_params("arbitrary"),
        name="rwkv_mix",
    )(p_r, p_r, *consts)


def _compress_kernel(zk_ref, zv_ref, pek_ref, pev_ref, k1_ref, k2_ref, v1_ref, v2_ref, ko_ref, vo_ref):
    def one(z_ref, pe_ref, w1_ref, w2_ref):
        n16 = z_ref.shape[2] // CMP_STRIDE
        dk = HEAD_DIM
        first = second = None
        for l in range(CMP_STRIDE):
            z = z_ref[0, 0, pl.ds(l, n16, stride=CMP_STRIDE), :]
            lo = _dot((z + pe_ref[l:l + 1, :]).astype(BF16), w1_ref[l * dk:(l + 1) * dk, :])
            u = CMP_STRIDE + l
            hi = _dot((z + pe_ref[u:u + 1, :]).astype(BF16), w1_ref[u * dk:(u + 1) * dk, :])
            first = lo if first is None else first + lo
            second = hi if second is None else second + hi
        hidden = first + pltpu.roll(second, n16 - 1, axis=0)
        return _dot(jax.nn.gelu(hidden).astype(BF16), w2_ref[...])

    kc = one(zk_ref, pek_ref, k1_ref, k2_ref)
    blk = lax.broadcasted_iota(jnp.int32, kc.shape, 0)
    col = lax.broadcasted_iota(jnp.int32, kc.shape, 1)
    per = SEL_BLOCK // CMP_STRIDE
    hi = blk // per
    lo = CMP_STRIDE * (blk % per) + (CMP_BLOCK - 1)
    feat = _position_features(hi, lo, col)
    ko_ref[0] = jnp.concatenate([kc, feat], axis=1).astype(BF16)
    vc = one(zv_ref, pev_ref, v1_ref, v2_ref)
    vc_t = jnp.transpose(jnp.concatenate([vc, jnp.zeros_like(vc)], axis=1))
    vo_ref[0] = vc_t[:HEAD_DIM].astype(BF16)


def _compress(zk, zv, pek, pev, k1, k2, v1, v2):
    batch, hk, t, dk = zk.shape
    n16 = t // CMP_STRIDE
    zspec = pl.BlockSpec((1, 1, t, dk), lambda b, h: (b, h, 0, 0))
    full = lambda a: pl.BlockSpec(a.shape, lambda b, h: (0,) * a.ndim)
    return pl.pallas_call(
        _compress_kernel,
        grid=(batch, hk),
        in_specs=[zspec, zspec, full(pek), full(pev), full(k1), full(k2), full(v1), full(v2)],
        out_specs=[pl.BlockSpec((1, n16, 2 * dk), lambda b, h: (b * hk + h, 0, 0)),
                   pl.BlockSpec((1, dk, n16), lambda b, h: (b * hk + h, 0, 0))],
        out_shape=[jax.ShapeDtypeStruct((batch * hk, n16, 2 * dk), BF16),
                   jax.ShapeDtypeStruct((batch * hk, dk, n16), BF16)],
        compiler_params=_params("arbitrary", "arbitrary"),
        name="nsa_compress",
    )(zk, zv, pek, pev, k1, k2, v1, v2)


def _query_features(qt, slopes_ref, hk):
    lane_g = lax.broadcasted_iota(jnp.int32, (1, QG), 1) // Q_BLOCK
    slope = jnp.zeros((1, QG), F32)
    for g in range(NSA_GROUP):
        slope = jnp.where(lane_g == g, slopes_ref[hk * NSA_GROUP + g], slope)
    row = lax.broadcasted_iota(jnp.int32, (HEAD_DIM, QG), 0)
    extra = jnp.zeros((HEAD_DIM, QG), F32)
    for n, term in enumerate(LOG2E_TERMS):
        extra = jnp.where(row == 2 * n, (SEL_BLOCK * term) * slope, jnp.where(row == 2 * n + 1, term * slope, extra))
    return jnp.concatenate([qt * (HEAD_DIM ** -0.5 * LOG2E), extra], axis=0).astype(BF16)


def _position_features(hi, lo, col):
    return jnp.where(col < 2 * len(LOG2E_TERMS), jnp.where(col % 2 == 0, hi, lo), 0).astype(F32)


def _cmp_select_kernel(slopes_ref, qt_ref, kc_ref, vct_ref, ovt_ref, oct_ref, selt_ref, imp_ref):
    hk = pl.program_id(1)
    nblk = qt_ref.shape[2]
    width = nblk * Q_BLOCK
    ncp = kc_ref.shape[1]

    def attend(rows):
        cmp_end = lax.broadcasted_iota(jnp.int32, (rows, Q_BLOCK), 0) * CMP_STRIDE + (CMP_BLOCK - 1)
        for u in range(nblk):
            q0 = (pl.program_id(2) * nblk + u) * Q_BLOCK
            q_aug = _query_features(qt_ref[0, 0, u], slopes_ref, hk)
            s = _dot(kc_ref[0, 0:rows, :], q_aug)
            ok = cmp_end <= q0 + lax.broadcasted_iota(jnp.int32, (rows, Q_BLOCK), 1)
            any_ok = (q0 + lax.broadcasted_iota(jnp.int32, (1, Q_BLOCK), 1) >= CMP_BLOCK - 1).astype(F32)
            p_sum = jnp.zeros((rows, Q_BLOCK), F32)
            probs = []
            for g in range(NSA_GROUP):
                sg = jnp.where(ok, s[:, g * Q_BLOCK:(g + 1) * Q_BLOCK], NEG)
                e = jnp.exp2(sg - jnp.max(sg, axis=0, keepdims=True))
                p = e * (any_ok / jnp.sum(e, axis=0, keepdims=True))
                p_sum = p_sum + p
                probs.append(p.astype(BF16))
            oct_ref[0, 0, u] = _dot(vct_ref[0, :, 0:rows], jnp.concatenate(probs, axis=1))
            imp_ref[:, u * Q_BLOCK:(u + 1) * Q_BLOCK] = _dot_exact_lhs(ovt_ref[:, 0:rows], p_sum)

    chunk = min(CMP_ROW_CHUNK, ncp)
    n_chunks = ncp // chunk
    last_t = (pl.program_id(2) + 1) * width - 1
    needed = jnp.maximum((last_t - (CMP_BLOCK - 1)) // CMP_STRIDE + 1, 1)
    needed_chunks = jnp.minimum((needed + chunk - 1) // chunk, n_chunks)
    for nck in range(1, n_chunks + 1):
        pl.when(needed_chunks == nck)(lambda nck=nck: attend(nck * chunk))

    imp = imp_ref[...]
    blk = lax.broadcasted_iota(jnp.int32, (LANES, width), 0)
    cur = (pl.program_id(2) * width + lax.broadcasted_iota(jnp.int32, (LANES, width), 1)) // SEL_BLOCK
    valid = blk <= cur
    forced = (blk == 0) | (blk == cur) | (blk == cur - 1)
    x = jnp.where(valid, jnp.where(forced, BIG, imp), NEG)
    blk_f = blk.astype(F32)
    chosen = jnp.zeros((LANES, width), jnp.bool_)
    for _ in range(SEL_TOPK):
        m = jnp.max(x, axis=0, keepdims=True)
        first = jnp.min(jnp.where(x == m, blk_f, float(LANES)), axis=0, keepdims=True)
        hit = blk_f == first
        chosen = chosen | hit
        x = jnp.where(hit, -jnp.inf, x)
    sel = (chosen & valid).astype(F32)
    for u in range(nblk):
        selt_ref[0, 0, u] = sel[:, u * Q_BLOCK:(u + 1) * Q_BLOCK]


def _cmp_select(slopes, q_t, kc_aug, vc_t, ov_t):
    batch, hk, nq, dk, qg = q_t.shape
    ncp = kc_aug.shape[1]
    nblk = CMP_Q_BLOCKS
    return pl.pallas_call(
        _cmp_select_kernel,
        grid=(batch, hk, nq // nblk),
        in_specs=[pl.BlockSpec(memory_space=pltpu.SMEM),
                  pl.BlockSpec((1, 1, nblk, dk, qg), lambda b, h, i: (b, h, i, 0, 0)),
                  pl.BlockSpec((1, ncp, 2 * dk), lambda b, h, i: (b * hk + h, 0, 0)),
                  pl.BlockSpec((1, dk, ncp), lambda b, h, i: (b * hk + h, 0, 0)),
                  pl.BlockSpec(ov_t.shape, lambda b, h, i: (0, 0))],
        out_specs=[pl.BlockSpec((1, 1, nblk, dk, qg), lambda b, h, i: (b, h, i, 0, 0)),
                   pl.BlockSpec((1, 1, nblk, LANES, Q_BLOCK), lambda b, h, i: (b, h, i, 0, 0))],
        out_shape=[jax.ShapeDtypeStruct(q_t.shape, F32),
                   jax.ShapeDtypeStruct((batch, hk, nq, LANES, Q_BLOCK), F32)],
        scratch_shapes=[pltpu.VMEM((LANES, nblk * Q_BLOCK), F32)],
        compiler_params=_params("arbitrary", "arbitrary", "arbitrary"),
        name="nsa_cmp_select",
    )(slopes, q_t, kc_aug, vc_t, ov_t)


def _sel_win_kernel(ids_ref, cnt_ref, slopes_ref, qt_ref, selt_ref, ks_ref, vst_ref, kw_ref, vwt_ref,
                    oct_ref, glt_ref, yt_ref):
    b, hk, i = pl.program_id(0), pl.program_id(1), pl.program_id(2)
    step = (b * pl.num_programs(1) + hk) * pl.num_programs(2) + i
    q0 = i * Q_BLOCK
    kt = KEY_TILE
    q_aug = _query_features(qt_ref[0, 0, 0], slopes_ref, hk)
    lane_minus_row = (lax.broadcasted_iota(jnp.int32, (kt, Q_BLOCK), 1)
                      - lax.broadcasted_iota(jnp.int32, (kt, Q_BLOCK), 0))

    def all_scores(tiles):
        s_all = _dot(jnp.concatenate([k_tile for k_tile, _, _ in tiles], axis=0), q_aug)
        return [s_all[n * kt:(n + 1) * kt] for n in range(len(tiles))]

    def attend(state, tiles, scores=None):
        m_old, acc_old = state
        scores = all_scores(tiles) if scores is None else scores
        probs, maxes = [[] for _ in tiles], []
        for g in range(NSA_GROUP):
            gsl = slice(g * Q_BLOCK, (g + 1) * Q_BLOCK)
            masked = [jnp.where(mask, s[:, gsl], NEG) for s, (_, _, mask) in zip(scores, tiles)]
            top = jnp.max(masked[0].reshape(kt // 8, 8, Q_BLOCK), axis=0)
            for sg in masked[1:]:
                top = jnp.maximum(top, jnp.max(sg.reshape(kt // 8, 8, Q_BLOCK), axis=0))
            mg = jnp.maximum(m_old[:, gsl], jnp.max(top, axis=0, keepdims=True))
            for n, sg in enumerate(masked):
                probs[n].append(jnp.exp2(sg - mg).astype(BF16))
            maxes.append(mg)
        m_new = jnp.concatenate(maxes, axis=1)
        p_all = jnp.concatenate([jnp.concatenate(p, axis=1) for p in probs], axis=0)
        vt_all = jnp.concatenate([vt_tile for _, vt_tile, _ in tiles], axis=1)
        return m_new, jnp.exp2(m_old - m_new) * acc_old + _dot(vt_all, p_all)

    start = (jnp.full((1, QG), SOFTMAX_FLOOR, F32), jnp.zeros((V_ROWS, QG), F32))
    result = lambda state: state[1][:HEAD_DIM] / state[1][HEAD_DIM:HEAD_DIM + 1]

    count = cnt_ref[step]
    last_tile = ks_ref.shape[2] // kt - 1

    def sel_tile(n):
        j = jnp.minimum(ids_ref[step * MAX_KEY_TILES + jnp.minimum(n, MAX_KEY_TILES - 1)], last_tile)
        k0 = pl.multiple_of(j * kt, kt)
        per = kt // SEL_BLOCK
        picked = jnp.concatenate(
            [jnp.broadcast_to(selt_ref[0, 0, 0, pl.ds(j * per + u, 1), :], (SEL_BLOCK, Q_BLOCK)) for u in range(per)],
            axis=0)
        causal_from = jnp.where(n < count, k0 - q0, 1 << 30)
        mask = (picked > 0.5) & (lane_minus_row >= causal_from)
        return ks_ref[0, 0, pl.ds(k0, kt), :], vst_ref[0, 0, j], mask

    last = (q0 + Q_BLOCK - 1) // kt

    def win_tile(u):
        j = last - u
        jc = jnp.maximum(j, 0)
        k0 = pl.multiple_of(jc * kt, kt)
        dist = lane_minus_row + jnp.where(j >= 0, q0 - k0, -(1 << 30))
        mask = (dist >= 0) & (dist < WINDOW)
        return kw_ref[0, 0, pl.ds(k0, kt), :], vwt_ref[0, 0, jc], mask

    win_tiles = [win_tile(u) for u in range(WIN_TILES)]
    sel_tiles = [sel_tile(n) for n in range(SEL_FIRST)]
    scores = all_scores(win_tiles + sel_tiles)
    state_w = attend(start, win_tiles, scores[:WIN_TILES])
    state_s = attend(start, sel_tiles, scores[WIN_TILES:])
    state_s = lax.fori_loop(
        0, (jnp.maximum(count - SEL_FIRST, 0) + SEL_GROUP - 1) // SEL_GROUP,
        lambda n, st: attend(st, [sel_tile(SEL_FIRST + SEL_GROUP * n + u) for u in range(SEL_GROUP)]), state_s)
    o_s, o_w = result(state_s), result(state_w)

    gates = jax.nn.sigmoid(glt_ref[0, 0])
    o_c = oct_ref[0, 0, 0]
    outs = []
    for g in range(NSA_GROUP):
        gsl = slice(g * Q_BLOCK, (g + 1) * Q_BLOCK)
        outs.append(gates[3 * g:3 * g + 1] * o_c[:, gsl] + gates[3 * g + 1:3 * g + 2] * o_s[:, gsl]
                    + gates[3 * g + 2:3 * g + 3] * o_w[:, gsl])
    yt_ref[0] = jnp.concatenate(outs, axis=0).astype(yt_ref.dtype)


def _sel_win(tile_ids, tile_cnt, slopes, q_t, sel_t, ks, vs_t, kw, vw_t, oc_t, gl_t):
    batch, hk, nq, dk, qg = q_t.shape
    t = nq * Q_BLOCK
    nkt = t // KEY_TILE
    grp = qg // Q_BLOCK
    qspec = pl.BlockSpec((1, 1, 1, dk, qg), lambda b, h, i, *_: (b, h, i, 0, 0))
    kspec = pl.BlockSpec((1, 1, t, 2 * dk), lambda b, h, i, *_: (b, h, 0, 0))
    vspec = pl.BlockSpec((1, 1, nkt, V_ROWS, KEY_TILE), lambda b, h, i, *_: (b, h, 0, 0, 0))
    grid_spec = pltpu.PrefetchScalarGridSpec(
        num_scalar_prefetch=2,
        grid=(batch, hk, nq),
        in_specs=[pl.BlockSpec(memory_space=pltpu.SMEM), qspec,
                  pl.BlockSpec((1, 1, 1, LANES, Q_BLOCK), lambda b, h, i, *_: (b, h, i, 0, 0)),
                  kspec, vspec, kspec, vspec, qspec,
                  pl.BlockSpec((1, 1, 3 * grp, Q_BLOCK), lambda b, h, i, *_: (b, h, 0, i))],
        out_specs=pl.BlockSpec((1, grp * dk, Q_BLOCK), lambda b, h, i, *_: (b, h, i)),
    )
    return pl.pallas_call(
        _sel_win_kernel,
        grid_spec=grid_spec,
        out_shape=jax.ShapeDtypeStruct((batch, hk * grp * dk, t), BF16),
        compiler_params=_params("arbitrary", "arbitrary", "arbitrary"),
        name="nsa_sel_win",
    )(tile_ids, tile_cnt, slopes, q_t, sel_t, ks, vs_t, kw, vw_t, oc_t, gl_t)


def _ffn_kernel(x_ref, yr_ref, ynt_ref, wo1_ref, wo2_ref, g2_ref, wg_ref, wu_ref, wd_ref, gf_ref, o_ref):
    h1 = (x_ref[...] + _dot(yr_ref[...].astype(BF16), wo1_ref[...])
          + _dot_tn(ynt_ref[0].astype(BF16), wo2_ref[...]))
    hn = _rms(h1, g2_ref[...]).astype(BF16)
    gate = _dot(hn, wg_ref[...])
    up = _dot(hn, wu_ref[...])
    act = gate * jax.nn.sigmoid(gate) * up
    o_ref[...] = _rms(h1 + _dot(act.astype(BF16), wd_ref[...]), gf_ref[...])


def _ffn(x2, yr, yn_t, wo1, wo2, g2, wg, wu, wd, gf, tm=512):
    m, d = x2.shape
    _, dn, t = yn_t.shape
    per_seq = t // tm
    row = lambda n: pl.BlockSpec((tm, n), lambda i: (i, 0))
    const = lambda a: pl.BlockSpec(a.shape, lambda i: (0, 0), pipeline_mode=pl.Buffered(1))
    return pl.pallas_call(
        _ffn_kernel,
        grid=(m // tm,),
        in_specs=[row(d), row(yr.shape[1]),
                  pl.BlockSpec((1, dn, tm), lambda i: (i // per_seq, 0, i % per_seq)),
                  const(wo1), const(wo2), const(g2), const(wg), const(wu), const(wd), const(gf)],
        out_specs=row(d),
        out_shape=jax.ShapeDtypeStruct((m, d), F32),
        compiler_params=_params("arbitrary"),
        name="outproj_ffn",
    )(x2, yr, yn_t, wo1, wo2, g2, wg, wu, wd, gf)


def _overlap_matrix_t(t):
    n16 = t // CMP_STRIDE
    n_cmp = (t - CMP_BLOCK) // CMP_STRIDE + 1
    n_sel = t // SEL_BLOCK
    cmp_start = np.arange(n_cmp) * CMP_STRIDE
    sel_start = np.arange(n_sel) * SEL_BLOCK
    ov = np.clip(np.minimum(cmp_start[:, None] + CMP_BLOCK, sel_start[None, :] + SEL_BLOCK)
                 - np.maximum(cmp_start[:, None], sel_start[None, :]), 0, None) / CMP_STRIDE
    full = np.zeros((LANES, n16), np.float32)
    full[:n_sel, :n_cmp] = ov.T
    return jnp.asarray(full, BF16)


def kernel(x, norm1_g, w_in, mu_shift, rwkv_w0, rwkv_w2, rwkv_a0, rwkv_a2, rwkv_g2, rwkv_k_k, rwkv_k_a,
           rwkv_r_k, rwkv_lnx_w, rwkv_lnx_b, nsa_pe_k, nsa_pe_v, nsa_cmp_k_w1, nsa_cmp_k_w2, nsa_cmp_v_w1,
           nsa_cmp_v_w2, w_out, norm2_g, ffn_w_gate, ffn_w_up, ffn_w_down, norm_f_g):
    batch, t, d_model = x.shape
    assert w_in.shape[0] == 1, "the final RMSNorm is fused into the (single) layer's FFN kernel"
    assert t % INPROJ_ROWS == 0 and t // SEL_BLOCK <= LANES and batch <= 8
    hk, grp, dk = NSA_KV_HEADS, NSA_GROUP, HEAD_DIM
    nq = t // Q_BLOCK
    slopes = 2.0 ** (-8.0 * jnp.arange(1, NSA_Q_HEADS + 1, dtype=F32) / NSA_Q_HEADS)
    nsa_pad = _round_up(N_NSA_COLS, LANES)
    row = lambda a: a.reshape(1, -1)
    i = 0

    h = x.reshape(batch * t, d_model)
    w_r = w_in[i][:, :N_RWKV_COLS].astype(BF16)
    w_n = jnp.pad(w_in[i][:, N_RWKV_COLS:], ((0, 0), (0, nsa_pad - N_NSA_COLS))).astype(BF16)
    p_r, q_t, ks, kw, vs_t, vw_t, zk, zv, gl_t = _inproj(h, batch, row(norm1_g[i]), w_r, w_n)

    y_rwkv = _rwkv(p_r.reshape(batch, t, N_RWKV_COLS), row(mu_shift[i]), row(rwkv_w0[i]), rwkv_w2[i].astype(BF16),
                   row(rwkv_a0[i]), rwkv_a2[i].astype(BF16), rwkv_g2[i].astype(BF16), row(rwkv_k_k[i]),
                   row(rwkv_k_a[i]), row(rwkv_r_k[i]), row(rwkv_lnx_w[i]), row(rwkv_lnx_b[i]))
    y_rwkv = y_rwkv.reshape(batch * t, D_RWKV)

    kc_aug, vc_t = _compress(zk, zv, nsa_pe_k[i], nsa_pe_v[i],
                             nsa_cmp_k_w1[i].astype(BF16), nsa_cmp_k_w2[i].astype(BF16),
                             nsa_cmp_v_w1[i].astype(BF16), nsa_cmp_v_w2[i].astype(BF16))
    oc_t, sel_t = _cmp_select(slopes, q_t, kc_aug, vc_t, _overlap_matrix_t(t))

    blocks_per_tile = KEY_TILE // SEL_BLOCK
    active = sel_t.reshape(batch, hk, nq, MAX_KEY_TILES, blocks_per_tile * Q_BLOCK).max(axis=-1) > 0
    tile_ids = jnp.argsort(jnp.logical_not(active), axis=-1, stable=True).astype(jnp.int32).reshape(-1)
    tile_cnt = active.sum(axis=-1).astype(jnp.int32).reshape(-1)
    y_nsa_t = _sel_win(tile_ids, tile_cnt, slopes, q_t, sel_t, ks, vs_t, kw, vw_t, oc_t, gl_t)

    out = _ffn(h, y_rwkv, y_nsa_t, w_out[i][:D_RWKV].astype(BF16), w_out[i][D_RWKV:].astype(BF16),
               row(norm2_g[i]), ffn_w_gate[i].astype(BF16), ffn_w_up[i].astype(BF16),
               ffn_w_down[i].astype(BF16), row(norm_f_g))
    return out.reshape(batch, t, d_model)
```

```python
import ml_dtypes
import numpy as np
import jax
import jax.numpy as jnp
from jax import lax
from jax.experimental import pallas as pl
from jax.experimental.pallas import tpu as pltpu

F32 = jnp.float32
BF16 = jnp.bfloat16

HEAD_DIM = 64
RWKV_HEADS = 8
D_RWKV = RWKV_HEADS * HEAD_DIM
NSA_Q_HEADS = 8
NSA_KV_HEADS = 2
NSA_GROUP = NSA_Q_HEADS // NSA_KV_HEADS
D_NSA = NSA_Q_HEADS * HEAD_DIM
D_KV = NSA_KV_HEADS * HEAD_DIM
LORA_W, LORA_A, LORA_G = 64, 64, 128
N_RWKV_COLS = 3 * D_RWKV + LORA_W + LORA_A + LORA_G
N_NSA_COLS = D_NSA + 6 * D_KV + 3 * NSA_Q_HEADS
CMP_BLOCK, CMP_STRIDE = 32, 16
SEL_BLOCK, SEL_TOPK = 64, 16
WINDOW = 512
Q_BLOCK = 128
NORM_EPS = 1e-6
GN_EPS = 64e-5
NEG = -1e30
BIG = 1e30

LANES = 128
RWKV_CHUNK = 64
RWKV_PREP_ROWS = 8 * RWKV_CHUNK
PREP_SPLIT = 2
KEY_TILE = 128
INPROJ_ROWS = 512
MAX_KEY_TILES = LANES * SEL_BLOCK // KEY_TILE
WIN_TILES = (WINDOW + Q_BLOCK) // KEY_TILE
SEL_FIRST = 9
SEL_GROUP = 2
CMP_Q_BLOCKS = 4
CMP_ROW_CHUNK = 128
V_ROWS = HEAD_DIM + 16
SOFTMAX_FLOOR = -1e20
QG = NSA_GROUP * Q_BLOCK
VMEM_LIMIT = 56 * 1024 * 1024


def _bf16_terms(x, count):
    terms = []
    for _ in range(count):
        terms.append(float(np.asarray(x, ml_dtypes.bfloat16)))
        x = x - terms[-1]
    return tuple(terms)


LOG2E = float(np.log2(np.e))
LOG2E_TERMS = _bf16_terms(LOG2E, 3)


def _round_up(n, m):
    return -(-n // m) * m


def _dot(a, b):
    return jnp.dot(a, b, preferred_element_type=F32)


def _dot_tn(a, b):
    return lax.dot_general(a, b, (((0,), (0,)), ((), ())), preferred_element_type=F32)


def _split(a, terms):
    pieces = []
    for _ in range(terms - 1):
        pieces.append(a.astype(BF16))
        a = a - pieces[-1].astype(F32)
    return pieces + [a.astype(BF16)]


def _dot_exact_lhs(a, b, terms=3):
    return sum(_dot(a, piece) for piece in _split(b, terms))


def _dot_exact_rhs(a, b, terms=3):
    return sum(_dot(piece, b) for piece in _split(a, terms))


def _bdot(spec, a, b):
    return jnp.einsum(spec, a, b, preferred_element_type=F32)


def _rms(x, g):
    return x * lax.rsqrt(jnp.mean(x * x, axis=-1, keepdims=True) + NORM_EPS) * g


def _params(*sem):
    return pltpu.CompilerParams(dimension_semantics=sem, vmem_limit_bytes=VMEM_LIMIT)


def _inproj_kernel(x_ref, g_ref, wr_ref, wn_ref, pr_ref, qt_ref, ks_ref, kw_ref, vst_ref, vwt_ref,
                   kc_ref, vc_ref, glt_ref):
    xb = _rms(x_ref[...], g_ref[...]).astype(BF16)
    pr_ref[...] = _dot(xb, wr_ref[...])
    pn = _dot(xb, wn_ref[...])
    rows = pn.shape[0]
    dk, hk_n = HEAD_DIM, NSA_KV_HEADS
    group = lambda j: pn[:, D_NSA + j * D_KV:D_NSA + (j + 1) * D_KV]

    for half in range(rows // Q_BLOCK):
        q_tr = jnp.transpose(pn[half * Q_BLOCK:(half + 1) * Q_BLOCK, :D_NSA])
        for hk in range(hk_n):
            base = hk * NSA_GROUP * dk
            qt_ref[0, hk, half] = jnp.concatenate(
                [q_tr[base + g * dk:base + (g + 1) * dk, :] for g in range(NSA_GROUP)], axis=1)

    pos = pl.program_id(1) * rows + lax.broadcasted_iota(jnp.int32, (rows, dk), 0)
    col = lax.broadcasted_iota(jnp.int32, (rows, dk), 1)
    pos_cols = _position_features(pos // SEL_BLOCK, pos % SEL_BLOCK, col)
    kt = KEY_TILE
    ones_rows = (lax.broadcasted_iota(jnp.int32, (V_ROWS - dk, kt), 0) == 0).astype(F32)
    kc, vc, ks, vs, kw, vw = (group(j) for j in range(6))
    vs_tr, vw_tr = jnp.transpose(vs), jnp.transpose(vw)
    gl_tr = jnp.transpose(pn[:, D_NSA + 6 * D_KV:])
    n_gate = 3 * NSA_GROUP
    for hk in range(hk_n):
        sl = slice(hk * dk, (hk + 1) * dk)
        kc_ref[0, hk] = kc[:, sl]
        vc_ref[0, hk] = vc[:, sl]
        ks_ref[0, hk] = jnp.concatenate([ks[:, sl], pos_cols], axis=1).astype(BF16)
        kw_ref[0, hk] = jnp.concatenate([kw[:, sl], pos_cols], axis=1).astype(BF16)
        for u in range(rows // kt):
            vst_ref[0, hk, u] = jnp.concatenate([vs_tr[sl, u * kt:(u + 1) * kt], ones_rows], axis=0).astype(BF16)
            vwt_ref[0, hk, u] = jnp.concatenate([vw_tr[sl, u * kt:(u + 1) * kt], ones_rows], axis=0).astype(BF16)
        glt_ref[0, hk] = gl_tr[hk * n_gate:(hk + 1) * n_gate, :]


def _inproj(x2, batch, g, w_r, w_n):
    m, d = x2.shape
    t = m // batch
    tm = INPROJ_ROWS
    nt = t // tm
    nr, nn = w_r.shape[1], w_n.shape[1]
    hk, dk = NSA_KV_HEADS, HEAD_DIM
    const = lambda a: pl.BlockSpec(a.shape, lambda b, i: (0, 0))
    keys = pl.BlockSpec((1, hk, tm, 2 * dk), lambda b, i: (b, 0, i, 0))
    vals = pl.BlockSpec((1, hk, tm // KEY_TILE, V_ROWS, KEY_TILE), lambda b, i: (b, 0, i, 0, 0))
    cmp_in = pl.BlockSpec((1, hk, tm, dk), lambda b, i: (b, 0, i, 0))
    keys_shape = jax.ShapeDtypeStruct((batch, hk, t, 2 * dk), BF16)
    vals_shape = jax.ShapeDtypeStruct((batch, hk, t // KEY_TILE, V_ROWS, KEY_TILE), BF16)
    cmp_shape = jax.ShapeDtypeStruct((batch, hk, t, dk), F32)
    return pl.pallas_call(
        _inproj_kernel,
        grid=(batch, nt),
        in_specs=[pl.BlockSpec((tm, d), lambda b, i: (b * nt + i, 0)), const(g), const(w_r), const(w_n)],
        out_specs=[pl.BlockSpec((tm, nr), lambda b, i: (b * nt + i, 0)),
                   pl.BlockSpec((1, hk, tm // Q_BLOCK, dk, QG), lambda b, i: (b, 0, i, 0, 0)),
                   keys, keys, vals, vals, cmp_in, cmp_in,
                   pl.BlockSpec((1, hk, 3 * NSA_GROUP, tm), lambda b, i: (b, 0, 0, i))],
        out_shape=[jax.ShapeDtypeStruct((m, nr), F32),
                   jax.ShapeDtypeStruct((batch, hk, t // Q_BLOCK, dk, QG), F32),
                   keys_shape, keys_shape, vals_shape, vals_shape, cmp_shape, cmp_shape,
                   jax.ShapeDtypeStruct((batch, hk, 3 * NSA_GROUP, t), F32)],
        compiler_params=_params("arbitrary", "arbitrary"),
        name="inproj",
    )(x2, g, w_r, w_n)


def _rwkv_operands(p, p_prev, mu, w0, w2, a0, a2, g2, k_k, k_a, r_k, bd, tri):
    c = RWKV_CHUNK
    ps = p + mu * (p_prev - p)
    d = D_RWKV
    r, k, v = ps[:, 0:d], ps[:, d:2 * d], ps[:, 2 * d:3 * d]
    dw = ps[:, 3 * d:3 * d + LORA_W]
    da = ps[:, 3 * d + LORA_W:3 * d + LORA_W + LORA_A]
    dg = ps[:, 3 * d + LORA_W + LORA_A:]

    z = -(w0 + _dot(jnp.tanh(dw).astype(BF16), w2))
    softplus = jnp.maximum(z, 0.0) + jnp.log1p(jnp.exp(-jnp.abs(z)))
    logw = -jnp.exp(-softplus - 0.5)
    a = jax.nn.sigmoid(a0 + _dot(da.astype(BF16), a2))
    g = _dot(jax.nn.sigmoid(dg).astype(BF16), g2)

    kk = k * k_k
    kk = kk / jnp.maximum(jnp.sqrt(_dot_exact_rhs(kk * kk, bd, PREP_SPLIT)), 1e-12)
    kp = k * (1.0 + (a - 1.0) * k_a)
    bonus = _dot_exact_rhs(r * kp * r_k, bd, PREP_SPLIT) * v

    n_chunks = p.shape[0] // c
    cum = jnp.concatenate([_dot_exact_lhs(tri, logw[ci * c:(ci + 1) * c], PREP_SPLIT) for ci in range(n_chunks)],
                          axis=0)
    e_pos, e_neg = jnp.exp(cum), jnp.exp(-cum)
    wc = jnp.concatenate([e_pos[(ci + 1) * c - 1:(ci + 1) * c, :] for ci in range(n_chunks)], axis=0)
    ops = (r * e_pos, kp * e_neg, kk * a * e_neg, -kk * jnp.exp(cum - logw), v)
    return tuple(o.astype(BF16) for o in ops), wc, bonus, g


def _rwkv_prep_kernel(p_ref, mu_ref, w0_ref, w2_ref, a0_ref, a2_ref, g2_ref, kk_ref, ka_ref, rk_ref, bd_ref, tri_ref,
                      ops_ref, aux_ref, wc_ref, carry_ref):
    rows = p_ref.shape[1]

    @pl.when(pl.program_id(1) == 0)
    def _():
        carry_ref[...] = jnp.zeros_like(carry_ref)

    p = p_ref[0]
    row = lax.broadcasted_iota(jnp.int32, p.shape, 0)
    p_prev = jnp.where(row == 0, carry_ref[7:8, :], pltpu.roll(p, 1, axis=0))
    carry_ref[...] = p[rows - 8:, :]
    ops, wc, bonus, g = _rwkv_operands(
        p, p_prev, mu_ref[...], w0_ref[...], w2_ref[...], a0_ref[...], a2_ref[...], g2_ref[...],
        kk_ref[...], ka_ref[...], rk_ref[...], bd_ref[...], tri_ref[...])
    for idx, o in enumerate(ops):
        ops_ref[idx, 0] = o
    aux_ref[0, 0] = bonus
    aux_ref[1, 0] = g
    wc_ref[0] = wc


def _rwkv_scan_kernel(ops_ref, aux_ref, wc_ref, lnw_ref, lnb_ref, y_ref, s_ref):
    @pl.when(pl.program_id(0) == 0)
    def _():
        s_ref[...] = jnp.zeros_like(s_ref)

    _rwkv_step(ops_ref, aux_ref, wc_ref, lnw_ref, lnb_ref, y_ref, s_ref)


def _rwkv_step(ops_ref, aux_ref, wc_ref, lnw_ref, lnb_ref, y_ref, s_ref):
    _, nb, c, d = ops_ref.shape
    nh, dk = RWKV_HEADS, HEAD_DIM
    n = nb * nh

    row = lax.broadcasted_iota(jnp.int32, (1, c, c), 1)
    col = lax.broadcasted_iota(jnp.int32, (1, c, c), 2)
    strict = col < row
    incl = col <= row
    eye = (row == col).astype(F32)

    def heads(x, rows_per):
        return jnp.stack([x[b * rows_per:(b + 1) * rows_per, h * dk:(h + 1) * dk]
                          for b in range(nb) for h in range(nh)], axis=0)

    rt, kt, bt, at, vb = (heads(ops_ref[idx].reshape(nb * c, d), c) for idx in range(5))
    ar = jnp.concatenate([at, rt], axis=1)
    bk = jnp.concatenate([bt, kt], axis=1)
    amat = _bdot("nik,njk->nij", ar, bk)
    n_ab = jnp.where(strict, amat[:, :c, :c], 0.0)
    a_ak = jnp.where(strict, amat[:, :c, c:], 0.0)
    a_rb = jnp.where(incl, amat[:, c:, :c], 0.0)
    a_rk = jnp.where(incl, amat[:, c:, c:], 0.0)
    s0 = s_ref[...]
    ar_s = _bdot("nik,nvk->niv", ar, s0.astype(BF16))
    rhs = ar_s[:, :c] + _bdot("nij,njv->niv", a_ak.astype(BF16), vb)
    inv = eye + n_ab
    pw = n_ab
    for _ in range(int(np.log2(c)) - 1):
        pwb = pw.astype(BF16)
        pw = _bdot("nij,njk->nik", pwb, pwb)
        inv = inv + _bdot("nij,njk->nik", pw.astype(BF16), inv.astype(BF16))
    ub = _bdot("nij,njv->niv", inv.astype(BF16), rhs.astype(BF16)).astype(BF16)
    y = ar_s[:, c:] + _bdot("nij,njv->niv", a_rb.astype(BF16), ub) + _bdot("nij,njv->niv", a_rk.astype(BF16), vb)
    uv_t = jnp.swapaxes(jnp.concatenate([ub, vb], axis=1), 1, 2)
    s_ref[...] = (s0 + _bdot("nvi,nik->nvk", uv_t, bk)) * heads(wc_ref[:, 0, 0, :], 1)

    mean = jnp.mean(y, axis=-1, keepdims=True)
    var = jnp.mean(jnp.square(y - mean), axis=-1, keepdims=True)
    yn = (y - mean) * lax.rsqrt(var + GN_EPS)
    bonus, g = aux_ref[0].reshape(nb * c, d), aux_ref[1].reshape(nb * c, d)
    for b in range(nb):
        wide = jnp.concatenate([yn[b * nh + h] for h in range(nh)], axis=-1)
        rsl = slice(b * c, (b + 1) * c)
        y_ref[b] = ((wide * lnw_ref[...] + lnb_ref[...] + bonus[rsl]) * g[rsl]).astype(y_ref.dtype)


def _rwkv(p_r, mu, w0, w2, a0, a2, g2, k_k, k_a, r_k, lnw, lnb):
    batch, t, width = p_r.shape
    c = RWKV_CHUNK
    rows = RWKV_PREP_ROWS
    d = D_RWKV
    head = np.arange(d) // HEAD_DIM
    bd = jnp.asarray(head[:, None] == head[None, :], BF16)
    tri = jnp.asarray(np.tril(np.ones((c, c))), BF16)
    const2 = lambda a: pl.BlockSpec(a.shape, lambda b, i: (0, 0))
    consts = (mu, w0, w2, a0, a2, g2, k_k, k_a, r_k, bd, tri)
    ops, aux, wc = pl.pallas_call(
        _rwkv_prep_kernel,
        grid=(batch, t // rows),
        in_specs=[pl.BlockSpec((1, rows, width), lambda b, i: (b, i, 0))] + [const2(a) for a in consts],
        out_specs=[pl.BlockSpec((5, 1, rows, d), lambda b, i: (0, b, i, 0)),
                   pl.BlockSpec((2, 1, rows, d), lambda b, i: (0, b, i, 0)),
                   pl.BlockSpec((1, rows // c, d), lambda b, i: (b, i, 0))],
        out_shape=[jax.ShapeDtypeStruct((5, batch, t, d), BF16),
                   jax.ShapeDtypeStruct((2, batch, t, d), F32),
                   jax.ShapeDtypeStruct((batch, t // c, d), F32)],
        scratch_shapes=[pltpu.VMEM((8, width), F32)],
        compiler_params=_params("arbitrary", "arbitrary"),
        name="rwkv_prep",
    )(p_r, *consts)
    const1 = lambda a: pl.BlockSpec(a.shape, lambda i: (0, 0))
    return pl.pallas_call(
        _rwkv_scan_kernel,
        grid=(t // c,),
        in_specs=[pl.BlockSpec((5, batch, c, d), lambda i: (0, 0, i, 0)),
                  pl.BlockSpec((2, batch, c, d), lambda i: (0, 0, i, 0)),
                  pl.BlockSpec((batch, 1, 1, d), lambda i: (0, i, 0, 0)),
                  const1(lnw), const1(lnb)],
        out_specs=pl.BlockSpec((batch, c, d), lambda i: (0, i, 0)),
        out_shape=jax.ShapeDtypeStruct((batch, t, d), BF16),
        scratch_shapes=[pltpu.VMEM((batch * RWKV_HEADS, HEAD_DIM, HEAD_DIM), F32)],
        compiler_params=_params("arbitrary"),
        name="rwkv_scan",
    )(ops, aux, wc.reshape(batch, t // c, 1, d), lnw, lnb)


def _compress_kernel(zk_ref, zv_ref, pek_ref, pev_ref, k1_ref, k2_ref, v1_ref, v2_ref, ko_ref, vo_ref):
    def one(z_ref, pe_ref, w1_ref, w2_ref):
        n16 = z_ref.shape[2] // CMP_STRIDE
        dk = HEAD_DIM
        first = second = None
        for l in range(CMP_STRIDE):
            z = z_ref[0, 0, pl.ds(l, n16, stride=CMP_STRIDE), :]
            lo = _dot((z + pe_ref[l:l + 1, :]).astype(BF16), w1_ref[l * dk:(l + 1) * dk, :])
            u = CMP_STRIDE + l
            hi = _dot((z + pe_ref[u:u + 1, :]).astype(BF16), w1_ref[u * dk:(u + 1) * dk, :])
            first = lo if first is None else first + lo
            second = hi if second is None else second + hi
        hidden = first + pltpu.roll(second, n16 - 1, axis=0)
        return _dot(jax.nn.gelu(hidden).astype(BF16), w2_ref[...])

    kc = one(zk_ref, pek_ref, k1_ref, k2_ref)
    blk = lax.broadcasted_iota(jnp.int32, kc.shape, 0)
    col = lax.broadcasted_iota(jnp.int32, kc.shape, 1)
    per = SEL_BLOCK // CMP_STRIDE
    hi = blk // per
    lo = CMP_STRIDE * (blk % per) + (CMP_BLOCK - 1)
    feat = _position_features(hi, lo, col)
    ko_ref[0] = jnp.concatenate([kc, feat], axis=1).astype(BF16)
    vc = one(zv_ref, pev_ref, v1_ref, v2_ref)
    vc_t = jnp.transpose(jnp.concatenate([vc, jnp.zeros_like(vc)], axis=1))
    vo_ref[0] = vc_t[:HEAD_DIM].astype(BF16)


def _compress(zk, zv, pek, pev, k1, k2, v1, v2):
    batch, hk, t, dk = zk.shape
    n16 = t // CMP_STRIDE
    zspec = pl.BlockSpec((1, 1, t, dk), lambda b, h: (b, h, 0, 0))
    full = lambda a: pl.BlockSpec(a.shape, lambda b, h: (0,) * a.ndim)
    return pl.pallas_call(
        _compress_kernel,
        grid=(batch, hk),
        in_specs=[zspec, zspec, full(pek), full(pev), full(k1), full(k2), full(v1), full(v2)],
        out_specs=[pl.BlockSpec((1, n16, 2 * dk), lambda b, h: (b * hk + h, 0, 0)),
                   pl.BlockSpec((1, dk, n16), lambda b, h: (b * hk + h, 0, 0))],
        out_shape=[jax.ShapeDtypeStruct((batch * hk, n16, 2 * dk), BF16),
                   jax.ShapeDtypeStruct((batch * hk, dk, n16), BF16)],
        compiler_params=_params("arbitrary", "arbitrary"),
        name="nsa_compress",
    )(zk, zv, pek, pev, k1, k2, v1, v2)


def _query_features(qt, slopes_ref, hk):
    lane_g = lax.broadcasted_iota(jnp.int32, (1, QG), 1) // Q_BLOCK
    slope = jnp.zeros((1, QG), F32)
    for g in range(NSA_GROUP):
        slope = jnp.where(lane_g == g, slopes_ref[hk * NSA_GROUP + g], slope)
    row = lax.broadcasted_iota(jnp.int32, (HEAD_DIM, QG), 0)
    extra = jnp.zeros((HEAD_DIM, QG), F32)
    for n, term in enumerate(LOG2E_TERMS):
        extra = jnp.where(row == 2 * n, (SEL_BLOCK * term) * slope, jnp.where(row == 2 * n + 1, term * slope, extra))
    return jnp.concatenate([qt * (HEAD_DIM ** -0.5 * LOG2E), extra], axis=0).astype(BF16)


def _position_features(hi, lo, col):
    return jnp.where(col < 2 * len(LOG2E_TERMS), jnp.where(col % 2 == 0, hi, lo), 0).astype(F32)


def _cmp_select_kernel(slopes_ref, qt_ref, kc_ref, vct_ref, ovt_ref, oct_ref, selt_ref, imp_ref):
    hk = pl.program_id(1)
    nblk = qt_ref.shape[2]
    width = nblk * Q_BLOCK
    ncp = kc_ref.shape[1]

    def attend(rows):
        cmp_end = lax.broadcasted_iota(jnp.int32, (rows, Q_BLOCK), 0) * CMP_STRIDE + (CMP_BLOCK - 1)
        for u in range(nblk):
            q0 = (pl.program_id(2) * nblk + u) * Q_BLOCK
            q_aug = _query_features(qt_ref[0, 0, u], slopes_ref, hk)
            s = _dot(kc_ref[0, 0:rows, :], q_aug)
            ok = cmp_end <= q0 + lax.broadcasted_iota(jnp.int32, (rows, Q_BLOCK), 1)
            any_ok = (q0 + lax.broadcasted_iota(jnp.int32, (1, Q_BLOCK), 1) >= CMP_BLOCK - 1).astype(F32)
            p_sum = jnp.zeros((rows, Q_BLOCK), F32)
            probs = []
            for g in range(NSA_GROUP):
                sg = jnp.where(ok, s[:, g * Q_BLOCK:(g + 1) * Q_BLOCK], NEG)
                e = jnp.exp2(sg - jnp.max(sg, axis=0, keepdims=True))
                p = e * (any_ok / jnp.sum(e, axis=0, keepdims=True))
                p_sum = p_sum + p
                probs.append(p.astype(BF16))
            oct_ref[0, 0, u] = _dot(vct_ref[0, :, 0:rows], jnp.concatenate(probs, axis=1))
            imp_ref[:, u * Q_BLOCK:(u + 1) * Q_BLOCK] = _dot_exact_lhs(ovt_ref[:, 0:rows], p_sum)

    chunk = min(CMP_ROW_CHUNK, ncp)
    n_chunks = ncp // chunk
    last_t = (pl.program_id(2) + 1) * width - 1
    needed = jnp.maximum((last_t - (CMP_BLOCK - 1)) // CMP_STRIDE + 1, 1)
    needed_chunks = jnp.minimum((needed + chunk - 1) // chunk, n_chunks)
    for nck in range(1, n_chunks + 1):
        pl.when(needed_chunks == nck)(lambda nck=nck: attend(nck * chunk))

    imp = imp_ref[...]
    blk = lax.broadcasted_iota(jnp.int32, (LANES, width), 0)
    cur = (pl.program_id(2) * width + lax.broadcasted_iota(jnp.int32, (LANES, width), 1)) // SEL_BLOCK
    valid = blk <= cur
    forced = (blk == 0) | (blk == cur) | (blk == cur - 1)
    x = jnp.where(valid, jnp.where(forced, BIG, imp), NEG)
    blk_f = blk.astype(F32)
    chosen = jnp.zeros((LANES, width), jnp.bool_)
    for _ in range(SEL_TOPK):
        m = jnp.max(x, axis=0, keepdims=True)
        first = jnp.min(jnp.where(x == m, blk_f, float(LANES)), axis=0, keepdims=True)
        hit = blk_f == first
        chosen = chosen | hit
        x = jnp.where(hit, -jnp.inf, x)
    sel = (chosen & valid).astype(F32)
    for u in range(nblk):
        selt_ref[0, 0, u] = sel[:, u * Q_BLOCK:(u + 1) * Q_BLOCK]


def _cmp_select(slopes, q_t, kc_aug, vc_t, ov_t):
    batch, hk, nq, dk, qg = q_t.shape
    ncp = kc_aug.shape[1]
    nblk = CMP_Q_BLOCKS
    return pl.pallas_call(
        _cmp_select_kernel,
        grid=(batch, hk, nq // nblk),
        in_specs=[pl.BlockSpec(memory_space=pltpu.SMEM),
                  pl.BlockSpec((1, 1, nblk, dk, qg), lambda b, h, i: (b, h, i, 0, 0)),
                  pl.BlockSpec((1, ncp, 2 * dk), lambda b, h, i: (b * hk + h, 0, 0)),
                  pl.BlockSpec((1, dk, ncp), lambda b, h, i: (b * hk + h, 0, 0)),
                  pl.BlockSpec(ov_t.shape, lambda b, h, i: (0, 0))],
        out_specs=[pl.BlockSpec((1, 1, nblk, dk, qg), lambda b, h, i: (b, h, i, 0, 0)),
                   pl.BlockSpec((1, 1, nblk, LANES, Q_BLOCK), lambda b, h, i: (b, h, i, 0, 0))],
        out_shape=[jax.ShapeDtypeStruct(q_t.shape, F32),
                   jax.ShapeDtypeStruct((batch, hk, nq, LANES, Q_BLOCK), F32)],
        scratch_shapes=[pltpu.VMEM((LANES, nblk * Q_BLOCK), F32)],
        compiler_params=_params("arbitrary", "arbitrary", "arbitrary"),
        name="nsa_cmp_select",
    )(slopes, q_t, kc_aug, vc_t, ov_t)


def _sel_win_kernel(ids_ref, cnt_ref, slopes_ref, qt_ref, selt_ref, ks_ref, vst_ref, kw_ref, vwt_ref,
                    oct_ref, glt_ref, yt_ref):
    b, hk, i = pl.program_id(0), pl.program_id(1), pl.program_id(2)
    step = (b * pl.num_programs(1) + hk) * pl.num_programs(2) + i
    q0 = i * Q_BLOCK
    kt = KEY_TILE
    q_aug = _query_features(qt_ref[0, 0, 0], slopes_ref, hk)
    lane_minus_row = (lax.broadcasted_iota(jnp.int32, (kt, Q_BLOCK), 1)
                      - lax.broadcasted_iota(jnp.int32, (kt, Q_BLOCK), 0))

    def all_scores(tiles):
        s_all = _dot(jnp.concatenate([k_tile for k_tile, _, _ in tiles], axis=0), q_aug)
        return [s_all[n * kt:(n + 1) * kt] for n in range(len(tiles))]

    def attend(state, tiles, scores=None):
        m_old, acc_old = state
        scores = all_scores(tiles) if scores is None else scores
        probs, maxes = [[] for _ in tiles], []
        for g in range(NSA_GROUP):
            gsl = slice(g * Q_BLOCK, (g + 1) * Q_BLOCK)
            masked = [jnp.where(mask, s[:, gsl], NEG) for s, (_, _, mask) in zip(scores, tiles)]
            top = jnp.max(masked[0].reshape(kt // 8, 8, Q_BLOCK), axis=0)
            for sg in masked[1:]:
                top = jnp.maximum(top, jnp.max(sg.reshape(kt // 8, 8, Q_BLOCK), axis=0))
            mg = jnp.maximum(m_old[:, gsl], jnp.max(top, axis=0, keepdims=True))
            for n, sg in enumerate(masked):
                probs[n].append(jnp.exp2(sg - mg).astype(BF16))
            maxes.append(mg)
        m_new = jnp.concatenate(maxes, axis=1)
        p_all = jnp.concatenate([jnp.concatenate(p, axis=1) for p in probs], axis=0)
        vt_all = jnp.concatenate([vt_tile for _, vt_tile, _ in tiles], axis=1)
        return m_new, jnp.exp2(m_old - m_new) * acc_old + _dot(vt_all, p_all)

    start = (jnp.full((1, QG), SOFTMAX_FLOOR, F32), jnp.zeros((V_ROWS, QG), F32))
    result = lambda state: state[1][:HEAD_DIM] / state[1][HEAD_DIM:HEAD_DIM + 1]

    count = cnt_ref[step]
    last_tile = ks_ref.shape[2] // kt - 1

    def sel_tile(n):
        j = jnp.minimum(ids_ref[step * MAX_KEY_TILES + jnp.minimum(n, MAX_KEY_TILES - 1)], last_tile)
        k0 = pl.multiple_of(j * kt, kt)
        per = kt // SEL_BLOCK
        picked = jnp.concatenate(
            [jnp.broadcast_to(selt_ref[0, 0, 0, pl.ds(j * per + u, 1), :], (SEL_BLOCK, Q_BLOCK)) for u in range(per)],
            axis=0)
        causal_from = jnp.where(n < count, k0 - q0, 1 << 30)
        mask = (picked > 0.5) & (lane_minus_row >= causal_from)
        return ks_ref[0, 0, pl.ds(k0, kt), :], vst_ref[0, 0, j], mask

    last = (q0 + Q_BLOCK - 1) // kt

    def win_tile(u):
        j = last - u
        jc = jnp.maximum(j, 0)
        k0 = pl.multiple_of(jc * kt, kt)
        dist = lane_minus_row + jnp.where(j >= 0, q0 - k0, -(1 << 30))
        mask = (dist >= 0) & (dist < WINDOW)
        return kw_ref[0, 0, pl.ds(k0, kt), :], vwt_ref[0, 0, jc], mask

    win_tiles = [win_tile(u) for u in range(WIN_TILES)]
    sel_tiles = [sel_tile(n) for n in range(SEL_FIRST)]
    scores = all_scores(win_tiles + sel_tiles)
    state_w = attend(start, win_tiles, scores[:WIN_TILES])
    state_s = attend(start, sel_tiles, scores[WIN_TILES:])
    state_s = lax.fori_loop(
        0, (jnp.maximum(count - SEL_FIRST, 0) + SEL_GROUP - 1) // SEL_GROUP,
        lambda n, st: attend(st, [sel_tile(SEL_FIRST + SEL_GROUP * n + u) for u in range(SEL_GROUP)]), state_s)
    o_s, o_w = result(state_s), result(state_w)

    gates = jax.nn.sigmoid(glt_ref[0, 0])
    o_c = oct_ref[0, 0, 0]
    outs = []
    for g in range(NSA_GROUP):
        gsl = slice(g * Q_BLOCK, (g + 1) * Q_BLOCK)
        outs.append(gates[3 * g:3 * g + 1] * o_c[:, gsl] + gates[3 * g + 1:3 * g + 2] * o_s[:, gsl]
                    + gates[3 * g + 2:3 * g + 3] * o_w[:, gsl])
    yt_ref[0] = jnp.concatenate(outs, axis=0).astype(yt_ref.dtype)


def _sel_win(tile_ids, tile_cnt, slopes, q_t, sel_t, ks, vs_t, kw, vw_t, oc_t, gl_t):
    batch, hk, nq, dk, qg = q_t.shape
    t = nq * Q_BLOCK
    nkt = t // KEY_TILE
    grp = qg // Q_BLOCK
    qspec = pl.BlockSpec((1, 1, 1, dk, qg), lambda b, h, i, *_: (b, h, i, 0, 0))
    kspec = pl.BlockSpec((1, 1, t, 2 * dk), lambda b, h, i, *_: (b, h, 0, 0))
    vspec = pl.BlockSpec((1, 1, nkt, V_ROWS, KEY_TILE), lambda b, h, i, *_: (b, h, 0, 0, 0))
    grid_spec = pltpu.PrefetchScalarGridSpec(
        num_scalar_prefetch=2,
        grid=(batch, hk, nq),
        in_specs=[pl.BlockSpec(memory_space=pltpu.SMEM), qspec,
                  pl.BlockSpec((1, 1, 1, LANES, Q_BLOCK), lambda b, h, i, *_: (b, h, i, 0, 0)),
                  kspec, vspec, kspec, vspec, qspec,
                  pl.BlockSpec((1, 1, 3 * grp, Q_BLOCK), lambda b, h, i, *_: (b, h, 0, i))],
        out_specs=pl.BlockSpec((1, grp * dk, Q_BLOCK), lambda b, h, i, *_: (b, h, i)),
    )
    return pl.pallas_call(
        _sel_win_kernel,
        grid_spec=grid_spec,
        out_shape=jax.ShapeDtypeStruct((batch, hk * grp * dk, t), BF16),
        compiler_params=_params("arbitrary", "arbitrary", "arbitrary"),
        name="nsa_sel_win",
    )(tile_ids, tile_cnt, slopes, q_t, sel_t, ks, vs_t, kw, vw_t, oc_t, gl_t)


def _ffn_kernel(x_ref, yr_ref, ynt_ref, wo1_ref, wo2_ref, g2_ref, wg_ref, wu_ref, wd_ref, gf_ref, o_ref):
    h1 = (x_ref[...] + _dot(yr_ref[...].astype(BF16), wo1_ref[...])
          + _dot_tn(ynt_ref[0].astype(BF16), wo2_ref[...]))
    hn = _rms(h1, g2_ref[...]).astype(BF16)
    gate = _dot(hn, wg_ref[...])
    up = _dot(hn, wu_ref[...])
    act = gate * jax.nn.sigmoid(gate) * up
    o_ref[...] = _rms(h1 + _dot(act.astype(BF16), wd_ref[...]), gf_ref[...])


def _ffn(x2, yr, yn_t, wo1, wo2, g2, wg, wu, wd, gf, tm=512):
    m, d = x2.shape
    _, dn, t = yn_t.shape
    per_seq = t // tm
    row = lambda n: pl.BlockSpec((tm, n), lambda i: (i, 0))
    const = lambda a: pl.BlockSpec(a.shape, lambda i: (0, 0), pipeline_mode=pl.Buffered(1))
    return pl.pallas_call(
        _ffn_kernel,
        grid=(m // tm,),
        in_specs=[row(d), row(yr.shape[1]),
                  pl.BlockSpec((1, dn, tm), lambda i: (i // per_seq, 0, i % per_seq)),
                  const(wo1), const(wo2), const(g2), const(wg), const(wu), const(wd), const(gf)],
        out_specs=row(d),
        out_shape=jax.ShapeDtypeStruct((m, d), F32),
        compiler_params=_params("arbitrary"),
        name="outproj_ffn",
    )(x2, yr, yn_t, wo1, wo2, g2, wg, wu, wd, gf)


def _overlap_matrix_t(t):
    n16 = t // CMP_STRIDE
    n_cmp = (t - CMP_BLOCK) // CMP_STRIDE + 1
    n_sel = t // SEL_BLOCK
    cmp_start = np.arange(n_cmp) * CMP_STRIDE
    sel_start = np.arange(n_sel) * SEL_BLOCK
    ov = np.clip(np.minimum(cmp_start[:, None] + CMP_BLOCK, sel_start[None, :] + SEL_BLOCK)
                 - np.maximum(cmp_start[:, None], sel_start[None, :]), 0, None) / CMP_STRIDE
    full = np.zeros((LANES, n16), np.float32)
    full[:n_sel, :n_cmp] = ov.T
    return jnp.asarray(full, BF16)


def kernel(x, norm1_g, w_in, mu_shift, rwkv_w0, rwkv_w2, rwkv_a0, rwkv_a2, rwkv_g2, rwkv_k_k, rwkv_k_a,
           rwkv_r_k, rwkv_lnx_w, rwkv_lnx_b, nsa_pe_k, nsa_pe_v, nsa_cmp_k_w1, nsa_cmp_k_w2, nsa_cmp_v_w1,
           nsa_cmp_v_w2, w_out, norm2_g, ffn_w_gate, ffn_w_up, ffn_w_down, norm_f_g):
    batch, t, d_model = x.shape
    assert w_in.shape[0] == 1, "the final RMSNorm is fused into the (single) layer's FFN kernel"
    assert t % INPROJ_ROWS == 0 and t % RWKV_PREP_ROWS == 0 and t // SEL_BLOCK <= LANES
    hk, grp, dk = NSA_KV_HEADS, NSA_GROUP, HEAD_DIM
    nq = t // Q_BLOCK
    slopes = 2.0 ** (-8.0 * jnp.arange(1, NSA_Q_HEADS + 1, dtype=F32) / NSA_Q_HEADS)
    nsa_pad = _round_up(N_NSA_COLS, LANES)
    row = lambda a: a.reshape(1, -1)
    i = 0

    h = x.reshape(batch * t, d_model)
    w_r = w_in[i][:, :N_RWKV_COLS].astype(BF16)
    w_n = jnp.pad(w_in[i][:, N_RWKV_COLS:], ((0, 0), (0, nsa_pad - N_NSA_COLS))).astype(BF16)
    p_r, q_t, ks, kw, vs_t, vw_t, zk, zv, gl_t = _inproj(h, batch, row(norm1_g[i]), w_r, w_n)

    y_rwkv = _rwkv(p_r.reshape(batch, t, N_RWKV_COLS), row(mu_shift[i]), row(rwkv_w0[i]), rwkv_w2[i].astype(BF16),
                   row(rwkv_a0[i]), rwkv_a2[i].astype(BF16), rwkv_g2[i].astype(BF16), row(rwkv_k_k[i]),
                   row(rwkv_k_a[i]), row(rwkv_r_k[i]), row(rwkv_lnx_w[i]), row(rwkv_lnx_b[i]))
    y_rwkv = y_rwkv.reshape(batch * t, D_RWKV)

    kc_aug, vc_t = _compress(zk, zv, nsa_pe_k[i], nsa_pe_v[i],
                             nsa_cmp_k_w1[i].astype(BF16), nsa_cmp_k_w2[i].astype(BF16),
                             nsa_cmp_v_w1[i].astype(BF16), nsa_cmp_v_w2[i].astype(BF16))
    oc_t, sel_t = _cmp_select(slopes, q_t, kc_aug, vc_t, _overlap_matrix_t(t))

    blocks_per_tile = KEY_TILE // SEL_BLOCK
    active = sel_t.reshape(batch, hk, nq, MAX_KEY_TILES, blocks_per_tile * Q_BLOCK).max(axis=-1) > 0
    tile_ids = jnp.argsort(jnp.logical_not(active), axis=-1, stable=True).astype(jnp.int32).reshape(-1)
    tile_cnt = active.sum(axis=-1).astype(jnp.int32).reshape(-1)
    y_nsa_t = _sel_win(tile_ids, tile_cnt, slopes, q_t, sel_t, ks, vs_t, kw, vw_t, oc_t, gl_t)

    out = _ffn(h, y_rwkv, y_nsa_t, w_out[i][:D_RWKV].astype(BF16), w_out[i][D_RWKV:].astype(BF16),
               row(norm2_g[i]), ffn_w_gate[i].astype(BF16), ffn_w_up[i].astype(BF16),
               ffn_w_down[i].astype(BF16), row(norm_f_g))
    return out.reshape(batch, t, d_model)
```

```python
import ml_dtypes
import numpy as np
import jax
import jax.numpy as jnp
from jax import lax
from jax.experimental import pallas as pl
from jax.experimental.pallas import tpu as pltpu

F32 = jnp.float32
BF16 = jnp.bfloat16

HEAD_DIM = 64
RWKV_HEADS = 8
D_RWKV = RWKV_HEADS * HEAD_DIM
NSA_Q_HEADS = 8
NSA_KV_HEADS = 2
NSA_GROUP = NSA_Q_HEADS // NSA_KV_HEADS
D_NSA = NSA_Q_HEADS * HEAD_DIM
D_KV = NSA_KV_HEADS * HEAD_DIM
LORA_W, LORA_A, LORA_G = 64, 64, 128
N_RWKV_COLS = 3 * D_RWKV + LORA_W + LORA_A + LORA_G
N_NSA_COLS = D_NSA + 6 * D_KV + 3 * NSA_Q_HEADS
CMP_BLOCK, CMP_STRIDE = 32, 16
SEL_BLOCK, SEL_TOPK = 64, 16
WINDOW = 512
Q_BLOCK = 128
NORM_EPS = 1e-6
GN_EPS = 64e-5
NEG = -1e30
BIG = 1e30

LANES = 128
RWKV_CHUNK = 64
RWKV_PREP_ROWS = 8 * RWKV_CHUNK
RWKV_SCAN_CHUNKS = 4
PREP_SPLIT = 2
KEY_TILE = 128
INPROJ_ROWS = 512
MAX_KEY_TILES = LANES * SEL_BLOCK // KEY_TILE
WIN_TILES = (WINDOW + Q_BLOCK) // KEY_TILE
SEL_FIRST = 11
SEL_GROUP = 2
CMP_Q_BLOCKS = 4
CMP_ROW_CHUNK = 128
V_ROWS = HEAD_DIM + 16
SOFTMAX_FLOOR = -1e20
QG = NSA_GROUP * Q_BLOCK
VMEM_LIMIT = 56 * 1024 * 1024


def _bf16_terms(x, count):
    terms = []
    for _ in range(count):
        terms.append(float(np.asarray(x, ml_dtypes.bfloat16)))
        x = x - terms[-1]
    return tuple(terms)


LOG2E = float(np.log2(np.e))
LOG2E_TERMS = _bf16_terms(LOG2E, 3)


def _round_up(n, m):
    return -(-n // m) * m


def _dot(a, b):
    return jnp.dot(a, b, preferred_element_type=F32)


def _dot_tn(a, b):
    return lax.dot_general(a, b, (((0,), (0,)), ((), ())), preferred_element_type=F32)


def _split(a, terms):
    pieces = []
    for _ in range(terms - 1):
        pieces.append(a.astype(BF16))
        a = a - pieces[-1].astype(F32)
    return pieces + [a.astype(BF16)]


def _dot_exact_lhs(a, b, terms=3):
    return sum(_dot(a, piece) for piece in _split(b, terms))


def _dot_exact_rhs(a, b, terms=3):
    return sum(_dot(piece, b) for piece in _split(a, terms))


def _bdot(spec, a, b):
    return jnp.einsum(spec, a, b, preferred_element_type=F32)


def _rms(x, g):
    return x * lax.rsqrt(jnp.mean(x * x, axis=-1, keepdims=True) + NORM_EPS) * g


def _params(*sem):
    return pltpu.CompilerParams(dimension_semantics=sem, vmem_limit_bytes=VMEM_LIMIT)


def _inproj_kernel(x_ref, g_ref, wr_ref, wn_ref, pr_ref, qt_ref, ks_ref, kw_ref, vst_ref, vwt_ref,
                   kc_ref, vc_ref, glt_ref):
    xb = _rms(x_ref[...], g_ref[...]).astype(BF16)
    pr_ref[...] = _dot(xb, wr_ref[...])
    pn = _dot(xb, wn_ref[...])
    rows = pn.shape[0]
    dk, hk_n = HEAD_DIM, NSA_KV_HEADS
    group = lambda j: pn[:, D_NSA + j * D_KV:D_NSA + (j + 1) * D_KV]

    for half in range(rows // Q_BLOCK):
        q_tr = jnp.transpose(pn[half * Q_BLOCK:(half + 1) * Q_BLOCK, :D_NSA])
        for hk in range(hk_n):
            base = hk * NSA_GROUP * dk
            qt_ref[0, hk, half] = jnp.concatenate(
                [q_tr[base + g * dk:base + (g + 1) * dk, :] for g in range(NSA_GROUP)], axis=1)

    pos = pl.program_id(1) * rows + lax.broadcasted_iota(jnp.int32, (rows, dk), 0)
    col = lax.broadcasted_iota(jnp.int32, (rows, dk), 1)
    pos_cols = _position_features(pos // SEL_BLOCK, pos % SEL_BLOCK, col)
    kt = KEY_TILE
    ones_rows = (lax.broadcasted_iota(jnp.int32, (V_ROWS - dk, kt), 0) == 0).astype(F32)
    kc, vc, ks, vs, kw, vw = (group(j) for j in range(6))
    vs_tr, vw_tr = jnp.transpose(vs), jnp.transpose(vw)
    gl_tr = jnp.transpose(pn[:, D_NSA + 6 * D_KV:])
    n_gate = 3 * NSA_GROUP
    for hk in range(hk_n):
        sl = slice(hk * dk, (hk + 1) * dk)
        kc_ref[0, hk] = kc[:, sl]
        vc_ref[0, hk] = vc[:, sl]
        ks_ref[0, hk] = jnp.concatenate([ks[:, sl], pos_cols], axis=1).astype(BF16)
        kw_ref[0, hk] = jnp.concatenate([kw[:, sl], pos_cols], axis=1).astype(BF16)
        for u in range(rows // kt):
            vst_ref[0, hk, u] = jnp.concatenate([vs_tr[sl, u * kt:(u + 1) * kt], ones_rows], axis=0).astype(BF16)
            vwt_ref[0, hk, u] = jnp.concatenate([vw_tr[sl, u * kt:(u + 1) * kt], ones_rows], axis=0).astype(BF16)
        glt_ref[0, hk] = gl_tr[hk * n_gate:(hk + 1) * n_gate, :]


def _inproj(x2, batch, g, w_r, w_n):
    m, d = x2.shape
    t = m // batch
    tm = INPROJ_ROWS
    nt = t // tm
    nr, nn = w_r.shape[1], w_n.shape[1]
    hk, dk = NSA_KV_HEADS, HEAD_DIM
    const = lambda a: pl.BlockSpec(a.shape, lambda b, i: (0, 0))
    keys = pl.BlockSpec((1, hk, tm, 2 * dk), lambda b, i: (b, 0, i, 0))
    vals = pl.BlockSpec((1, hk, tm // KEY_TILE, V_ROWS, KEY_TILE), lambda b, i: (b, 0, i, 0, 0))
    cmp_in = pl.BlockSpec((1, hk, tm, dk), lambda b, i: (b, 0, i, 0))
    keys_shape = jax.ShapeDtypeStruct((batch, hk, t, 2 * dk), BF16)
    vals_shape = jax.ShapeDtypeStruct((batch, hk, t // KEY_TILE, V_ROWS, KEY_TILE), BF16)
    cmp_shape = jax.ShapeDtypeStruct((batch, hk, t, dk), F32)
    return pl.pallas_call(
        _inproj_kernel,
        grid=(batch, nt),
        in_specs=[pl.BlockSpec((tm, d), lambda b, i: (b * nt + i, 0)), const(g), const(w_r), const(w_n)],
        out_specs=[pl.BlockSpec((tm, nr), lambda b, i: (b * nt + i, 0)),
                   pl.BlockSpec((1, hk, tm // Q_BLOCK, dk, QG), lambda b, i: (b, 0, i, 0, 0)),
                   keys, keys, vals, vals, cmp_in, cmp_in,
                   pl.BlockSpec((1, hk, 3 * NSA_GROUP, tm), lambda b, i: (b, 0, 0, i))],
        out_shape=[jax.ShapeDtypeStruct((m, nr), F32),
                   jax.ShapeDtypeStruct((batch, hk, t // Q_BLOCK, dk, QG), F32),
                   keys_shape, keys_shape, vals_shape, vals_shape, cmp_shape, cmp_shape,
                   jax.ShapeDtypeStruct((batch, hk, 3 * NSA_GROUP, t), F32)],
        compiler_params=_params("arbitrary", "arbitrary"),
        name="inproj",
    )(x2, g, w_r, w_n)


def _rwkv_operands(p, p_prev, mu, w0, w2, a0, a2, g2, k_k, k_a, r_k, bd, tri):
    c = RWKV_CHUNK
    ps = p + mu * (p_prev - p)
    d = D_RWKV
    r, k, v = ps[:, 0:d], ps[:, d:2 * d], ps[:, 2 * d:3 * d]
    dw = ps[:, 3 * d:3 * d + LORA_W]
    da = ps[:, 3 * d + LORA_W:3 * d + LORA_W + LORA_A]
    dg = ps[:, 3 * d + LORA_W + LORA_A:]

    z = -(w0 + _dot(jnp.tanh(dw).astype(BF16), w2))
    softplus = jnp.maximum(z, 0.0) + jnp.log1p(jnp.exp(-jnp.abs(z)))
    logw = -jnp.exp(-softplus - 0.5)
    a = jax.nn.sigmoid(a0 + _dot(da.astype(BF16), a2))
    g = _dot(jax.nn.sigmoid(dg).astype(BF16), g2)

    kk = k * k_k
    kk = kk / jnp.maximum(jnp.sqrt(_dot_exact_rhs(kk * kk, bd, PREP_SPLIT)), 1e-12)
    kp = k * (1.0 + (a - 1.0) * k_a)
    bonus = _dot_exact_rhs(r * kp * r_k, bd, PREP_SPLIT) * v

    n_chunks = p.shape[0] // c
    cum = jnp.concatenate([_dot_exact_lhs(tri, logw[ci * c:(ci + 1) * c], PREP_SPLIT) for ci in range(n_chunks)],
                          axis=0)
    e_pos, e_neg = jnp.exp(cum), jnp.exp(-cum)
    wc = jnp.concatenate([e_pos[(ci + 1) * c - 1:(ci + 1) * c, :] for ci in range(n_chunks)], axis=0)
    ops = (r * e_pos, kp * e_neg, kk * a * e_neg, -kk * jnp.exp(cum - logw), v)
    return tuple(o.astype(BF16) for o in ops), wc, bonus, g


def _rwkv_prep_kernel(p_ref, mu_ref, w0_ref, w2_ref, a0_ref, a2_ref, g2_ref, kk_ref, ka_ref, rk_ref, bd_ref, tri_ref,
                      ops_ref, aux_ref, wc_ref, carry_ref):
    rows = p_ref.shape[1]

    @pl.when(pl.program_id(1) == 0)
    def _():
        carry_ref[...] = jnp.zeros_like(carry_ref)

    p = p_ref[0]
    row = lax.broadcasted_iota(jnp.int32, p.shape, 0)
    p_prev = jnp.where(row == 0, carry_ref[7:8, :], pltpu.roll(p, 1, axis=0))
    carry_ref[...] = p[rows - 8:, :]
    ops, wc, bonus, g = _rwkv_operands(
        p, p_prev, mu_ref[...], w0_ref[...], w2_ref[...], a0_ref[...], a2_ref[...], g2_ref[...],
        kk_ref[...], ka_ref[...], rk_ref[...], bd_ref[...], tri_ref[...])
    for idx, o in enumerate(ops):
        ops_ref[idx, 0] = o
    aux_ref[0, 0] = bonus
    aux_ref[1, 0] = g
    wc_ref[0] = wc


def _rwkv_scan_kernel(ops_ref, aux_ref, wc_ref, lnw_ref, lnb_ref, y_ref, s_ref):
    @pl.when(pl.program_id(0) == 0)
    def _():
        s_ref[...] = jnp.zeros_like(s_ref)

    _rwkv_step(ops_ref, aux_ref, wc_ref, lnw_ref, lnb_ref, y_ref, s_ref)


def _rwkv_step(ops_ref, aux_ref, wc_ref, lnw_ref, lnb_ref, y_ref, s_ref):
    _, nb, span, d = ops_ref.shape
    c = RWKV_CHUNK
    nh, dk = RWKV_HEADS, HEAD_DIM

    row = lax.broadcasted_iota(jnp.int32, (1, c, c), 1)
    col = lax.broadcasted_iota(jnp.int32, (1, c, c), 2)
    strict = col < row
    incl = col <= row
    eye = (row == col).astype(F32)

    def heads(x):
        return jnp.stack([x[b, :, h * dk:(h + 1) * dk] for b in range(nb) for h in range(nh)], axis=0)

    def independent(j):
        rt, kt, bt, at, vb = (heads(ops_ref[idx, :, j * c:(j + 1) * c, :]) for idx in range(5))
        ar = jnp.concatenate([at, rt], axis=1)
        bk = jnp.concatenate([bt, kt], axis=1)
        amat = _bdot("nik,njk->nij", ar, bk)
        n_ab = jnp.where(strict, amat[:, :c, :c], 0.0)
        a_ak = jnp.where(strict, amat[:, :c, c:], 0.0).astype(BF16)
        a_rb = jnp.where(incl, amat[:, c:, :c], 0.0).astype(BF16)
        a_rk = jnp.where(incl, amat[:, c:, c:], 0.0).astype(BF16)
        inv = eye + n_ab
        pw = n_ab
        for _ in range(int(np.log2(c)) - 1):
            pwb = pw.astype(BF16)
            pw = _bdot("nij,njk->nik", pwb, pwb)
            inv = inv + _bdot("nij,njk->nik", pw.astype(BF16), inv.astype(BF16))
        return ar, bk, vb, _bdot("nij,njv->niv", a_ak, vb), a_rb, a_rk, inv.astype(BF16)

    parts = [independent(j) for j in range(span // c)]
    s = s_ref[...]
    for j, (ar, bk, vb, akv, a_rb, a_rk, inv) in enumerate(parts):
        ar_s = _bdot("nik,nvk->niv", ar, s.astype(BF16))
        ub = _bdot("nij,njv->niv", inv, (ar_s[:, :c] + akv).astype(BF16)).astype(BF16)
        y = ar_s[:, c:] + _bdot("nij,njv->niv", a_rb, ub) + _bdot("nij,njv->niv", a_rk, vb)
        uv_t = jnp.swapaxes(jnp.concatenate([ub, vb], axis=1), 1, 2)
        s = (s + _bdot("nvi,nik->nvk", uv_t, bk)) * heads(wc_ref[:, j])

        mean = jnp.mean(y, axis=-1, keepdims=True)
        var = jnp.mean(jnp.square(y - mean), axis=-1, keepdims=True)
        yn = (y - mean) * lax.rsqrt(var + GN_EPS)
        rsl = slice(j * c, (j + 1) * c)
        for b in range(nb):
            wide = jnp.concatenate([yn[b * nh + h] for h in range(nh)], axis=-1)
            y_ref[b, rsl, :] = ((wide * lnw_ref[...] + lnb_ref[...] + aux_ref[0, b, rsl, :])
                                * aux_ref[1, b, rsl, :]).astype(y_ref.dtype)
    s_ref[...] = s


def _rwkv(p_r, mu, w0, w2, a0, a2, g2, k_k, k_a, r_k, lnw, lnb):
    batch, t, width = p_r.shape
    c = RWKV_CHUNK
    rows = RWKV_PREP_ROWS
    d = D_RWKV
    head = np.arange(d) // HEAD_DIM
    bd = jnp.asarray(head[:, None] == head[None, :], BF16)
    tri = jnp.asarray(np.tril(np.ones((c, c))), BF16)
    const2 = lambda a: pl.BlockSpec(a.shape, lambda b, i: (0, 0))
    consts = (mu, w0, w2, a0, a2, g2, k_k, k_a, r_k, bd, tri)
    ops, aux, wc = pl.pallas_call(
        _rwkv_prep_kernel,
        grid=(batch, t // rows),
        in_specs=[pl.BlockSpec((1, rows, width), lambda b, i: (b, i, 0))] + [const2(a) for a in consts],
        out_specs=[pl.BlockSpec((5, 1, rows, d), lambda b, i: (0, b, i, 0)),
                   pl.BlockSpec((2, 1, rows, d), lambda b, i: (0, b, i, 0)),
                   pl.BlockSpec((1, rows // c, d), lambda b, i: (b, i, 0))],
        out_shape=[jax.ShapeDtypeStruct((5, batch, t, d), BF16),
                   jax.ShapeDtypeStruct((2, batch, t, d), F32),
                   jax.ShapeDtypeStruct((batch, t // c, d), F32)],
        scratch_shapes=[pltpu.VMEM((8, width), F32)],
        compiler_params=_params("arbitrary", "arbitrary"),
        name="rwkv_prep",
    )(p_r, *consts)
    const1 = lambda a: pl.BlockSpec(a.shape, lambda i: (0, 0))
    span = RWKV_SCAN_CHUNKS * c
    return pl.pallas_call(
        _rwkv_scan_kernel,
        grid=(t // span,),
        in_specs=[pl.BlockSpec((5, batch, span, d), lambda i: (0, 0, i, 0)),
                  pl.BlockSpec((2, batch, span, d), lambda i: (0, 0, i, 0)),
                  pl.BlockSpec((batch, RWKV_SCAN_CHUNKS, 1, d), lambda i: (0, i, 0, 0)),
                  const1(lnw), const1(lnb)],
        out_specs=pl.BlockSpec((batch, span, d), lambda i: (0, i, 0)),
        out_shape=jax.ShapeDtypeStruct((batch, t, d), BF16),
        scratch_shapes=[pltpu.VMEM((batch * RWKV_HEADS, HEAD_DIM, HEAD_DIM), F32)],
        compiler_params=_params("arbitrary"),
        name="rwkv_scan",
    )(ops, aux, wc.reshape(batch, t // c, 1, d), lnw, lnb)


def _compress_kernel(zk_ref, zv_ref, pek_ref, pev_ref, k1_ref, k2_ref, v1_ref, v2_ref, ko_ref, vo_ref):
    def one(z_ref, pe_ref, w1_ref, w2_ref):
        n16 = z_ref.shape[2] // CMP_STRIDE
        dk = HEAD_DIM
        first = second = None
        for l in range(CMP_STRIDE):
            z = z_ref[0, 0, pl.ds(l, n16, stride=CMP_STRIDE), :]
            lo = _dot((z + pe_ref[l:l + 1, :]).astype(BF16), w1_ref[l * dk:(l + 1) * dk, :])
            u = CMP_STRIDE + l
            hi = _dot((z + pe_ref[u:u + 1, :]).astype(BF16), w1_ref[u * dk:(u + 1) * dk, :])
            first = lo if first is None else first + lo
            second = hi if second is None else second + hi
        hidden = first + pltpu.roll(second, n16 - 1, axis=0)
        return _dot(jax.nn.gelu(hidden).astype(BF16), w2_ref[...])

    kc = one(zk_ref, pek_ref, k1_ref, k2_ref)
    blk = lax.broadcasted_iota(jnp.int32, kc.shape, 0)
    col = lax.broadcasted_iota(jnp.int32, kc.shape, 1)
    per = SEL_BLOCK // CMP_STRIDE
    hi = blk // per
    lo = CMP_STRIDE * (blk % per) + (CMP_BLOCK - 1)
    feat = _position_features(hi, lo, col)
    ko_ref[0] = jnp.concatenate([kc, feat], axis=1).astype(BF16)
    vc = one(zv_ref, pev_ref, v1_ref, v2_ref)
    vc_t = jnp.transpose(jnp.concatenate([vc, jnp.zeros_like(vc)], axis=1))
    vo_ref[0] = vc_t[:HEAD_DIM].astype(BF16)


def _compress(zk, zv, pek, pev, k1, k2, v1, v2):
    batch, hk, t, dk = zk.shape
    n16 = t // CMP_STRIDE
    zspec = pl.BlockSpec((1, 1, t, dk), lambda b, h: (b, h, 0, 0))
    full = lambda a: pl.BlockSpec(a.shape, lambda b, h: (0,) * a.ndim)
    return pl.pallas_call(
        _compress_kernel,
        grid=(batch, hk),
        in_specs=[zspec, zspec, full(pek), full(pev), full(k1), full(k2), full(v1), full(v2)],
        out_specs=[pl.BlockSpec((1, n16, 2 * dk), lambda b, h: (b * hk + h, 0, 0)),
                   pl.BlockSpec((1, dk, n16), lambda b, h: (b * hk + h, 0, 0))],
        out_shape=[jax.ShapeDtypeStruct((batch * hk, n16, 2 * dk), BF16),
                   jax.ShapeDtypeStruct((batch * hk, dk, n16), BF16)],
        compiler_params=_params("arbitrary", "arbitrary"),
        name="nsa_compress",
    )(zk, zv, pek, pev, k1, k2, v1, v2)


def _query_features(qt, slopes_ref, hk):
    lane_g = lax.broadcasted_iota(jnp.int32, (1, QG), 1) // Q_BLOCK
    slope = jnp.zeros((1, QG), F32)
    for g in range(NSA_GROUP):
        slope = jnp.where(lane_g == g, slopes_ref[hk * NSA_GROUP + g], slope)
    row = lax.broadcasted_iota(jnp.int32, (HEAD_DIM, QG), 0)
    extra = jnp.zeros((HEAD_DIM, QG), F32)
    for n, term in enumerate(LOG2E_TERMS):
        extra = jnp.where(row == 2 * n, (SEL_BLOCK * term) * slope, jnp.where(row == 2 * n + 1, term * slope, extra))
    return jnp.concatenate([qt * (HEAD_DIM ** -0.5 * LOG2E), extra], axis=0).astype(BF16)


def _position_features(hi, lo, col):
    return jnp.where(col < 2 * len(LOG2E_TERMS), jnp.where(col % 2 == 0, hi, lo), 0).astype(F32)


def _cmp_select_kernel(slopes_ref, qt_ref, kc_ref, vct_ref, ovt_ref, oct_ref, selt_ref, imp_ref):
    hk = pl.program_id(1)
    nblk = qt_ref.shape[2]
    width = nblk * Q_BLOCK
    ncp = kc_ref.shape[1]

    def attend(rows):
        cmp_end = lax.broadcasted_iota(jnp.int32, (rows, Q_BLOCK), 0) * CMP_STRIDE + (CMP_BLOCK - 1)
        for u in range(nblk):
            q0 = (pl.program_id(2) * nblk + u) * Q_BLOCK
            q_aug = _query_features(qt_ref[0, 0, u], slopes_ref, hk)
            s = _dot(kc_ref[0, 0:rows, :], q_aug)
            ok = cmp_end <= q0 + lax.broadcasted_iota(jnp.int32, (rows, Q_BLOCK), 1)
            any_ok = (q0 + lax.broadcasted_iota(jnp.int32, (1, Q_BLOCK), 1) >= CMP_BLOCK - 1).astype(F32)
            p_sum = jnp.zeros((rows, Q_BLOCK), F32)
            probs = []
            for g in range(NSA_GROUP):
                sg = jnp.where(ok, s[:, g * Q_BLOCK:(g + 1) * Q_BLOCK], NEG)
                e = jnp.exp2(sg - jnp.max(sg, axis=0, keepdims=True))
                p = e * (any_ok / jnp.sum(e, axis=0, keepdims=True))
                p_sum = p_sum + p
                probs.append(p.astype(BF16))
            oct_ref[0, 0, u] = _dot(vct_ref[0, :, 0:rows], jnp.concatenate(probs, axis=1))
            imp_ref[:, u * Q_BLOCK:(u + 1) * Q_BLOCK] = _dot_exact_lhs(ovt_ref[:, 0:rows], p_sum)

    chunk = min(CMP_ROW_CHUNK, ncp)
    n_chunks = ncp // chunk
    last_t = (pl.program_id(2) + 1) * width - 1
    needed = jnp.maximum((last_t - (CMP_BLOCK - 1)) // CMP_STRIDE + 1, 1)
    needed_chunks = jnp.minimum((needed + chunk - 1) // chunk, n_chunks)
    for nck in range(1, n_chunks + 1):
        pl.when(needed_chunks == nck)(lambda nck=nck: attend(nck * chunk))

    imp = imp_ref[...]
    blk = lax.broadcasted_iota(jnp.int32, (LANES, width), 0)
    cur = (pl.program_id(2) * width + lax.broadcasted_iota(jnp.int32, (LANES, width), 1)) // SEL_BLOCK
    valid = blk <= cur
    forced = (blk == 0) | (blk == cur) | (blk == cur - 1)
    x = jnp.where(forced, -jnp.inf, jnp.where(valid, imp, NEG))
    blk_f = blk.astype(F32)
    chosen = forced
    for _ in range(SEL_TOPK - 3):
        m = jnp.max(x, axis=0, keepdims=True)
        first = jnp.min(jnp.where(x == m, blk_f, float(LANES)), axis=0, keepdims=True)
        hit = blk_f == first
        chosen = chosen | hit
        x = jnp.where(hit, -jnp.inf, x)
    sel = (chosen & valid).astype(F32)
    for u in range(nblk):
        selt_ref[0, 0, u] = sel[:, u * Q_BLOCK:(u + 1) * Q_BLOCK]


def _cmp_select(slopes, q_t, kc_aug, vc_t, ov_t):
    batch, hk, nq, dk, qg = q_t.shape
    ncp = kc_aug.shape[1]
    nblk = CMP_Q_BLOCKS
    return pl.pallas_call(
        _cmp_select_kernel,
        grid=(batch, hk, nq // nblk),
        in_specs=[pl.BlockSpec(memory_space=pltpu.SMEM),
                  pl.BlockSpec((1, 1, nblk, dk, qg), lambda b, h, i: (b, h, i, 0, 0)),
                  pl.BlockSpec((1, ncp, 2 * dk), lambda b, h, i: (b * hk + h, 0, 0)),
                  pl.BlockSpec((1, dk, ncp), lambda b, h, i: (b * hk + h, 0, 0)),
                  pl.BlockSpec(ov_t.shape, lambda b, h, i: (0, 0))],
        out_specs=[pl.BlockSpec((1, 1, nblk, dk, qg), lambda b, h, i: (b, h, i, 0, 0)),
                   pl.BlockSpec((1, 1, nblk, LANES, Q_BLOCK), lambda b, h, i: (b, h, i, 0, 0))],
        out_shape=[jax.ShapeDtypeStruct(q_t.shape, F32),
                   jax.ShapeDtypeStruct((batch, hk, nq, LANES, Q_BLOCK), F32)],
        scratch_shapes=[pltpu.VMEM((LANES, nblk * Q_BLOCK), F32)],
        compiler_params=_params("arbitrary", "arbitrary", "arbitrary"),
        name="nsa_cmp_select",
    )(slopes, q_t, kc_aug, vc_t, ov_t)


def _sel_win_kernel(ids_ref, cnt_ref, slopes_ref, qt_ref, selt_ref, ks_ref, vst_ref, kw_ref, vwt_ref,
                    oct_ref, glt_ref, yt_ref):
    b, hk, i = pl.program_id(0), pl.program_id(1), pl.program_id(2)
    step = (b * pl.num_programs(1) + hk) * pl.num_programs(2) + i
    q0 = i * Q_BLOCK
    kt = KEY_TILE
    q_aug = _query_features(qt_ref[0, 0, 0], slopes_ref, hk)
    lane_minus_row = (lax.broadcasted_iota(jnp.int32, (kt, Q_BLOCK), 1)
                      - lax.broadcasted_iota(jnp.int32, (kt, Q_BLOCK), 0))

    def all_scores(tiles):
        s_all = _dot(jnp.concatenate([k_tile for k_tile, _, _ in tiles], axis=0), q_aug)
        return [s_all[n * kt:(n + 1) * kt] for n in range(len(tiles))]

    def attend(state, tiles, scores=None):
        m_old, acc_old = state
        scores = all_scores(tiles) if scores is None else scores
        probs, maxes = [[] for _ in tiles], []
        for g in range(NSA_GROUP):
            gsl = slice(g * Q_BLOCK, (g + 1) * Q_BLOCK)
            masked = [jnp.where(mask, s[:, gsl], NEG) for s, (_, _, mask) in zip(scores, tiles)]
            top = jnp.max(masked[0].reshape(kt // 8, 8, Q_BLOCK), axis=0)
            for sg in masked[1:]:
                top = jnp.maximum(top, jnp.max(sg.reshape(kt // 8, 8, Q_BLOCK), axis=0))
            mg = jnp.maximum(m_old[:, gsl], jnp.max(top, axis=0, keepdims=True))
            for n, sg in enumerate(masked):
                probs[n].append(jnp.exp2(sg - mg).astype(BF16))
            maxes.append(mg)
        m_new = jnp.concatenate(maxes, axis=1)
        p_all = jnp.concatenate([jnp.concatenate(p, axis=1) for p in probs], axis=0)
        vt_all = jnp.concatenate([vt_tile for _, vt_tile, _ in tiles], axis=1)
        return m_new, jnp.exp2(m_old - m_new) * acc_old + _dot(vt_all, p_all)

    start = (jnp.full((1, QG), SOFTMAX_FLOOR, F32), jnp.zeros((V_ROWS, QG), F32))
    result = lambda state: state[1][:HEAD_DIM] / state[1][HEAD_DIM:HEAD_DIM + 1]

    count = cnt_ref[step]
    last_tile = ks_ref.shape[2] // kt - 1

    def sel_tile(n):
        j = jnp.minimum(ids_ref[step * MAX_KEY_TILES + jnp.minimum(n, MAX_KEY_TILES - 1)], last_tile)
        k0 = pl.multiple_of(j * kt, kt)
        per = kt // SEL_BLOCK
        picked = jnp.concatenate(
            [jnp.broadcast_to(selt_ref[0, 0, 0, pl.ds(j * per + u, 1), :], (SEL_BLOCK, Q_BLOCK)) for u in range(per)],
            axis=0)
        causal_from = jnp.where(n < count, k0 - q0, 1 << 30)
        mask = (picked > 0.5) & (lane_minus_row >= causal_from)
        return ks_ref[0, 0, pl.ds(k0, kt), :], vst_ref[0, 0, j], mask

    last = (q0 + Q_BLOCK - 1) // kt

    def win_tile(u):
        j = last - u
        jc = jnp.maximum(j, 0)
        k0 = pl.multiple_of(jc * kt, kt)
        dist = lane_minus_row + jnp.where(j >= 0, q0 - k0, -(1 << 30))
        mask = (dist >= 0) & (dist < WINDOW)
        return kw_ref[0, 0, pl.ds(k0, kt), :], vwt_ref[0, 0, jc], mask

    win_tiles = [win_tile(u) for u in range(WIN_TILES)]
    sel_tiles = [sel_tile(n) for n in range(SEL_FIRST)]
    scores = all_scores(win_tiles + sel_tiles)
    state_w = attend(start, win_tiles, scores[:WIN_TILES])
    state_s = attend(start, sel_tiles, scores[WIN_TILES:])
    state_s = lax.fori_loop(
        0, (jnp.maximum(count - SEL_FIRST, 0) + SEL_GROUP - 1) // SEL_GROUP,
        lambda n, st: attend(st, [sel_tile(SEL_FIRST + SEL_GROUP * n + u) for u in range(SEL_GROUP)]), state_s)
    o_s, o_w = result(state_s), result(state_w)

    gates = jax.nn.sigmoid(glt_ref[0, 0])
    o_c = oct_ref[0, 0, 0]
    outs = []
    for g in range(NSA_GROUP):
        gsl = slice(g * Q_BLOCK, (g + 1) * Q_BLOCK)
        outs.append(gates[3 * g:3 * g + 1] * o_c[:, gsl] + gates[3 * g + 1:3 * g + 2] * o_s[:, gsl]
                    + gates[3 * g + 2:3 * g + 3] * o_w[:, gsl])
    yt_ref[0] = jnp.concatenate(outs, axis=0).astype(yt_ref.dtype)


def _sel_win(tile_ids, tile_cnt, slopes, q_t, sel_t, ks, vs_t, kw, vw_t, oc_t, gl_t):
    batch, hk, nq, dk, qg = q_t.shape
    t = nq * Q_BLOCK
    nkt = t // KEY_TILE
    grp = qg // Q_BLOCK
    qspec = pl.BlockSpec((1, 1, 1, dk, qg), lambda b, h, i, *_: (b, h, i, 0, 0))
    kspec = pl.BlockSpec((1, 1, t, 2 * dk), lambda b, h, i, *_: (b, h, 0, 0))
    vspec = pl.BlockSpec((1, 1, nkt, V_ROWS, KEY_TILE), lambda b, h, i, *_: (b, h, 0, 0, 0))
    grid_spec = pltpu.PrefetchScalarGridSpec(
        num_scalar_prefetch=2,
        grid=(batch, hk, nq),
        in_specs=[pl.BlockSpec(memory_space=pltpu.SMEM), qspec,
                  pl.BlockSpec((1, 1, 1, LANES, Q_BLOCK), lambda b, h, i, *_: (b, h, i, 0, 0)),
                  kspec, vspec, kspec, vspec, qspec,
                  pl.BlockSpec((1, 1, 3 * grp, Q_BLOCK), lambda b, h, i, *_: (b, h, 0, i))],
        out_specs=pl.BlockSpec((1, grp * dk, Q_BLOCK), lambda b, h, i, *_: (b, h, i)),
    )
    return pl.pallas_call(
        _sel_win_kernel,
        grid_spec=grid_spec,
        out_shape=jax.ShapeDtypeStruct((batch, hk * grp * dk, t), BF16),
        compiler_params=_params("arbitrary", "arbitrary", "arbitrary"),
        name="nsa_sel_win",
    )(tile_ids, tile_cnt, slopes, q_t, sel_t, ks, vs_t, kw, vw_t, oc_t, gl_t)


def _ffn_kernel(x_ref, yr_ref, ynt_ref, wo1_ref, wo2_ref, g2_ref, wg_ref, wu_ref, wd_ref, gf_ref, o_ref):
    h1 = (x_ref[...] + _dot(yr_ref[...].astype(BF16), wo1_ref[...])
          + _dot_tn(ynt_ref[0].astype(BF16), wo2_ref[...]))
    hn = _rms(h1, g2_ref[...]).astype(BF16)
    gate = _dot(hn, wg_ref[...])
    up = _dot(hn, wu_ref[...])
    act = gate * jax.nn.sigmoid(gate) * up
    o_ref[...] = _rms(h1 + _dot(act.astype(BF16), wd_ref[...]), gf_ref[...])


def _ffn(x2, yr, yn_t, wo1, wo2, g2, wg, wu, wd, gf, tm=512):
    m, d = x2.shape
    _, dn, t = yn_t.shape
    per_seq = t // tm
    row = lambda n: pl.BlockSpec((tm, n), lambda i: (i, 0))
    const = lambda a: pl.BlockSpec(a.shape, lambda i: (0, 0), pipeline_mode=pl.Buffered(1))
    return pl.pallas_call(
        _ffn_kernel,
        grid=(m // tm,),
        in_specs=[row(d), row(yr.shape[1]),
                  pl.BlockSpec((1, dn, tm), lambda i: (i // per_seq, 0, i % per_seq)),
                  const(wo1), const(wo2), const(g2), const(wg), const(wu), const(wd), const(gf)],
        out_specs=row(d),
        out_shape=jax.ShapeDtypeStruct((m, d), F32),
        compiler_params=_params("arbitrary"),
        name="outproj_ffn",
    )(x2, yr, yn_t, wo1, wo2, g2, wg, wu, wd, gf)


def _overlap_matrix_t(t):
    n16 = t // CMP_STRIDE
    n_cmp = (t - CMP_BLOCK) // CMP_STRIDE + 1
    n_sel = t // SEL_BLOCK
    cmp_start = np.arange(n_cmp) * CMP_STRIDE
    sel_start = np.arange(n_sel) * SEL_BLOCK
    ov = np.clip(np.minimum(cmp_start[:, None] + CMP_BLOCK, sel_start[None, :] + SEL_BLOCK)
                 - np.maximum(cmp_start[:, None], sel_start[None, :]), 0, None) / CMP_STRIDE
    full = np.zeros((LANES, n16), np.float32)
    full[:n_sel, :n_cmp] = ov.T
    return jnp.asarray(full, BF16)


def kernel(x, norm1_g, w_in, mu_shift, rwkv_w0, rwkv_w2, rwkv_a0, rwkv_a2, rwkv_g2, rwkv_k_k, rwkv_k_a,
           rwkv_r_k, rwkv_lnx_w, rwkv_lnx_b, nsa_pe_k, nsa_pe_v, nsa_cmp_k_w1, nsa_cmp_k_w2, nsa_cmp_v_w1,
           nsa_cmp_v_w2, w_out, norm2_g, ffn_w_gate, ffn_w_up, ffn_w_down, norm_f_g):
    batch, t, d_model = x.shape
    assert w_in.shape[0] == 1, "the final RMSNorm is fused into the (single) layer's FFN kernel"
    assert t % INPROJ_ROWS == 0 and t % RWKV_PREP_ROWS == 0 and t // SEL_BLOCK <= LANES
    hk, grp, dk = NSA_KV_HEADS, NSA_GROUP, HEAD_DIM
    nq = t // Q_BLOCK
    slopes = 2.0 ** (-8.0 * jnp.arange(1, NSA_Q_HEADS + 1, dtype=F32) / NSA_Q_HEADS)
    nsa_pad = _round_up(N_NSA_COLS, LANES)
    row = lambda a: a.reshape(1, -1)
    i = 0

    h = x.reshape(batch * t, d_model)
    w_r = w_in[i][:, :N_RWKV_COLS].astype(BF16)
    w_n = jnp.pad(w_in[i][:, N_RWKV_COLS:], ((0, 0), (0, nsa_pad - N_NSA_COLS))).astype(BF16)
    p_r, q_t, ks, kw, vs_t, vw_t, zk, zv, gl_t = _inproj(h, batch, row(norm1_g[i]), w_r, w_n)

    y_rwkv = _rwkv(p_r.reshape(batch, t, N_RWKV_COLS), row(mu_shift[i]), row(rwkv_w0[i]), rwkv_w2[i].astype(BF16),
                   row(rwkv_a0[i]), rwkv_a2[i].astype(BF16), rwkv_g2[i].astype(BF16), row(rwkv_k_k[i]),
                   row(rwkv_k_a[i]), row(rwkv_r_k[i]), row(rwkv_lnx_w[i]), row(rwkv_lnx_b[i]))
    y_rwkv = y_rwkv.reshape(batch * t, D_RWKV)

    kc_aug, vc_t = _compress(zk, zv, nsa_pe_k[i], nsa_pe_v[i],
                             nsa_cmp_k_w1[i].astype(BF16), nsa_cmp_k_w2[i].astype(BF16),
                             nsa_cmp_v_w1[i].astype(BF16), nsa_cmp_v_w2[i].astype(BF16))
    oc_t, sel_t = _cmp_select(slopes, q_t, kc_aug, vc_t, _overlap_matrix_t(t))

    blocks_per_tile = KEY_TILE // SEL_BLOCK
    active = (sel_t.max(axis=-1) > 0).reshape(batch, hk, nq, MAX_KEY_TILES, blocks_per_tile).any(axis=-1)
    tile_ids = jnp.argsort(jnp.logical_not(active), axis=-1, stable=True).astype(jnp.int32).reshape(-1)
    tile_cnt = active.sum(axis=-1).astype(jnp.int32).reshape(-1)
    y_nsa_t = _sel_win(tile_ids, tile_cnt, slopes, q_t, sel_t, ks, vs_t, kw, vw_t, oc_t, gl_t)

    out = _ffn(h, y_rwkv, y_nsa_t, w_out[i][:D_RWKV].astype(BF16), w_out[i][D_RWKV:].astype(BF16),
               row(norm2_g[i]), ffn_w_gate[i].astype(BF16), ffn_w_up[i].astype(BF16),
               ffn_w_down[i].astype(BF16), row(norm_f_g))
    return out.reshape(batch, t, d_model)
```

```python
import ml_dtypes
import numpy as np
import jax
import jax.numpy as jnp
from jax import lax
from jax.experimental import pallas as pl
from jax.experimental.pallas import tpu as pltpu

F32 = jnp.float32
BF16 = jnp.bfloat16

HEAD_DIM = 64
RWKV_HEADS = 8
D_RWKV = RWKV_HEADS * HEAD_DIM
NSA_Q_HEADS = 8
NSA_KV_HEADS = 2
NSA_GROUP = NSA_Q_HEADS // NSA_KV_HEADS
D_NSA = NSA_Q_HEADS * HEAD_DIM
D_KV = NSA_KV_HEADS * HEAD_DIM
LORA_W, LORA_A, LORA_G = 64, 64, 128
N_RWKV_COLS = 3 * D_RWKV + LORA_W + LORA_A + LORA_G
N_NSA_COLS = D_NSA + 6 * D_KV + 3 * NSA_Q_HEADS
CMP_BLOCK, CMP_STRIDE = 32, 16
SEL_BLOCK, SEL_TOPK = 64, 16
WINDOW = 512
Q_BLOCK = 128
NORM_EPS = 1e-6
GN_EPS = 64e-5
NEG = -1e30
BIG = 1e30

LANES = 128
RWKV_CHUNK = 64
RWKV_PREP_ROWS = 8 * RWKV_CHUNK
RWKV_SCAN_CHUNKS = 4
PREP_SPLIT = 2
KEY_TILE = 128
INPROJ_ROWS = 512
MAX_KEY_TILES = LANES * SEL_BLOCK // KEY_TILE
WIN_TILES = (WINDOW + Q_BLOCK) // KEY_TILE
SEL_FIRST = 11
SEL_GROUP = 2
CMP_Q_BLOCKS = 4
CMP_ROW_CHUNK = 128
V_ROWS = HEAD_DIM + 16
SOFTMAX_FLOOR = -1e20
QG = NSA_GROUP * Q_BLOCK
VMEM_LIMIT = 56 * 1024 * 1024


def _bf16_terms(x, count):
    terms = []
    for _ in range(count):
        terms.append(float(np.asarray(x, ml_dtypes.bfloat16)))
        x = x - terms[-1]
    return tuple(terms)


LOG2E = float(np.log2(np.e))
LOG2E_TERMS = _bf16_terms(LOG2E, 3)


def _round_up(n, m):
    return -(-n // m) * m


def _dot(a, b):
    return jnp.dot(a, b, preferred_element_type=F32)


def _dot_tn(a, b):
    return lax.dot_general(a, b, (((0,), (0,)), ((), ())), preferred_element_type=F32)


def _split(a, terms):
    pieces = []
    for _ in range(terms - 1):
        pieces.append(a.astype(BF16))
        a = a - pieces[-1].astype(F32)
    return pieces + [a.astype(BF16)]


def _dot_exact_lhs(a, b, terms=3):
    return sum(_dot(a, piece) for piece in _split(b, terms))


def _dot_exact_rhs(a, b, terms=3):
    return sum(_dot(piece, b) for piece in _split(a, terms))


def _bdot(spec, a, b):
    return jnp.einsum(spec, a, b, preferred_element_type=F32)


def _rms(x, g):
    return x * lax.rsqrt(jnp.mean(x * x, axis=-1, keepdims=True) + NORM_EPS) * g


def _params(*sem):
    return pltpu.CompilerParams(dimension_semantics=sem, vmem_limit_bytes=VMEM_LIMIT)


def _inproj_kernel(x_ref, g_ref, wr_ref, wn_ref, qx_ref, pr_ref, qt_ref, ks_ref, kw_ref, vst_ref, vwt_ref,
                   kc_ref, vc_ref, glt_ref):
    xb = _rms(x_ref[...], g_ref[...]).astype(BF16)
    pr_ref[...] = _dot(xb, wr_ref[...])
    pn = _dot(xb, wn_ref[...])
    rows = pn.shape[0]
    dk, hk_n = HEAD_DIM, NSA_KV_HEADS
    group = lambda j: pn[:, D_NSA + j * D_KV:D_NSA + (j + 1) * D_KV]

    for half in range(rows // Q_BLOCK):
        q_tr = jnp.transpose(pn[half * Q_BLOCK:(half + 1) * Q_BLOCK, :D_NSA])
        for hk in range(hk_n):
            base = hk * NSA_GROUP * dk
            heads = jnp.concatenate([q_tr[base + g * dk:base + (g + 1) * dk, :] for g in range(NSA_GROUP)], axis=1)
            qt_ref[0, hk, half] = jnp.concatenate([heads * (dk ** -0.5 * LOG2E), qx_ref[hk]], axis=0).astype(BF16)

    pos = pl.program_id(1) * rows + lax.broadcasted_iota(jnp.int32, (rows, dk), 0)
    col = lax.broadcasted_iota(jnp.int32, (rows, dk), 1)
    pos_cols = _position_features(pos // SEL_BLOCK, pos % SEL_BLOCK, col)
    kt = KEY_TILE
    ones_rows = (lax.broadcasted_iota(jnp.int32, (V_ROWS - dk, kt), 0) == 0).astype(F32)
    kc, vc, ks, vs, kw, vw = (group(j) for j in range(6))
    vs_tr, vw_tr = jnp.transpose(vs), jnp.transpose(vw)
    gl_tr = jnp.transpose(pn[:, D_NSA + 6 * D_KV:])
    n_gate = 3 * NSA_GROUP
    for hk in range(hk_n):
        sl = slice(hk * dk, (hk + 1) * dk)
        kc_ref[0, hk] = kc[:, sl]
        vc_ref[0, hk] = vc[:, sl]
        ks_ref[0, hk] = jnp.concatenate([ks[:, sl], pos_cols], axis=1).astype(BF16)
        kw_ref[0, hk] = jnp.concatenate([kw[:, sl], pos_cols], axis=1).astype(BF16)
        for u in range(rows // kt):
            vst_ref[0, hk, u] = jnp.concatenate([vs_tr[sl, u * kt:(u + 1) * kt], ones_rows], axis=0).astype(BF16)
            vwt_ref[0, hk, u] = jnp.concatenate([vw_tr[sl, u * kt:(u + 1) * kt], ones_rows], axis=0).astype(BF16)
        glt_ref[0, hk] = gl_tr[hk * n_gate:(hk + 1) * n_gate, :]


def _inproj(x2, batch, g, w_r, w_n, q_extra):
    m, d = x2.shape
    t = m // batch
    tm = INPROJ_ROWS
    nt = t // tm
    nr, nn = w_r.shape[1], w_n.shape[1]
    hk, dk = NSA_KV_HEADS, HEAD_DIM
    const = lambda a: pl.BlockSpec(a.shape, lambda b, i: (0, 0))
    keys = pl.BlockSpec((1, hk, tm, 2 * dk), lambda b, i: (b, 0, i, 0))
    vals = pl.BlockSpec((1, hk, tm // KEY_TILE, V_ROWS, KEY_TILE), lambda b, i: (b, 0, i, 0, 0))
    cmp_in = pl.BlockSpec((1, hk, tm, dk), lambda b, i: (b, 0, i, 0))
    keys_shape = jax.ShapeDtypeStruct((batch, hk, t, 2 * dk), BF16)
    vals_shape = jax.ShapeDtypeStruct((batch, hk, t // KEY_TILE, V_ROWS, KEY_TILE), BF16)
    cmp_shape = jax.ShapeDtypeStruct((batch, hk, t, dk), F32)
    return pl.pallas_call(
        _inproj_kernel,
        grid=(batch, nt),
        in_specs=[pl.BlockSpec((tm, d), lambda b, i: (b * nt + i, 0)), const(g), const(w_r), const(w_n),
                  pl.BlockSpec(q_extra.shape, lambda b, i: (0, 0, 0))],
        out_specs=[pl.BlockSpec((tm, nr), lambda b, i: (b * nt + i, 0)),
                   pl.BlockSpec((1, hk, tm // Q_BLOCK, 2 * dk, QG), lambda b, i: (b, 0, i, 0, 0)),
                   keys, keys, vals, vals, cmp_in, cmp_in,
                   pl.BlockSpec((1, hk, 3 * NSA_GROUP, tm), lambda b, i: (b, 0, 0, i))],
        out_shape=[jax.ShapeDtypeStruct((m, nr), F32),
                   jax.ShapeDtypeStruct((batch, hk, t // Q_BLOCK, 2 * dk, QG), BF16),
                   keys_shape, keys_shape, vals_shape, vals_shape, cmp_shape, cmp_shape,
                   jax.ShapeDtypeStruct((batch, hk, 3 * NSA_GROUP, t), F32)],
        compiler_params=_params("arbitrary", "arbitrary"),
        name="inproj",
    )(x2, g, w_r, w_n, q_extra)


def _rwkv_operands(p, p_prev, mu, w0, w2, a0, a2, g2, k_k, k_a, r_k, bd, tri):
    c = RWKV_CHUNK
    ps = p + mu * (p_prev - p)
    d = D_RWKV
    r, k, v = ps[:, 0:d], ps[:, d:2 * d], ps[:, 2 * d:3 * d]
    dw = ps[:, 3 * d:3 * d + LORA_W]
    da = ps[:, 3 * d + LORA_W:3 * d + LORA_W + LORA_A]
    dg = ps[:, 3 * d + LORA_W + LORA_A:]

    z = -(w0 + _dot(jnp.tanh(dw).astype(BF16), w2))
    softplus = jnp.maximum(z, 0.0) + jnp.log1p(jnp.exp(-jnp.abs(z)))
    logw = -jnp.exp(-softplus - 0.5)
    a = jax.nn.sigmoid(a0 + _dot(da.astype(BF16), a2))
    g = _dot(jax.nn.sigmoid(dg).astype(BF16), g2)

    kk = k * k_k
    kk = kk / jnp.maximum(jnp.sqrt(_dot_exact_rhs(kk * kk, bd, PREP_SPLIT)), 1e-12)
    kp = k * (1.0 + (a - 1.0) * k_a)
    bonus = _dot_exact_rhs(r * kp * r_k, bd, PREP_SPLIT) * v

    n_chunks = p.shape[0] // c
    cum = jnp.concatenate([_dot_exact_lhs(tri, logw[ci * c:(ci + 1) * c], PREP_SPLIT) for ci in range(n_chunks)],
                          axis=0)
    e_pos, e_neg = jnp.exp(cum), jnp.exp(-cum)
    wc = jnp.concatenate([e_pos[(ci + 1) * c - 1:(ci + 1) * c, :] for ci in range(n_chunks)], axis=0)
    ops = (r * e_pos, kp * e_neg, kk * a * e_neg, -kk * jnp.exp(cum - logw), v)
    return tuple(o.astype(BF16) for o in ops), wc, bonus, g


def _rwkv_prep_kernel(p_ref, mu_ref, w0_ref, w2_ref, a0_ref, a2_ref, g2_ref, kk_ref, ka_ref, rk_ref, bd_ref, tri_ref,
                      ops_ref, aux_ref, wc_ref, carry_ref):
    rows = p_ref.shape[1]

    @pl.when(pl.program_id(1) == 0)
    def _():
        carry_ref[...] = jnp.zeros_like(carry_ref)

    p = p_ref[0]
    row = lax.broadcasted_iota(jnp.int32, p.shape, 0)
    p_prev = jnp.where(row == 0, carry_ref[7:8, :], pltpu.roll(p, 1, axis=0))
    carry_ref[...] = p[rows - 8:, :]
    ops, wc, bonus, g = _rwkv_operands(
        p, p_prev, mu_ref[...], w0_ref[...], w2_ref[...], a0_ref[...], a2_ref[...], g2_ref[...],
        kk_ref[...], ka_ref[...], rk_ref[...], bd_ref[...], tri_ref[...])
    for idx, o in enumerate(ops):
        ops_ref[idx, 0] = o
    aux_ref[0, 0] = bonus
    aux_ref[1, 0] = g
    for ci in range(wc.shape[0]):
        wc_ref[0, ci] = wc[ci:ci + 1, :]


def _rwkv_scan_kernel(ops_ref, aux_ref, wc_ref, lnw_ref, lnb_ref, y_ref, s_ref):
    @pl.when(pl.program_id(0) == 0)
    def _():
        s_ref[...] = jnp.zeros_like(s_ref)

    _rwkv_step(ops_ref, aux_ref, wc_ref, lnw_ref, lnb_ref, y_ref, s_ref)


def _rwkv_step(ops_ref, aux_ref, wc_ref, lnw_ref, lnb_ref, y_ref, s_ref):
    _, nb, span, d = ops_ref.shape
    c = RWKV_CHUNK
    nh, dk = RWKV_HEADS, HEAD_DIM

    row = lax.broadcasted_iota(jnp.int32, (1, c, c), 1)
    col = lax.broadcasted_iota(jnp.int32, (1, c, c), 2)
    strict = col < row
    incl = col <= row
    eye = (row == col).astype(F32)

    def heads(x):
        return jnp.stack([x[b, :, h * dk:(h + 1) * dk] for b in range(nb) for h in range(nh)], axis=0)

    def independent(j):
        rt, kt, bt, at, vb = (heads(ops_ref[idx, :, j * c:(j + 1) * c, :]) for idx in range(5))
        ar = jnp.concatenate([at, rt], axis=1)
        bk = jnp.concatenate([bt, kt], axis=1)
        amat = _bdot("nik,njk->nij", ar, bk)
        n_ab = jnp.where(strict, amat[:, :c, :c], 0.0)
        a_ak = jnp.where(strict, amat[:, :c, c:], 0.0).astype(BF16)
        a_rb = jnp.where(incl, amat[:, c:, :c], 0.0).astype(BF16)
        a_rk = jnp.where(incl, amat[:, c:, c:], 0.0).astype(BF16)
        inv = eye + n_ab
        pw = n_ab
        for _ in range(int(np.log2(c)) - 1):
            pwb = pw.astype(BF16)
            pw = _bdot("nij,njk->nik", pwb, pwb)
            inv = inv + _bdot("nij,njk->nik", pw.astype(BF16), inv.astype(BF16))
        return ar, bk, vb, _bdot("nij,njv->niv", a_ak, vb), a_rb, a_rk, inv.astype(BF16)

    parts = [independent(j) for j in range(span // c)]
    s = s_ref[...]
    for j, (ar, bk, vb, akv, a_rb, a_rk, inv) in enumerate(parts):
        ar_s = _bdot("nik,nvk->niv", ar, s.astype(BF16))
        ub = _bdot("nij,njv->niv", inv, (ar_s[:, :c] + akv).astype(BF16)).astype(BF16)
        y = ar_s[:, c:] + _bdot("nij,njv->niv", a_rb, ub) + _bdot("nij,njv->niv", a_rk, vb)
        uv_t = jnp.swapaxes(jnp.concatenate([ub, vb], axis=1), 1, 2)
        s = (s + _bdot("nvi,nik->nvk", uv_t, bk)) * heads(wc_ref[:, j])

        mean = jnp.mean(y, axis=-1, keepdims=True)
        var = jnp.mean(jnp.square(y - mean), axis=-1, keepdims=True)
        yn = (y - mean) * lax.rsqrt(var + GN_EPS)
        rsl = slice(j * c, (j + 1) * c)
        for b in range(nb):
            wide = jnp.concatenate([yn[b * nh + h] for h in range(nh)], axis=-1)
            y_ref[b, rsl, :] = ((wide * lnw_ref[...] + lnb_ref[...] + aux_ref[0, b, rsl, :])
                                * aux_ref[1, b, rsl, :]).astype(y_ref.dtype)
    s_ref[...] = s


def _rwkv(p_r, mu, w0, w2, a0, a2, g2, k_k, k_a, r_k, lnw, lnb):
    batch, t, width = p_r.shape
    c = RWKV_CHUNK
    rows = RWKV_PREP_ROWS
    d = D_RWKV
    head = np.arange(d) // HEAD_DIM
    bd = jnp.asarray(head[:, None] == head[None, :], BF16)
    tri = jnp.asarray(np.tril(np.ones((c, c))), BF16)
    const2 = lambda a: pl.BlockSpec(a.shape, lambda b, i: (0, 0))
    consts = (mu, w0, w2, a0, a2, g2, k_k, k_a, r_k, bd, tri)
    ops, aux, wc = pl.pallas_call(
        _rwkv_prep_kernel,
        grid=(batch, t // rows),
        in_specs=[pl.BlockSpec((1, rows, width), lambda b, i: (b, i, 0))] + [const2(a) for a in consts],
        out_specs=[pl.BlockSpec((5, 1, rows, d), lambda b, i: (0, b, i, 0)),
                   pl.BlockSpec((2, 1, rows, d), lambda b, i: (0, b, i, 0)),
                   pl.BlockSpec((1, rows // c, 1, d), lambda b, i: (b, i, 0, 0))],
        out_shape=[jax.ShapeDtypeStruct((5, batch, t, d), BF16),
                   jax.ShapeDtypeStruct((2, batch, t, d), F32),
                   jax.ShapeDtypeStruct((batch, t // c, 1, d), F32)],
        scratch_shapes=[pltpu.VMEM((8, width), F32)],
        compiler_params=_params("arbitrary", "arbitrary"),
        name="rwkv_prep",
    )(p_r, *consts)
    const1 = lambda a: pl.BlockSpec(a.shape, lambda i: (0, 0))
    span = RWKV_SCAN_CHUNKS * c
    return pl.pallas_call(
        _rwkv_scan_kernel,
        grid=(t // span,),
        in_specs=[pl.BlockSpec((5, batch, span, d), lambda i: (0, 0, i, 0)),
                  pl.BlockSpec((2, batch, span, d), lambda i: (0, 0, i, 0)),
                  pl.BlockSpec((batch, RWKV_SCAN_CHUNKS, 1, d), lambda i: (0, i, 0, 0)),
                  const1(lnw), const1(lnb)],
        out_specs=pl.BlockSpec((batch, span, d), lambda i: (0, i, 0)),
        out_shape=jax.ShapeDtypeStruct((batch, t, d), BF16),
        scratch_shapes=[pltpu.VMEM((batch * RWKV_HEADS, HEAD_DIM, HEAD_DIM), F32)],
        compiler_params=_params("arbitrary"),
        name="rwkv_scan",
    )(ops, aux, wc, lnw, lnb)


def _compress_kernel(zk_ref, zv_ref, pek_ref, pev_ref, k1_ref, k2_ref, v1_ref, v2_ref, ko_ref, vo_ref):
    def one(z_ref, pe_ref, w1_ref, w2_ref):
        n16 = z_ref.shape[2] // CMP_STRIDE
        dk = HEAD_DIM
        first = second = None
        for l in range(CMP_STRIDE):
            z = z_ref[0, 0, pl.ds(l, n16, stride=CMP_STRIDE), :]
            lo = _dot((z + pe_ref[l:l + 1, :]).astype(BF16), w1_ref[l * dk:(l + 1) * dk, :])
            u = CMP_STRIDE + l
            hi = _dot((z + pe_ref[u:u + 1, :]).astype(BF16), w1_ref[u * dk:(u + 1) * dk, :])
            first = lo if first is None else first + lo
            second = hi if second is None else second + hi
        hidden = first + pltpu.roll(second, n16 - 1, axis=0)
        return _dot(jax.nn.gelu(hidden).astype(BF16), w2_ref[...])

    kc = one(zk_ref, pek_ref, k1_ref, k2_ref)
    blk = lax.broadcasted_iota(jnp.int32, kc.shape, 0)
    col = lax.broadcasted_iota(jnp.int32, kc.shape, 1)
    per = SEL_BLOCK // CMP_STRIDE
    hi = blk // per
    lo = CMP_STRIDE * (blk % per) + (CMP_BLOCK - 1)
    feat = _position_features(hi, lo, col)
    ko_ref[0] = jnp.concatenate([kc, feat], axis=1).astype(BF16)
    vc = one(zv_ref, pev_ref, v1_ref, v2_ref)
    vc_t = jnp.transpose(jnp.concatenate([vc, jnp.zeros_like(vc)], axis=1))
    vo_ref[0] = vc_t[:HEAD_DIM].astype(BF16)


def _compress(zk, zv, pek, pev, k1, k2, v1, v2):
    batch, hk, t, dk = zk.shape
    n16 = t // CMP_STRIDE
    zspec = pl.BlockSpec((1, 1, t, dk), lambda b, h: (b, h, 0, 0))
    full = lambda a: pl.BlockSpec(a.shape, lambda b, h: (0,) * a.ndim)
    return pl.pallas_call(
        _compress_kernel,
        grid=(batch, hk),
        in_specs=[zspec, zspec, full(pek), full(pev), full(k1), full(k2), full(v1), full(v2)],
        out_specs=[pl.BlockSpec((1, n16, 2 * dk), lambda b, h: (b * hk + h, 0, 0)),
                   pl.BlockSpec((1, dk, n16), lambda b, h: (b * hk + h, 0, 0))],
        out_shape=[jax.ShapeDtypeStruct((batch * hk, n16, 2 * dk), BF16),
                   jax.ShapeDtypeStruct((batch * hk, dk, n16), BF16)],
        compiler_params=_params("arbitrary", "arbitrary"),
        name="nsa_compress",
    )(zk, zv, pek, pev, k1, k2, v1, v2)


def _alibi_query_rows(slopes):
    per_lane = jnp.repeat(slopes.reshape(NSA_KV_HEADS, NSA_GROUP), Q_BLOCK, axis=1)
    rows = []
    for term in LOG2E_TERMS:
        rows += [(SEL_BLOCK * term) * per_lane, term * per_lane]
    rows = jnp.stack(rows, axis=1)
    return jnp.pad(rows, ((0, 0), (0, HEAD_DIM - rows.shape[1]), (0, 0)))


def _position_features(hi, lo, col):
    return jnp.where(col < 2 * len(LOG2E_TERMS), jnp.where(col % 2 == 0, hi, lo), 0).astype(F32)


def _cmp_select_kernel(qt_ref, kc_ref, vct_ref, ovt_ref, oct_ref, selt_ref, act_ref, imp_ref):
    nblk = qt_ref.shape[2]
    width = nblk * Q_BLOCK
    ncp = kc_ref.shape[1]

    def attend(rows):
        cmp_end = lax.broadcasted_iota(jnp.int32, (rows, Q_BLOCK), 0) * CMP_STRIDE + (CMP_BLOCK - 1)
        for u in range(nblk):
            q0 = (pl.program_id(2) * nblk + u) * Q_BLOCK
            s = _dot(kc_ref[0, 0:rows, :], qt_ref[0, 0, u])
            ok = cmp_end <= q0 + lax.broadcasted_iota(jnp.int32, (rows, Q_BLOCK), 1)
            any_ok = (q0 + lax.broadcasted_iota(jnp.int32, (1, Q_BLOCK), 1) >= CMP_BLOCK - 1).astype(F32)
            p_sum = jnp.zeros((rows, Q_BLOCK), F32)
            probs = []
            for g in range(NSA_GROUP):
                sg = jnp.where(ok, s[:, g * Q_BLOCK:(g + 1) * Q_BLOCK], NEG)
                e = jnp.exp2(sg - jnp.max(sg, axis=0, keepdims=True))
                p = e * (any_ok / jnp.sum(e, axis=0, keepdims=True))
                p_sum = p_sum + p
                probs.append(p.astype(BF16))
            oct_ref[0, 0, u] = _dot(vct_ref[0, :, 0:rows], jnp.concatenate(probs, axis=1))
            imp_ref[:, u * Q_BLOCK:(u + 1) * Q_BLOCK] = _dot_exact_lhs(ovt_ref[:, 0:rows], p_sum)

    chunk = min(CMP_ROW_CHUNK, ncp)
    n_chunks = ncp // chunk
    last_t = (pl.program_id(2) + 1) * width - 1
    needed = jnp.maximum((last_t - (CMP_BLOCK - 1)) // CMP_STRIDE + 1, 1)
    needed_chunks = jnp.minimum((needed + chunk - 1) // chunk, n_chunks)
    for nck in range(1, n_chunks + 1):
        pl.when(needed_chunks == nck)(lambda nck=nck: attend(nck * chunk))

    imp = imp_ref[...]
    blk = lax.broadcasted_iota(jnp.int32, (LANES, width), 0)
    cur = (pl.program_id(2) * width + lax.broadcasted_iota(jnp.int32, (LANES, width), 1)) // SEL_BLOCK
    valid = blk <= cur
    forced = (blk == 0) | (blk == cur) | (blk == cur - 1)
    x = jnp.where(forced, -jnp.inf, jnp.where(valid, imp, NEG))
    blk_f = blk.astype(F32)
    chosen = forced
    for _ in range(SEL_TOPK - 3):
        m = jnp.max(x, axis=0, keepdims=True)
        first = jnp.min(jnp.where(x == m, blk_f, float(LANES)), axis=0, keepdims=True)
        hit = blk_f == first
        chosen = chosen | hit
        x = jnp.where(hit, -jnp.inf, x)
    sel = (chosen & valid).astype(F32)
    ones = jnp.ones((8, Q_BLOCK), BF16)
    for u in range(nblk):
        sel_u = sel[:, u * Q_BLOCK:(u + 1) * Q_BLOCK]
        selt_ref[0, 0, u] = sel_u
        act_ref[0, 0, u] = lax.dot_general(ones, sel_u.astype(BF16), (((1,), (1,)), ((), ())),
                                           preferred_element_type=F32)


def _cmp_select(q_t, kc_aug, vc_t, ov_t):
    batch, hk, nq, dk2, qg = q_t.shape
    dk = dk2 // 2
    ncp = kc_aug.shape[1]
    nblk = CMP_Q_BLOCKS
    return pl.pallas_call(
        _cmp_select_kernel,
        grid=(batch, hk, nq // nblk),
        in_specs=[pl.BlockSpec((1, 1, nblk, dk2, qg), lambda b, h, i: (b, h, i, 0, 0)),
                  pl.BlockSpec((1, ncp, 2 * dk), lambda b, h, i: (b * hk + h, 0, 0)),
                  pl.BlockSpec((1, dk, ncp), lambda b, h, i: (b * hk + h, 0, 0)),
                  pl.BlockSpec(ov_t.shape, lambda b, h, i: (0, 0))],
        out_specs=[pl.BlockSpec((1, 1, nblk, dk, qg), lambda b, h, i: (b, h, i, 0, 0)),
                   pl.BlockSpec((1, 1, nblk, LANES, Q_BLOCK), lambda b, h, i: (b, h, i, 0, 0)),
                   pl.BlockSpec((1, 1, nblk, 8, LANES), lambda b, h, i: (b, h, i, 0, 0))],
        out_shape=[jax.ShapeDtypeStruct((batch, hk, nq, dk, qg), F32),
                   jax.ShapeDtypeStruct((batch, hk, nq, LANES, Q_BLOCK), F32),
                   jax.ShapeDtypeStruct((batch, hk, nq, 8, LANES), F32)],
        scratch_shapes=[pltpu.VMEM((LANES, nblk * Q_BLOCK), F32)],
        compiler_params=_params("arbitrary", "arbitrary", "arbitrary"),
        name="nsa_cmp_select",
    )(q_t, kc_aug, vc_t, ov_t)


def _sel_win_kernel(ids_ref, cnt_ref, qt_ref, selt_ref, ks_ref, vst_ref, kw_ref, vwt_ref,
                    oct_ref, glt_ref, yt_ref):
    b, hk, i = pl.program_id(0), pl.program_id(1), pl.program_id(2)
    step = (b * pl.num_programs(1) + hk) * pl.num_programs(2) + i
    q0 = i * Q_BLOCK
    kt = KEY_TILE
    q_aug = qt_ref[0, 0, 0]
    lane_minus_row = (lax.broadcasted_iota(jnp.int32, (kt, Q_BLOCK), 1)
                      - lax.broadcasted_iota(jnp.int32, (kt, Q_BLOCK), 0))

    def all_scores(tiles):
        s_all = _dot(jnp.concatenate([k_tile for k_tile, _, _ in tiles], axis=0), q_aug)
        return [s_all[n * kt:(n + 1) * kt] for n in range(len(tiles))]

    def attend(state, tiles, scores=None):
        m_old, acc_old = state
        scores = all_scores(tiles) if scores is None else scores
        probs, maxes = [[] for _ in tiles], []
        for g in range(NSA_GROUP):
            gsl = slice(g * Q_BLOCK, (g + 1) * Q_BLOCK)
            masked = [jnp.where(mask, s[:, gsl], NEG) for s, (_, _, mask) in zip(scores, tiles)]
            top = jnp.max(masked[0].reshape(kt // 8, 8, Q_BLOCK), axis=0)
            for sg in masked[1:]:
                top = jnp.maximum(top, jnp.max(sg.reshape(kt // 8, 8, Q_BLOCK), axis=0))
            mg = jnp.maximum(m_old[:, gsl], jnp.max(top, axis=0, keepdims=True))
            for n, sg in enumerate(masked):
                probs[n].append(jnp.exp2(sg - mg).astype(BF16))
            maxes.append(mg)
        m_new = jnp.concatenate(maxes, axis=1)
        p_all = jnp.concatenate([jnp.concatenate(p, axis=1) for p in probs], axis=0)
        vt_all = jnp.concatenate([vt_tile for _, vt_tile, _ in tiles], axis=1)
        return m_new, jnp.exp2(m_old - m_new) * acc_old + _dot(vt_all, p_all)

    start = (jnp.full((1, QG), SOFTMAX_FLOOR, F32), jnp.zeros((V_ROWS, QG), F32))
    result = lambda state: state[1][:HEAD_DIM] / state[1][HEAD_DIM:HEAD_DIM + 1]

    count = cnt_ref[step]
    last_tile = ks_ref.shape[2] // kt - 1

    def sel_tile(n):
        j = jnp.minimum(ids_ref[step * MAX_KEY_TILES + jnp.minimum(n, MAX_KEY_TILES - 1)], last_tile)
        k0 = pl.multiple_of(j * kt, kt)
        per = kt // SEL_BLOCK
        picked = jnp.concatenate(
            [jnp.broadcast_to(selt_ref[0, 0, 0, pl.ds(j * per + u, 1), :], (SEL_BLOCK, Q_BLOCK)) for u in range(per)],
            axis=0)
        causal_from = jnp.where(n < count, k0 - q0, 1 << 30)
        mask = (picked > 0.5) & (lane_minus_row >= causal_from)
        return ks_ref[0, 0, pl.ds(k0, kt), :], vst_ref[0, 0, j], mask

    last = (q0 + Q_BLOCK - 1) // kt

    def win_tile(u):
        j = last - u
        jc = jnp.maximum(j, 0)
        k0 = pl.multiple_of(jc * kt, kt)
        dist = lane_minus_row + jnp.where(j >= 0, q0 - k0, -(1 << 30))
        mask = (dist >= 0) & (dist < WINDOW)
        return kw_ref[0, 0, pl.ds(k0, kt), :], vwt_ref[0, 0, jc], mask

    win_tiles = [win_tile(u) for u in range(WIN_TILES)]
    sel_tiles = [sel_tile(n) for n in range(SEL_FIRST)]
    scores = all_scores(win_tiles + sel_tiles)
    state_w = attend(start, win_tiles, scores[:WIN_TILES])
    state_s = attend(start, sel_tiles, scores[WIN_TILES:])
    state_s = lax.fori_loop(
        0, (jnp.maximum(count - SEL_FIRST, 0) + SEL_GROUP - 1) // SEL_GROUP,
        lambda n, st: attend(st, [sel_tile(SEL_FIRST + SEL_GROUP * n + u) for u in range(SEL_GROUP)]), state_s)
    o_s, o_w = result(state_s), result(state_w)

    gates = jax.nn.sigmoid(glt_ref[0, 0])
    o_c = oct_ref[0, 0, 0]
    outs = []
    for g in range(NSA_GROUP):
        gsl = slice(g * Q_BLOCK, (g + 1) * Q_BLOCK)
        outs.append(gates[3 * g:3 * g + 1] * o_c[:, gsl] + gates[3 * g + 1:3 * g + 2] * o_s[:, gsl]
                    + gates[3 * g + 2:3 * g + 3] * o_w[:, gsl])
    yt_ref[0] = jnp.concatenate(outs, axis=0).astype(yt_ref.dtype)


def _sel_win(tile_ids, tile_cnt, q_t, sel_t, ks, vs_t, kw, vw_t, oc_t, gl_t):
    batch, hk, nq, dk, qg = oc_t.shape
    t = nq * Q_BLOCK
    nkt = t // KEY_TILE
    grp = qg // Q_BLOCK
    qspec = pl.BlockSpec((1, 1, 1, 2 * dk, qg), lambda b, h, i, *_: (b, h, i, 0, 0))
    ospec = pl.BlockSpec((1, 1, 1, dk, qg), lambda b, h, i, *_: (b, h, i, 0, 0))
    kspec = pl.BlockSpec((1, 1, t, 2 * dk), lambda b, h, i, *_: (b, h, 0, 0))
    vspec = pl.BlockSpec((1, 1, nkt, V_ROWS, KEY_TILE), lambda b, h, i, *_: (b, h, 0, 0, 0))
    grid_spec = pltpu.PrefetchScalarGridSpec(
        num_scalar_prefetch=2,
        grid=(batch, hk, nq),
        in_specs=[qspec,
                  pl.BlockSpec((1, 1, 1, LANES, Q_BLOCK), lambda b, h, i, *_: (b, h, i, 0, 0)),
                  kspec, vspec, kspec, vspec, ospec,
                  pl.BlockSpec((1, 1, 3 * grp, Q_BLOCK), lambda b, h, i, *_: (b, h, 0, i))],
        out_specs=pl.BlockSpec((1, grp * dk, Q_BLOCK), lambda b, h, i, *_: (b, h, i)),
    )
    return pl.pallas_call(
        _sel_win_kernel,
        grid_spec=grid_spec,
        out_shape=jax.ShapeDtypeStruct((batch, hk * grp * dk, t), BF16),
        compiler_params=_params("arbitrary", "arbitrary", "arbitrary"),
        name="nsa_sel_win",
    )(tile_ids, tile_cnt, q_t, sel_t, ks, vs_t, kw, vw_t, oc_t, gl_t)


def _ffn_kernel(x_ref, yr_ref, ynt_ref, wo1_ref, wo2_ref, g2_ref, wg_ref, wu_ref, wd_ref, gf_ref, o_ref):
    h1 = (x_ref[...] + _dot(yr_ref[...].astype(BF16), wo1_ref[...])
          + _dot_tn(ynt_ref[0].astype(BF16), wo2_ref[...]))
    hn = _rms(h1, g2_ref[...]).astype(BF16)
    gate = _dot(hn, wg_ref[...])
    up = _dot(hn, wu_ref[...])
    act = gate * jax.nn.sigmoid(gate) * up
    o_ref[...] = _rms(h1 + _dot(act.astype(BF16), wd_ref[...]), gf_ref[...])


def _ffn(x2, yr, yn_t, wo1, wo2, g2, wg, wu, wd, gf, tm=512):
    m, d = x2.shape
    _, dn, t = yn_t.shape
    per_seq = t // tm
    row = lambda n: pl.BlockSpec((tm, n), lambda i: (i, 0))
    const = lambda a: pl.BlockSpec(a.shape, lambda i: (0, 0), pipeline_mode=pl.Buffered(1))
    return pl.pallas_call(
        _ffn_kernel,
        grid=(m // tm,),
        in_specs=[row(d), row(yr.shape[1]),
                  pl.BlockSpec((1, dn, tm), lambda i: (i // per_seq, 0, i % per_seq)),
                  const(wo1), const(wo2), const(g2), const(wg), const(wu), const(wd), const(gf)],
        out_specs=row(d),
        out_shape=jax.ShapeDtypeStruct((m, d), F32),
        compiler_params=_params("arbitrary"),
        name="outproj_ffn",
    )(x2, yr, yn_t, wo1, wo2, g2, wg, wu, wd, gf)


def _overlap_matrix_t(t):
    n16 = t // CMP_STRIDE
    n_cmp = (t - CMP_BLOCK) // CMP_STRIDE + 1
    n_sel = t // SEL_BLOCK
    cmp_start = np.arange(n_cmp) * CMP_STRIDE
    sel_start = np.arange(n_sel) * SEL_BLOCK
    ov = np.clip(np.minimum(cmp_start[:, None] + CMP_BLOCK, sel_start[None, :] + SEL_BLOCK)
                 - np.maximum(cmp_start[:, None], sel_start[None, :]), 0, None) / CMP_STRIDE
    full = np.zeros((LANES, n16), np.float32)
    full[:n_sel, :n_cmp] = ov.T
    return jnp.asarray(full, BF16)


def kernel(x, norm1_g, w_in, mu_shift, rwkv_w0, rwkv_w2, rwkv_a0, rwkv_a2, rwkv_g2, rwkv_k_k, rwkv_k_a,
           rwkv_r_k, rwkv_lnx_w, rwkv_lnx_b, nsa_pe_k, nsa_pe_v, nsa_cmp_k_w1, nsa_cmp_k_w2, nsa_cmp_v_w1,
           nsa_cmp_v_w2, w_out, norm2_g, ffn_w_gate, ffn_w_up, ffn_w_down, norm_f_g):
    batch, t, d_model = x.shape
    assert w_in.shape[0] == 1, "the final RMSNorm is fused into the (single) layer's FFN kernel"
    assert t % INPROJ_ROWS == 0 and t % RWKV_PREP_ROWS == 0 and t // SEL_BLOCK <= LANES
    hk, grp, dk = NSA_KV_HEADS, NSA_GROUP, HEAD_DIM
    nq = t // Q_BLOCK
    slopes = 2.0 ** (-8.0 * jnp.arange(1, NSA_Q_HEADS + 1, dtype=F32) / NSA_Q_HEADS)
    nsa_pad = _round_up(N_NSA_COLS, LANES)
    row = lambda a: a.reshape(1, -1)
    i = 0

    h = x.reshape(batch * t, d_model)
    w_r = w_in[i][:, :N_RWKV_COLS].astype(BF16)
    w_n = jnp.pad(w_in[i][:, N_RWKV_COLS:], ((0, 0), (0, nsa_pad - N_NSA_COLS))).astype(BF16)
    p_r, q_t, ks, kw, vs_t, vw_t, zk, zv, gl_t = _inproj(h, batch, row(norm1_g[i]), w_r, w_n,
                                                         _alibi_query_rows(slopes))

    y_rwkv = _rwkv(p_r.reshape(batch, t, N_RWKV_COLS), row(mu_shift[i]), row(rwkv_w0[i]), rwkv_w2[i].astype(BF16),
                   row(rwkv_a0[i]), rwkv_a2[i].astype(BF16), rwkv_g2[i].astype(BF16), row(rwkv_k_k[i]),
                   row(rwkv_k_a[i]), row(rwkv_r_k[i]), row(rwkv_lnx_w[i]), row(rwkv_lnx_b[i]))
    y_rwkv = y_rwkv.reshape(batch * t, D_RWKV)

    kc_aug, vc_t = _compress(zk, zv, nsa_pe_k[i], nsa_pe_v[i],
                             nsa_cmp_k_w1[i].astype(BF16), nsa_cmp_k_w2[i].astype(BF16),
                             nsa_cmp_v_w1[i].astype(BF16), nsa_cmp_v_w2[i].astype(BF16))
    oc_t, sel_t, picked = _cmp_select(q_t, kc_aug, vc_t, _overlap_matrix_t(t))

    blocks_per_tile = KEY_TILE // SEL_BLOCK
    active = (picked[:, :, :, 0, :] > 0).reshape(batch, hk, nq, MAX_KEY_TILES, blocks_per_tile).any(axis=-1)
    tile_ids = jnp.argsort(jnp.logical_not(active), axis=-1, stable=True).astype(jnp.int32).reshape(-1)
    tile_cnt = active.sum(axis=-1).astype(jnp.int32).reshape(-1)
    y_nsa_t = _sel_win(tile_ids, tile_cnt, q_t, sel_t, ks, vs_t, kw, vw_t, oc_t, gl_t)

    out = _ffn(h, y_rwkv, y_nsa_t, w_out[i][:D_RWKV].astype(BF16), w_out[i][D_RWKV:].astype(BF16),
               row(norm2_g[i]), ffn_w_gate[i].astype(BF16), ffn_w_up[i].astype(BF16),
               ffn_w_down[i].astype(BF16), row(norm_f_g))
    return out.reshape(batch, t, d_model)
```

```python
import ml_dtypes
import numpy as np
import jax
import jax.numpy as jnp
from jax import lax
from jax.experimental import pallas as pl
from jax.experimental.pallas import tpu as pltpu

F32 = jnp.float32
BF16 = jnp.bfloat16

HEAD_DIM = 64
RWKV_HEADS = 8
D_RWKV = RWKV_HEADS * HEAD_DIM
NSA_Q_HEADS = 8
NSA_KV_HEADS = 2
NSA_GROUP = NSA_Q_HEADS // NSA_KV_HEADS
D_NSA = NSA_Q_HEADS * HEAD_DIM
D_KV = NSA_KV_HEADS * HEAD_DIM
LORA_W, LORA_A, LORA_G = 64, 64, 128
N_RWKV_COLS = 3 * D_RWKV + LORA_W + LORA_A + LORA_G
N_NSA_COLS = D_NSA + 6 * D_KV + 3 * NSA_Q_HEADS
CMP_BLOCK, CMP_STRIDE = 32, 16
SEL_BLOCK, SEL_TOPK = 64, 16
WINDOW = 512
Q_BLOCK = 128
NORM_EPS = 1e-6
GN_EPS = 64e-5
NEG = -1e30
BIG = 1e30

LANES = 128
RWKV_CHUNK = 64
RWKV_PREP_ROWS = 8 * RWKV_CHUNK
RWKV_SCAN_CHUNKS = 4
PREP_SPLIT = 2
KEY_TILE = 128
INPROJ_ROWS = 512
MAX_KEY_TILES = LANES * SEL_BLOCK // KEY_TILE
WIN_TILES = (WINDOW + Q_BLOCK) // KEY_TILE
SEL_FIRST = 11
SEL_GROUP = 2
SEL_Q_BLOCKS = 2
CMP_Q_BLOCKS = 4
CMP_ROW_CHUNK = 128
V_ROWS = HEAD_DIM + 16
SOFTMAX_FLOOR = -1e20
QG = NSA_GROUP * Q_BLOCK
VMEM_LIMIT = 56 * 1024 * 1024


def _bf16_terms(x, count):
    terms = []
    for _ in range(count):
        terms.append(float(np.asarray(x, ml_dtypes.bfloat16)))
        x = x - terms[-1]
    return tuple(terms)


LOG2E = float(np.log2(np.e))
LOG2E_TERMS = _bf16_terms(LOG2E, 3)


def _round_up(n, m):
    return -(-n // m) * m


def _dot(a, b):
    return jnp.dot(a, b, preferred_element_type=F32)


def _dot_tn(a, b):
    return lax.dot_general(a, b, (((0,), (0,)), ((), ())), preferred_element_type=F32)


def _split(a, terms):
    pieces = []
    for _ in range(terms - 1):
        pieces.append(a.astype(BF16))
        a = a - pieces[-1].astype(F32)
    return pieces + [a.astype(BF16)]


def _dot_exact_lhs(a, b, terms=3):
    return sum(_dot(a, piece) for piece in _split(b, terms))


def _dot_exact_rhs(a, b, terms=3):
    return sum(_dot(piece, b) for piece in _split(a, terms))


def _bdot(spec, a, b):
    return jnp.einsum(spec, a, b, preferred_element_type=F32)


def _rms(x, g):
    return x * lax.rsqrt(jnp.mean(x * x, axis=-1, keepdims=True) + NORM_EPS) * g


def _params(*sem):
    return pltpu.CompilerParams(dimension_semantics=sem, vmem_limit_bytes=VMEM_LIMIT)


def _inproj_kernel(x_ref, g_ref, wr_ref, wn_ref, qx_ref, pr_ref, qt_ref, ks_ref, kw_ref, vst_ref, vwt_ref,
                   kc_ref, vc_ref, glt_ref):
    xb = _rms(x_ref[...], g_ref[...]).astype(BF16)
    pr_ref[...] = _dot(xb, wr_ref[...])
    pn = _dot(xb, wn_ref[...])
    rows = pn.shape[0]
    dk, hk_n = HEAD_DIM, NSA_KV_HEADS
    group = lambda j: pn[:, D_NSA + j * D_KV:D_NSA + (j + 1) * D_KV]

    for half in range(rows // Q_BLOCK):
        q_tr = jnp.transpose(pn[half * Q_BLOCK:(half + 1) * Q_BLOCK, :D_NSA])
        for hk in range(hk_n):
            base = hk * NSA_GROUP * dk
            heads = jnp.concatenate([q_tr[base + g * dk:base + (g + 1) * dk, :] for g in range(NSA_GROUP)], axis=1)
            qt_ref[0, hk, half] = jnp.concatenate([heads * (dk ** -0.5 * LOG2E), qx_ref[hk]], axis=0).astype(BF16)

    pos = pl.program_id(1) * rows + lax.broadcasted_iota(jnp.int32, (rows, dk), 0)
    col = lax.broadcasted_iota(jnp.int32, (rows, dk), 1)
    pos_cols = _position_features(pos // SEL_BLOCK, pos % SEL_BLOCK, col)
    kt = KEY_TILE
    ones_rows = (lax.broadcasted_iota(jnp.int32, (V_ROWS - dk, kt), 0) == 0).astype(F32)
    kc, vc, ks, vs, kw, vw = (group(j) for j in range(6))
    vs_tr, vw_tr = jnp.transpose(vs), jnp.transpose(vw)
    gl_tr = jnp.transpose(pn[:, D_NSA + 6 * D_KV:])
    n_gate = 3 * NSA_GROUP
    for hk in range(hk_n):
        sl = slice(hk * dk, (hk + 1) * dk)
        kc_ref[0, hk] = kc[:, sl]
        vc_ref[0, hk] = vc[:, sl]
        ks_ref[0, hk] = jnp.concatenate([ks[:, sl], pos_cols], axis=1).astype(BF16)
        kw_ref[0, hk] = jnp.concatenate([kw[:, sl], pos_cols], axis=1).astype(BF16)
        for u in range(rows // kt):
            vst_ref[0, hk, u] = jnp.concatenate([vs_tr[sl, u * kt:(u + 1) * kt], ones_rows], axis=0).astype(BF16)
            vwt_ref[0, hk, u] = jnp.concatenate([vw_tr[sl, u * kt:(u + 1) * kt], ones_rows], axis=0).astype(BF16)
        glt_ref[0, hk] = gl_tr[hk * n_gate:(hk + 1) * n_gate, :]


def _inproj(x2, batch, g, w_r, w_n, q_extra):
    m, d = x2.shape
    t = m // batch
    tm = INPROJ_ROWS
    nt = t // tm
    nr, nn = w_r.shape[1], w_n.shape[1]
    hk, dk = NSA_KV_HEADS, HEAD_DIM
    const = lambda a: pl.BlockSpec(a.shape, lambda b, i: (0, 0))
    keys = pl.BlockSpec((1, hk, tm, 2 * dk), lambda b, i: (b, 0, i, 0))
    vals = pl.BlockSpec((1, hk, tm // KEY_TILE, V_ROWS, KEY_TILE), lambda b, i: (b, 0, i, 0, 0))
    cmp_in = pl.BlockSpec((1, hk, tm, dk), lambda b, i: (b, 0, i, 0))
    keys_shape = jax.ShapeDtypeStruct((batch, hk, t, 2 * dk), BF16)
    vals_shape = jax.ShapeDtypeStruct((batch, hk, t // KEY_TILE, V_ROWS, KEY_TILE), BF16)
    cmp_shape = jax.ShapeDtypeStruct((batch, hk, t, dk), F32)
    return pl.pallas_call(
        _inproj_kernel,
        grid=(batch, nt),
        in_specs=[pl.BlockSpec((tm, d), lambda b, i: (b * nt + i, 0)), const(g), const(w_r), const(w_n),
                  pl.BlockSpec(q_extra.shape, lambda b, i: (0, 0, 0))],
        out_specs=[pl.BlockSpec((tm, nr), lambda b, i: (b * nt + i, 0)),
                   pl.BlockSpec((1, hk, tm // Q_BLOCK, 2 * dk, QG), lambda b, i: (b, 0, i, 0, 0)),
                   keys, keys, vals, vals, cmp_in, cmp_in,
                   pl.BlockSpec((1, hk, 3 * NSA_GROUP, tm), lambda b, i: (b, 0, 0, i))],
        out_shape=[jax.ShapeDtypeStruct((m, nr), F32),
                   jax.ShapeDtypeStruct((batch, hk, t // Q_BLOCK, 2 * dk, QG), BF16),
                   keys_shape, keys_shape, vals_shape, vals_shape, cmp_shape, cmp_shape,
                   jax.ShapeDtypeStruct((batch, hk, 3 * NSA_GROUP, t), F32)],
        compiler_params=_params("arbitrary", "arbitrary"),
        name="inproj",
    )(x2, g, w_r, w_n, q_extra)


def _rwkv_operands(p, p_prev, mu, w0, w2, a0, a2, g2, k_k, k_a, r_k, bd, tri):
    c = RWKV_CHUNK
    ps = p + mu * (p_prev - p)
    d = D_RWKV
    r, k, v = ps[:, 0:d], ps[:, d:2 * d], ps[:, 2 * d:3 * d]
    dw = ps[:, 3 * d:3 * d + LORA_W]
    da = ps[:, 3 * d + LORA_W:3 * d + LORA_W + LORA_A]
    dg = ps[:, 3 * d + LORA_W + LORA_A:]

    z = -(w0 + _dot(jnp.tanh(dw).astype(BF16), w2))
    softplus = jnp.maximum(z, 0.0) + jnp.log1p(jnp.exp(-jnp.abs(z)))
    logw = -jnp.exp(-softplus - 0.5)
    a = jax.nn.sigmoid(a0 + _dot(da.astype(BF16), a2))
    g = _dot(jax.nn.sigmoid(dg).astype(BF16), g2)

    kk = k * k_k
    kk = kk / jnp.maximum(jnp.sqrt(_dot_exact_rhs(kk * kk, bd, PREP_SPLIT)), 1e-12)
    kp = k * (1.0 + (a - 1.0) * k_a)
    bonus = _dot_exact_rhs(r * kp * r_k, bd, PREP_SPLIT) * v

    n_chunks = p.shape[0] // c
    cum = jnp.concatenate([_dot_exact_lhs(tri, logw[ci * c:(ci + 1) * c], PREP_SPLIT) for ci in range(n_chunks)],
                          axis=0)
    e_pos, e_neg = jnp.exp(cum), jnp.exp(-cum)
    wc = jnp.concatenate([e_pos[(ci + 1) * c - 1:(ci + 1) * c, :] for ci in range(n_chunks)], axis=0)
    ops = (r * e_pos, kp * e_neg, kk * a * e_neg, -kk * jnp.exp(cum - logw), v)
    return tuple(o.astype(BF16) for o in ops), wc, bonus, g


def _rwkv_prep_kernel(p_ref, mu_ref, w0_ref, w2_ref, a0_ref, a2_ref, g2_ref, kk_ref, ka_ref, rk_ref, bd_ref, tri_ref,
                      ops_ref, aux_ref, wc_ref, carry_ref):
    rows = p_ref.shape[1]

    @pl.when(pl.program_id(1) == 0)
    def _():
        carry_ref[...] = jnp.zeros_like(carry_ref)

    p = p_ref[0]
    row = lax.broadcasted_iota(jnp.int32, p.shape, 0)
    p_prev = jnp.where(row == 0, carry_ref[7:8, :], pltpu.roll(p, 1, axis=0))
    carry_ref[...] = p[rows - 8:, :]
    ops, wc, bonus, g = _rwkv_operands(
        p, p_prev, mu_ref[...], w0_ref[...], w2_ref[...], a0_ref[...], a2_ref[...], g2_ref[...],
        kk_ref[...], ka_ref[...], rk_ref[...], bd_ref[...], tri_ref[...])
    for idx, o in enumerate(ops):
        ops_ref[idx, 0] = o
    aux_ref[0, 0] = bonus
    aux_ref[1, 0] = g
    for ci in range(wc.shape[0]):
        wc_ref[0, ci] = wc[ci:ci + 1, :]


def _rwkv_scan_kernel(ops_ref, aux_ref, wc_ref, lnw_ref, lnb_ref, y_ref, s_ref):
    @pl.when(pl.program_id(0) == 0)
    def _():
        s_ref[...] = jnp.zeros_like(s_ref)

    _rwkv_step(ops_ref, aux_ref, wc_ref, lnw_ref, lnb_ref, y_ref, s_ref)


def _rwkv_step(ops_ref, aux_ref, wc_ref, lnw_ref, lnb_ref, y_ref, s_ref):
    _, nb, span, d = ops_ref.shape
    c = RWKV_CHUNK
    nh, dk = RWKV_HEADS, HEAD_DIM

    row = lax.broadcasted_iota(jnp.int32, (1, c, c), 1)
    col = lax.broadcasted_iota(jnp.int32, (1, c, c), 2)
    strict = col < row
    incl = col <= row
    eye = (row == col).astype(F32)

    def heads(x):
        return jnp.stack([x[b, :, h * dk:(h + 1) * dk] for b in range(nb) for h in range(nh)], axis=0)

    def independent(j):
        rt, kt, bt, at, vb = (heads(ops_ref[idx, :, j * c:(j + 1) * c, :]) for idx in range(5))
        ar = jnp.concatenate([at, rt], axis=1)
        bk = jnp.concatenate([bt, kt], axis=1)
        amat = _bdot("nik,njk->nij", ar, bk)
        n_ab = jnp.where(strict, amat[:, :c, :c], 0.0)
        a_ak = jnp.where(strict, amat[:, :c, c:], 0.0).astype(BF16)
        a_rb = jnp.where(incl, amat[:, c:, :c], 0.0).astype(BF16)
        a_rk = jnp.where(incl, amat[:, c:, c:], 0.0).astype(BF16)
        inv = eye + n_ab
        pw = n_ab
        for _ in range(int(np.log2(c)) - 1):
            pwb = pw.astype(BF16)
            pw = _bdot("nij,njk->nik", pwb, pwb)
            inv = inv + _bdot("nij,njk->nik", pw.astype(BF16), inv.astype(BF16))
        return ar, bk, vb, _bdot("nij,njv->niv", a_ak, vb), a_rb, a_rk, inv.astype(BF16)

    parts = [independent(j) for j in range(span // c)]
    s = s_ref[...]
    for j, (ar, bk, vb, akv, a_rb, a_rk, inv) in enumerate(parts):
        ar_s = _bdot("nik,nvk->niv", ar, s.astype(BF16))
        ub = _bdot("nij,njv->niv", inv, (ar_s[:, :c] + akv).astype(BF16)).astype(BF16)
        y = ar_s[:, c:] + _bdot("nij,njv->niv", a_rb, ub) + _bdot("nij,njv->niv", a_rk, vb)
        uv_t = jnp.swapaxes(jnp.concatenate([ub, vb], axis=1), 1, 2)
        s = (s + _bdot("nvi,nik->nvk", uv_t, bk)) * heads(wc_ref[:, j])

        mean = jnp.mean(y, axis=-1, keepdims=True)
        var = jnp.mean(jnp.square(y - mean), axis=-1, keepdims=True)
        yn = (y - mean) * lax.rsqrt(var + GN_EPS)
        rsl = slice(j * c, (j + 1) * c)
        for b in range(nb):
            wide = jnp.concatenate([yn[b * nh + h] for h in range(nh)], axis=-1)
            y_ref[b, rsl, :] = ((wide * lnw_ref[...] + lnb_ref[...] + aux_ref[0, b, rsl, :])
                                * aux_ref[1, b, rsl, :]).astype(y_ref.dtype)
    s_ref[...] = s


def _rwkv(p_r, mu, w0, w2, a0, a2, g2, k_k, k_a, r_k, lnw, lnb):
    batch, t, width = p_r.shape
    c = RWKV_CHUNK
    rows = RWKV_PREP_ROWS
    d = D_RWKV
    head = np.arange(d) // HEAD_DIM
    bd = jnp.asarray(head[:, None] == head[None, :], BF16)
    tri = jnp.asarray(np.tril(np.ones((c, c))), BF16)
    const2 = lambda a: pl.BlockSpec(a.shape, lambda b, i: (0, 0))
    consts = (mu, w0, w2, a0, a2, g2, k_k, k_a, r_k, bd, tri)
    ops, aux, wc = pl.pallas_call(
        _rwkv_prep_kernel,
        grid=(batch, t // rows),
        in_specs=[pl.BlockSpec((1, rows, width), lambda b, i: (b, i, 0))] + [const2(a) for a in consts],
        out_specs=[pl.BlockSpec((5, 1, rows, d), lambda b, i: (0, b, i, 0)),
                   pl.BlockSpec((2, 1, rows, d), lambda b, i: (0, b, i, 0)),
                   pl.BlockSpec((1, rows // c, 1, d), lambda b, i: (b, i, 0, 0))],
        out_shape=[jax.ShapeDtypeStruct((5, batch, t, d), BF16),
                   jax.ShapeDtypeStruct((2, batch, t, d), F32),
                   jax.ShapeDtypeStruct((batch, t // c, 1, d), F32)],
        scratch_shapes=[pltpu.VMEM((8, width), F32)],
        compiler_params=_params("arbitrary", "arbitrary"),
        name="rwkv_prep",
    )(p_r, *consts)
    const1 = lambda a: pl.BlockSpec(a.shape, lambda i: (0, 0))
    span = RWKV_SCAN_CHUNKS * c
    return pl.pallas_call(
        _rwkv_scan_kernel,
        grid=(t // span,),
        in_specs=[pl.BlockSpec((5, batch, span, d), lambda i: (0, 0, i, 0)),
                  pl.BlockSpec((2, batch, span, d), lambda i: (0, 0, i, 0)),
                  pl.BlockSpec((batch, RWKV_SCAN_CHUNKS, 1, d), lambda i: (0, i, 0, 0)),
                  const1(lnw), const1(lnb)],
        out_specs=pl.BlockSpec((batch, span, d), lambda i: (0, i, 0)),
        out_shape=jax.ShapeDtypeStruct((batch, t, d), BF16),
        scratch_shapes=[pltpu.VMEM((batch * RWKV_HEADS, HEAD_DIM, HEAD_DIM), F32)],
        compiler_params=_params("arbitrary"),
        name="rwkv_scan",
    )(ops, aux, wc, lnw, lnb)


def _compress_kernel(zk_ref, zv_ref, pek_ref, pev_ref, k1_ref, k2_ref, v1_ref, v2_ref, ko_ref, vo_ref):
    def one(z_ref, pe_ref, w1_ref, w2_ref):
        n16 = z_ref.shape[2] // CMP_STRIDE
        dk = HEAD_DIM
        first = second = None
        for l in range(CMP_STRIDE):
            z = z_ref[0, 0, pl.ds(l, n16, stride=CMP_STRIDE), :]
            lo = _dot((z + pe_ref[l:l + 1, :]).astype(BF16), w1_ref[l * dk:(l + 1) * dk, :])
            u = CMP_STRIDE + l
            hi = _dot((z + pe_ref[u:u + 1, :]).astype(BF16), w1_ref[u * dk:(u + 1) * dk, :])
            first = lo if first is None else first + lo
            second = hi if second is None else second + hi
        hidden = first + pltpu.roll(second, n16 - 1, axis=0)
        return _dot(jax.nn.gelu(hidden).astype(BF16), w2_ref[...])

    kc = one(zk_ref, pek_ref, k1_ref, k2_ref)
    blk = lax.broadcasted_iota(jnp.int32, kc.shape, 0)
    col = lax.broadcasted_iota(jnp.int32, kc.shape, 1)
    per = SEL_BLOCK // CMP_STRIDE
    hi = blk // per
    lo = CMP_STRIDE * (blk % per) + (CMP_BLOCK - 1)
    feat = _position_features(hi, lo, col)
    ko_ref[0] = jnp.concatenate([kc, feat], axis=1).astype(BF16)
    vc = one(zv_ref, pev_ref, v1_ref, v2_ref)
    vc_t = jnp.transpose(jnp.concatenate([vc, jnp.zeros_like(vc)], axis=1))
    vo_ref[0] = vc_t[:HEAD_DIM].astype(BF16)


def _compress(zk, zv, pek, pev, k1, k2, v1, v2):
    batch, hk, t, dk = zk.shape
    n16 = t // CMP_STRIDE
    zspec = pl.BlockSpec((1, 1, t, dk), lambda b, h: (b, h, 0, 0))
    full = lambda a: pl.BlockSpec(a.shape, lambda b, h: (0,) * a.ndim)
    return pl.pallas_call(
        _compress_kernel,
        grid=(batch, hk),
        in_specs=[zspec, zspec, full(pek), full(pev), full(k1), full(k2), full(v1), full(v2)],
        out_specs=[pl.BlockSpec((1, n16, 2 * dk), lambda b, h: (b * hk + h, 0, 0)),
                   pl.BlockSpec((1, dk, n16), lambda b, h: (b * hk + h, 0, 0))],
        out_shape=[jax.ShapeDtypeStruct((batch * hk, n16, 2 * dk), BF16),
                   jax.ShapeDtypeStruct((batch * hk, dk, n16), BF16)],
        compiler_params=_params("arbitrary", "arbitrary"),
        name="nsa_compress",
    )(zk, zv, pek, pev, k1, k2, v1, v2)


def _alibi_query_rows(slopes):
    per_lane = jnp.repeat(slopes.reshape(NSA_KV_HEADS, NSA_GROUP), Q_BLOCK, axis=1)
    rows = []
    for term in LOG2E_TERMS:
        rows += [(SEL_BLOCK * term) * per_lane, term * per_lane]
    rows = jnp.stack(rows, axis=1)
    return jnp.pad(rows, ((0, 0), (0, HEAD_DIM - rows.shape[1]), (0, 0)))


def _position_features(hi, lo, col):
    return jnp.where(col < 2 * len(LOG2E_TERMS), jnp.where(col % 2 == 0, hi, lo), 0).astype(F32)


def _cmp_select_kernel(qt_ref, kc_ref, vct_ref, ovt_ref, oct_ref, selt_ref, act_ref, imp_ref):
    nblk = qt_ref.shape[2]
    width = nblk * Q_BLOCK
    ncp = kc_ref.shape[1]

    def attend(rows):
        cmp_end = lax.broadcasted_iota(jnp.int32, (rows, Q_BLOCK), 0) * CMP_STRIDE + (CMP_BLOCK - 1)
        for u in range(nblk):
            q0 = (pl.program_id(2) * nblk + u) * Q_BLOCK
            s = _dot(kc_ref[0, 0:rows, :], qt_ref[0, 0, u])
            ok = cmp_end <= q0 + lax.broadcasted_iota(jnp.int32, (rows, Q_BLOCK), 1)
            any_ok = (q0 + lax.broadcasted_iota(jnp.int32, (1, Q_BLOCK), 1) >= CMP_BLOCK - 1).astype(F32)
            p_sum = jnp.zeros((rows, Q_BLOCK), F32)
            probs = []
            for g in range(NSA_GROUP):
                sg = jnp.where(ok, s[:, g * Q_BLOCK:(g + 1) * Q_BLOCK], NEG)
                e = jnp.exp2(sg - jnp.max(sg, axis=0, keepdims=True))
                p = e * (any_ok / jnp.sum(e, axis=0, keepdims=True))
                p_sum = p_sum + p
                probs.append(p.astype(BF16))
            oct_ref[0, 0, u] = _dot(vct_ref[0, :, 0:rows], jnp.concatenate(probs, axis=1))
            imp_ref[:, u * Q_BLOCK:(u + 1) * Q_BLOCK] = _dot_exact_lhs(ovt_ref[:, 0:rows], p_sum)

    chunk = min(CMP_ROW_CHUNK, ncp)
    n_chunks = ncp // chunk
    last_t = (pl.program_id(2) + 1) * width - 1
    needed = jnp.maximum((last_t - (CMP_BLOCK - 1)) // CMP_STRIDE + 1, 1)
    needed_chunks = jnp.minimum((needed + chunk - 1) // chunk, n_chunks)
    for nck in range(1, n_chunks + 1):
        pl.when(needed_chunks == nck)(lambda nck=nck: attend(nck * chunk))

    imp = imp_ref[...]
    blk = lax.broadcasted_iota(jnp.int32, (LANES, width), 0)
    cur = (pl.program_id(2) * width + lax.broadcasted_iota(jnp.int32, (LANES, width), 1)) // SEL_BLOCK
    valid = blk <= cur
    forced = (blk == 0) | (blk == cur) | (blk == cur - 1)
    x = jnp.where(forced, -jnp.inf, jnp.where(valid, imp, NEG))
    blk_f = blk.astype(F32)
    chosen = forced
    for _ in range(SEL_TOPK - 3):
        m = jnp.max(x, axis=0, keepdims=True)
        first = jnp.min(jnp.where(x == m, blk_f, float(LANES)), axis=0, keepdims=True)
        hit = blk_f == first
        chosen = chosen | hit
        x = jnp.where(hit, -jnp.inf, x)
    sel = (chosen & valid).astype(F32)
    ones = jnp.ones((8, Q_BLOCK), BF16)
    for u in range(nblk):
        sel_u = sel[:, u * Q_BLOCK:(u + 1) * Q_BLOCK]
        selt_ref[0, 0, u] = sel_u
        act_ref[0, 0, u] = lax.dot_general(ones, sel_u.astype(BF16), (((1,), (1,)), ((), ())),
                                           preferred_element_type=F32)


def _cmp_select(q_t, kc_aug, vc_t, ov_t):
    batch, hk, nq, dk2, qg = q_t.shape
    dk = dk2 // 2
    ncp = kc_aug.shape[1]
    nblk = CMP_Q_BLOCKS
    return pl.pallas_call(
        _cmp_select_kernel,
        grid=(batch, hk, nq // nblk),
        in_specs=[pl.BlockSpec((1, 1, nblk, dk2, qg), lambda b, h, i: (b, h, i, 0, 0)),
                  pl.BlockSpec((1, ncp, 2 * dk), lambda b, h, i: (b * hk + h, 0, 0)),
                  pl.BlockSpec((1, dk, ncp), lambda b, h, i: (b * hk + h, 0, 0)),
                  pl.BlockSpec(ov_t.shape, lambda b, h, i: (0, 0))],
        out_specs=[pl.BlockSpec((1, 1, nblk, dk, qg), lambda b, h, i: (b, h, i, 0, 0)),
                   pl.BlockSpec((1, 1, nblk, LANES, Q_BLOCK), lambda b, h, i: (b, h, i, 0, 0)),
                   pl.BlockSpec((1, 1, nblk, 8, LANES), lambda b, h, i: (b, h, i, 0, 0))],
        out_shape=[jax.ShapeDtypeStruct((batch, hk, nq, dk, qg), F32),
                   jax.ShapeDtypeStruct((batch, hk, nq, LANES, Q_BLOCK), F32),
                   jax.ShapeDtypeStruct((batch, hk, nq, 8, LANES), F32)],
        scratch_shapes=[pltpu.VMEM((LANES, nblk * Q_BLOCK), F32)],
        compiler_params=_params("arbitrary", "arbitrary", "arbitrary"),
        name="nsa_cmp_select",
    )(q_t, kc_aug, vc_t, ov_t)


def _sel_win_kernel(ids_ref, cnt_ref, qt_ref, selt_ref, ks_ref, vst_ref, kw_ref, vwt_ref,
                    oct_ref, glt_ref, yt_ref):
    b, hk, i = pl.program_id(0), pl.program_id(1), pl.program_id(2)
    nblk = qt_ref.shape[2]
    kt = KEY_TILE
    last_tile = ks_ref.shape[2] // kt - 1
    lane_minus_row = (lax.broadcasted_iota(jnp.int32, (kt, Q_BLOCK), 1)
                      - lax.broadcasted_iota(jnp.int32, (kt, Q_BLOCK), 0))
    start = (jnp.full((1, QG), SOFTMAX_FLOOR, F32), jnp.zeros((V_ROWS, QG), F32))
    result = lambda state: state[1][:HEAD_DIM] / state[1][HEAD_DIM:HEAD_DIM + 1]

    def block(u):
        blk = i * nblk + u
        step = (b * pl.num_programs(1) + hk) * (pl.num_programs(2) * nblk) + blk
        q0 = blk * Q_BLOCK
        q_aug = qt_ref[0, 0, u]
        count = cnt_ref[step]

        def all_scores(tiles):
            s_all = _dot(jnp.concatenate([k_tile for k_tile, _, _ in tiles], axis=0), q_aug)
            return [s_all[n * kt:(n + 1) * kt] for n in range(len(tiles))]

        def attend(state, tiles, scores=None):
            m_old, acc_old = state
            scores = all_scores(tiles) if scores is None else scores
            probs, maxes = [[] for _ in tiles], []
            for g in range(NSA_GROUP):
                gsl = slice(g * Q_BLOCK, (g + 1) * Q_BLOCK)
                masked = [jnp.where(mask, s[:, gsl], NEG) for s, (_, _, mask) in zip(scores, tiles)]
                top = jnp.max(masked[0].reshape(kt // 8, 8, Q_BLOCK), axis=0)
                for sg in masked[1:]:
                    top = jnp.maximum(top, jnp.max(sg.reshape(kt // 8, 8, Q_BLOCK), axis=0))
                mg = jnp.maximum(m_old[:, gsl], jnp.max(top, axis=0, keepdims=True))
                for n, sg in enumerate(masked):
                    probs[n].append(jnp.exp2(sg - mg).astype(BF16))
                maxes.append(mg)
            m_new = jnp.concatenate(maxes, axis=1)
            p_all = jnp.concatenate([jnp.concatenate(p, axis=1) for p in probs], axis=0)
            vt_all = jnp.concatenate([vt_tile for _, vt_tile, _ in tiles], axis=1)
            return m_new, jnp.exp2(m_old - m_new) * acc_old + _dot(vt_all, p_all)

        def sel_tile(n):
            j = jnp.minimum(ids_ref[step * MAX_KEY_TILES + jnp.minimum(n, MAX_KEY_TILES - 1)], last_tile)
            k0 = pl.multiple_of(j * kt, kt)
            per = kt // SEL_BLOCK
            picked = jnp.concatenate(
                [jnp.broadcast_to(selt_ref[0, 0, u, pl.ds(j * per + r, 1), :], (SEL_BLOCK, Q_BLOCK))
                 for r in range(per)], axis=0)
            causal_from = jnp.where(n < count, k0 - q0, 1 << 30)
            mask = (picked > 0.5) & (lane_minus_row >= causal_from)
            return ks_ref[0, 0, pl.ds(k0, kt), :], vst_ref[0, 0, j], mask

        last = (q0 + Q_BLOCK - 1) // kt

        def win_tile(r):
            j = last - r
            jc = jnp.maximum(j, 0)
            k0 = pl.multiple_of(jc * kt, kt)
            dist = lane_minus_row + jnp.where(j >= 0, q0 - k0, -(1 << 30))
            mask = (dist >= 0) & (dist < WINDOW)
            return kw_ref[0, 0, pl.ds(k0, kt), :], vwt_ref[0, 0, jc], mask

        win_tiles = [win_tile(r) for r in range(WIN_TILES)]
        sel_tiles = [sel_tile(n) for n in range(SEL_FIRST)]
        scores = all_scores(win_tiles + sel_tiles)
        state_w = attend(start, win_tiles, scores[:WIN_TILES])
        state_s = attend(start, sel_tiles, scores[WIN_TILES:])
        extra_steps = (jnp.maximum(count - SEL_FIRST, 0) + SEL_GROUP - 1) // SEL_GROUP
        more = lambda n, st: attend(st, [sel_tile(SEL_FIRST + SEL_GROUP * n + r) for r in range(SEL_GROUP)])
        return state_w, state_s, extra_steps, more

    blocks = [block(u) for u in range(nblk)]
    for u, (state_w, state_s, extra_steps, more) in enumerate(blocks):
        state_s = lax.fori_loop(0, extra_steps, more, state_s)
        o_s, o_w = result(state_s), result(state_w)
        lanes = slice(u * Q_BLOCK, (u + 1) * Q_BLOCK)
        gates = jax.nn.sigmoid(glt_ref[0, 0, :, lanes])
        o_c = oct_ref[0, 0, u]
        outs = []
        for g in range(NSA_GROUP):
            gsl = slice(g * Q_BLOCK, (g + 1) * Q_BLOCK)
            outs.append(gates[3 * g:3 * g + 1] * o_c[:, gsl] + gates[3 * g + 1:3 * g + 2] * o_s[:, gsl]
                        + gates[3 * g + 2:3 * g + 3] * o_w[:, gsl])
        yt_ref[0, :, lanes] = jnp.concatenate(outs, axis=0).astype(yt_ref.dtype)


def _sel_win(tile_ids, tile_cnt, q_t, sel_t, ks, vs_t, kw, vw_t, oc_t, gl_t):
    batch, hk, nq, dk, qg = oc_t.shape
    t = nq * Q_BLOCK
    nkt = t // KEY_TILE
    grp = qg // Q_BLOCK
    nblk = SEL_Q_BLOCKS
    qspec = pl.BlockSpec((1, 1, nblk, 2 * dk, qg), lambda b, h, i, *_: (b, h, i, 0, 0))
    ospec = pl.BlockSpec((1, 1, nblk, dk, qg), lambda b, h, i, *_: (b, h, i, 0, 0))
    kspec = pl.BlockSpec((1, 1, t, 2 * dk), lambda b, h, i, *_: (b, h, 0, 0))
    vspec = pl.BlockSpec((1, 1, nkt, V_ROWS, KEY_TILE), lambda b, h, i, *_: (b, h, 0, 0, 0))
    grid_spec = pltpu.PrefetchScalarGridSpec(
        num_scalar_prefetch=2,
        grid=(batch, hk, nq // nblk),
        in_specs=[qspec,
                  pl.BlockSpec((1, 1, nblk, LANES, Q_BLOCK), lambda b, h, i, *_: (b, h, i, 0, 0)),
                  kspec, vspec, kspec, vspec, ospec,
                  pl.BlockSpec((1, 1, 3 * grp, nblk * Q_BLOCK), lambda b, h, i, *_: (b, h, 0, i))],
        out_specs=pl.BlockSpec((1, grp * dk, nblk * Q_BLOCK), lambda b, h, i, *_: (b, h, i)),
    )
    return pl.pallas_call(
        _sel_win_kernel,
        grid_spec=grid_spec,
        out_shape=jax.ShapeDtypeStruct((batch, hk * grp * dk, t), BF16),
        compiler_params=_params("arbitrary", "arbitrary", "arbitrary"),
        name="nsa_sel_win",
    )(tile_ids, tile_cnt, q_t, sel_t, ks, vs_t, kw, vw_t, oc_t, gl_t)


def _ffn_kernel(x_ref, yr_ref, ynt_ref, wo1_ref, wo2_ref, g2_ref, wg_ref, wu_ref, wd_ref, gf_ref, o_ref):
    h1 = (x_ref[...] + _dot(yr_ref[...].astype(BF16), wo1_ref[...])
          + _dot_tn(ynt_ref[0].astype(BF16), wo2_ref[...]))
    hn = _rms(h1, g2_ref[...]).astype(BF16)
    gate = _dot(hn, wg_ref[...])
    up = _dot(hn, wu_ref[...])
    act = gate * jax.nn.sigmoid(gate) * up
    o_ref[...] = _rms(h1 + _dot(act.astype(BF16), wd_ref[...]), gf_ref[...])


def _ffn(x2, yr, yn_t, wo1, wo2, g2, wg, wu, wd, gf, tm=512):
    m, d = x2.shape
    _, dn, t = yn_t.shape
    per_seq = t // tm
    row = lambda n: pl.BlockSpec((tm, n), lambda i: (i, 0))
    const = lambda a: pl.BlockSpec(a.shape, lambda i: (0, 0), pipeline_mode=pl.Buffered(1))
    return pl.pallas_call(
        _ffn_kernel,
        grid=(m // tm,),
        in_specs=[row(d), row(yr.shape[1]),
                  pl.BlockSpec((1, dn, tm), lambda i: (i // per_seq, 0, i % per_seq)),
                  const(wo1), const(wo2), const(g2), const(wg), const(wu), const(wd), const(gf)],
        out_specs=row(d),
        out_shape=jax.ShapeDtypeStruct((m, d), F32),
        compiler_params=_params("arbitrary"),
        name="outproj_ffn",
    )(x2, yr, yn_t, wo1, wo2, g2, wg, wu, wd, gf)


def _overlap_matrix_t(t):
    n16 = t // CMP_STRIDE
    n_cmp = (t - CMP_BLOCK) // CMP_STRIDE + 1
    n_sel = t // SEL_BLOCK
    cmp_start = np.arange(n_cmp) * CMP_STRIDE
    sel_start = np.arange(n_sel) * SEL_BLOCK
    ov = np.clip(np.minimum(cmp_start[:, None] + CMP_BLOCK, sel_start[None, :] + SEL_BLOCK)
                 - np.maximum(cmp_start[:, None], sel_start[None, :]), 0, None) / CMP_STRIDE
    full = np.zeros((LANES, n16), np.float32)
    full[:n_sel, :n_cmp] = ov.T
    return jnp.asarray(full, BF16)


def kernel(x, norm1_g, w_in, mu_shift, rwkv_w0, rwkv_w2, rwkv_a0, rwkv_a2, rwkv_g2, rwkv_k_k, rwkv_k_a,
           rwkv_r_k, rwkv_lnx_w, rwkv_lnx_b, nsa_pe_k, nsa_pe_v, nsa_cmp_k_w1, nsa_cmp_k_w2, nsa_cmp_v_w1,
           nsa_cmp_v_w2, w_out, norm2_g, ffn_w_gate, ffn_w_up, ffn_w_down, norm_f_g):
    batch, t, d_model = x.shape
    assert w_in.shape[0] == 1, "the final RMSNorm is fused into the (single) layer's FFN kernel"
    assert t % INPROJ_ROWS == 0 and t % RWKV_PREP_ROWS == 0 and t // SEL_BLOCK <= LANES
    hk, grp, dk = NSA_KV_HEADS, NSA_GROUP, HEAD_DIM
    nq = t // Q_BLOCK
    slopes = 2.0 ** (-8.0 * jnp.arange(1, NSA_Q_HEADS + 1, dtype=F32) / NSA_Q_HEADS)
    nsa_pad = _round_up(N_NSA_COLS, LANES)
    row = lambda a: a.reshape(1, -1)
    i = 0

    h = x.reshape(batch * t, d_model)
    w_r = w_in[i][:, :N_RWKV_COLS].astype(BF16)
    w_n = jnp.pad(w_in[i][:, N_RWKV_COLS:], ((0, 0), (0, nsa_pad - N_NSA_COLS))).astype(BF16)
    p_r, q_t, ks, kw, vs_t, vw_t, zk, zv, gl_t = _inproj(h, batch, row(norm1_g[i]), w_r, w_n,
                                                         _alibi_query_rows(slopes))

    y_rwkv = _rwkv(p_r.reshape(batch, t, N_RWKV_COLS), row(mu_shift[i]), row(rwkv_w0[i]), rwkv_w2[i].astype(BF16),
                   row(rwkv_a0[i]), rwkv_a2[i].astype(BF16), rwkv_g2[i].astype(BF16), row(rwkv_k_k[i]),
                   row(rwkv_k_a[i]), row(rwkv_r_k[i]), row(rwkv_lnx_w[i]), row(rwkv_lnx_b[i]))
    y_rwkv = y_rwkv.reshape(batch * t, D_RWKV)

    kc_aug, vc_t = _compress(zk, zv, nsa_pe_k[i], nsa_pe_v[i],
                             nsa_cmp_k_w1[i].astype(BF16), nsa_cmp_k_w2[i].astype(BF16),
                             nsa_cmp_v_w1[i].astype(BF16), nsa_cmp_v_w2[i].astype(BF16))
    oc_t, sel_t, picked = _cmp_select(q_t, kc_aug, vc_t, _overlap_matrix_t(t))

    blocks_per_tile = KEY_TILE // SEL_BLOCK
    active = (picked[:, :, :, 0, :] > 0).reshape(batch, hk, nq, MAX_KEY_TILES, blocks_per_tile).any(axis=-1)
    tile_ids = jnp.argsort(jnp.logical_not(active), axis=-1, stable=True).astype(jnp.int32).reshape(-1)
    tile_cnt = active.sum(axis=-1).astype(jnp.int32).reshape(-1)
    y_nsa_t = _sel_win(tile_ids, tile_cnt, q_t, sel_t, ks, vs_t, kw, vw_t, oc_t, gl_t)

    out = _ffn(h, y_rwkv, y_nsa_t, w_out[i][:D_RWKV].astype(BF16), w_out[i][D_RWKV:].astype(BF16),
               row(norm2_g[i]), ffn_w_gate[i].astype(BF16), ffn_w_up[i].astype(BF16),
               ffn_w_down[i].astype(BF16), row(norm_f_g))
    return out.reshape(batch, t, d_model)
```

```python
import ml_dtypes
import numpy as np
import jax
import jax.numpy as jnp
from jax import lax
from jax.experimental import pallas as pl
from jax.experimental.pallas import tpu as pltpu

F32 = jnp.float32
BF16 = jnp.bfloat16

HEAD_DIM = 64
RWKV_HEADS = 8
D_RWKV = RWKV_HEADS * HEAD_DIM
NSA_Q_HEADS = 8
NSA_KV_HEADS = 2
NSA_GROUP = NSA_Q_HEADS // NSA_KV_HEADS
D_NSA = NSA_Q_HEADS * HEAD_DIM
D_KV = NSA_KV_HEADS * HEAD_DIM
LORA_W, LORA_A, LORA_G = 64, 64, 128
N_RWKV_COLS = 3 * D_RWKV + LORA_W + LORA_A + LORA_G
N_NSA_COLS = D_NSA + 6 * D_KV + 3 * NSA_Q_HEADS
CMP_BLOCK, CMP_STRIDE = 32, 16
SEL_BLOCK, SEL_TOPK = 64, 16
WINDOW = 512
Q_BLOCK = 128
NORM_EPS = 1e-6
GN_EPS = 64e-5
NEG = -1e30
BIG = 1e30

LANES = 128
RWKV_CHUNK = 64
RWKV_PREP_ROWS = 8 * RWKV_CHUNK
RWKV_SCAN_CHUNKS = 8
PREP_SPLIT = 2
KEY_TILE = 128
INPROJ_ROWS = 512
MAX_KEY_TILES = LANES * SEL_BLOCK // KEY_TILE
WIN_TILES = (WINDOW + Q_BLOCK) // KEY_TILE
SEL_FIRST = 11
SEL_GROUP = 2
SEL_Q_BLOCKS = 4
CMP_Q_BLOCKS = 4
CMP_ROW_CHUNK = 128
V_ROWS = HEAD_DIM + 16
SOFTMAX_FLOOR = -1e20
QG = NSA_GROUP * Q_BLOCK
VMEM_LIMIT = 56 * 1024 * 1024


def _bf16_terms(x, count):
    terms = []
    for _ in range(count):
        terms.append(float(np.asarray(x, ml_dtypes.bfloat16)))
        x = x - terms[-1]
    return tuple(terms)


LOG2E = float(np.log2(np.e))
LOG2E_TERMS = _bf16_terms(LOG2E, 3)


def _round_up(n, m):
    return -(-n // m) * m


def _dot(a, b):
    return jnp.dot(a, b, preferred_element_type=F32)


def _dot_tn(a, b):
    return lax.dot_general(a, b, (((0,), (0,)), ((), ())), preferred_element_type=F32)


def _split(a, terms):
    pieces = []
    for _ in range(terms - 1):
        pieces.append(a.astype(BF16))
        a = a - pieces[-1].astype(F32)
    return pieces + [a.astype(BF16)]


def _dot_exact_lhs(a, b, terms=3):
    return sum(_dot(a, piece) for piece in _split(b, terms))


def _dot_exact_rhs(a, b, terms=3):
    return sum(_dot(piece, b) for piece in _split(a, terms))


def _bdot(spec, a, b):
    return jnp.einsum(spec, a, b, preferred_element_type=F32)


def _rms(x, g):
    return x * lax.rsqrt(jnp.mean(x * x, axis=-1, keepdims=True) + NORM_EPS) * g


def _params(*sem):
    return pltpu.CompilerParams(dimension_semantics=sem, vmem_limit_bytes=VMEM_LIMIT)


def _inproj_kernel(x_ref, g_ref, wr_ref, wn_ref, qx_ref, pr_ref, qt_ref, ks_ref, kw_ref, vst_ref, vwt_ref,
                   kc_ref, vc_ref, glt_ref):
    xb = _rms(x_ref[...], g_ref[...]).astype(BF16)
    pr_ref[...] = _dot(xb, wr_ref[...])
    pn = _dot(xb, wn_ref[...])
    rows = pn.shape[0]
    dk, hk_n = HEAD_DIM, NSA_KV_HEADS
    group = lambda j: pn[:, D_NSA + j * D_KV:D_NSA + (j + 1) * D_KV]

    for half in range(rows // Q_BLOCK):
        q_tr = jnp.transpose(pn[half * Q_BLOCK:(half + 1) * Q_BLOCK, :D_NSA])
        for hk in range(hk_n):
            base = hk * NSA_GROUP * dk
            heads = jnp.concatenate([q_tr[base + g * dk:base + (g + 1) * dk, :] for g in range(NSA_GROUP)], axis=1)
            qt_ref[0, hk, half] = jnp.concatenate([heads * (dk ** -0.5 * LOG2E), qx_ref[hk]], axis=0).astype(BF16)

    pos = pl.program_id(1) * rows + lax.broadcasted_iota(jnp.int32, (rows, dk), 0)
    col = lax.broadcasted_iota(jnp.int32, (rows, dk), 1)
    pos_cols = _position_features(pos // SEL_BLOCK, pos % SEL_BLOCK, col)
    kt = KEY_TILE
    ones_rows = (lax.broadcasted_iota(jnp.int32, (V_ROWS - dk, kt), 0) == 0).astype(F32)
    kc, vc, ks, vs, kw, vw = (group(j) for j in range(6))
    vs_tr, vw_tr = jnp.transpose(vs), jnp.transpose(vw)
    gl_tr = jnp.transpose(pn[:, D_NSA + 6 * D_KV:])
    n_gate = 3 * NSA_GROUP
    for hk in range(hk_n):
        sl = slice(hk * dk, (hk + 1) * dk)
        kc_ref[0, hk] = kc[:, sl]
        vc_ref[0, hk] = vc[:, sl]
        ks_ref[0, hk] = jnp.concatenate([ks[:, sl], pos_cols], axis=1).astype(BF16)
        kw_ref[0, hk] = jnp.concatenate([kw[:, sl], pos_cols], axis=1).astype(BF16)
        for u in range(rows // kt):
            vst_ref[0, hk, u] = jnp.concatenate([vs_tr[sl, u * kt:(u + 1) * kt], ones_rows], axis=0).astype(BF16)
            vwt_ref[0, hk, u] = jnp.concatenate([vw_tr[sl, u * kt:(u + 1) * kt], ones_rows], axis=0).astype(BF16)
        glt_ref[0, hk] = gl_tr[hk * n_gate:(hk + 1) * n_gate, :]


def _inproj(x2, batch, g, w_r, w_n, q_extra):
    m, d = x2.shape
    t = m // batch
    tm = INPROJ_ROWS
    nt = t // tm
    nr, nn = w_r.shape[1], w_n.shape[1]
    hk, dk = NSA_KV_HEADS, HEAD_DIM
    const = lambda a: pl.BlockSpec(a.shape, lambda b, i: (0, 0))
    keys = pl.BlockSpec((1, hk, tm, 2 * dk), lambda b, i: (b, 0, i, 0))
    vals = pl.BlockSpec((1, hk, tm // KEY_TILE, V_ROWS, KEY_TILE), lambda b, i: (b, 0, i, 0, 0))
    cmp_in = pl.BlockSpec((1, hk, tm, dk), lambda b, i: (b, 0, i, 0))
    keys_shape = jax.ShapeDtypeStruct((batch, hk, t, 2 * dk), BF16)
    vals_shape = jax.ShapeDtypeStruct((batch, hk, t // KEY_TILE, V_ROWS, KEY_TILE), BF16)
    cmp_shape = jax.ShapeDtypeStruct((batch, hk, t, dk), F32)
    return pl.pallas_call(
        _inproj_kernel,
        grid=(batch, nt),
        in_specs=[pl.BlockSpec((tm, d), lambda b, i: (b * nt + i, 0)), const(g), const(w_r), const(w_n),
                  pl.BlockSpec(q_extra.shape, lambda b, i: (0, 0, 0))],
        out_specs=[pl.BlockSpec((tm, nr), lambda b, i: (b * nt + i, 0)),
                   pl.BlockSpec((1, hk, tm // Q_BLOCK, 2 * dk, QG), lambda b, i: (b, 0, i, 0, 0)),
                   keys, keys, vals, vals, cmp_in, cmp_in,
                   pl.BlockSpec((1, hk, 3 * NSA_GROUP, tm), lambda b, i: (b, 0, 0, i))],
        out_shape=[jax.ShapeDtypeStruct((m, nr), F32),
                   jax.ShapeDtypeStruct((batch, hk, t // Q_BLOCK, 2 * dk, QG), BF16),
                   keys_shape, keys_shape, vals_shape, vals_shape, cmp_shape, cmp_shape,
                   jax.ShapeDtypeStruct((batch, hk, 3 * NSA_GROUP, t), F32)],
        compiler_params=_params("arbitrary", "arbitrary"),
        name="inproj",
    )(x2, g, w_r, w_n, q_extra)


def _rwkv_operands(p, p_prev, mu, w0, w2, a0, a2, g2, k_k, k_a, r_k, bd, tri):
    c = RWKV_CHUNK
    ps = p + mu * (p_prev - p)
    d = D_RWKV
    r, k, v = ps[:, 0:d], ps[:, d:2 * d], ps[:, 2 * d:3 * d]
    dw = ps[:, 3 * d:3 * d + LORA_W]
    da = ps[:, 3 * d + LORA_W:3 * d + LORA_W + LORA_A]
    dg = ps[:, 3 * d + LORA_W + LORA_A:]

    z = -(w0 + _dot(jnp.tanh(dw).astype(BF16), w2))
    softplus = jnp.maximum(z, 0.0) + jnp.log1p(jnp.exp(-jnp.abs(z)))
    logw = -jnp.exp(-softplus - 0.5)
    a = jax.nn.sigmoid(a0 + _dot(da.astype(BF16), a2))
    g = _dot(jax.nn.sigmoid(dg).astype(BF16), g2)

    kk = k * k_k
    kk = kk / jnp.maximum(jnp.sqrt(_dot_exact_rhs(kk * kk, bd, PREP_SPLIT)), 1e-12)
    kp = k * (1.0 + (a - 1.0) * k_a)
    bonus = _dot_exact_rhs(r * kp * r_k, bd, PREP_SPLIT) * v

    n_chunks = p.shape[0] // c
    cum = jnp.concatenate([_dot_exact_lhs(tri, logw[ci * c:(ci + 1) * c], PREP_SPLIT) for ci in range(n_chunks)],
                          axis=0)
    e_pos, e_neg = jnp.exp(cum), jnp.exp(-cum)
    wc = jnp.concatenate([e_pos[(ci + 1) * c - 1:(ci + 1) * c, :] for ci in range(n_chunks)], axis=0)
    ops = (r * e_pos, kp * e_neg, kk * a * e_neg, -kk * jnp.exp(cum - logw), v)
    return tuple(o.astype(BF16) for o in ops), wc, bonus, g


def _rwkv_prep_kernel(p_ref, mu_ref, w0_ref, w2_ref, a0_ref, a2_ref, g2_ref, kk_ref, ka_ref, rk_ref, bd_ref, tri_ref,
                      ops_ref, aux_ref, wc_ref, carry_ref):
    rows = p_ref.shape[1]

    @pl.when(pl.program_id(1) == 0)
    def _():
        carry_ref[...] = jnp.zeros_like(carry_ref)

    p = p_ref[0]
    row = lax.broadcasted_iota(jnp.int32, p.shape, 0)
    p_prev = jnp.where(row == 0, carry_ref[7:8, :], pltpu.roll(p, 1, axis=0))
    carry_ref[...] = p[rows - 8:, :]
    ops, wc, bonus, g = _rwkv_operands(
        p, p_prev, mu_ref[...], w0_ref[...], w2_ref[...], a0_ref[...], a2_ref[...], g2_ref[...],
        kk_ref[...], ka_ref[...], rk_ref[...], bd_ref[...], tri_ref[...])
    for idx, o in enumerate(ops):
        ops_ref[idx, 0] = o
    aux_ref[0, 0] = bonus
    aux_ref[1, 0] = g
    for ci in range(wc.shape[0]):
        wc_ref[0, ci] = wc[ci:ci + 1, :]


def _rwkv_scan_kernel(ops_ref, aux_ref, wc_ref, lnw_ref, lnb_ref, y_ref, s_ref):
    @pl.when(pl.program_id(0) == 0)
    def _():
        s_ref[...] = jnp.zeros_like(s_ref)

    _rwkv_step(ops_ref, aux_ref, wc_ref, lnw_ref, lnb_ref, y_ref, s_ref)


def _rwkv_step(ops_ref, aux_ref, wc_ref, lnw_ref, lnb_ref, y_ref, s_ref):
    _, nb, span, d = ops_ref.shape
    c = RWKV_CHUNK
    nh, dk = RWKV_HEADS, HEAD_DIM

    row = lax.broadcasted_iota(jnp.int32, (1, c, c), 1)
    col = lax.broadcasted_iota(jnp.int32, (1, c, c), 2)
    strict = col < row
    incl = col <= row
    eye = (row == col).astype(F32)

    def heads(x):
        return jnp.stack([x[b, :, h * dk:(h + 1) * dk] for b in range(nb) for h in range(nh)], axis=0)

    def independent(j):
        rt, kt, bt, at, vb = (heads(ops_ref[idx, :, j * c:(j + 1) * c, :]) for idx in range(5))
        ar = jnp.concatenate([at, rt], axis=1)
        bk = jnp.concatenate([bt, kt], axis=1)
        amat = _bdot("nik,njk->nij", ar, bk)
        n_ab = jnp.where(strict, amat[:, :c, :c], 0.0)
        a_ak = jnp.where(strict, amat[:, :c, c:], 0.0).astype(BF16)
        a_rb = jnp.where(incl, amat[:, c:, :c], 0.0).astype(BF16)
        a_rk = jnp.where(incl, amat[:, c:, c:], 0.0).astype(BF16)
        inv = eye + n_ab
        pw = n_ab
        for _ in range(int(np.log2(c)) - 1):
            pwb = pw.astype(BF16)
            pw = _bdot("nij,njk->nik", pwb, pwb)
            inv = inv + _bdot("nij,njk->nik", pw.astype(BF16), inv.astype(BF16))
        return ar, bk, vb, _bdot("nij,njv->niv", a_ak, vb), a_rb, a_rk, inv.astype(BF16)

    parts = [independent(j) for j in range(span // c)]
    s = s_ref[...]
    for j, (ar, bk, vb, akv, a_rb, a_rk, inv) in enumerate(parts):
        ar_s = _bdot("nik,nvk->niv", ar, s.astype(BF16))
        ub = _bdot("nij,njv->niv", inv, (ar_s[:, :c] + akv).astype(BF16)).astype(BF16)
        y = ar_s[:, c:] + _bdot("nij,njv->niv", a_rb, ub) + _bdot("nij,njv->niv", a_rk, vb)
        uv_t = jnp.swapaxes(jnp.concatenate([ub, vb], axis=1), 1, 2)
        s = (s + _bdot("nvi,nik->nvk", uv_t, bk)) * heads(wc_ref[:, j])

        mean = jnp.mean(y, axis=-1, keepdims=True)
        var = jnp.mean(jnp.square(y - mean), axis=-1, keepdims=True)
        yn = (y - mean) * lax.rsqrt(var + GN_EPS)
        rsl = slice(j * c, (j + 1) * c)
        for b in range(nb):
            wide = jnp.concatenate([yn[b * nh + h] for h in range(nh)], axis=-1)
            y_ref[b, rsl, :] = ((wide * lnw_ref[...] + lnb_ref[...] + aux_ref[0, b, rsl, :])
                                * aux_ref[1, b, rsl, :]).astype(y_ref.dtype)
    s_ref[...] = s


def _rwkv(p_r, mu, w0, w2, a0, a2, g2, k_k, k_a, r_k, lnw, lnb):
    batch, t, width = p_r.shape
    c = RWKV_CHUNK
    rows = RWKV_PREP_ROWS
    d = D_RWKV
    head = np.arange(d) // HEAD_DIM
    bd = jnp.asarray(head[:, None] == head[None, :], BF16)
    tri = jnp.asarray(np.tril(np.ones((c, c))), BF16)
    const2 = lambda a: pl.BlockSpec(a.shape, lambda b, i: (0, 0))
    consts = (mu, w0, w2, a0, a2, g2, k_k, k_a, r_k, bd, tri)
    ops, aux, wc = pl.pallas_call(
        _rwkv_prep_kernel,
        grid=(batch, t // rows),
        in_specs=[pl.BlockSpec((1, rows, width), lambda b, i: (b, i, 0))] + [const2(a) for a in consts],
        out_specs=[pl.BlockSpec((5, 1, rows, d), lambda b, i: (0, b, i, 0)),
                   pl.BlockSpec((2, 1, rows, d), lambda b, i: (0, b, i, 0)),
                   pl.BlockSpec((1, rows // c, 1, d), lambda b, i: (b, i, 0, 0))],
        out_shape=[jax.ShapeDtypeStruct((5, batch, t, d), BF16),
                   jax.ShapeDtypeStruct((2, batch, t, d), F32),
                   jax.ShapeDtypeStruct((batch, t // c, 1, d), F32)],
        scratch_shapes=[pltpu.VMEM((8, width), F32)],
        compiler_params=_params("arbitrary", "arbitrary"),
        name="rwkv_prep",
    )(p_r, *consts)
    const1 = lambda a: pl.BlockSpec(a.shape, lambda i: (0, 0))
    span = RWKV_SCAN_CHUNKS * c
    return pl.pallas_call(
        _rwkv_scan_kernel,
        grid=(t // span,),
        in_specs=[pl.BlockSpec((5, batch, span, d), lambda i: (0, 0, i, 0)),
                  pl.BlockSpec((2, batch, span, d), lambda i: (0, 0, i, 0)),
                  pl.BlockSpec((batch, RWKV_SCAN_CHUNKS, 1, d), lambda i: (0, i, 0, 0)),
                  const1(lnw), const1(lnb)],
        out_specs=pl.BlockSpec((batch, span, d), lambda i: (0, i, 0)),
        out_shape=jax.ShapeDtypeStruct((batch, t, d), BF16),
        scratch_shapes=[pltpu.VMEM((batch * RWKV_HEADS, HEAD_DIM, HEAD_DIM), F32)],
        compiler_params=_params("arbitrary"),
        name="rwkv_scan",
    )(ops, aux, wc, lnw, lnb)


def _compress_kernel(zk_ref, zv_ref, pek_ref, pev_ref, k1_ref, k2_ref, v1_ref, v2_ref, ko_ref, vo_ref):
    def one(z_ref, pe_ref, w1_ref, w2_ref):
        n16 = z_ref.shape[2] // CMP_STRIDE
        dk = HEAD_DIM
        first = second = None
        for l in range(CMP_STRIDE):
            z = z_ref[0, 0, pl.ds(l, n16, stride=CMP_STRIDE), :]
            lo = _dot((z + pe_ref[l:l + 1, :]).astype(BF16), w1_ref[l * dk:(l + 1) * dk, :])
            u = CMP_STRIDE + l
            hi = _dot((z + pe_ref[u:u + 1, :]).astype(BF16), w1_ref[u * dk:(u + 1) * dk, :])
            first = lo if first is None else first + lo
            second = hi if second is None else second + hi
        hidden = first + pltpu.roll(second, n16 - 1, axis=0)
        return _dot(jax.nn.gelu(hidden).astype(BF16), w2_ref[...])

    kc = one(zk_ref, pek_ref, k1_ref, k2_ref)
    blk = lax.broadcasted_iota(jnp.int32, kc.shape, 0)
    col = lax.broadcasted_iota(jnp.int32, kc.shape, 1)
    per = SEL_BLOCK // CMP_STRIDE
    hi = blk // per
    lo = CMP_STRIDE * (blk % per) + (CMP_BLOCK - 1)
    feat = _position_features(hi, lo, col)
    ko_ref[0] = jnp.concatenate([kc, feat], axis=1).astype(BF16)
    vc = one(zv_ref, pev_ref, v1_ref, v2_ref)
    vc_t = jnp.transpose(jnp.concatenate([vc, jnp.zeros_like(vc)], axis=1))
    vo_ref[0] = vc_t[:HEAD_DIM].astype(BF16)


def _compress(zk, zv, pek, pev, k1, k2, v1, v2):
    batch, hk, t, dk = zk.shape
    n16 = t // CMP_STRIDE
    zspec = pl.BlockSpec((1, 1, t, dk), lambda b, h: (b, h, 0, 0))
    full = lambda a: pl.BlockSpec(a.shape, lambda b, h: (0,) * a.ndim)
    return pl.pallas_call(
        _compress_kernel,
        grid=(batch, hk),
        in_specs=[zspec, zspec, full(pek), full(pev), full(k1), full(k2), full(v1), full(v2)],
        out_specs=[pl.BlockSpec((1, n16, 2 * dk), lambda b, h: (b * hk + h, 0, 0)),
                   pl.BlockSpec((1, dk, n16), lambda b, h: (b * hk + h, 0, 0))],
        out_shape=[jax.ShapeDtypeStruct((batch * hk, n16, 2 * dk), BF16),
                   jax.ShapeDtypeStruct((batch * hk, dk, n16), BF16)],
        compiler_params=_params("arbitrary", "arbitrary"),
        name="nsa_compress",
    )(zk, zv, pek, pev, k1, k2, v1, v2)


def _alibi_query_rows(slopes):
    per_lane = jnp.repeat(slopes.reshape(NSA_KV_HEADS, NSA_GROUP), Q_BLOCK, axis=1)
    rows = []
    for term in LOG2E_TERMS:
        rows += [(SEL_BLOCK * term) * per_lane, term * per_lane]
    rows = jnp.stack(rows, axis=1)
    return jnp.pad(rows, ((0, 0), (0, HEAD_DIM - rows.shape[1]), (0, 0)))


def _position_features(hi, lo, col):
    return jnp.where(col < 2 * len(LOG2E_TERMS), jnp.where(col % 2 == 0, hi, lo), 0).astype(F32)


def _cmp_select_kernel(qt_ref, kc_ref, vct_ref, ovt_ref, oct_ref, selt_ref, act_ref, imp_ref):
    nblk = qt_ref.shape[2]
    width = nblk * Q_BLOCK
    ncp = kc_ref.shape[1]

    def attend(rows):
        cmp_end = lax.broadcasted_iota(jnp.int32, (rows, Q_BLOCK), 0) * CMP_STRIDE + (CMP_BLOCK - 1)
        for u in range(nblk):
            q0 = (pl.program_id(2) * nblk + u) * Q_BLOCK
            s = _dot(kc_ref[0, 0:rows, :], qt_ref[0, 0, u])
            ok = cmp_end <= q0 + lax.broadcasted_iota(jnp.int32, (rows, Q_BLOCK), 1)
            any_ok = (q0 + lax.broadcasted_iota(jnp.int32, (1, Q_BLOCK), 1) >= CMP_BLOCK - 1).astype(F32)
            p_sum = jnp.zeros((rows, Q_BLOCK), F32)
            probs = []
            for g in range(NSA_GROUP):
                sg = jnp.where(ok, s[:, g * Q_BLOCK:(g + 1) * Q_BLOCK], NEG)
                e = jnp.exp2(sg - jnp.max(sg, axis=0, keepdims=True))
                p = e * (any_ok / jnp.sum(e, axis=0, keepdims=True))
                p_sum = p_sum + p
                probs.append(p.astype(BF16))
            oct_ref[0, 0, u] = _dot(vct_ref[0, :, 0:rows], jnp.concatenate(probs, axis=1))
            imp_ref[:, u * Q_BLOCK:(u + 1) * Q_BLOCK] = _dot_exact_lhs(ovt_ref[:, 0:rows], p_sum)

    chunk = min(CMP_ROW_CHUNK, ncp)
    n_chunks = ncp // chunk
    last_t = (pl.program_id(2) + 1) * width - 1
    needed = jnp.maximum((last_t - (CMP_BLOCK - 1)) // CMP_STRIDE + 1, 1)
    needed_chunks = jnp.minimum((needed + chunk - 1) // chunk, n_chunks)
    for nck in range(1, n_chunks + 1):
        pl.when(needed_chunks == nck)(lambda nck=nck: attend(nck * chunk))

    imp = imp_ref[...]
    blk = lax.broadcasted_iota(jnp.int32, (LANES, width), 0)
    cur = (pl.program_id(2) * width + lax.broadcasted_iota(jnp.int32, (LANES, width), 1)) // SEL_BLOCK
    valid = blk <= cur
    forced = (blk == 0) | (blk == cur) | (blk == cur - 1)
    x = jnp.where(forced, -jnp.inf, jnp.where(valid, imp, NEG))
    blk_f = blk.astype(F32)
    for _ in range(SEL_TOPK - 3):
        m = jnp.max(x, axis=0, keepdims=True)
        first = jnp.min(jnp.where(x == m, blk_f, float(LANES)), axis=0, keepdims=True)
        x = jnp.where(blk_f == first, -jnp.inf, x)
    sel = ((x == -jnp.inf) & valid).astype(F32)
    ones = jnp.ones((8, Q_BLOCK), BF16)
    for u in range(nblk):
        sel_u = sel[:, u * Q_BLOCK:(u + 1) * Q_BLOCK]
        selt_ref[0, 0, u] = sel_u
        act_ref[0, 0, u] = lax.dot_general(ones, sel_u.astype(BF16), (((1,), (1,)), ((), ())),
                                           preferred_element_type=F32)


def _cmp_select(q_t, kc_aug, vc_t, ov_t):
    batch, hk, nq, dk2, qg = q_t.shape
    dk = dk2 // 2
    ncp = kc_aug.shape[1]
    nblk = CMP_Q_BLOCKS
    return pl.pallas_call(
        _cmp_select_kernel,
        grid=(batch, hk, nq // nblk),
        in_specs=[pl.BlockSpec((1, 1, nblk, dk2, qg), lambda b, h, i: (b, h, i, 0, 0)),
                  pl.BlockSpec((1, ncp, 2 * dk), lambda b, h, i: (b * hk + h, 0, 0)),
                  pl.BlockSpec((1, dk, ncp), lambda b, h, i: (b * hk + h, 0, 0)),
                  pl.BlockSpec(ov_t.shape, lambda b, h, i: (0, 0))],
        out_specs=[pl.BlockSpec((1, 1, nblk, dk, qg), lambda b, h, i: (b, h, i, 0, 0)),
                   pl.BlockSpec((1, 1, nblk, LANES, Q_BLOCK), lambda b, h, i: (b, h, i, 0, 0)),
                   pl.BlockSpec((1, 1, nblk, 8, LANES), lambda b, h, i: (b, h, i, 0, 0))],
        out_shape=[jax.ShapeDtypeStruct((batch, hk, nq, dk, qg), F32),
                   jax.ShapeDtypeStruct((batch, hk, nq, LANES, Q_BLOCK), F32),
                   jax.ShapeDtypeStruct((batch, hk, nq, 8, LANES), F32)],
        scratch_shapes=[pltpu.VMEM((LANES, nblk * Q_BLOCK), F32)],
        compiler_params=_params("arbitrary", "arbitrary", "arbitrary"),
        name="nsa_cmp_select",
    )(q_t, kc_aug, vc_t, ov_t)


def _sel_win_kernel(ids_ref, cnt_ref, qt_ref, selt_ref, ks_ref, vst_ref, kw_ref, vwt_ref,
                    oct_ref, glt_ref, yt_ref):
    b, hk, i = pl.program_id(0), pl.program_id(1), pl.program_id(2)
    nblk = qt_ref.shape[2]
    kt = KEY_TILE
    last_tile = ks_ref.shape[2] // kt - 1
    lane_minus_row = (lax.broadcasted_iota(jnp.int32, (kt, Q_BLOCK), 1)
                      - lax.broadcasted_iota(jnp.int32, (kt, Q_BLOCK), 0))
    start = (jnp.full((1, QG), SOFTMAX_FLOOR, F32), jnp.zeros((V_ROWS, QG), F32))
    result = lambda state: state[1][:HEAD_DIM] / state[1][HEAD_DIM:HEAD_DIM + 1]

    def block(u):
        blk = i * nblk + u
        step = (b * pl.num_programs(1) + hk) * (pl.num_programs(2) * nblk) + blk
        q0 = blk * Q_BLOCK
        q_aug = qt_ref[0, 0, u]
        count = cnt_ref[step]

        def all_scores(tiles):
            s_all = _dot(jnp.concatenate([k_tile for k_tile, _, _ in tiles], axis=0), q_aug)
            return [s_all[n * kt:(n + 1) * kt] for n in range(len(tiles))]

        def attend(state, tiles, scores=None):
            m_old, acc_old = state
            scores = all_scores(tiles) if scores is None else scores
            probs, maxes = [[] for _ in tiles], []
            for g in range(NSA_GROUP):
                gsl = slice(g * Q_BLOCK, (g + 1) * Q_BLOCK)
                masked = [jnp.where(mask, s[:, gsl], NEG) for s, (_, _, mask) in zip(scores, tiles)]
                top = jnp.max(masked[0].reshape(kt // 8, 8, Q_BLOCK), axis=0)
                for sg in masked[1:]:
                    top = jnp.maximum(top, jnp.max(sg.reshape(kt // 8, 8, Q_BLOCK), axis=0))
                mg = jnp.maximum(m_old[:, gsl], jnp.max(top, axis=0, keepdims=True))
                for n, sg in enumerate(masked):
                    probs[n].append(jnp.exp2(sg - mg).astype(BF16))
                maxes.append(mg)
            m_new = jnp.concatenate(maxes, axis=1)
            p_all = jnp.concatenate([jnp.concatenate(p, axis=1) for p in probs], axis=0)
            vt_all = jnp.concatenate([vt_tile for _, vt_tile, _ in tiles], axis=1)
            return m_new, jnp.exp2(m_old - m_new) * acc_old + _dot(vt_all, p_all)

        def sel_tile(n):
            j = jnp.minimum(ids_ref[step * MAX_KEY_TILES + jnp.minimum(n, MAX_KEY_TILES - 1)], last_tile)
            k0 = pl.multiple_of(j * kt, kt)
            per = kt // SEL_BLOCK
            picked = jnp.concatenate(
                [jnp.broadcast_to(selt_ref[0, 0, u, pl.ds(j * per + r, 1), :], (SEL_BLOCK, Q_BLOCK))
                 for r in range(per)], axis=0)
            causal_from = jnp.where(n < count, k0 - q0, 1 << 30)
            mask = (picked > 0.5) & (lane_minus_row >= causal_from)
            return ks_ref[0, 0, pl.ds(k0, kt), :], vst_ref[0, 0, j], mask

        last = (q0 + Q_BLOCK - 1) // kt

        def win_tile(r):
            j = last - r
            jc = jnp.maximum(j, 0)
            k0 = pl.multiple_of(jc * kt, kt)
            dist = lane_minus_row + jnp.where(j >= 0, q0 - k0, -(1 << 30))
            mask = (dist >= 0) & (dist < WINDOW)
            return kw_ref[0, 0, pl.ds(k0, kt), :], vwt_ref[0, 0, jc], mask

        win_tiles = [win_tile(r) for r in range(WIN_TILES)]
        sel_tiles = [sel_tile(n) for n in range(SEL_FIRST)]
        scores = all_scores(win_tiles + sel_tiles)
        state_w = attend(start, win_tiles, scores[:WIN_TILES])
        state_s = attend(start, sel_tiles, scores[WIN_TILES:])
        extra_steps = (jnp.maximum(count - SEL_FIRST, 0) + SEL_GROUP - 1) // SEL_GROUP
        more = lambda n, st: attend(st, [sel_tile(SEL_FIRST + SEL_GROUP * n + r) for r in range(SEL_GROUP)])
        return state_w, state_s, extra_steps, more

    blocks = [block(u) for u in range(nblk)]
    for u, (state_w, state_s, extra_steps, more) in enumerate(blocks):
        state_s = lax.fori_loop(0, extra_steps, more, state_s)
        o_s, o_w = result(state_s), result(state_w)
        lanes = slice(u * Q_BLOCK, (u + 1) * Q_BLOCK)
        gates = jax.nn.sigmoid(glt_ref[0, 0, :, lanes])
        o_c = oct_ref[0, 0, u]
        outs = []
        for g in range(NSA_GROUP):
            gsl = slice(g * Q_BLOCK, (g + 1) * Q_BLOCK)
            outs.append(gates[3 * g:3 * g + 1] * o_c[:, gsl] + gates[3 * g + 1:3 * g + 2] * o_s[:, gsl]
                        + gates[3 * g + 2:3 * g + 3] * o_w[:, gsl])
        yt_ref[0, :, lanes] = jnp.concatenate(outs, axis=0).astype(yt_ref.dtype)


def _sel_win(tile_ids, tile_cnt, q_t, sel_t, ks, vs_t, kw, vw_t, oc_t, gl_t):
    batch, hk, nq, dk, qg = oc_t.shape
    t = nq * Q_BLOCK
    nkt = t // KEY_TILE
    grp = qg // Q_BLOCK
    nblk = SEL_Q_BLOCKS
    qspec = pl.BlockSpec((1, 1, nblk, 2 * dk, qg), lambda b, h, i, *_: (b, h, i, 0, 0))
    ospec = pl.BlockSpec((1, 1, nblk, dk, qg), lambda b, h, i, *_: (b, h, i, 0, 0))
    kspec = pl.BlockSpec((1, 1, t, 2 * dk), lambda b, h, i, *_: (b, h, 0, 0))
    vspec = pl.BlockSpec((1, 1, nkt, V_ROWS, KEY_TILE), lambda b, h, i, *_: (b, h, 0, 0, 0))
    grid_spec = pltpu.PrefetchScalarGridSpec(
        num_scalar_prefetch=2,
        grid=(batch, hk, nq // nblk),
        in_specs=[qspec,
                  pl.BlockSpec((1, 1, nblk, LANES, Q_BLOCK), lambda b, h, i, *_: (b, h, i, 0, 0)),
                  kspec, vspec, kspec, vspec, ospec,
                  pl.BlockSpec((1, 1, 3 * grp, nblk * Q_BLOCK), lambda b, h, i, *_: (b, h, 0, i))],
        out_specs=pl.BlockSpec((1, grp * dk, nblk * Q_BLOCK), lambda b, h, i, *_: (b, h, i)),
    )
    return pl.pallas_call(
        _sel_win_kernel,
        grid_spec=grid_spec,
        out_shape=jax.ShapeDtypeStruct((batch, hk * grp * dk, t), BF16),
        compiler_params=_params("arbitrary", "arbitrary", "arbitrary"),
        name="nsa_sel_win",
    )(tile_ids, tile_cnt, q_t, sel_t, ks, vs_t, kw, vw_t, oc_t, gl_t)


def _ffn_kernel(x_ref, yr_ref, ynt_ref, wo1_ref, wo2_ref, g2_ref, wg_ref, wu_ref, wd_ref, gf_ref, o_ref):
    h1 = (x_ref[...] + _dot(yr_ref[...].astype(BF16), wo1_ref[...])
          + _dot_tn(ynt_ref[0].astype(BF16), wo2_ref[...]))
    hn = _rms(h1, g2_ref[...]).astype(BF16)
    gate = _dot(hn, wg_ref[...])
    up = _dot(hn, wu_ref[...])
    act = gate * jax.nn.sigmoid(gate) * up
    o_ref[...] = _rms(h1 + _dot(act.astype(BF16), wd_ref[...]), gf_ref[...])


def _ffn(x2, yr, yn_t, wo1, wo2, g2, wg, wu, wd, gf, tm=512):
    m, d = x2.shape
    _, dn, t = yn_t.shape
    per_seq = t // tm
    row = lambda n: pl.BlockSpec((tm, n), lambda i: (i, 0))
    const = lambda a: pl.BlockSpec(a.shape, lambda i: (0, 0), pipeline_mode=pl.Buffered(1))
    return pl.pallas_call(
        _ffn_kernel,
        grid=(m // tm,),
        in_specs=[row(d), row(yr.shape[1]),
                  pl.BlockSpec((1, dn, tm), lambda i: (i // per_seq, 0, i % per_seq)),
                  const(wo1), const(wo2), const(g2), const(wg), const(wu), const(wd), const(gf)],
        out_specs=row(d),
        out_shape=jax.ShapeDtypeStruct((m, d), F32),
        compiler_params=_params("arbitrary"),
        name="outproj_ffn",
    )(x2, yr, yn_t, wo1, wo2, g2, wg, wu, wd, gf)


def _overlap_matrix_t(t):
    n16 = t // CMP_STRIDE
    n_cmp = (t - CMP_BLOCK) // CMP_STRIDE + 1
    n_sel = t // SEL_BLOCK
    cmp_start = np.arange(n_cmp) * CMP_STRIDE
    sel_start = np.arange(n_sel) * SEL_BLOCK
    ov = np.clip(np.minimum(cmp_start[:, None] + CMP_BLOCK, sel_start[None, :] + SEL_BLOCK)
                 - np.maximum(cmp_start[:, None], sel_start[None, :]), 0, None) / CMP_STRIDE
    full = np.zeros((LANES, n16), np.float32)
    full[:n_sel, :n_cmp] = ov.T
    return jnp.asarray(full, BF16)


def kernel(x, norm1_g, w_in, mu_shift, rwkv_w0, rwkv_w2, rwkv_a0, rwkv_a2, rwkv_g2, rwkv_k_k, rwkv_k_a,
           rwkv_r_k, rwkv_lnx_w, rwkv_lnx_b, nsa_pe_k, nsa_pe_v, nsa_cmp_k_w1, nsa_cmp_k_w2, nsa_cmp_v_w1,
           nsa_cmp_v_w2, w_out, norm2_g, ffn_w_gate, ffn_w_up, ffn_w_down, norm_f_g):
    batch, t, d_model = x.shape
    assert w_in.shape[0] == 1, "the final RMSNorm is fused into the (single) layer's FFN kernel"
    assert t % INPROJ_ROWS == 0 and t % RWKV_PREP_ROWS == 0 and t // SEL_BLOCK <= LANES
    hk, grp, dk = NSA_KV_HEADS, NSA_GROUP, HEAD_DIM
    nq = t // Q_BLOCK
    slopes = 2.0 ** (-8.0 * jnp.arange(1, NSA_Q_HEADS + 1, dtype=F32) / NSA_Q_HEADS)
    nsa_pad = _round_up(N_NSA_COLS, LANES)
    row = lambda a: a.reshape(1, -1)
    i = 0

    h = x.reshape(batch * t, d_model)
    w_r = w_in[i][:, :N_RWKV_COLS].astype(BF16)
    w_n = jnp.pad(w_in[i][:, N_RWKV_COLS:], ((0, 0), (0, nsa_pad - N_NSA_COLS))).astype(BF16)
    p_r, q_t, ks, kw, vs_t, vw_t, zk, zv, gl_t = _inproj(h, batch, row(norm1_g[i]), w_r, w_n,
                                                         _alibi_query_rows(slopes))

    y_rwkv = _rwkv(p_r.reshape(batch, t, N_RWKV_COLS), row(mu_shift[i]), row(rwkv_w0[i]), rwkv_w2[i].astype(BF16),
                   row(rwkv_a0[i]), rwkv_a2[i].astype(BF16), rwkv_g2[i].astype(BF16), row(rwkv_k_k[i]),
                   row(rwkv_k_a[i]), row(rwkv_r_k[i]), row(rwkv_lnx_w[i]), row(rwkv_lnx_b[i]))
    y_rwkv = y_rwkv.reshape(batch * t, D_RWKV)

    kc_aug, vc_t = _compress(zk, zv, nsa_pe_k[i], nsa_pe_v[i],
                             nsa_cmp_k_w1[i].astype(BF16), nsa_cmp_k_w2[i].astype(BF16),
                             nsa_cmp_v_w1[i].astype(BF16), nsa_cmp_v_w2[i].astype(BF16))
    oc_t, sel_t, picked = _cmp_select(q_t, kc_aug, vc_t, _overlap_matrix_t(t))

    blocks_per_tile = KEY_TILE // SEL_BLOCK
    active = (picked[:, :, :, 0, :] > 0).reshape(batch, hk, nq, MAX_KEY_TILES, blocks_per_tile).any(axis=-1)
    tile_ids = jnp.argsort(jnp.logical_not(active), axis=-1, stable=True).astype(jnp.int32).reshape(-1)
    tile_cnt = active.sum(axis=-1).astype(jnp.int32).reshape(-1)
    y_nsa_t = _sel_win(tile_ids, tile_cnt, q_t, sel_t, ks, vs_t, kw, vw_t, oc_t, gl_t)

    out = _ffn(h, y_rwkv, y_nsa_t, w_out[i][:D_RWKV].astype(BF16), w_out[i][D_RWKV:].astype(BF16),
               row(norm2_g[i]), ffn_w_gate[i].astype(BF16), ffn_w_up[i].astype(BF16),
               ffn_w_down[i].astype(BF16), row(norm_f_g))
    return out.reshape(batch, t, d_model)
```

```python
import ml_dtypes
import numpy as np
import jax
import jax.numpy as jnp
from jax import lax
from jax.experimental import pallas as pl
from jax.experimental.pallas import tpu as pltpu

F32 = jnp.float32
BF16 = jnp.bfloat16

HEAD_DIM = 64
RWKV_HEADS = 8
D_RWKV = RWKV_HEADS * HEAD_DIM
NSA_Q_HEADS = 8
NSA_KV_HEADS = 2
NSA_GROUP = NSA_Q_HEADS // NSA_KV_HEADS
D_NSA = NSA_Q_HEADS * HEAD_DIM
D_KV = NSA_KV_HEADS * HEAD_DIM
LORA_W, LORA_A, LORA_G = 64, 64, 128
N_RWKV_COLS = 3 * D_RWKV + LORA_W + LORA_A + LORA_G
N_NSA_COLS = D_NSA + 6 * D_KV + 3 * NSA_Q_HEADS
CMP_BLOCK, CMP_STRIDE = 32, 16
SEL_BLOCK, SEL_TOPK = 64, 16
WINDOW = 512
Q_BLOCK = 128
NORM_EPS = 1e-6
GN_EPS = 64e-5
NEG = -1e30
BIG = 1e30

LANES = 128
RWKV_CHUNK = 64
RWKV_PREP_ROWS = 8 * RWKV_CHUNK
RWKV_SCAN_CHUNKS = 8
PREP_SPLIT = 2
KEY_TILE = 128
INPROJ_ROWS = 512
MAX_KEY_TILES = LANES * SEL_BLOCK // KEY_TILE
WIN_TILES = (WINDOW + Q_BLOCK) // KEY_TILE
SEL_FIRST = 11
SEL_GROUP = 2
SEL_Q_BLOCKS = 8
CMP_Q_BLOCKS = 8
CMP_ROW_CHUNK = 128
V_ROWS = HEAD_DIM + 16
SOFTMAX_FLOOR = -1e20
QG = NSA_GROUP * Q_BLOCK
VMEM_LIMIT = 56 * 1024 * 1024


def _bf16_terms(x, count):
    terms = []
    for _ in range(count):
        terms.append(float(np.asarray(x, ml_dtypes.bfloat16)))
        x = x - terms[-1]
    return tuple(terms)


LOG2E = float(np.log2(np.e))
LOG2E_TERMS = _bf16_terms(LOG2E, 3)


def _round_up(n, m):
    return -(-n // m) * m


def _dot(a, b):
    return jnp.dot(a, b, preferred_element_type=F32)


def _dot_tn(a, b):
    return lax.dot_general(a, b, (((0,), (0,)), ((), ())), preferred_element_type=F32)


def _split(a, terms):
    pieces = []
    for _ in range(terms - 1):
        pieces.append(a.astype(BF16))
        a = a - pieces[-1].astype(F32)
    return pieces + [a.astype(BF16)]


def _dot_exact_lhs(a, b, terms=3):
    return sum(_dot(a, piece) for piece in _split(b, terms))


def _dot_exact_rhs(a, b, terms=3):
    return sum(_dot(piece, b) for piece in _split(a, terms))


def _bdot(spec, a, b):
    return jnp.einsum(spec, a, b, preferred_element_type=F32)


def _rms(x, g):
    return x * lax.rsqrt(jnp.mean(x * x, axis=-1, keepdims=True) + NORM_EPS) * g


def _params(*sem):
    return pltpu.CompilerParams(dimension_semantics=sem, vmem_limit_bytes=VMEM_LIMIT)


def _inproj_kernel(x_ref, g_ref, wr_ref, wn_ref, qx_ref, pr_ref, qt_ref, ks_ref, kw_ref, vst_ref, vwt_ref,
                   kc_ref, vc_ref, glt_ref):
    xb = _rms(x_ref[...], g_ref[...]).astype(BF16)
    pr_ref[...] = _dot(xb, wr_ref[...])
    pn = _dot(xb, wn_ref[...])
    rows = pn.shape[0]
    dk, hk_n = HEAD_DIM, NSA_KV_HEADS
    group = lambda j: pn[:, D_NSA + j * D_KV:D_NSA + (j + 1) * D_KV]

    for half in range(rows // Q_BLOCK):
        q_tr = jnp.transpose(pn[half * Q_BLOCK:(half + 1) * Q_BLOCK, :D_NSA])
        for hk in range(hk_n):
            base = hk * NSA_GROUP * dk
            heads = jnp.concatenate([q_tr[base + g * dk:base + (g + 1) * dk, :] for g in range(NSA_GROUP)], axis=1)
            qt_ref[0, hk, half] = jnp.concatenate([heads * (dk ** -0.5 * LOG2E), qx_ref[hk]], axis=0).astype(BF16)

    pos = pl.program_id(1) * rows + lax.broadcasted_iota(jnp.int32, (rows, dk), 0)
    col = lax.broadcasted_iota(jnp.int32, (rows, dk), 1)
    pos_cols = _position_features(pos // SEL_BLOCK, pos % SEL_BLOCK, col)
    kt = KEY_TILE
    ones_rows = (lax.broadcasted_iota(jnp.int32, (V_ROWS - dk, kt), 0) == 0).astype(F32)
    kc, vc, ks, vs, kw, vw = (group(j) for j in range(6))
    vs_tr, vw_tr = jnp.transpose(vs), jnp.transpose(vw)
    gl_tr = jnp.transpose(pn[:, D_NSA + 6 * D_KV:])
    n_gate = 3 * NSA_GROUP
    for hk in range(hk_n):
        sl = slice(hk * dk, (hk + 1) * dk)
        kc_ref[0, hk] = kc[:, sl]
        vc_ref[0, hk] = vc[:, sl]
        ks_ref[0, hk] = jnp.concatenate([ks[:, sl], pos_cols], axis=1).astype(BF16)
        kw_ref[0, hk] = jnp.concatenate([kw[:, sl], pos_cols], axis=1).astype(BF16)
        for u in range(rows // kt):
            vst_ref[0, hk, u] = jnp.concatenate([vs_tr[sl, u * kt:(u + 1) * kt], ones_rows], axis=0).astype(BF16)
            vwt_ref[0, hk, u] = jnp.concatenate([vw_tr[sl, u * kt:(u + 1) * kt], ones_rows], axis=0).astype(BF16)
        glt_ref[0, hk] = gl_tr[hk * n_gate:(hk + 1) * n_gate, :]


def _inproj(x2, batch, g, w_r, w_n, q_extra):
    m, d = x2.shape
    t = m // batch
    tm = INPROJ_ROWS
    nt = t // tm
    nr, nn = w_r.shape[1], w_n.shape[1]
    hk, dk = NSA_KV_HEADS, HEAD_DIM
    const = lambda a: pl.BlockSpec(a.shape, lambda b, i: (0, 0))
    keys = pl.BlockSpec((1, hk, tm, 2 * dk), lambda b, i: (b, 0, i, 0))
    vals = pl.BlockSpec((1, hk, tm // KEY_TILE, V_ROWS, KEY_TILE), lambda b, i: (b, 0, i, 0, 0))
    cmp_in = pl.BlockSpec((1, hk, tm, dk), lambda b, i: (b, 0, i, 0))
    keys_shape = jax.ShapeDtypeStruct((batch, hk, t, 2 * dk), BF16)
    vals_shape = jax.ShapeDtypeStruct((batch, hk, t // KEY_TILE, V_ROWS, KEY_TILE), BF16)
    cmp_shape = jax.ShapeDtypeStruct((batch, hk, t, dk), F32)
    return pl.pallas_call(
        _inproj_kernel,
        grid=(batch, nt),
        in_specs=[pl.BlockSpec((tm, d), lambda b, i: (b * nt + i, 0)), const(g), const(w_r), const(w_n),
                  pl.BlockSpec(q_extra.shape, lambda b, i: (0, 0, 0))],
        out_specs=[pl.BlockSpec((tm, nr), lambda b, i: (b * nt + i, 0)),
                   pl.BlockSpec((1, hk, tm // Q_BLOCK, 2 * dk, QG), lambda b, i: (b, 0, i, 0, 0)),
                   keys, keys, vals, vals, cmp_in, cmp_in,
                   pl.BlockSpec((1, hk, 3 * NSA_GROUP, tm), lambda b, i: (b, 0, 0, i))],
        out_shape=[jax.ShapeDtypeStruct((m, nr), F32),
                   jax.ShapeDtypeStruct((batch, hk, t // Q_BLOCK, 2 * dk, QG), BF16),
                   keys_shape, keys_shape, vals_shape, vals_shape, cmp_shape, cmp_shape,
                   jax.ShapeDtypeStruct((batch, hk, 3 * NSA_GROUP, t), F32)],
        compiler_params=_params("arbitrary", "arbitrary"),
        name="inproj",
    )(x2, g, w_r, w_n, q_extra)


def _rwkv_operands(p, p_prev, mu, w0, w2, a0, a2, g2, k_k, k_a, r_k, bd, tri):
    c = RWKV_CHUNK
    ps = p + mu * (p_prev - p)
    d = D_RWKV
    r, k, v = ps[:, 0:d], ps[:, d:2 * d], ps[:, 2 * d:3 * d]
    dw = ps[:, 3 * d:3 * d + LORA_W]
    da = ps[:, 3 * d + LORA_W:3 * d + LORA_W + LORA_A]
    dg = ps[:, 3 * d + LORA_W + LORA_A:]

    z = -(w0 + _dot(jnp.tanh(dw).astype(BF16), w2))
    softplus = jnp.maximum(z, 0.0) + jnp.log1p(jnp.exp(-jnp.abs(z)))
    logw = -jnp.exp(-softplus - 0.5)
    a = jax.nn.sigmoid(a0 + _dot(da.astype(BF16), a2))
    g = _dot(jax.nn.sigmoid(dg).astype(BF16), g2)

    kk = k * k_k
    kk = kk / jnp.maximum(jnp.sqrt(_dot_exact_rhs(kk * kk, bd, PREP_SPLIT)), 1e-12)
    kp = k * (1.0 + (a - 1.0) * k_a)
    bonus = _dot_exact_rhs(r * kp * r_k, bd, PREP_SPLIT) * v

    n_chunks = p.shape[0] // c
    cum = jnp.concatenate([_dot_exact_lhs(tri, logw[ci * c:(ci + 1) * c], PREP_SPLIT) for ci in range(n_chunks)],
                          axis=0)
    e_pos, e_neg = jnp.exp(cum), jnp.exp(-cum)
    wc = jnp.concatenate([e_pos[(ci + 1) * c - 1:(ci + 1) * c, :] for ci in range(n_chunks)], axis=0)
    ops = (r * e_pos, kp * e_neg, kk * a * e_neg, -kk * jnp.exp(cum - logw), v)
    return tuple(o.astype(BF16) for o in ops), wc, bonus, g


def _rwkv_prep_kernel(p_ref, mu_ref, w0_ref, w2_ref, a0_ref, a2_ref, g2_ref, kk_ref, ka_ref, rk_ref, bd_ref, tri_ref,
                      ops_ref, aux_ref, wc_ref, carry_ref):
    rows = p_ref.shape[1]

    @pl.when(pl.program_id(1) == 0)
    def _():
        carry_ref[...] = jnp.zeros_like(carry_ref)

    p = p_ref[0]
    row = lax.broadcasted_iota(jnp.int32, p.shape, 0)
    p_prev = jnp.where(row == 0, carry_ref[7:8, :], pltpu.roll(p, 1, axis=0))
    carry_ref[...] = p[rows - 8:, :]
    ops, wc, bonus, g = _rwkv_operands(
        p, p_prev, mu_ref[...], w0_ref[...], w2_ref[...], a0_ref[...], a2_ref[...], g2_ref[...],
        kk_ref[...], ka_ref[...], rk_ref[...], bd_ref[...], tri_ref[...])
    for idx, o in enumerate(ops):
        ops_ref[idx, 0] = o
    aux_ref[0, 0] = bonus
    aux_ref[1, 0] = g
    for ci in range(wc.shape[0]):
        wc_ref[0, ci] = wc[ci:ci + 1, :]


def _rwkv_scan_kernel(ops_ref, aux_ref, wc_ref, lnw_ref, lnb_ref, y_ref, s_ref):
    @pl.when(pl.program_id(0) == 0)
    def _():
        s_ref[...] = jnp.zeros_like(s_ref)

    _rwkv_step(ops_ref, aux_ref, wc_ref, lnw_ref, lnb_ref, y_ref, s_ref)


def _rwkv_step(ops_ref, aux_ref, wc_ref, lnw_ref, lnb_ref, y_ref, s_ref):
    _, nb, span, d = ops_ref.shape
    c = RWKV_CHUNK
    nh, dk = RWKV_HEADS, HEAD_DIM

    row = lax.broadcasted_iota(jnp.int32, (1, c, c), 1)
    col = lax.broadcasted_iota(jnp.int32, (1, c, c), 2)
    strict = col < row
    incl = col <= row
    eye = (row == col).astype(F32)

    def heads(x):
        return jnp.stack([x[b, :, h * dk:(h + 1) * dk] for b in range(nb) for h in range(nh)], axis=0)

    def independent(j):
        rt, kt, bt, at, vb = (heads(ops_ref[idx, :, j * c:(j + 1) * c, :]) for idx in range(5))
        ar = jnp.concatenate([at, rt], axis=1)
        bk = jnp.concatenate([bt, kt], axis=1)
        amat = _bdot("nik,njk->nij", ar, bk)
        n_ab = jnp.where(strict, amat[:, :c, :c], 0.0)
        a_ak = jnp.where(strict, amat[:, :c, c:], 0.0).astype(BF16)
        a_rb = jnp.where(incl, amat[:, c:, :c], 0.0).astype(BF16)
        a_rk = jnp.where(incl, amat[:, c:, c:], 0.0).astype(BF16)
        inv = eye + n_ab
        pw = n_ab
        for _ in range(int(np.log2(c)) - 1):
            pwb = pw.astype(BF16)
            pw = _bdot("nij,njk->nik", pwb, pwb)
            inv = inv + _bdot("nij,njk->nik", pw.astype(BF16), inv.astype(BF16))
        return ar, bk, vb, _bdot("nij,njv->niv", a_ak, vb), a_rb, a_rk, inv.astype(BF16)

    parts = [independent(j) for j in range(span // c)]
    s = s_ref[...]
    for j, (ar, bk, vb, akv, a_rb, a_rk, inv) in enumerate(parts):
        ar_s = _bdot("nik,nvk->niv", ar, s.astype(BF16))
        ub = _bdot("nij,njv->niv", inv, (ar_s[:, :c] + akv).astype(BF16)).astype(BF16)
        y = ar_s[:, c:] + _bdot("nij,njv->niv", a_rb, ub) + _bdot("nij,njv->niv", a_rk, vb)
        uv_t = jnp.swapaxes(jnp.concatenate([ub, vb], axis=1), 1, 2)
        s = (s + _bdot("nvi,nik->nvk", uv_t, bk)) * heads(wc_ref[:, j])

        mean = jnp.mean(y, axis=-1, keepdims=True)
        var = jnp.mean(jnp.square(y - mean), axis=-1, keepdims=True)
        yn = (y - mean) * lax.rsqrt(var + GN_EPS)
        rsl = slice(j * c, (j + 1) * c)
        for b in range(nb):
            wide = jnp.concatenate([yn[b * nh + h] for h in range(nh)], axis=-1)
            y_ref[b, rsl, :] = ((wide * lnw_ref[...] + lnb_ref[...] + aux_ref[0, b, rsl, :])
                                * aux_ref[1, b, rsl, :]).astype(y_ref.dtype)
    s_ref[...] = s


def _rwkv(p_r, mu, w0, w2, a0, a2, g2, k_k, k_a, r_k, lnw, lnb):
    batch, t, width = p_r.shape
    c = RWKV_CHUNK
    rows = RWKV_PREP_ROWS
    d = D_RWKV
    head = np.arange(d) // HEAD_DIM
    bd = jnp.asarray(head[:, None] == head[None, :], BF16)
    tri = jnp.asarray(np.tril(np.ones((c, c))), BF16)
    const2 = lambda a: pl.BlockSpec(a.shape, lambda b, i: (0, 0))
    consts = (mu, w0, w2, a0, a2, g2, k_k, k_a, r_k, bd, tri)
    ops, aux, wc = pl.pallas_call(
        _rwkv_prep_kernel,
        grid=(batch, t // rows),
        in_specs=[pl.BlockSpec((1, rows, width), lambda b, i: (b, i, 0))] + [const2(a) for a in consts],
        out_specs=[pl.BlockSpec((5, 1, rows, d), lambda b, i: (0, b, i, 0)),
                   pl.BlockSpec((2, 1, rows, d), lambda b, i: (0, b, i, 0)),
                   pl.BlockSpec((1, rows // c, 1, d), lambda b, i: (b, i, 0, 0))],
        out_shape=[jax.ShapeDtypeStruct((5, batch, t, d), BF16),
                   jax.ShapeDtypeStruct((2, batch, t, d), F32),
                   jax.ShapeDtypeStruct((batch, t // c, 1, d), F32)],
        scratch_shapes=[pltpu.VMEM((8, width), F32)],
        compiler_params=_params("arbitrary", "arbitrary"),
        name="rwkv_prep",
    )(p_r, *consts)
    const1 = lambda a: pl.BlockSpec(a.shape, lambda i: (0, 0))
    span = RWKV_SCAN_CHUNKS * c
    return pl.pallas_call(
        _rwkv_scan_kernel,
        grid=(t // span,),
        in_specs=[pl.BlockSpec((5, batch, span, d), lambda i: (0, 0, i, 0)),
                  pl.BlockSpec((2, batch, span, d), lambda i: (0, 0, i, 0)),
                  pl.BlockSpec((batch, RWKV_SCAN_CHUNKS, 1, d), lambda i: (0, i, 0, 0)),
                  const1(lnw), const1(lnb)],
        out_specs=pl.BlockSpec((batch, span, d), lambda i: (0, i, 0)),
        out_shape=jax.ShapeDtypeStruct((batch, t, d), BF16),
        scratch_shapes=[pltpu.VMEM((batch * RWKV_HEADS, HEAD_DIM, HEAD_DIM), F32)],
        compiler_params=_params("arbitrary"),
        name="rwkv_scan",
    )(ops, aux, wc, lnw, lnb)


def _compress_kernel(zk_ref, zv_ref, pek_ref, pev_ref, k1_ref, k2_ref, v1_ref, v2_ref, ko_ref, vo_ref):
    def one(z_ref, pe_ref, w1_ref, w2_ref):
        n16 = z_ref.shape[2] // CMP_STRIDE
        dk = HEAD_DIM
        first = second = None
        for l in range(CMP_STRIDE):
            z = z_ref[0, 0, pl.ds(l, n16, stride=CMP_STRIDE), :]
            lo = _dot((z + pe_ref[l:l + 1, :]).astype(BF16), w1_ref[l * dk:(l + 1) * dk, :])
            u = CMP_STRIDE + l
            hi = _dot((z + pe_ref[u:u + 1, :]).astype(BF16), w1_ref[u * dk:(u + 1) * dk, :])
            first = lo if first is None else first + lo
            second = hi if second is None else second + hi
        hidden = first + pltpu.roll(second, n16 - 1, axis=0)
        return _dot(jax.nn.gelu(hidden).astype(BF16), w2_ref[...])

    kc = one(zk_ref, pek_ref, k1_ref, k2_ref)
    blk = lax.broadcasted_iota(jnp.int32, kc.shape, 0)
    col = lax.broadcasted_iota(jnp.int32, kc.shape, 1)
    per = SEL_BLOCK // CMP_STRIDE
    hi = blk // per
    lo = CMP_STRIDE * (blk % per) + (CMP_BLOCK - 1)
    feat = _position_features(hi, lo, col)
    ko_ref[0] = jnp.concatenate([kc, feat], axis=1).astype(BF16)
    vc = one(zv_ref, pev_ref, v1_ref, v2_ref)
    vc_t = jnp.transpose(jnp.concatenate([vc, jnp.zeros_like(vc)], axis=1))
    vo_ref[0] = vc_t[:HEAD_DIM].astype(BF16)


def _compress(zk, zv, pek, pev, k1, k2, v1, v2):
    batch, hk, t, dk = zk.shape
    n16 = t // CMP_STRIDE
    zspec = pl.BlockSpec((1, 1, t, dk), lambda b, h: (b, h, 0, 0))
    full = lambda a: pl.BlockSpec(a.shape, lambda b, h: (0,) * a.ndim)
    return pl.pallas_call(
        _compress_kernel,
        grid=(batch, hk),
        in_specs=[zspec, zspec, full(pek), full(pev), full(k1), full(k2), full(v1), full(v2)],
        out_specs=[pl.BlockSpec((1, n16, 2 * dk), lambda b, h: (b * hk + h, 0, 0)),
                   pl.BlockSpec((1, dk, n16), lambda b, h: (b * hk + h, 0, 0))],
        out_shape=[jax.ShapeDtypeStruct((batch * hk, n16, 2 * dk), BF16),
                   jax.ShapeDtypeStruct((batch * hk, dk, n16), BF16)],
        compiler_params=_params("arbitrary", "arbitrary"),
        name="nsa_compress",
    )(zk, zv, pek, pev, k1, k2, v1, v2)


def _alibi_query_rows(slopes):
    per_lane = jnp.repeat(slopes.reshape(NSA_KV_HEADS, NSA_GROUP), Q_BLOCK, axis=1)
    rows = []
    for term in LOG2E_TERMS:
        rows += [(SEL_BLOCK * term) * per_lane, term * per_lane]
    rows = jnp.stack(rows, axis=1)
    return jnp.pad(rows, ((0, 0), (0, HEAD_DIM - rows.shape[1]), (0, 0)))


def _position_features(hi, lo, col):
    return jnp.where(col < 2 * len(LOG2E_TERMS), jnp.where(col % 2 == 0, hi, lo), 0).astype(F32)


def _cmp_select_kernel(qt_ref, kc_ref, vct_ref, ovt_ref, oct_ref, selt_ref, act_ref, imp_ref):
    nblk = qt_ref.shape[2]
    width = nblk * Q_BLOCK
    ncp = kc_ref.shape[1]

    def attend(rows):
        cmp_end = lax.broadcasted_iota(jnp.int32, (rows, Q_BLOCK), 0) * CMP_STRIDE + (CMP_BLOCK - 1)
        for u in range(nblk):
            q0 = (pl.program_id(2) * nblk + u) * Q_BLOCK
            s = _dot(kc_ref[0, 0:rows, :], qt_ref[0, 0, u])
            ok = cmp_end <= q0 + lax.broadcasted_iota(jnp.int32, (rows, Q_BLOCK), 1)
            any_ok = (q0 + lax.broadcasted_iota(jnp.int32, (1, Q_BLOCK), 1) >= CMP_BLOCK - 1).astype(F32)
            p_sum = jnp.zeros((rows, Q_BLOCK), F32)
            probs = []
            for g in range(NSA_GROUP):
                sg = jnp.where(ok, s[:, g * Q_BLOCK:(g + 1) * Q_BLOCK], NEG)
                e = jnp.exp2(sg - jnp.max(sg, axis=0, keepdims=True))
                p = e * (any_ok / jnp.sum(e, axis=0, keepdims=True))
                p_sum = p_sum + p
                probs.append(p.astype(BF16))
            oct_ref[0, 0, u] = _dot(vct_ref[0, :, 0:rows], jnp.concatenate(probs, axis=1))
            imp_ref[:, u * Q_BLOCK:(u + 1) * Q_BLOCK] = _dot_exact_lhs(ovt_ref[:, 0:rows], p_sum)

    chunk = min(CMP_ROW_CHUNK, ncp)
    n_chunks = ncp // chunk
    last_t = (pl.program_id(2) + 1) * width - 1
    needed = jnp.maximum((last_t - (CMP_BLOCK - 1)) // CMP_STRIDE + 1, 1)
    needed_chunks = jnp.minimum((needed + chunk - 1) // chunk, n_chunks)
    for nck in range(1, n_chunks + 1):
        pl.when(needed_chunks == nck)(lambda nck=nck: attend(nck * chunk))

    imp = imp_ref[...]
    blk = lax.broadcasted_iota(jnp.int32, (LANES, width), 0)
    cur = (pl.program_id(2) * width + lax.broadcasted_iota(jnp.int32, (LANES, width), 1)) // SEL_BLOCK
    valid = blk <= cur
    forced = (blk == 0) | (blk == cur) | (blk == cur - 1)
    x = jnp.where(forced, -jnp.inf, jnp.where(valid, imp, NEG))
    blk_f = blk.astype(F32)
    for _ in range(SEL_TOPK - 3):
        m = jnp.max(x, axis=0, keepdims=True)
        first = jnp.min(jnp.where(x == m, blk_f, float(LANES)), axis=0, keepdims=True)
        x = jnp.where(blk_f == first, -jnp.inf, x)
    sel = ((x == -jnp.inf) & valid).astype(F32)
    ones = jnp.ones((8, Q_BLOCK), BF16)
    for u in range(nblk):
        sel_u = sel[:, u * Q_BLOCK:(u + 1) * Q_BLOCK]
        selt_ref[0, 0, u] = sel_u
        act_ref[0, 0, u] = lax.dot_general(ones, sel_u.astype(BF16), (((1,), (1,)), ((), ())),
                                           preferred_element_type=F32)


def _cmp_select(q_t, kc_aug, vc_t, ov_t):
    batch, hk, nq, dk2, qg = q_t.shape
    dk = dk2 // 2
    ncp = kc_aug.shape[1]
    nblk = CMP_Q_BLOCKS
    return pl.pallas_call(
        _cmp_select_kernel,
        grid=(batch, hk, nq // nblk),
        in_specs=[pl.BlockSpec((1, 1, nblk, dk2, qg), lambda b, h, i: (b, h, i, 0, 0)),
                  pl.BlockSpec((1, ncp, 2 * dk), lambda b, h, i: (b * hk + h, 0, 0)),
                  pl.BlockSpec((1, dk, ncp), lambda b, h, i: (b * hk + h, 0, 0)),
                  pl.BlockSpec(ov_t.shape, lambda b, h, i: (0, 0))],
        out_specs=[pl.BlockSpec((1, 1, nblk, dk, qg), lambda b, h, i: (b, h, i, 0, 0)),
                   pl.BlockSpec((1, 1, nblk, LANES, Q_BLOCK), lambda b, h, i: (b, h, i, 0, 0)),
                   pl.BlockSpec((1, 1, nblk, 8, LANES), lambda b, h, i: (b, h, i, 0, 0))],
        out_shape=[jax.ShapeDtypeStruct((batch, hk, nq, dk, qg), F32),
                   jax.ShapeDtypeStruct((batch, hk, nq, LANES, Q_BLOCK), F32),
                   jax.ShapeDtypeStruct((batch, hk, nq, 8, LANES), F32)],
        scratch_shapes=[pltpu.VMEM((LANES, nblk * Q_BLOCK), F32)],
        compiler_params=_params("arbitrary", "arbitrary", "arbitrary"),
        name="nsa_cmp_select",
    )(q_t, kc_aug, vc_t, ov_t)


def _sel_win_kernel(ids_ref, cnt_ref, qt_ref, selt_ref, ks_ref, vst_ref, kw_ref, vwt_ref,
                    oct_ref, glt_ref, yt_ref):
    b, hk, i = pl.program_id(0), pl.program_id(1), pl.program_id(2)
    nblk = qt_ref.shape[2]
    kt = KEY_TILE
    last_tile = ks_ref.shape[2] // kt - 1
    lane_minus_row = (lax.broadcasted_iota(jnp.int32, (kt, Q_BLOCK), 1)
                      - lax.broadcasted_iota(jnp.int32, (kt, Q_BLOCK), 0))
    start = (jnp.full((1, QG), SOFTMAX_FLOOR, F32), jnp.zeros((V_ROWS, QG), F32))
    result = lambda state: state[1][:HEAD_DIM] / state[1][HEAD_DIM:HEAD_DIM + 1]

    def block(u):
        blk = i * nblk + u
        step = (b * pl.num_programs(1) + hk) * (pl.num_programs(2) * nblk) + blk
        q0 = blk * Q_BLOCK
        q_aug = qt_ref[0, 0, u]
        count = cnt_ref[step]

        def all_scores(tiles):
            s_all = _dot(jnp.concatenate([k_tile for k_tile, _, _ in tiles], axis=0), q_aug)
            return [s_all[n * kt:(n + 1) * kt] for n in range(len(tiles))]

        def attend(state, tiles, scores=None):
            m_old, acc_old = state
            scores = all_scores(tiles) if scores is None else scores
            probs, maxes = [[] for _ in tiles], []
            for g in range(NSA_GROUP):
                gsl = slice(g * Q_BLOCK, (g + 1) * Q_BLOCK)
                masked = [jnp.where(mask, s[:, gsl], NEG) for s, (_, _, mask) in zip(scores, tiles)]
                top = jnp.max(masked[0].reshape(kt // 8, 8, Q_BLOCK), axis=0)
                for sg in masked[1:]:
                    top = jnp.maximum(top, jnp.max(sg.reshape(kt // 8, 8, Q_BLOCK), axis=0))
                mg = jnp.maximum(m_old[:, gsl], jnp.max(top, axis=0, keepdims=True))
                for n, sg in enumerate(masked):
                    probs[n].append(jnp.exp2(sg - mg).astype(BF16))
                maxes.append(mg)
            m_new = jnp.concatenate(maxes, axis=1)
            p_all = jnp.concatenate([jnp.concatenate(p, axis=1) for p in probs], axis=0)
            vt_all = jnp.concatenate([vt_tile for _, vt_tile, _ in tiles], axis=1)
            return m_new, jnp.exp2(m_old - m_new) * acc_old + _dot(vt_all, p_all)

        def sel_tile(n):
            j = jnp.minimum(ids_ref[step * MAX_KEY_TILES + jnp.minimum(n, MAX_KEY_TILES - 1)], last_tile)
            k0 = pl.multiple_of(j * kt, kt)
            per = kt // SEL_BLOCK
            picked = jnp.concatenate(
                [jnp.broadcast_to(selt_ref[0, 0, u, pl.ds(j * per + r, 1), :], (SEL_BLOCK, Q_BLOCK))
                 for r in range(per)], axis=0)
            causal_from = jnp.where(n < count, k0 - q0, 1 << 30)
            mask = (picked > 0.5) & (lane_minus_row >= causal_from)
            return ks_ref[0, 0, pl.ds(k0, kt), :], vst_ref[0, 0, j], mask

        last = (q0 + Q_BLOCK - 1) // kt

        def win_tile(r):
            j = last - r
            jc = jnp.maximum(j, 0)
            k0 = pl.multiple_of(jc * kt, kt)
            dist = lane_minus_row + jnp.where(j >= 0, q0 - k0, -(1 << 30))
            mask = (dist >= 0) & (dist < WINDOW)
            return kw_ref[0, 0, pl.ds(k0, kt), :], vwt_ref[0, 0, jc], mask

        win_tiles = [win_tile(r) for r in range(WIN_TILES)]
        sel_tiles = [sel_tile(n) for n in range(SEL_FIRST)]
        scores = all_scores(win_tiles + sel_tiles)
        state_w = attend(start, win_tiles, scores[:WIN_TILES])
        state_s = attend(start, sel_tiles, scores[WIN_TILES:])
        extra_steps = (jnp.maximum(count - SEL_FIRST, 0) + SEL_GROUP - 1) // SEL_GROUP
        more = lambda n, st: attend(st, [sel_tile(SEL_FIRST + SEL_GROUP * n + r) for r in range(SEL_GROUP)])
        return state_w, state_s, extra_steps, more

    blocks = [block(u) for u in range(nblk)]
    for u, (state_w, state_s, extra_steps, more) in enumerate(blocks):
        state_s = lax.fori_loop(0, extra_steps, more, state_s)
        o_s, o_w = result(state_s), result(state_w)
        lanes = slice(u * Q_BLOCK, (u + 1) * Q_BLOCK)
        gates = jax.nn.sigmoid(glt_ref[0, 0, :, lanes])
        o_c = oct_ref[0, 0, u]
        outs = []
        for g in range(NSA_GROUP):
            gsl = slice(g * Q_BLOCK, (g + 1) * Q_BLOCK)
            outs.append(gates[3 * g:3 * g + 1] * o_c[:, gsl] + gates[3 * g + 1:3 * g + 2] * o_s[:, gsl]
                        + gates[3 * g + 2:3 * g + 3] * o_w[:, gsl])
        yt_ref[0, :, lanes] = jnp.concatenate(outs, axis=0).astype(yt_ref.dtype)


def _sel_win(tile_ids, tile_cnt, q_t, sel_t, ks, vs_t, kw, vw_t, oc_t, gl_t):
    batch, hk, nq, dk, qg = oc_t.shape
    t = nq * Q_BLOCK
    nkt = t // KEY_TILE
    grp = qg // Q_BLOCK
    nblk = SEL_Q_BLOCKS
    qspec = pl.BlockSpec((1, 1, nblk, 2 * dk, qg), lambda b, h, i, *_: (b, h, i, 0, 0))
    ospec = pl.BlockSpec((1, 1, nblk, dk, qg), lambda b, h, i, *_: (b, h, i, 0, 0))
    kspec = pl.BlockSpec((1, 1, t, 2 * dk), lambda b, h, i, *_: (b, h, 0, 0))
    vspec = pl.BlockSpec((1, 1, nkt, V_ROWS, KEY_TILE), lambda b, h, i, *_: (b, h, 0, 0, 0))
    grid_spec = pltpu.PrefetchScalarGridSpec(
        num_scalar_prefetch=2,
        grid=(batch, hk, nq // nblk),
        in_specs=[qspec,
                  pl.BlockSpec((1, 1, nblk, LANES, Q_BLOCK), lambda b, h, i, *_: (b, h, i, 0, 0)),
                  kspec, vspec, kspec, vspec, ospec,
                  pl.BlockSpec((1, 1, 3 * grp, nblk * Q_BLOCK), lambda b, h, i, *_: (b, h, 0, i))],
        out_specs=pl.BlockSpec((1, grp * dk, nblk * Q_BLOCK), lambda b, h, i, *_: (b, h, i)),
    )
    return pl.pallas_call(
        _sel_win_kernel,
        grid_spec=grid_spec,
        out_shape=jax.ShapeDtypeStruct((batch, hk * grp * dk, t), BF16),
        compiler_params=_params("arbitrary", "arbitrary", "arbitrary"),
        name="nsa_sel_win",
    )(tile_ids, tile_cnt, q_t, sel_t, ks, vs_t, kw, vw_t, oc_t, gl_t)


def _ffn_kernel(x_ref, yr_ref, ynt_ref, wo1_ref, wo2_ref, g2_ref, wg_ref, wu_ref, wd_ref, gf_ref, o_ref):
    h1 = (x_ref[...] + _dot(yr_ref[...].astype(BF16), wo1_ref[...])
          + _dot_tn(ynt_ref[0].astype(BF16), wo2_ref[...]))
    hn = _rms(h1, g2_ref[...]).astype(BF16)
    gate = _dot(hn, wg_ref[...])
    up = _dot(hn, wu_ref[...])
    act = gate * jax.nn.sigmoid(gate) * up
    o_ref[...] = _rms(h1 + _dot(act.astype(BF16), wd_ref[...]), gf_ref[...])


def _ffn(x2, yr, yn_t, wo1, wo2, g2, wg, wu, wd, gf, tm=512):
    m, d = x2.shape
    _, dn, t = yn_t.shape
    per_seq = t // tm
    row = lambda n: pl.BlockSpec((tm, n), lambda i: (i, 0))
    const = lambda a: pl.BlockSpec(a.shape, lambda i: (0, 0), pipeline_mode=pl.Buffered(1))
    return pl.pallas_call(
        _ffn_kernel,
        grid=(m // tm,),
        in_specs=[row(d), row(yr.shape[1]),
                  pl.BlockSpec((1, dn, tm), lambda i: (i // per_seq, 0, i % per_seq)),
                  const(wo1), const(wo2), const(g2), const(wg), const(wu), const(wd), const(gf)],
        out_specs=row(d),
        out_shape=jax.ShapeDtypeStruct((m, d), F32),
        compiler_params=_params("arbitrary"),
        name="outproj_ffn",
    )(x2, yr, yn_t, wo1, wo2, g2, wg, wu, wd, gf)


def _overlap_matrix_t(t):
    n16 = t // CMP_STRIDE
    n_cmp = (t - CMP_BLOCK) // CMP_STRIDE + 1
    n_sel = t // SEL_BLOCK
    cmp_start = np.arange(n_cmp) * CMP_STRIDE
    sel_start = np.arange(n_sel) * SEL_BLOCK
    ov = np.clip(np.minimum(cmp_start[:, None] + CMP_BLOCK, sel_start[None, :] + SEL_BLOCK)
                 - np.maximum(cmp_start[:, None], sel_start[None, :]), 0, None) / CMP_STRIDE
    full = np.zeros((LANES, n16), np.float32)
    full[:n_sel, :n_cmp] = ov.T
    return jnp.asarray(full, BF16)


def kernel(x, norm1_g, w_in, mu_shift, rwkv_w0, rwkv_w2, rwkv_a0, rwkv_a2, rwkv_g2, rwkv_k_k, rwkv_k_a,
           rwkv_r_k, rwkv_lnx_w, rwkv_lnx_b, nsa_pe_k, nsa_pe_v, nsa_cmp_k_w1, nsa_cmp_k_w2, nsa_cmp_v_w1,
           nsa_cmp_v_w2, w_out, norm2_g, ffn_w_gate, ffn_w_up, ffn_w_down, norm_f_g):
    batch, t, d_model = x.shape
    assert w_in.shape[0] == 1, "the final RMSNorm is fused into the (single) layer's FFN kernel"
    assert t % INPROJ_ROWS == 0 and t % RWKV_PREP_ROWS == 0 and t // SEL_BLOCK <= LANES
    hk, grp, dk = NSA_KV_HEADS, NSA_GROUP, HEAD_DIM
    nq = t // Q_BLOCK
    slopes = 2.0 ** (-8.0 * jnp.arange(1, NSA_Q_HEADS + 1, dtype=F32) / NSA_Q_HEADS)
    nsa_pad = _round_up(N_NSA_COLS, LANES)
    row = lambda a: a.reshape(1, -1)
    i = 0

    h = x.reshape(batch * t, d_model)
    w_r = w_in[i][:, :N_RWKV_COLS].astype(BF16)
    w_n = jnp.pad(w_in[i][:, N_RWKV_COLS:], ((0, 0), (0, nsa_pad - N_NSA_COLS))).astype(BF16)
    p_r, q_t, ks, kw, vs_t, vw_t, zk, zv, gl_t = _inproj(h, batch, row(norm1_g[i]), w_r, w_n,
                                                         _alibi_query_rows(slopes))

    y_rwkv = _rwkv(p_r.reshape(batch, t, N_RWKV_COLS), row(mu_shift[i]), row(rwkv_w0[i]), rwkv_w2[i].astype(BF16),
                   row(rwkv_a0[i]), rwkv_a2[i].astype(BF16), rwkv_g2[i].astype(BF16), row(rwkv_k_k[i]),
                   row(rwkv_k_a[i]), row(rwkv_r_k[i]), row(rwkv_lnx_w[i]), row(rwkv_lnx_b[i]))
    y_rwkv = y_rwkv.reshape(batch * t, D_RWKV)

    kc_aug, vc_t = _compress(zk, zv, nsa_pe_k[i], nsa_pe_v[i],
                             nsa_cmp_k_w1[i].astype(BF16), nsa_cmp_k_w2[i].astype(BF16),
                             nsa_cmp_v_w1[i].astype(BF16), nsa_cmp_v_w2[i].astype(BF16))
    oc_t, sel_t, picked = _cmp_select(q_t, kc_aug, vc_t, _overlap_matrix_t(t))

    blocks_per_tile = KEY_TILE // SEL_BLOCK
    active = (picked[:, :, :, 0, :] > 0).reshape(batch, hk, nq, MAX_KEY_TILES, blocks_per_tile).any(axis=-1)
    tile_ids = jnp.argsort(jnp.logical_not(active), axis=-1, stable=True).astype(jnp.int32).reshape(-1)
    tile_cnt = active.sum(axis=-1).astype(jnp.int32).reshape(-1)
    y_nsa_t = _sel_win(tile_ids, tile_cnt, q_t, sel_t, ks, vs_t, kw, vw_t, oc_t, gl_t)

    out = _ffn(h, y_rwkv, y_nsa_t, w_out[i][:D_RWKV].astype(BF16), w_out[i][D_RWKV:].astype(BF16),
               row(norm2_g[i]), ffn_w_gate[i].astype(BF16), ffn_w_up[i].astype(BF16),
               ffn_w_down[i].astype(BF16), row(norm_f_g))
    return out.reshape(batch, t, d_model)
```

```python
import ml_dtypes
import numpy as np
import jax
import jax.numpy as jnp
from jax import lax
from jax.experimental import pallas as pl
from jax.experimental.pallas import tpu as pltpu

F32 = jnp.float32
BF16 = jnp.bfloat16

HEAD_DIM = 64
RWKV_HEADS = 8
D_RWKV = RWKV_HEADS * HEAD_DIM
NSA_Q_HEADS = 8
NSA_KV_HEADS = 2
NSA_GROUP = NSA_Q_HEADS // NSA_KV_HEADS
D_NSA = NSA_Q_HEADS * HEAD_DIM
D_KV = NSA_KV_HEADS * HEAD_DIM
LORA_W, LORA_A, LORA_G = 64, 64, 128
N_RWKV_COLS = 3 * D_RWKV + LORA_W + LORA_A + LORA_G
N_NSA_COLS = D_NSA + 6 * D_KV + 3 * NSA_Q_HEADS
CMP_BLOCK, CMP_STRIDE = 32, 16
SEL_BLOCK, SEL_TOPK = 64, 16
WINDOW = 512
Q_BLOCK = 128
NORM_EPS = 1e-6
GN_EPS = 64e-5
NEG = -1e30
BIG = 1e30

LANES = 128
RWKV_CHUNK = 64
RWKV_PREP_ROWS = 8 * RWKV_CHUNK
RWKV_SCAN_CHUNKS = 8
PREP_SPLIT = 2
KEY_TILE = 128
INPROJ_ROWS = 512
MAX_KEY_TILES = LANES * SEL_BLOCK // KEY_TILE
WIN_TILES = (WINDOW + Q_BLOCK) // KEY_TILE
SEL_FIRST = 11
SEL_GROUP = 2
SEL_Q_BLOCKS = 8
CMP_Q_BLOCKS = 8
CMP_ROW_CHUNK = 128
V_ROWS = HEAD_DIM + 16
SOFTMAX_FLOOR = -1e20
QG = NSA_GROUP * Q_BLOCK
VMEM_LIMIT = 56 * 1024 * 1024


def _bf16_terms(x, count):
    terms = []
    for _ in range(count):
        terms.append(float(np.asarray(x, ml_dtypes.bfloat16)))
        x = x - terms[-1]
    return tuple(terms)


LOG2E = float(np.log2(np.e))
LOG2E_TERMS = _bf16_terms(LOG2E, 3)
DECAY_LOG2_SCALE = float(np.exp(-0.5)) * LOG2E


def _round_up(n, m):
    return -(-n // m) * m


def _dot(a, b):
    return jnp.dot(a, b, preferred_element_type=F32)


def _dot_tn(a, b):
    return lax.dot_general(a, b, (((0,), (0,)), ((), ())), preferred_element_type=F32)


def _split(a, terms):
    pieces = []
    for _ in range(terms - 1):
        pieces.append(a.astype(BF16))
        a = a - pieces[-1].astype(F32)
    return pieces + [a.astype(BF16)]


def _dot_exact_lhs(a, b, terms=3):
    return sum(_dot(a, piece) for piece in _split(b, terms))


def _dot_exact_rhs(a, b, terms=3):
    return sum(_dot(piece, b) for piece in _split(a, terms))


def _bdot(spec, a, b):
    return jnp.einsum(spec, a, b, preferred_element_type=F32)


def _rms(x, g):
    return x * lax.rsqrt(jnp.mean(x * x, axis=-1, keepdims=True) + NORM_EPS) * g


def _params(*sem):
    return pltpu.CompilerParams(dimension_semantics=sem, vmem_limit_bytes=VMEM_LIMIT)


def _inproj_kernel(x_ref, g_ref, wr_ref, wn_ref, qx_ref, pr_ref, qt_ref, ks_ref, kw_ref, vst_ref, vwt_ref,
                   kc_ref, vc_ref, glt_ref):
    xb = _rms(x_ref[...], g_ref[...]).astype(BF16)
    pr_ref[...] = _dot(xb, wr_ref[...])
    pn = _dot(xb, wn_ref[...])
    rows = pn.shape[0]
    dk, hk_n = HEAD_DIM, NSA_KV_HEADS
    group = lambda j: pn[:, D_NSA + j * D_KV:D_NSA + (j + 1) * D_KV]

    for half in range(rows // Q_BLOCK):
        q_tr = jnp.transpose(pn[half * Q_BLOCK:(half + 1) * Q_BLOCK, :D_NSA])
        for hk in range(hk_n):
            base = hk * NSA_GROUP * dk
            heads = jnp.concatenate([q_tr[base + g * dk:base + (g + 1) * dk, :] for g in range(NSA_GROUP)], axis=1)
            qt_ref[0, hk, half] = jnp.concatenate([heads * (dk ** -0.5 * LOG2E), qx_ref[hk]], axis=0).astype(BF16)

    pos = pl.program_id(1) * rows + lax.broadcasted_iota(jnp.int32, (rows, dk), 0)
    col = lax.broadcasted_iota(jnp.int32, (rows, dk), 1)
    pos_cols = _position_features(pos // SEL_BLOCK, pos % SEL_BLOCK, col)
    kt = KEY_TILE
    ones_rows = (lax.broadcasted_iota(jnp.int32, (V_ROWS - dk, kt), 0) == 0).astype(F32)
    kc, vc, ks, vs, kw, vw = (group(j) for j in range(6))
    vs_tr, vw_tr = jnp.transpose(vs), jnp.transpose(vw)
    gl_tr = jnp.transpose(pn[:, D_NSA + 6 * D_KV:])
    n_gate = 3 * NSA_GROUP
    for hk in range(hk_n):
        sl = slice(hk * dk, (hk + 1) * dk)
        kc_ref[0, hk] = kc[:, sl]
        vc_ref[0, hk] = vc[:, sl]
        ks_ref[0, hk] = jnp.concatenate([ks[:, sl], pos_cols], axis=1).astype(BF16)
        kw_ref[0, hk] = jnp.concatenate([kw[:, sl], pos_cols], axis=1).astype(BF16)
        for u in range(rows // kt):
            vst_ref[0, hk, u] = jnp.concatenate([vs_tr[sl, u * kt:(u + 1) * kt], ones_rows], axis=0).astype(BF16)
            vwt_ref[0, hk, u] = jnp.concatenate([vw_tr[sl, u * kt:(u + 1) * kt], ones_rows], axis=0).astype(BF16)
        glt_ref[0, hk] = gl_tr[hk * n_gate:(hk + 1) * n_gate, :]


def _inproj(x2, batch, g, w_r, w_n, q_extra):
    m, d = x2.shape
    t = m // batch
    tm = INPROJ_ROWS
    nt = t // tm
    nr, nn = w_r.shape[1], w_n.shape[1]
    hk, dk = NSA_KV_HEADS, HEAD_DIM
    const = lambda a: pl.BlockSpec(a.shape, lambda b, i: (0, 0))
    keys = pl.BlockSpec((1, hk, tm, 2 * dk), lambda b, i: (b, 0, i, 0))
    vals = pl.BlockSpec((1, hk, tm // KEY_TILE, V_ROWS, KEY_TILE), lambda b, i: (b, 0, i, 0, 0))
    cmp_in = pl.BlockSpec((1, hk, tm, dk), lambda b, i: (b, 0, i, 0))
    keys_shape = jax.ShapeDtypeStruct((batch, hk, t, 2 * dk), BF16)
    vals_shape = jax.ShapeDtypeStruct((batch, hk, t // KEY_TILE, V_ROWS, KEY_TILE), BF16)
    cmp_shape = jax.ShapeDtypeStruct((batch, hk, t, dk), F32)
    return pl.pallas_call(
        _inproj_kernel,
        grid=(batch, nt),
        in_specs=[pl.BlockSpec((tm, d), lambda b, i: (b * nt + i, 0)), const(g), const(w_r), const(w_n),
                  pl.BlockSpec(q_extra.shape, lambda b, i: (0, 0, 0))],
        out_specs=[pl.BlockSpec((tm, nr), lambda b, i: (b * nt + i, 0)),
                   pl.BlockSpec((1, hk, tm // Q_BLOCK, 2 * dk, QG), lambda b, i: (b, 0, i, 0, 0)),
                   keys, keys, vals, vals, cmp_in, cmp_in,
                   pl.BlockSpec((1, hk, 3 * NSA_GROUP, tm), lambda b, i: (b, 0, 0, i))],
        out_shape=[jax.ShapeDtypeStruct((m, nr), F32),
                   jax.ShapeDtypeStruct((batch, hk, t // Q_BLOCK, 2 * dk, QG), BF16),
                   keys_shape, keys_shape, vals_shape, vals_shape, cmp_shape, cmp_shape,
                   jax.ShapeDtypeStruct((batch, hk, 3 * NSA_GROUP, t), F32)],
        compiler_params=_params("arbitrary", "arbitrary"),
        name="inproj",
    )(x2, g, w_r, w_n, q_extra)


def _rwkv_operands(p, p_prev, mu, w0, w2, a0, a2, g2, k_k, k_a, r_k, bd, tri):
    c = RWKV_CHUNK
    ps = p + mu * (p_prev - p)
    d = D_RWKV
    r, k, v = ps[:, 0:d], ps[:, d:2 * d], ps[:, 2 * d:3 * d]
    dw = ps[:, 3 * d:3 * d + LORA_W]
    da = ps[:, 3 * d + LORA_W:3 * d + LORA_W + LORA_A]
    dg = ps[:, 3 * d + LORA_W + LORA_A:]

    logw = -DECAY_LOG2_SCALE * jax.nn.sigmoid(w0 + _dot(jnp.tanh(dw).astype(BF16), w2))
    a = jax.nn.sigmoid(a0 + _dot(da.astype(BF16), a2))
    g = _dot(jax.nn.sigmoid(dg).astype(BF16), g2)

    kk = k * k_k
    kk = kk / jnp.maximum(jnp.sqrt(_dot_exact_rhs(kk * kk, bd, PREP_SPLIT)), 1e-12)
    kp = k * (1.0 + (a - 1.0) * k_a)
    bonus = _dot_exact_rhs(r * kp * r_k, bd, PREP_SPLIT) * v

    n_chunks = p.shape[0] // c
    cum = jnp.concatenate([_dot_exact_lhs(tri, logw[ci * c:(ci + 1) * c], PREP_SPLIT) for ci in range(n_chunks)],
                          axis=0)
    e_pos, e_neg = jnp.exp2(cum), jnp.exp2(-cum)
    wc = jnp.concatenate([e_pos[(ci + 1) * c - 1:(ci + 1) * c, :] for ci in range(n_chunks)], axis=0)
    ops = (r * e_pos, kp * e_neg, kk * a * e_neg, -kk * jnp.exp2(cum - logw), v)
    return tuple(o.astype(BF16) for o in ops), wc, bonus, g


def _rwkv_prep_kernel(p_ref, mu_ref, w0_ref, w2_ref, a0_ref, a2_ref, g2_ref, kk_ref, ka_ref, rk_ref, bd_ref, tri_ref,
                      ops_ref, aux_ref, wc_ref, carry_ref):
    rows = p_ref.shape[1]

    @pl.when(pl.program_id(1) == 0)
    def _():
        carry_ref[...] = jnp.zeros_like(carry_ref)

    p = p_ref[0]
    row = lax.broadcasted_iota(jnp.int32, p.shape, 0)
    p_prev = jnp.where(row == 0, carry_ref[7:8, :], pltpu.roll(p, 1, axis=0))
    carry_ref[...] = p[rows - 8:, :]
    ops, wc, bonus, g = _rwkv_operands(
        p, p_prev, mu_ref[...], w0_ref[...], w2_ref[...], a0_ref[...], a2_ref[...], g2_ref[...],
        kk_ref[...], ka_ref[...], rk_ref[...], bd_ref[...], tri_ref[...])
    for idx, o in enumerate(ops):
        ops_ref[idx, 0] = o
    aux_ref[0, 0] = bonus
    aux_ref[1, 0] = g
    for ci in range(wc.shape[0]):
        wc_ref[0, ci] = wc[ci:ci + 1, :]


def _rwkv_scan_kernel(ops_ref, aux_ref, wc_ref, lnw_ref, lnb_ref, y_ref, s_ref):
    @pl.when(pl.program_id(0) == 0)
    def _():
        s_ref[...] = jnp.zeros_like(s_ref)

    _rwkv_step(ops_ref, aux_ref, wc_ref, lnw_ref, lnb_ref, y_ref, s_ref)


def _rwkv_step(ops_ref, aux_ref, wc_ref, lnw_ref, lnb_ref, y_ref, s_ref):
    _, nb, span, d = ops_ref.shape
    c = RWKV_CHUNK
    nh, dk = RWKV_HEADS, HEAD_DIM

    row = lax.broadcasted_iota(jnp.int32, (1, c, c), 1)
    col = lax.broadcasted_iota(jnp.int32, (1, c, c), 2)
    strict = col < row
    incl = col <= row
    eye = (row == col).astype(F32)

    def heads(x):
        return jnp.stack([x[b, :, h * dk:(h + 1) * dk] for b in range(nb) for h in range(nh)], axis=0)

    def independent(j):
        rt, kt, bt, at, vb = (heads(ops_ref[idx, :, j * c:(j + 1) * c, :]) for idx in range(5))
        ar = jnp.concatenate([at, rt], axis=1)
        bk = jnp.concatenate([bt, kt], axis=1)
        amat = _bdot("nik,njk->nij", ar, bk)
        n_ab = jnp.where(strict, amat[:, :c, :c], 0.0)
        a_ak = jnp.where(strict, amat[:, :c, c:], 0.0).astype(BF16)
        a_rb = jnp.where(incl, amat[:, c:, :c], 0.0).astype(BF16)
        a_rk = jnp.where(incl, amat[:, c:, c:], 0.0).astype(BF16)
        inv = eye + n_ab
        pw = n_ab
        for _ in range(int(np.log2(c)) - 1):
            pwb = pw.astype(BF16)
            pw = _bdot("nij,njk->nik", pwb, pwb)
            inv = inv + _bdot("nij,njk->nik", pw.astype(BF16), inv.astype(BF16))
        return ar, bk, vb, _bdot("nij,njv->niv", a_ak, vb), a_rb, a_rk, inv.astype(BF16)

    parts = [independent(j) for j in range(span // c)]
    s = s_ref[...]
    for j, (ar, bk, vb, akv, a_rb, a_rk, inv) in enumerate(parts):
        ar_s = _bdot("nik,nvk->niv", ar, s.astype(BF16))
        ub = _bdot("nij,njv->niv", inv, (ar_s[:, :c] + akv).astype(BF16)).astype(BF16)
        y = ar_s[:, c:] + _bdot("nij,njv->niv", a_rb, ub) + _bdot("nij,njv->niv", a_rk, vb)
        uv_t = jnp.swapaxes(jnp.concatenate([ub, vb], axis=1), 1, 2)
        s = (s + _bdot("nvi,nik->nvk", uv_t, bk)) * heads(wc_ref[:, j])

        mean = jnp.mean(y, axis=-1, keepdims=True)
        var = jnp.mean(jnp.square(y - mean), axis=-1, keepdims=True)
        yn = (y - mean) * lax.rsqrt(var + GN_EPS)
        rsl = slice(j * c, (j + 1) * c)
        for b in range(nb):
            wide = jnp.concatenate([yn[b * nh + h] for h in range(nh)], axis=-1)
            y_ref[b, rsl, :] = ((wide * lnw_ref[...] + lnb_ref[...] + aux_ref[0, b, rsl, :])
                                * aux_ref[1, b, rsl, :]).astype(y_ref.dtype)
    s_ref[...] = s


def _rwkv(p_r, mu, w0, w2, a0, a2, g2, k_k, k_a, r_k, lnw, lnb):
    batch, t, width = p_r.shape
    c = RWKV_CHUNK
    rows = RWKV_PREP_ROWS
    d = D_RWKV
    head = np.arange(d) // HEAD_DIM
    bd = jnp.asarray(head[:, None] == head[None, :], BF16)
    tri = jnp.asarray(np.tril(np.ones((c, c))), BF16)
    const2 = lambda a: pl.BlockSpec(a.shape, lambda b, i: (0, 0))
    consts = (mu, w0, w2, a0, a2, g2, k_k, k_a, r_k, bd, tri)
    ops, aux, wc = pl.pallas_call(
        _rwkv_prep_kernel,
        grid=(batch, t // rows),
        in_specs=[pl.BlockSpec((1, rows, width), lambda b, i: (b, i, 0))] + [const2(a) for a in consts],
        out_specs=[pl.BlockSpec((5, 1, rows, d), lambda b, i: (0, b, i, 0)),
                   pl.BlockSpec((2, 1, rows, d), lambda b, i: (0, b, i, 0)),
                   pl.BlockSpec((1, rows // c, 1, d), lambda b, i: (b, i, 0, 0))],
        out_shape=[jax.ShapeDtypeStruct((5, batch, t, d), BF16),
                   jax.ShapeDtypeStruct((2, batch, t, d), F32),
                   jax.ShapeDtypeStruct((batch, t // c, 1, d), F32)],
        scratch_shapes=[pltpu.VMEM((8, width), F32)],
        compiler_params=_params("arbitrary", "arbitrary"),
        name="rwkv_prep",
    )(p_r, *consts)
    const1 = lambda a: pl.BlockSpec(a.shape, lambda i: (0, 0))
    span = RWKV_SCAN_CHUNKS * c
    return pl.pallas_call(
        _rwkv_scan_kernel,
        grid=(t // span,),
        in_specs=[pl.BlockSpec((5, batch, span, d), lambda i: (0, 0, i, 0)),
                  pl.BlockSpec((2, batch, span, d), lambda i: (0, 0, i, 0)),
                  pl.BlockSpec((batch, RWKV_SCAN_CHUNKS, 1, d), lambda i: (0, i, 0, 0)),
                  const1(lnw), const1(lnb)],
        out_specs=pl.BlockSpec((batch, span, d), lambda i: (0, i, 0)),
        out_shape=jax.ShapeDtypeStruct((batch, t, d), BF16),
        scratch_shapes=[pltpu.VMEM((batch * RWKV_HEADS, HEAD_DIM, HEAD_DIM), F32)],
        compiler_params=_params("arbitrary"),
        name="rwkv_scan",
    )(ops, aux, wc, lnw, lnb)


def _compress_kernel(zk_ref, zv_ref, pek_ref, pev_ref, k1_ref, k2_ref, v1_ref, v2_ref, ko_ref, vo_ref):
    def one(z_ref, pe_ref, w1_ref, w2_ref):
        n16 = z_ref.shape[2] // CMP_STRIDE
        dk = HEAD_DIM
        first = second = None
        for l in range(CMP_STRIDE):
            z = z_ref[0, 0, pl.ds(l, n16, stride=CMP_STRIDE), :]
            lo = _dot((z + pe_ref[l:l + 1, :]).astype(BF16), w1_ref[l * dk:(l + 1) * dk, :])
            u = CMP_STRIDE + l
            hi = _dot((z + pe_ref[u:u + 1, :]).astype(BF16), w1_ref[u * dk:(u + 1) * dk, :])
            first = lo if first is None else first + lo
            second = hi if second is None else second + hi
        hidden = first + pltpu.roll(second, n16 - 1, axis=0)
        return _dot(jax.nn.gelu(hidden).astype(BF16), w2_ref[...])

    kc = one(zk_ref, pek_ref, k1_ref, k2_ref)
    blk = lax.broadcasted_iota(jnp.int32, kc.shape, 0)
    col = lax.broadcasted_iota(jnp.int32, kc.shape, 1)
    per = SEL_BLOCK // CMP_STRIDE
    hi = blk // per
    lo = CMP_STRIDE * (blk % per) + (CMP_BLOCK - 1)
    feat = _position_features(hi, lo, col)
    ko_ref[0] = jnp.concatenate([kc, feat], axis=1).astype(BF16)
    vc = one(zv_ref, pev_ref, v1_ref, v2_ref)
    vc_t = jnp.transpose(jnp.concatenate([vc, jnp.zeros_like(vc)], axis=1))
    vo_ref[0] = vc_t[:HEAD_DIM].astype(BF16)


def _compress(zk, zv, pek, pev, k1, k2, v1, v2):
    batch, hk, t, dk = zk.shape
    n16 = t // CMP_STRIDE
    zspec = pl.BlockSpec((1, 1, t, dk), lambda b, h: (b, h, 0, 0))
    full = lambda a: pl.BlockSpec(a.shape, lambda b, h: (0,) * a.ndim)
    return pl.pallas_call(
        _compress_kernel,
        grid=(batch, hk),
        in_specs=[zspec, zspec, full(pek), full(pev), full(k1), full(k2), full(v1), full(v2)],
        out_specs=[pl.BlockSpec((1, n16, 2 * dk), lambda b, h: (b * hk + h, 0, 0)),
                   pl.BlockSpec((1, dk, n16), lambda b, h: (b * hk + h, 0, 0))],
        out_shape=[jax.ShapeDtypeStruct((batch * hk, n16, 2 * dk), BF16),
                   jax.ShapeDtypeStruct((batch * hk, dk, n16), BF16)],
        compiler_params=_params("arbitrary", "arbitrary"),
        name="nsa_compress",
    )(zk, zv, pek, pev, k1, k2, v1, v2)


def _alibi_query_rows(slopes):
    per_lane = jnp.repeat(slopes.reshape(NSA_KV_HEADS, NSA_GROUP), Q_BLOCK, axis=1)
    rows = []
    for term in LOG2E_TERMS:
        rows += [(SEL_BLOCK * term) * per_lane, term * per_lane]
    rows = jnp.stack(rows, axis=1)
    return jnp.pad(rows, ((0, 0), (0, HEAD_DIM - rows.shape[1]), (0, 0)))


def _position_features(hi, lo, col):
    return jnp.where(col < 2 * len(LOG2E_TERMS), jnp.where(col % 2 == 0, hi, lo), 0).astype(F32)


def _cmp_select_kernel(qt_ref, kc_ref, vct_ref, ovt_ref, oct_ref, selt_ref, act_ref, imp_ref):
    nblk = qt_ref.shape[2]
    width = nblk * Q_BLOCK
    ncp = kc_ref.shape[1]

    def attend(rows):
        cmp_end = lax.broadcasted_iota(jnp.int32, (rows, Q_BLOCK), 0) * CMP_STRIDE + (CMP_BLOCK - 1)
        for u in range(nblk):
            q0 = (pl.program_id(2) * nblk + u) * Q_BLOCK
            s = _dot(kc_ref[0, 0:rows, :], qt_ref[0, 0, u])
            ok = cmp_end <= q0 + lax.broadcasted_iota(jnp.int32, (rows, Q_BLOCK), 1)
            any_ok = (q0 + lax.broadcasted_iota(jnp.int32, (1, Q_BLOCK), 1) >= CMP_BLOCK - 1).astype(F32)
            p_sum = jnp.zeros((rows, Q_BLOCK), F32)
            probs = []
            for g in range(NSA_GROUP):
                sg = jnp.where(ok, s[:, g * Q_BLOCK:(g + 1) * Q_BLOCK], NEG)
                e = jnp.exp2(sg - jnp.max(sg, axis=0, keepdims=True))
                p = e * (any_ok / jnp.sum(e, axis=0, keepdims=True))
                p_sum = p_sum + p
                probs.append(p.astype(BF16))
            oct_ref[0, 0, u] = _dot(vct_ref[0, :, 0:rows], jnp.concatenate(probs, axis=1))
            imp_ref[:, u * Q_BLOCK:(u + 1) * Q_BLOCK] = _dot_exact_lhs(ovt_ref[:, 0:rows], p_sum)

    chunk = min(CMP_ROW_CHUNK, ncp)
    n_chunks = ncp // chunk
    last_t = (pl.program_id(2) + 1) * width - 1
    needed = jnp.maximum((last_t - (CMP_BLOCK - 1)) // CMP_STRIDE + 1, 1)
    needed_chunks = jnp.minimum((needed + chunk - 1) // chunk, n_chunks)
    for nck in range(1, n_chunks + 1):
        pl.when(needed_chunks == nck)(lambda nck=nck: attend(nck * chunk))

    imp = imp_ref[...]
    blk = lax.broadcasted_iota(jnp.int32, (LANES, width), 0)
    cur = (pl.program_id(2) * width + lax.broadcasted_iota(jnp.int32, (LANES, width), 1)) // SEL_BLOCK
    valid = blk <= cur
    forced = (blk == 0) | (blk == cur) | (blk == cur - 1)
    x = jnp.where(forced, -jnp.inf, jnp.where(valid, imp, NEG))
    blk_f = blk.astype(F32)
    for _ in range(SEL_TOPK - 3):
        m = jnp.max(x, axis=0, keepdims=True)
        first = jnp.min(jnp.where(x == m, blk_f, float(LANES)), axis=0, keepdims=True)
        x = jnp.where(blk_f == first, -jnp.inf, x)
    sel = ((x == -jnp.inf) & valid).astype(F32)
    ones = jnp.ones((8, Q_BLOCK), BF16)
    for u in range(nblk):
        sel_u = sel[:, u * Q_BLOCK:(u + 1) * Q_BLOCK]
        selt_ref[0, 0, u] = sel_u
        act_ref[0, 0, u] = lax.dot_general(ones, sel_u.astype(BF16), (((1,), (1,)), ((), ())),
                                           preferred_element_type=F32)


def _cmp_select(q_t, kc_aug, vc_t, ov_t):
    batch, hk, nq, dk2, qg = q_t.shape
    dk = dk2 // 2
    ncp = kc_aug.shape[1]
    nblk = CMP_Q_BLOCKS
    return pl.pallas_call(
        _cmp_select_kernel,
        grid=(batch, hk, nq // nblk),
        in_specs=[pl.BlockSpec((1, 1, nblk, dk2, qg), lambda b, h, i: (b, h, i, 0, 0)),
                  pl.BlockSpec((1, ncp, 2 * dk), lambda b, h, i: (b * hk + h, 0, 0)),
                  pl.BlockSpec((1, dk, ncp), lambda b, h, i: (b * hk + h, 0, 0)),
                  pl.BlockSpec(ov_t.shape, lambda b, h, i: (0, 0))],
        out_specs=[pl.BlockSpec((1, 1, nblk, dk, qg), lambda b, h, i: (b, h, i, 0, 0)),
                   pl.BlockSpec((1, 1, nblk, LANES, Q_BLOCK), lambda b, h, i: (b, h, i, 0, 0)),
                   pl.BlockSpec((1, 1, nblk, 8, LANES), lambda b, h, i: (b, h, i, 0, 0))],
        out_shape=[jax.ShapeDtypeStruct((batch, hk, nq, dk, qg), F32),
                   jax.ShapeDtypeStruct((batch, hk, nq, LANES, Q_BLOCK), F32),
                   jax.ShapeDtypeStruct((batch, hk, nq, 8, LANES), F32)],
        scratch_shapes=[pltpu.VMEM((LANES, nblk * Q_BLOCK), F32)],
        compiler_params=_params("arbitrary", "arbitrary", "arbitrary"),
        name="nsa_cmp_select",
    )(q_t, kc_aug, vc_t, ov_t)


def _sel_win_kernel(ids_ref, cnt_ref, qt_ref, selt_ref, ks_ref, vst_ref, kw_ref, vwt_ref,
                    oct_ref, glt_ref, yt_ref):
    b, hk, i = pl.program_id(0), pl.program_id(1), pl.program_id(2)
    nblk = qt_ref.shape[2]
    kt = KEY_TILE
    last_tile = ks_ref.shape[2] // kt - 1
    lane_minus_row = (lax.broadcasted_iota(jnp.int32, (kt, Q_BLOCK), 1)
                      - lax.broadcasted_iota(jnp.int32, (kt, Q_BLOCK), 0))
    start = (jnp.full((1, QG), SOFTMAX_FLOOR, F32), jnp.zeros((V_ROWS, QG), F32))
    result = lambda state: state[1][:HEAD_DIM] / state[1][HEAD_DIM:HEAD_DIM + 1]

    def block(u):
        blk = i * nblk + u
        step = (b * pl.num_programs(1) + hk) * (pl.num_programs(2) * nblk) + blk
        q0 = blk * Q_BLOCK
        q_aug = qt_ref[0, 0, u]
        count = cnt_ref[step]

        def all_scores(tiles):
            s_all = _dot(jnp.concatenate([k_tile for k_tile, _, _ in tiles], axis=0), q_aug)
            return [s_all[n * kt:(n + 1) * kt] for n in range(len(tiles))]

        def attend(state, tiles, scores=None):
            m_old, acc_old = state
            scores = all_scores(tiles) if scores is None else scores
            probs, maxes = [[] for _ in tiles], []
            for g in range(NSA_GROUP):
                gsl = slice(g * Q_BLOCK, (g + 1) * Q_BLOCK)
                masked = [jnp.where(mask, s[:, gsl], NEG) for s, (_, _, mask) in zip(scores, tiles)]
                top = jnp.max(masked[0].reshape(kt // 8, 8, Q_BLOCK), axis=0)
                for sg in masked[1:]:
                    top = jnp.maximum(top, jnp.max(sg.reshape(kt // 8, 8, Q_BLOCK), axis=0))
                mg = jnp.maximum(m_old[:, gsl], jnp.max(top, axis=0, keepdims=True))
                for n, sg in enumerate(masked):
                    probs[n].append(jnp.exp2(sg - mg).astype(BF16))
                maxes.append(mg)
            m_new = jnp.concatenate(maxes, axis=1)
            p_all = jnp.concatenate([jnp.concatenate(p, axis=1) for p in probs], axis=0)
            vt_all = jnp.concatenate([vt_tile for _, vt_tile, _ in tiles], axis=1)
            return m_new, jnp.exp2(m_old - m_new) * acc_old + _dot(vt_all, p_all)

        def sel_tile(n):
            j = jnp.minimum(ids_ref[step * MAX_KEY_TILES + jnp.minimum(n, MAX_KEY_TILES - 1)], last_tile)
            k0 = pl.multiple_of(j * kt, kt)
            per = kt // SEL_BLOCK
            picked = jnp.concatenate(
                [jnp.broadcast_to(selt_ref[0, 0, u, pl.ds(j * per + r, 1), :], (SEL_BLOCK, Q_BLOCK))
                 for r in range(per)], axis=0)
            causal_from = jnp.where(n < count, k0 - q0, 1 << 30)
            mask = (picked > 0.5) & (lane_minus_row >= causal_from)
            return ks_ref[0, 0, pl.ds(k0, kt), :], vst_ref[0, 0, j], mask

        last = (q0 + Q_BLOCK - 1) // kt

        def win_tile(r):
            j = last - r
            jc = jnp.maximum(j, 0)
            k0 = pl.multiple_of(jc * kt, kt)
            dist = lane_minus_row + jnp.where(j >= 0, q0 - k0, -(1 << 30))
            mask = (dist >= 0) & (dist < WINDOW)
            return kw_ref[0, 0, pl.ds(k0, kt), :], vwt_ref[0, 0, jc], mask

        win_tiles = [win_tile(r) for r in range(WIN_TILES)]
        sel_tiles = [sel_tile(n) for n in range(SEL_FIRST)]
        scores = all_scores(win_tiles + sel_tiles)
        state_w = attend(start, win_tiles, scores[:WIN_TILES])
        state_s = attend(start, sel_tiles, scores[WIN_TILES:])
        extra_steps = (jnp.maximum(count - SEL_FIRST, 0) + SEL_GROUP - 1) // SEL_GROUP
        more = lambda n, st: attend(st, [sel_tile(SEL_FIRST + SEL_GROUP * n + r) for r in range(SEL_GROUP)])
        return state_w, state_s, extra_steps, more

    blocks = [block(u) for u in range(nblk)]
    for u, (state_w, state_s, extra_steps, more) in enumerate(blocks):
        state_s = lax.fori_loop(0, extra_steps, more, state_s)
        o_s, o_w = result(state_s), result(state_w)
        lanes = slice(u * Q_BLOCK, (u + 1) * Q_BLOCK)
        gates = jax.nn.sigmoid(glt_ref[0, 0, :, lanes])
        o_c = oct_ref[0, 0, u]
        outs = []
        for g in range(NSA_GROUP):
            gsl = slice(g * Q_BLOCK, (g + 1) * Q_BLOCK)
            outs.append(gates[3 * g:3 * g + 1] * o_c[:, gsl] + gates[3 * g + 1:3 * g + 2] * o_s[:, gsl]
                        + gates[3 * g + 2:3 * g + 3] * o_w[:, gsl])
        yt_ref[0, :, lanes] = jnp.concatenate(outs, axis=0).astype(yt_ref.dtype)


def _sel_win(tile_ids, tile_cnt, q_t, sel_t, ks, vs_t, kw, vw_t, oc_t, gl_t):
    batch, hk, nq, dk, qg = oc_t.shape
    t = nq * Q_BLOCK
    nkt = t // KEY_TILE
    grp = qg // Q_BLOCK
    nblk = SEL_Q_BLOCKS
    qspec = pl.BlockSpec((1, 1, nblk, 2 * dk, qg), lambda b, h, i, *_: (b, h, i, 0, 0))
    ospec = pl.BlockSpec((1, 1, nblk, dk, qg), lambda b, h, i, *_: (b, h, i, 0, 0))
    kspec = pl.BlockSpec((1, 1, t, 2 * dk), lambda b, h, i, *_: (b, h, 0, 0))
    vspec = pl.BlockSpec((1, 1, nkt, V_ROWS, KEY_TILE), lambda b, h, i, *_: (b, h, 0, 0, 0))
    grid_spec = pltpu.PrefetchScalarGridSpec(
        num_scalar_prefetch=2,
        grid=(batch, hk, nq // nblk),
        in_specs=[qspec,
                  pl.BlockSpec((1, 1, nblk, LANES, Q_BLOCK), lambda b, h, i, *_: (b, h, i, 0, 0)),
                  kspec, vspec, kspec, vspec, ospec,
                  pl.BlockSpec((1, 1, 3 * grp, nblk * Q_BLOCK), lambda b, h, i, *_: (b, h, 0, i))],
        out_specs=pl.BlockSpec((1, grp * dk, nblk * Q_BLOCK), lambda b, h, i, *_: (b, h, i)),
    )
    return pl.pallas_call(
        _sel_win_kernel,
        grid_spec=grid_spec,
        out_shape=jax.ShapeDtypeStruct((batch, hk * grp * dk, t), BF16),
        compiler_params=_params("arbitrary", "arbitrary", "arbitrary"),
        name="nsa_sel_win",
    )(tile_ids, tile_cnt, q_t, sel_t, ks, vs_t, kw, vw_t, oc_t, gl_t)


def _ffn_kernel(x_ref, yr_ref, ynt_ref, wo1_ref, wo2_ref, g2_ref, wg_ref, wu_ref, wd_ref, gf_ref, o_ref):
    h1 = (x_ref[...] + _dot(yr_ref[...].astype(BF16), wo1_ref[...])
          + _dot_tn(ynt_ref[0].astype(BF16), wo2_ref[...]))
    hn = _rms(h1, g2_ref[...]).astype(BF16)
    gate = _dot(hn, wg_ref[...])
    up = _dot(hn, wu_ref[...])
    act = gate * jax.nn.sigmoid(gate) * up
    o_ref[...] = _rms(h1 + _dot(act.astype(BF16), wd_ref[...]), gf_ref[...])


def _ffn(x2, yr, yn_t, wo1, wo2, g2, wg, wu, wd, gf, tm=512):
    m, d = x2.shape
    _, dn, t = yn_t.shape
    per_seq = t // tm
    row = lambda n: pl.BlockSpec((tm, n), lambda i: (i, 0))
    const = lambda a: pl.BlockSpec(a.shape, lambda i: (0, 0), pipeline_mode=pl.Buffered(1))
    return pl.pallas_call(
        _ffn_kernel,
        grid=(m // tm,),
        in_specs=[row(d), row(yr.shape[1]),
                  pl.BlockSpec((1, dn, tm), lambda i: (i // per_seq, 0, i % per_seq)),
                  const(wo1), const(wo2), const(g2), const(wg), const(wu), const(wd), const(gf)],
        out_specs=row(d),
        out_shape=jax.ShapeDtypeStruct((m, d), F32),
        compiler_params=_params("arbitrary"),
        name="outproj_ffn",
    )(x2, yr, yn_t, wo1, wo2, g2, wg, wu, wd, gf)


def _overlap_matrix_t(t):
    n16 = t // CMP_STRIDE
    n_cmp = (t - CMP_BLOCK) // CMP_STRIDE + 1
    n_sel = t // SEL_BLOCK
    cmp_start = np.arange(n_cmp) * CMP_STRIDE
    sel_start = np.arange(n_sel) * SEL_BLOCK
    ov = np.clip(np.minimum(cmp_start[:, None] + CMP_BLOCK, sel_start[None, :] + SEL_BLOCK)
                 - np.maximum(cmp_start[:, None], sel_start[None, :]), 0, None) / CMP_STRIDE
    full = np.zeros((LANES, n16), np.float32)
    full[:n_sel, :n_cmp] = ov.T
    return jnp.asarray(full, BF16)


def kernel(x, norm1_g, w_in, mu_shift, rwkv_w0, rwkv_w2, rwkv_a0, rwkv_a2, rwkv_g2, rwkv_k_k, rwkv_k_a,
           rwkv_r_k, rwkv_lnx_w, rwkv_lnx_b, nsa_pe_k, nsa_pe_v, nsa_cmp_k_w1, nsa_cmp_k_w2, nsa_cmp_v_w1,
           nsa_cmp_v_w2, w_out, norm2_g, ffn_w_gate, ffn_w_up, ffn_w_down, norm_f_g):
    batch, t, d_model = x.shape
    assert w_in.shape[0] == 1, "the final RMSNorm is fused into the (single) layer's FFN kernel"
    assert t % INPROJ_ROWS == 0 and t % RWKV_PREP_ROWS == 0 and t // SEL_BLOCK <= LANES
    hk = NSA_KV_HEADS
    nq = t // Q_BLOCK
    assert nq % CMP_Q_BLOCKS == 0 and nq % SEL_Q_BLOCKS == 0 and t % (RWKV_SCAN_CHUNKS * RWKV_CHUNK) == 0
    slopes = 2.0 ** (-8.0 * jnp.arange(1, NSA_Q_HEADS + 1, dtype=F32) / NSA_Q_HEADS)
    nsa_pad = _round_up(N_NSA_COLS, LANES)
    row = lambda a: a.reshape(1, -1)
    i = 0

    h = x.reshape(batch * t, d_model)
    w_r = w_in[i][:, :N_RWKV_COLS].astype(BF16)
    w_n = jnp.pad(w_in[i][:, N_RWKV_COLS:], ((0, 0), (0, nsa_pad - N_NSA_COLS))).astype(BF16)
    p_r, q_t, ks, kw, vs_t, vw_t, zk, zv, gl_t = _inproj(h, batch, row(norm1_g[i]), w_r, w_n,
                                                         _alibi_query_rows(slopes))

    y_rwkv = _rwkv(p_r.reshape(batch, t, N_RWKV_COLS), row(mu_shift[i]), row(rwkv_w0[i]), rwkv_w2[i].astype(BF16),
                   row(rwkv_a0[i]), rwkv_a2[i].astype(BF16), rwkv_g2[i].astype(BF16), row(rwkv_k_k[i]),
                   row(rwkv_k_a[i]), row(rwkv_r_k[i]), row(rwkv_lnx_w[i]), row(rwkv_lnx_b[i]))
    y_rwkv = y_rwkv.reshape(batch * t, D_RWKV)

    kc_aug, vc_t = _compress(zk, zv, nsa_pe_k[i], nsa_pe_v[i],
                             nsa_cmp_k_w1[i].astype(BF16), nsa_cmp_k_w2[i].astype(BF16),
                             nsa_cmp_v_w1[i].astype(BF16), nsa_cmp_v_w2[i].astype(BF16))
    oc_t, sel_t, picked = _cmp_select(q_t, kc_aug, vc_t, _overlap_matrix_t(t))

    blocks_per_tile = KEY_TILE // SEL_BLOCK
    active = (picked[:, :, :, 0, :] > 0).reshape(batch, hk, nq, MAX_KEY_TILES, blocks_per_tile).any(axis=-1)
    tile_ids = jnp.argsort(jnp.logical_not(active), axis=-1, stable=True).astype(jnp.int32).reshape(-1)
    tile_cnt = active.sum(axis=-1).astype(jnp.int32).reshape(-1)
    y_nsa_t = _sel_win(tile_ids, tile_cnt, q_t, sel_t, ks, vs_t, kw, vw_t, oc_t, gl_t)

    out = _ffn(h, y_rwkv, y_nsa_t, w_out[i][:D_RWKV].astype(BF16), w_out[i][D_RWKV:].astype(BF16),
               row(norm2_g[i]), ffn_w_gate[i].astype(BF16), ffn_w_up[i].astype(BF16),
               ffn_w_down[i].astype(BF16), row(norm_f_g))
    return out.reshape(batch, t, d_model)
```

```python
import ml_dtypes
import numpy as np
import jax
import jax.numpy as jnp
from jax import lax
from jax.experimental import pallas as pl
from jax.experimental.pallas import tpu as pltpu

F32 = jnp.float32
BF16 = jnp.bfloat16

HEAD_DIM = 64
RWKV_HEADS = 8
D_RWKV = RWKV_HEADS * HEAD_DIM
NSA_Q_HEADS = 8
NSA_KV_HEADS = 2
NSA_GROUP = NSA_Q_HEADS // NSA_KV_HEADS
D_NSA = NSA_Q_HEADS * HEAD_DIM
D_KV = NSA_KV_HEADS * HEAD_DIM
LORA_W, LORA_A, LORA_G = 64, 64, 128
N_RWKV_COLS = 3 * D_RWKV + LORA_W + LORA_A + LORA_G
N_NSA_COLS = D_NSA + 6 * D_KV + 3 * NSA_Q_HEADS
CMP_BLOCK, CMP_STRIDE = 32, 16
SEL_BLOCK, SEL_TOPK = 64, 16
WINDOW = 512
Q_BLOCK = 128
NORM_EPS = 1e-6
GN_EPS = 64e-5
NEG = -1e30
BIG = 1e30

LANES = 128
RWKV_CHUNK = 64
RWKV_PREP_ROWS = 8 * RWKV_CHUNK
RWKV_SCAN_CHUNKS = 8
PREP_SPLIT = 2
KEY_TILE = 128
INPROJ_ROWS = 512
MAX_KEY_TILES = LANES * SEL_BLOCK // KEY_TILE
WIN_TILES = (WINDOW + Q_BLOCK) // KEY_TILE
SEL_FIRST = 11
SEL_GROUP = 2
SEL_Q_BLOCKS = 8
CMP_Q_BLOCKS = 8
CMP_ROW_CHUNK = 128
V_ROWS = HEAD_DIM + 16
SOFTMAX_FLOOR = -1e20
QG = NSA_GROUP * Q_BLOCK
VMEM_LIMIT = 56 * 1024 * 1024


def _bf16_terms(x, count):
    terms = []
    for _ in range(count):
        terms.append(float(np.asarray(x, ml_dtypes.bfloat16)))
        x = x - terms[-1]
    return tuple(terms)


LOG2E = float(np.log2(np.e))
LOG2E_TERMS = _bf16_terms(LOG2E, 3)
DECAY_LOG2_SCALE = float(np.exp(-0.5)) * LOG2E


def _round_up(n, m):
    return -(-n // m) * m


def _dot(a, b):
    return jnp.dot(a, b, preferred_element_type=F32)


def _dot_tn(a, b):
    return lax.dot_general(a, b, (((0,), (0,)), ((), ())), preferred_element_type=F32)


def _split(a, terms):
    pieces = []
    for _ in range(terms - 1):
        pieces.append(a.astype(BF16))
        a = a - pieces[-1].astype(F32)
    return pieces + [a.astype(BF16)]


def _dot_exact_lhs(a, b, terms=3):
    return sum(_dot(a, piece) for piece in _split(b, terms))


def _dot_exact_rhs(a, b, terms=3):
    return sum(_dot(piece, b) for piece in _split(a, terms))


def _bdot(spec, a, b):
    return jnp.einsum(spec, a, b, preferred_element_type=F32)


def _rms(x, g):
    return x * lax.rsqrt(jnp.mean(x * x, axis=-1, keepdims=True) + NORM_EPS) * g


def _params(*sem):
    return pltpu.CompilerParams(dimension_semantics=sem, vmem_limit_bytes=VMEM_LIMIT)


def _inproj_kernel(x_ref, g_ref, wr_ref, wn_ref, qx_ref, pr_ref, qt_ref, ks_ref, kw_ref, vst_ref, vwt_ref,
                   kc_ref, vc_ref, glt_ref):
    xb = _rms(x_ref[...], g_ref[...]).astype(BF16)
    pr_ref[...] = _dot(xb, wr_ref[...]).astype(pr_ref.dtype)
    pn = _dot(xb, wn_ref[...])
    rows = pn.shape[0]
    dk, hk_n = HEAD_DIM, NSA_KV_HEADS
    group = lambda j: pn[:, D_NSA + j * D_KV:D_NSA + (j + 1) * D_KV]

    for half in range(rows // Q_BLOCK):
        q_tr = jnp.transpose(pn[half * Q_BLOCK:(half + 1) * Q_BLOCK, :D_NSA])
        for hk in range(hk_n):
            base = hk * NSA_GROUP * dk
            heads = jnp.concatenate([q_tr[base + g * dk:base + (g + 1) * dk, :] for g in range(NSA_GROUP)], axis=1)
            qt_ref[0, hk, half] = jnp.concatenate([heads * (dk ** -0.5 * LOG2E), qx_ref[hk]], axis=0).astype(BF16)

    pos = pl.program_id(1) * rows + lax.broadcasted_iota(jnp.int32, (rows, dk), 0)
    col = lax.broadcasted_iota(jnp.int32, (rows, dk), 1)
    pos_cols = _position_features(pos // SEL_BLOCK, pos % SEL_BLOCK, col)
    kt = KEY_TILE
    ones_rows = (lax.broadcasted_iota(jnp.int32, (V_ROWS - dk, kt), 0) == 0).astype(F32)
    kc, vc, ks, vs, kw, vw = (group(j) for j in range(6))
    vs_tr, vw_tr = jnp.transpose(vs), jnp.transpose(vw)
    gl_tr = jnp.transpose(pn[:, D_NSA + 6 * D_KV:])
    n_gate = 3 * NSA_GROUP
    for hk in range(hk_n):
        sl = slice(hk * dk, (hk + 1) * dk)
        kc_ref[0, hk] = kc[:, sl]
        vc_ref[0, hk] = vc[:, sl]
        ks_ref[0, hk] = jnp.concatenate([ks[:, sl], pos_cols], axis=1).astype(BF16)
        kw_ref[0, hk] = jnp.concatenate([kw[:, sl], pos_cols], axis=1).astype(BF16)
        for u in range(rows // kt):
            vst_ref[0, hk, u] = jnp.concatenate([vs_tr[sl, u * kt:(u + 1) * kt], ones_rows], axis=0).astype(BF16)
            vwt_ref[0, hk, u] = jnp.concatenate([vw_tr[sl, u * kt:(u + 1) * kt], ones_rows], axis=0).astype(BF16)
        glt_ref[0, hk] = gl_tr[hk * n_gate:(hk + 1) * n_gate, :]


def _inproj(x2, batch, g, w_r, w_n, q_extra):
    m, d = x2.shape
    t = m // batch
    tm = INPROJ_ROWS
    nt = t // tm
    nr, nn = w_r.shape[1], w_n.shape[1]
    hk, dk = NSA_KV_HEADS, HEAD_DIM
    const = lambda a: pl.BlockSpec(a.shape, lambda b, i: (0, 0))
    keys = pl.BlockSpec((1, hk, tm, 2 * dk), lambda b, i: (b, 0, i, 0))
    vals = pl.BlockSpec((1, hk, tm // KEY_TILE, V_ROWS, KEY_TILE), lambda b, i: (b, 0, i, 0, 0))
    cmp_in = pl.BlockSpec((1, hk, tm, dk), lambda b, i: (b, 0, i, 0))
    keys_shape = jax.ShapeDtypeStruct((batch, hk, t, 2 * dk), BF16)
    vals_shape = jax.ShapeDtypeStruct((batch, hk, t // KEY_TILE, V_ROWS, KEY_TILE), BF16)
    cmp_shape = jax.ShapeDtypeStruct((batch, hk, t, dk), F32)
    return pl.pallas_call(
        _inproj_kernel,
        grid=(batch, nt),
        in_specs=[pl.BlockSpec((tm, d), lambda b, i: (b * nt + i, 0)), const(g), const(w_r), const(w_n),
                  pl.BlockSpec(q_extra.shape, lambda b, i: (0, 0, 0))],
        out_specs=[pl.BlockSpec((tm, nr), lambda b, i: (b * nt + i, 0)),
                   pl.BlockSpec((1, hk, tm // Q_BLOCK, 2 * dk, QG), lambda b, i: (b, 0, i, 0, 0)),
                   keys, keys, vals, vals, cmp_in, cmp_in,
                   pl.BlockSpec((1, hk, 3 * NSA_GROUP, tm), lambda b, i: (b, 0, 0, i))],
        out_shape=[jax.ShapeDtypeStruct((m, nr), BF16),
                   jax.ShapeDtypeStruct((batch, hk, t // Q_BLOCK, 2 * dk, QG), BF16),
                   keys_shape, keys_shape, vals_shape, vals_shape, cmp_shape, cmp_shape,
                   jax.ShapeDtypeStruct((batch, hk, 3 * NSA_GROUP, t), F32)],
        compiler_params=_params("arbitrary", "arbitrary"),
        name="inproj",
    )(x2, g, w_r, w_n, q_extra)


def _rwkv_operands(p, p_prev, mu, w0, w2, a0, a2, g2, k_k, k_a, r_k, bd, tri):
    c = RWKV_CHUNK
    ps = p + mu * (p_prev - p)
    d = D_RWKV
    r, k, v = ps[:, 0:d], ps[:, d:2 * d], ps[:, 2 * d:3 * d]
    dw = ps[:, 3 * d:3 * d + LORA_W]
    da = ps[:, 3 * d + LORA_W:3 * d + LORA_W + LORA_A]
    dg = ps[:, 3 * d + LORA_W + LORA_A:]

    logw = -DECAY_LOG2_SCALE * jax.nn.sigmoid(w0 + _dot(jnp.tanh(dw).astype(BF16), w2))
    a = jax.nn.sigmoid(a0 + _dot(da.astype(BF16), a2))
    g = _dot(jax.nn.sigmoid(dg).astype(BF16), g2)

    kk = k * k_k
    kk = kk / jnp.maximum(jnp.sqrt(_dot_exact_rhs(kk * kk, bd, PREP_SPLIT)), 1e-12)
    kp = k * (1.0 + (a - 1.0) * k_a)
    bonus = _dot_exact_rhs(r * kp * r_k, bd, PREP_SPLIT) * v

    n_chunks = p.shape[0] // c
    cum = jnp.concatenate([_dot_exact_lhs(tri, logw[ci * c:(ci + 1) * c], PREP_SPLIT) for ci in range(n_chunks)],
                          axis=0)
    e_pos, e_neg = jnp.exp2(cum), jnp.exp2(-cum)
    wc = jnp.concatenate([e_pos[(ci + 1) * c - 1:(ci + 1) * c, :] for ci in range(n_chunks)], axis=0)
    ops = (r * e_pos, kp * e_neg, kk * a * e_neg, -kk * jnp.exp2(cum - logw), v)
    return tuple(o.astype(BF16) for o in ops), wc, bonus, g


def _rwkv_prep_kernel(p_ref, mu_ref, w0_ref, w2_ref, a0_ref, a2_ref, g2_ref, kk_ref, ka_ref, rk_ref, bd_ref, tri_ref,
                      ops_ref, aux_ref, wc_ref, carry_ref):
    rows = p_ref.shape[1]

    @pl.when(pl.program_id(1) == 0)
    def _():
        carry_ref[...] = jnp.zeros_like(carry_ref)

    p = p_ref[0].astype(F32)
    row = lax.broadcasted_iota(jnp.int32, p.shape, 0)
    p_prev = jnp.where(row == 0, carry_ref[7:8, :], pltpu.roll(p, 1, axis=0))
    carry_ref[...] = p[rows - 8:, :]
    ops, wc, bonus, g = _rwkv_operands(
        p, p_prev, mu_ref[...], w0_ref[...], w2_ref[...], a0_ref[...], a2_ref[...], g2_ref[...],
        kk_ref[...], ka_ref[...], rk_ref[...], bd_ref[...], tri_ref[...])
    for idx, o in enumerate(ops):
        ops_ref[idx, 0] = o
    aux_ref[0, 0] = bonus.astype(aux_ref.dtype)
    aux_ref[1, 0] = g.astype(aux_ref.dtype)
    for ci in range(wc.shape[0]):
        wc_ref[0, ci] = wc[ci:ci + 1, :]


def _rwkv_scan_kernel(ops_ref, aux_ref, wc_ref, lnw_ref, lnb_ref, y_ref, s_ref):
    @pl.when(pl.program_id(0) == 0)
    def _():
        s_ref[...] = jnp.zeros_like(s_ref)

    _rwkv_step(ops_ref, aux_ref, wc_ref, lnw_ref, lnb_ref, y_ref, s_ref)


def _rwkv_step(ops_ref, aux_ref, wc_ref, lnw_ref, lnb_ref, y_ref, s_ref):
    _, nb, span, d = ops_ref.shape
    c = RWKV_CHUNK
    nh, dk = RWKV_HEADS, HEAD_DIM

    row = lax.broadcasted_iota(jnp.int32, (1, c, c), 1)
    col = lax.broadcasted_iota(jnp.int32, (1, c, c), 2)
    strict = col < row
    incl = col <= row
    eye = (row == col).astype(F32)

    def heads(x):
        return jnp.stack([x[b, :, h * dk:(h + 1) * dk] for b in range(nb) for h in range(nh)], axis=0)

    def independent(j):
        rt, kt, bt, at, vb = (heads(ops_ref[idx, :, j * c:(j + 1) * c, :]) for idx in range(5))
        ar = jnp.concatenate([at, rt], axis=1)
        bk = jnp.concatenate([bt, kt], axis=1)
        amat = _bdot("nik,njk->nij", ar, bk)
        n_ab = jnp.where(strict, amat[:, :c, :c], 0.0)
        a_ak = jnp.where(strict, amat[:, :c, c:], 0.0).astype(BF16)
        a_rb = jnp.where(incl, amat[:, c:, :c], 0.0).astype(BF16)
        a_rk = jnp.where(incl, amat[:, c:, c:], 0.0).astype(BF16)
        inv = eye + n_ab
        pw = n_ab
        for _ in range(int(np.log2(c)) - 1):
            pwb = pw.astype(BF16)
            pw = _bdot("nij,njk->nik", pwb, pwb)
            inv = inv + _bdot("nij,njk->nik", pw.astype(BF16), inv.astype(BF16))
        return ar, bk, vb, _bdot("nij,njv->niv", a_ak, vb), a_rb, a_rk, inv.astype(BF16)

    parts = [independent(j) for j in range(span // c)]
    s = s_ref[...]
    for j, (ar, bk, vb, akv, a_rb, a_rk, inv) in enumerate(parts):
        ar_s = _bdot("nik,nvk->niv", ar, s.astype(BF16))
        ub = _bdot("nij,njv->niv", inv, (ar_s[:, :c] + akv).astype(BF16)).astype(BF16)
        y = ar_s[:, c:] + _bdot("nij,njv->niv", a_rb, ub) + _bdot("nij,njv->niv", a_rk, vb)
        uv_t = jnp.swapaxes(jnp.concatenate([ub, vb], axis=1), 1, 2)
        s = (s + _bdot("nvi,nik->nvk", uv_t, bk)) * heads(wc_ref[:, j])

        mean = jnp.mean(y, axis=-1, keepdims=True)
        var = jnp.mean(jnp.square(y - mean), axis=-1, keepdims=True)
        yn = (y - mean) * lax.rsqrt(var + GN_EPS)
        rsl = slice(j * c, (j + 1) * c)
        for b in range(nb):
            wide = jnp.concatenate([yn[b * nh + h] for h in range(nh)], axis=-1)
            y_ref[b, rsl, :] = ((wide * lnw_ref[...] + lnb_ref[...] + aux_ref[0, b, rsl, :])
                                * aux_ref[1, b, rsl, :]).astype(y_ref.dtype)
    s_ref[...] = s


def _rwkv(p_r, mu, w0, w2, a0, a2, g2, k_k, k_a, r_k, lnw, lnb):
    batch, t, width = p_r.shape
    c = RWKV_CHUNK
    rows = RWKV_PREP_ROWS
    d = D_RWKV
    head = np.arange(d) // HEAD_DIM
    bd = jnp.asarray(head[:, None] == head[None, :], BF16)
    tri = jnp.asarray(np.tril(np.ones((c, c))), BF16)
    const2 = lambda a: pl.BlockSpec(a.shape, lambda b, i: (0, 0))
    consts = (mu, w0, w2, a0, a2, g2, k_k, k_a, r_k, bd, tri)
    ops, aux, wc = pl.pallas_call(
        _rwkv_prep_kernel,
        grid=(batch, t // rows),
        in_specs=[pl.BlockSpec((1, rows, width), lambda b, i: (b, i, 0))] + [const2(a) for a in consts],
        out_specs=[pl.BlockSpec((5, 1, rows, d), lambda b, i: (0, b, i, 0)),
                   pl.BlockSpec((2, 1, rows, d), lambda b, i: (0, b, i, 0)),
                   pl.BlockSpec((1, rows // c, 1, d), lambda b, i: (b, i, 0, 0))],
        out_shape=[jax.ShapeDtypeStruct((5, batch, t, d), BF16),
                   jax.ShapeDtypeStruct((2, batch, t, d), BF16),
                   jax.ShapeDtypeStruct((batch, t // c, 1, d), F32)],
        scratch_shapes=[pltpu.VMEM((8, width), F32)],
        compiler_params=_params("arbitrary", "arbitrary"),
        name="rwkv_prep",
    )(p_r, *consts)
    const1 = lambda a: pl.BlockSpec(a.shape, lambda i: (0, 0))
    span = RWKV_SCAN_CHUNKS * c
    return pl.pallas_call(
        _rwkv_scan_kernel,
        grid=(t // span,),
        in_specs=[pl.BlockSpec((5, batch, span, d), lambda i: (0, 0, i, 0)),
                  pl.BlockSpec((2, batch, span, d), lambda i: (0, 0, i, 0)),
                  pl.BlockSpec((batch, RWKV_SCAN_CHUNKS, 1, d), lambda i: (0, i, 0, 0)),
                  const1(lnw), const1(lnb)],
        out_specs=pl.BlockSpec((batch, span, d), lambda i: (0, i, 0)),
        out_shape=jax.ShapeDtypeStruct((batch, t, d), BF16),
        scratch_shapes=[pltpu.VMEM((batch * RWKV_HEADS, HEAD_DIM, HEAD_DIM), F32)],
        compiler_params=_params("arbitrary"),
        name="rwkv_scan",
    )(ops, aux, wc, lnw, lnb)


def _compress_kernel(zk_ref, zv_ref, pek_ref, pev_ref, k1_ref, k2_ref, v1_ref, v2_ref, ko_ref, vo_ref):
    def one(z_ref, pe_ref, w1_ref, w2_ref):
        n16 = z_ref.shape[2] // CMP_STRIDE
        dk = HEAD_DIM
        first = second = None
        for l in range(CMP_STRIDE):
            z = z_ref[0, 0, pl.ds(l, n16, stride=CMP_STRIDE), :]
            lo = _dot((z + pe_ref[l:l + 1, :]).astype(BF16), w1_ref[l * dk:(l + 1) * dk, :])
            u = CMP_STRIDE + l
            hi = _dot((z + pe_ref[u:u + 1, :]).astype(BF16), w1_ref[u * dk:(u + 1) * dk, :])
            first = lo if first is None else first + lo
            second = hi if second is None else second + hi
        hidden = first + pltpu.roll(second, n16 - 1, axis=0)
        return _dot(jax.nn.gelu(hidden).astype(BF16), w2_ref[...])

    kc = one(zk_ref, pek_ref, k1_ref, k2_ref)
    blk = lax.broadcasted_iota(jnp.int32, kc.shape, 0)
    col = lax.broadcasted_iota(jnp.int32, kc.shape, 1)
    per = SEL_BLOCK // CMP_STRIDE
    hi = blk // per
    lo = CMP_STRIDE * (blk % per) + (CMP_BLOCK - 1)
    feat = _position_features(hi, lo, col)
    ko_ref[0] = jnp.concatenate([kc, feat], axis=1).astype(BF16)
    vc = one(zv_ref, pev_ref, v1_ref, v2_ref)
    vc_t = jnp.transpose(jnp.concatenate([vc, jnp.zeros_like(vc)], axis=1))
    vo_ref[0] = vc_t[:HEAD_DIM].astype(BF16)


def _compress(zk, zv, pek, pev, k1, k2, v1, v2):
    batch, hk, t, dk = zk.shape
    n16 = t // CMP_STRIDE
    zspec = pl.BlockSpec((1, 1, t, dk), lambda b, h: (b, h, 0, 0))
    full = lambda a: pl.BlockSpec(a.shape, lambda b, h: (0,) * a.ndim)
    return pl.pallas_call(
        _compress_kernel,
        grid=(batch, hk),
        in_specs=[zspec, zspec, full(pek), full(pev), full(k1), full(k2), full(v1), full(v2)],
        out_specs=[pl.BlockSpec((1, n16, 2 * dk), lambda b, h: (b * hk + h, 0, 0)),
                   pl.BlockSpec((1, dk, n16), lambda b, h: (b * hk + h, 0, 0))],
        out_shape=[jax.ShapeDtypeStruct((batch * hk, n16, 2 * dk), BF16),
                   jax.ShapeDtypeStruct((batch * hk, dk, n16), BF16)],
        compiler_params=_params("arbitrary", "arbitrary"),
        name="nsa_compress",
    )(zk, zv, pek, pev, k1, k2, v1, v2)


def _alibi_query_rows(slopes):
    per_lane = jnp.repeat(slopes.reshape(NSA_KV_HEADS, NSA_GROUP), Q_BLOCK, axis=1)
    rows = []
    for term in LOG2E_TERMS:
        rows += [(SEL_BLOCK * term) * per_lane, term * per_lane]
    rows = jnp.stack(rows, axis=1)
    return jnp.pad(rows, ((0, 0), (0, HEAD_DIM - rows.shape[1]), (0, 0)))


def _position_features(hi, lo, col):
    return jnp.where(col < 2 * len(LOG2E_TERMS), jnp.where(col % 2 == 0, hi, lo), 0).astype(F32)


def _cmp_select_kernel(qt_ref, kc_ref, vct_ref, ovt_ref, oct_ref, selt_ref, act_ref, imp_ref):
    nblk = qt_ref.shape[2]
    width = nblk * Q_BLOCK
    ncp = kc_ref.shape[1]

    def attend(rows):
        cmp_end = lax.broadcasted_iota(jnp.int32, (rows, Q_BLOCK), 0) * CMP_STRIDE + (CMP_BLOCK - 1)
        for u in range(nblk):
            q0 = (pl.program_id(2) * nblk + u) * Q_BLOCK
            s = _dot(kc_ref[0, 0:rows, :], qt_ref[0, 0, u])
            ok = cmp_end <= q0 + lax.broadcasted_iota(jnp.int32, (rows, Q_BLOCK), 1)
            any_ok = (q0 + lax.broadcasted_iota(jnp.int32, (1, Q_BLOCK), 1) >= CMP_BLOCK - 1).astype(F32)
            p_sum = jnp.zeros((rows, Q_BLOCK), F32)
            probs = []
            for g in range(NSA_GROUP):
                sg = jnp.where(ok, s[:, g * Q_BLOCK:(g + 1) * Q_BLOCK], NEG)
                e = jnp.exp2(sg - jnp.max(sg, axis=0, keepdims=True))
                p = e * (any_ok / jnp.sum(e, axis=0, keepdims=True))
                p_sum = p_sum + p
                probs.append(p.astype(BF16))
            oct_ref[0, 0, u] = _dot(vct_ref[0, :, 0:rows], jnp.concatenate(probs, axis=1))
            imp_ref[:, u * Q_BLOCK:(u + 1) * Q_BLOCK] = _dot_exact_lhs(ovt_ref[:, 0:rows], p_sum)

    chunk = min(CMP_ROW_CHUNK, ncp)
    n_chunks = ncp // chunk
    last_t = (pl.program_id(2) + 1) * width - 1
    needed = jnp.maximum((last_t - (CMP_BLOCK - 1)) // CMP_STRIDE + 1, 1)
    needed_chunks = jnp.minimum((needed + chunk - 1) // chunk, n_chunks)
    for nck in range(1, n_chunks + 1):
        pl.when(needed_chunks == nck)(lambda nck=nck: attend(nck * chunk))

    imp = imp_ref[...]
    blk = lax.broadcasted_iota(jnp.int32, (LANES, width), 0)
    cur = (pl.program_id(2) * width + lax.broadcasted_iota(jnp.int32, (LANES, width), 1)) // SEL_BLOCK
    valid = blk <= cur
    forced = (blk == 0) | (blk == cur) | (blk == cur - 1)
    x = jnp.where(forced, -jnp.inf, jnp.where(valid, imp, NEG))
    blk_f = blk.astype(F32)
    for _ in range(SEL_TOPK - 3):
        m = jnp.max(x, axis=0, keepdims=True)
        first = jnp.min(jnp.where(x == m, blk_f, float(LANES)), axis=0, keepdims=True)
        x = jnp.where(blk_f == first, -jnp.inf, x)
    sel = ((x == -jnp.inf) & valid).astype(F32)
    ones = jnp.ones((8, Q_BLOCK), BF16)
    for u in range(nblk):
        sel_u = sel[:, u * Q_BLOCK:(u + 1) * Q_BLOCK]
        selt_ref[0, 0, u] = sel_u
        act_ref[0, 0, u] = lax.dot_general(ones, sel_u.astype(BF16), (((1,), (1,)), ((), ())),
                                           preferred_element_type=F32)


def _cmp_select(q_t, kc_aug, vc_t, ov_t):
    batch, hk, nq, dk2, qg = q_t.shape
    dk = dk2 // 2
    ncp = kc_aug.shape[1]
    nblk = CMP_Q_BLOCKS
    return pl.pallas_call(
        _cmp_select_kernel,
        grid=(batch, hk, nq // nblk),
        in_specs=[pl.BlockSpec((1, 1, nblk, dk2, qg), lambda b, h, i: (b, h, i, 0, 0)),
                  pl.BlockSpec((1, ncp, 2 * dk), lambda b, h, i: (b * hk + h, 0, 0)),
                  pl.BlockSpec((1, dk, ncp), lambda b, h, i: (b * hk + h, 0, 0)),
                  pl.BlockSpec(ov_t.shape, lambda b, h, i: (0, 0))],
        out_specs=[pl.BlockSpec((1, 1, nblk, dk, qg), lambda b, h, i: (b, h, i, 0, 0)),
                   pl.BlockSpec((1, 1, nblk, LANES, Q_BLOCK), lambda b, h, i: (b, h, i, 0, 0)),
                   pl.BlockSpec((1, 1, nblk, 8, LANES), lambda b, h, i: (b, h, i, 0, 0))],
        out_shape=[jax.ShapeDtypeStruct((batch, hk, nq, dk, qg), F32),
                   jax.ShapeDtypeStruct((batch, hk, nq, LANES, Q_BLOCK), F32),
                   jax.ShapeDtypeStruct((batch, hk, nq, 8, LANES), F32)],
        scratch_shapes=[pltpu.VMEM((LANES, nblk * Q_BLOCK), F32)],
        compiler_params=_params("arbitrary", "arbitrary", "arbitrary"),
        name="nsa_cmp_select",
    )(q_t, kc_aug, vc_t, ov_t)


def _sel_win_kernel(ids_ref, cnt_ref, qt_ref, selt_ref, ks_ref, vst_ref, kw_ref, vwt_ref,
                    oct_ref, glt_ref, yt_ref):
    b, hk, i = pl.program_id(0), pl.program_id(1), pl.program_id(2)
    nblk = qt_ref.shape[2]
    kt = KEY_TILE
    last_tile = ks_ref.shape[2] // kt - 1
    lane_minus_row = (lax.broadcasted_iota(jnp.int32, (kt, Q_BLOCK), 1)
                      - lax.broadcasted_iota(jnp.int32, (kt, Q_BLOCK), 0))
    start = (jnp.full((1, QG), SOFTMAX_FLOOR, F32), jnp.zeros((V_ROWS, QG), F32))
    result = lambda state: state[1][:HEAD_DIM] / state[1][HEAD_DIM:HEAD_DIM + 1]

    def block(u):
        blk = i * nblk + u
        step = (b * pl.num_programs(1) + hk) * (pl.num_programs(2) * nblk) + blk
        q0 = blk * Q_BLOCK
        q_aug = qt_ref[0, 0, u]
        count = cnt_ref[step]

        def all_scores(tiles):
            s_all = _dot(jnp.concatenate([k_tile for k_tile, _, _ in tiles], axis=0), q_aug)
            return [s_all[n * kt:(n + 1) * kt] for n in range(len(tiles))]

        def attend(state, tiles, scores=None):
            m_old, acc_old = state
            scores = all_scores(tiles) if scores is None else scores
            probs, maxes = [[] for _ in tiles], []
            for g in range(NSA_GROUP):
                gsl = slice(g * Q_BLOCK, (g + 1) * Q_BLOCK)
                masked = [jnp.where(mask, s[:, gsl], NEG) for s, (_, _, mask) in zip(scores, tiles)]
                top = jnp.max(masked[0].reshape(kt // 8, 8, Q_BLOCK), axis=0)
                for sg in masked[1:]:
                    top = jnp.maximum(top, jnp.max(sg.reshape(kt // 8, 8, Q_BLOCK), axis=0))
                mg = jnp.maximum(m_old[:, gsl], jnp.max(top, axis=0, keepdims=True))
                for n, sg in enumerate(masked):
                    probs[n].append(jnp.exp2(sg - mg).astype(BF16))
                maxes.append(mg)
            m_new = jnp.concatenate(maxes, axis=1)
            p_all = jnp.concatenate([jnp.concatenate(p, axis=1) for p in probs], axis=0)
            vt_all = jnp.concatenate([vt_tile for _, vt_tile, _ in tiles], axis=1)
            return m_new, jnp.exp2(m_old - m_new) * acc_old + _dot(vt_all, p_all)

        def sel_tile(n):
            j = jnp.minimum(ids_ref[step * MAX_KEY_TILES + jnp.minimum(n, MAX_KEY_TILES - 1)], last_tile)
            k0 = pl.multiple_of(j * kt, kt)
            per = kt // SEL_BLOCK
            picked = jnp.concatenate(
                [jnp.broadcast_to(selt_ref[0, 0, u, pl.ds(j * per + r, 1), :], (SEL_BLOCK, Q_BLOCK))
                 for r in range(per)], axis=0)
            causal_from = jnp.where(n < count, k0 - q0, 1 << 30)
            mask = (picked > 0.5) & (lane_minus_row >= causal_from)
            return ks_ref[0, 0, pl.ds(k0, kt), :], vst_ref[0, 0, j], mask

        last = (q0 + Q_BLOCK - 1) // kt

        def win_tile(r):
            j = last - r
            jc = jnp.maximum(j, 0)
            k0 = pl.multiple_of(jc * kt, kt)
            dist = lane_minus_row + jnp.where(j >= 0, q0 - k0, -(1 << 30))
            mask = (dist >= 0) & (dist < WINDOW)
            return kw_ref[0, 0, pl.ds(k0, kt), :], vwt_ref[0, 0, jc], mask

        win_tiles = [win_tile(r) for r in range(WIN_TILES)]
        sel_tiles = [sel_tile(n) for n in range(SEL_FIRST)]
        scores = all_scores(win_tiles + sel_tiles)
        state_w = attend(start, win_tiles, scores[:WIN_TILES])
        state_s = attend(start, sel_tiles, scores[WIN_TILES:])
        extra_steps = (jnp.maximum(count - SEL_FIRST, 0) + SEL_GROUP - 1) // SEL_GROUP
        more = lambda n, st: attend(st, [sel_tile(SEL_FIRST + SEL_GROUP * n + r) for r in range(SEL_GROUP)])
        return state_w, state_s, extra_steps, more

    blocks = [block(u) for u in range(nblk)]
    for u, (state_w, state_s, extra_steps, more) in enumerate(blocks):
        state_s = lax.fori_loop(0, extra_steps, more, state_s)
        o_s, o_w = result(state_s), result(state_w)
        lanes = slice(u * Q_BLOCK, (u + 1) * Q_BLOCK)
        gates = jax.nn.sigmoid(glt_ref[0, 0, :, lanes])
        o_c = oct_ref[0, 0, u]
        outs = []
        for g in range(NSA_GROUP):
            gsl = slice(g * Q_BLOCK, (g + 1) * Q_BLOCK)
            outs.append(gates[3 * g:3 * g + 1] * o_c[:, gsl] + gates[3 * g + 1:3 * g + 2] * o_s[:, gsl]
                        + gates[3 * g + 2:3 * g + 3] * o_w[:, gsl])
        yt_ref[0, :, lanes] = jnp.concatenate(outs, axis=0).astype(yt_ref.dtype)


def _sel_win(tile_ids, tile_cnt, q_t, sel_t, ks, vs_t, kw, vw_t, oc_t, gl_t):
    batch, hk, nq, dk, qg = oc_t.shape
    t = nq * Q_BLOCK
    nkt = t // KEY_TILE
    grp = qg // Q_BLOCK
    nblk = SEL_Q_BLOCKS
    qspec = pl.BlockSpec((1, 1, nblk, 2 * dk, qg), lambda b, h, i, *_: (b, h, i, 0, 0))
    ospec = pl.BlockSpec((1, 1, nblk, dk, qg), lambda b, h, i, *_: (b, h, i, 0, 0))
    kspec = pl.BlockSpec((1, 1, t, 2 * dk), lambda b, h, i, *_: (b, h, 0, 0))
    vspec = pl.BlockSpec((1, 1, nkt, V_ROWS, KEY_TILE), lambda b, h, i, *_: (b, h, 0, 0, 0))
    grid_spec = pltpu.PrefetchScalarGridSpec(
        num_scalar_prefetch=2,
        grid=(batch, hk, nq // nblk),
        in_specs=[qspec,
                  pl.BlockSpec((1, 1, nblk, LANES, Q_BLOCK), lambda b, h, i, *_: (b, h, i, 0, 0)),
                  kspec, vspec, kspec, vspec, ospec,
                  pl.BlockSpec((1, 1, 3 * grp, nblk * Q_BLOCK), lambda b, h, i, *_: (b, h, 0, i))],
        out_specs=pl.BlockSpec((1, grp * dk, nblk * Q_BLOCK), lambda b, h, i, *_: (b, h, i)),
    )
    return pl.pallas_call(
        _sel_win_kernel,
        grid_spec=grid_spec,
        out_shape=jax.ShapeDtypeStruct((batch, hk * grp * dk, t), BF16),
        compiler_params=_params("arbitrary", "arbitrary", "arbitrary"),
        name="nsa_sel_win",
    )(tile_ids, tile_cnt, q_t, sel_t, ks, vs_t, kw, vw_t, oc_t, gl_t)


def _ffn_kernel(x_ref, yr_ref, ynt_ref, wo1_ref, wo2_ref, g2_ref, wg_ref, wu_ref, wd_ref, gf_ref, o_ref):
    h1 = (x_ref[...] + _dot(yr_ref[...].astype(BF16), wo1_ref[...])
          + _dot_tn(ynt_ref[0].astype(BF16), wo2_ref[...]))
    hn = _rms(h1, g2_ref[...]).astype(BF16)
    gate = _dot(hn, wg_ref[...])
    up = _dot(hn, wu_ref[...])
    act = gate * jax.nn.sigmoid(gate) * up
    o_ref[...] = _rms(h1 + _dot(act.astype(BF16), wd_ref[...]), gf_ref[...])


def _ffn(x2, yr, yn_t, wo1, wo2, g2, wg, wu, wd, gf, tm=512):
    m, d = x2.shape
    _, dn, t = yn_t.shape
    per_seq = t // tm
    row = lambda n: pl.BlockSpec((tm, n), lambda i: (i, 0))
    const = lambda a: pl.BlockSpec(a.shape, lambda i: (0, 0), pipeline_mode=pl.Buffered(1))
    return pl.pallas_call(
        _ffn_kernel,
        grid=(m // tm,),
        in_specs=[row(d), row(yr.shape[1]),
                  pl.BlockSpec((1, dn, tm), lambda i: (i // per_seq, 0, i % per_seq)),
                  const(wo1), const(wo2), const(g2), const(wg), const(wu), const(wd), const(gf)],
        out_specs=row(d),
        out_shape=jax.ShapeDtypeStruct((m, d), F32),
        compiler_params=_params("arbitrary"),
        name="outproj_ffn",
    )(x2, yr, yn_t, wo1, wo2, g2, wg, wu, wd, gf)


def _overlap_matrix_t(t):
    n16 = t // CMP_STRIDE
    n_cmp = (t - CMP_BLOCK) // CMP_STRIDE + 1
    n_sel = t // SEL_BLOCK
    cmp_start = np.arange(n_cmp) * CMP_STRIDE
    sel_start = np.arange(n_sel) * SEL_BLOCK
    ov = np.clip(np.minimum(cmp_start[:, None] + CMP_BLOCK, sel_start[None, :] + SEL_BLOCK)
                 - np.maximum(cmp_start[:, None], sel_start[None, :]), 0, None) / CMP_STRIDE
    full = np.zeros((LANES, n16), np.float32)
    full[:n_sel, :n_cmp] = ov.T
    return jnp.asarray(full, BF16)


def kernel(x, norm1_g, w_in, mu_shift, rwkv_w0, rwkv_w2, rwkv_a0, rwkv_a2, rwkv_g2, rwkv_k_k, rwkv_k_a,
           rwkv_r_k, rwkv_lnx_w, rwkv_lnx_b, nsa_pe_k, nsa_pe_v, nsa_cmp_k_w1, nsa_cmp_k_w2, nsa_cmp_v_w1,
           nsa_cmp_v_w2, w_out, norm2_g, ffn_w_gate, ffn_w_up, ffn_w_down, norm_f_g):
    batch, t, d_model = x.shape
    assert w_in.shape[0] == 1, "the final RMSNorm is fused into the (single) layer's FFN kernel"
    assert t % INPROJ_ROWS == 0 and t % RWKV_PREP_ROWS == 0 and t // SEL_BLOCK <= LANES
    hk = NSA_KV_HEADS
    nq = t // Q_BLOCK
    assert nq % CMP_Q_BLOCKS == 0 and nq % SEL_Q_BLOCKS == 0 and t % (RWKV_SCAN_CHUNKS * RWKV_CHUNK) == 0
    slopes = 2.0 ** (-8.0 * jnp.arange(1, NSA_Q_HEADS + 1, dtype=F32) / NSA_Q_HEADS)
    nsa_pad = _round_up(N_NSA_COLS, LANES)
    row = lambda a: a.reshape(1, -1)
    i = 0

    h = x.reshape(batch * t, d_model)
    w_r = w_in[i][:, :N_RWKV_COLS].astype(BF16)
    w_n = jnp.pad(w_in[i][:, N_RWKV_COLS:], ((0, 0), (0, nsa_pad - N_NSA_COLS))).astype(BF16)
    p_r, q_t, ks, kw, vs_t, vw_t, zk, zv, gl_t = _inproj(h, batch, row(norm1_g[i]), w_r, w_n,
                                                         _alibi_query_rows(slopes))

    y_rwkv = _rwkv(p_r.reshape(batch, t, N_RWKV_COLS), row(mu_shift[i]), row(rwkv_w0[i]), rwkv_w2[i].astype(BF16),
                   row(rwkv_a0[i]), rwkv_a2[i].astype(BF16), rwkv_g2[i].astype(BF16), row(rwkv_k_k[i]),
                   row(rwkv_k_a[i]), row(rwkv_r_k[i]), row(rwkv_lnx_w[i]), row(rwkv_lnx_b[i]))
    y_rwkv = y_rwkv.reshape(batch * t, D_RWKV)

    kc_aug, vc_t = _compress(zk, zv, nsa_pe_k[i], nsa_pe_v[i],
                             nsa_cmp_k_w1[i].astype(BF16), nsa_cmp_k_w2[i].astype(BF16),
                             nsa_cmp_v_w1[i].astype(BF16), nsa_cmp_v_w2[i].astype(BF16))
    oc_t, sel_t, picked = _cmp_select(q_t, kc_aug, vc_t, _overlap_matrix_t(t))

    blocks_per_tile = KEY_TILE // SEL_BLOCK
    active = (picked[:, :, :, 0, :] > 0).reshape(batch, hk, nq, MAX_KEY_TILES, blocks_per_tile).any(axis=-1)
    tile_ids = jnp.argsort(jnp.logical_not(active), axis=-1, stable=True).astype(jnp.int32).reshape(-1)
    tile_cnt = active.sum(axis=-1).astype(jnp.int32).reshape(-1)
    y_nsa_t = _sel_win(tile_ids, tile_cnt, q_t, sel_t, ks, vs_t, kw, vw_t, oc_t, gl_t)

    out = _ffn(h, y_rwkv, y_nsa_t, w_out[i][:D_RWKV].astype(BF16), w_out[i][D_RWKV:].astype(BF16),
               row(norm2_g[i]), ffn_w_gate[i].astype(BF16), ffn_w_up[i].astype(BF16),
               ffn_w_down[i].astype(BF16), row(norm_f_g))
    return out.reshape(batch, t, d_model)
```

```python
import ml_dtypes
import numpy as np
import jax
import jax.numpy as jnp
from jax import lax
from jax.experimental import pallas as pl
from jax.experimental.pallas import tpu as pltpu

F32 = jnp.float32
BF16 = jnp.bfloat16

HEAD_DIM = 64
RWKV_HEADS = 8
D_RWKV = RWKV_HEADS * HEAD_DIM
NSA_Q_HEADS = 8
NSA_KV_HEADS = 2
NSA_GROUP = NSA_Q_HEADS // NSA_KV_HEADS
D_NSA = NSA_Q_HEADS * HEAD_DIM
D_KV = NSA_KV_HEADS * HEAD_DIM
LORA_W, LORA_A, LORA_G = 64, 64, 128
N_RWKV_COLS = 3 * D_RWKV + LORA_W + LORA_A + LORA_G
N_NSA_COLS = D_NSA + 6 * D_KV + 3 * NSA_Q_HEADS
CMP_BLOCK, CMP_STRIDE = 32, 16
SEL_BLOCK, SEL_TOPK = 64, 16
WINDOW = 512
Q_BLOCK = 128
NORM_EPS = 1e-6
GN_EPS = 64e-5
NEG = -1e30
BIG = 1e30

LANES = 128
RWKV_CHUNK = 64
TRI_BASE = 8
RWKV_SCAN_CHUNKS = 8
PREP_SPLIT = 2
KEY_TILE = 128
INPROJ_ROWS = 512
MAX_KEY_TILES = LANES * SEL_BLOCK // KEY_TILE
WIN_TILES = (WINDOW + Q_BLOCK) // KEY_TILE
SEL_FIRST = 11
SEL_GROUP = 2
SEL_Q_BLOCKS = 8
CMP_Q_BLOCKS = 8
CMP_ROW_CHUNK = 128
V_ROWS = HEAD_DIM + 16
SOFTMAX_FLOOR = -1e20
QG = NSA_GROUP * Q_BLOCK
VMEM_LIMIT = 56 * 1024 * 1024


def _bf16_terms(x, count):
    terms = []
    for _ in range(count):
        terms.append(float(np.asarray(x, ml_dtypes.bfloat16)))
        x = x - terms[-1]
    return tuple(terms)


LOG2E = float(np.log2(np.e))
LOG2E_TERMS = _bf16_terms(LOG2E, 3)
DECAY_LOG2_SCALE = float(np.exp(-0.5)) * LOG2E


def _round_up(n, m):
    return -(-n // m) * m


def _dot(a, b):
    return jnp.dot(a, b, preferred_element_type=F32)


def _dot_tn(a, b):
    return lax.dot_general(a, b, (((0,), (0,)), ((), ())), preferred_element_type=F32)


def _split(a, terms):
    pieces = []
    for _ in range(terms - 1):
        pieces.append(a.astype(BF16))
        a = a - pieces[-1].astype(F32)
    return pieces + [a.astype(BF16)]


def _dot_exact_lhs(a, b, terms=3):
    return sum(_dot(a, piece) for piece in _split(b, terms))


def _dot_exact_rhs(a, b, terms=3):
    return sum(_dot(piece, b) for piece in _split(a, terms))


def _bdot(spec, a, b):
    return jnp.einsum(spec, a, b, preferred_element_type=F32)


def _rms(x, g):
    return x * lax.rsqrt(jnp.mean(x * x, axis=-1, keepdims=True) + NORM_EPS) * g


def _params(*sem):
    return pltpu.CompilerParams(dimension_semantics=sem, vmem_limit_bytes=VMEM_LIMIT)


def _inproj_kernel(x_ref, g_ref, wr_ref, wn_ref, qx_ref, mu_ref, w0_ref, w2_ref, a0_ref, a2_ref, g2_ref, kk_ref,
                   ka_ref, rk_ref, bd_ref, tri_ref,
                   ops_ref, aux_ref, wc_ref, qt_ref, ks_ref, kw_ref, vst_ref, vwt_ref, kc_ref, vc_ref, glt_ref,
                   carry_ref):
    xb = _rms(x_ref[...], g_ref[...]).astype(BF16)

    @pl.when(pl.program_id(1) == 0)
    def _():
        carry_ref[...] = jnp.zeros_like(carry_ref)

    p = _dot(xb, wr_ref[...])
    prow = lax.broadcasted_iota(jnp.int32, p.shape, 0)
    p_prev = jnp.where(prow == 0, carry_ref[7:8, :], pltpu.roll(p, 1, axis=0))
    carry_ref[...] = p[p.shape[0] - 8:, :]
    ops, wc, bonus, gate = _rwkv_operands(
        p, p_prev, mu_ref[...], w0_ref[...], w2_ref[...], a0_ref[...], a2_ref[...], g2_ref[...],
        kk_ref[...], ka_ref[...], rk_ref[...], bd_ref[...], tri_ref[...])
    for idx, o in enumerate(ops):
        ops_ref[idx, 0] = o
    aux_ref[0, 0] = bonus
    aux_ref[1, 0] = gate
    for ci in range(wc.shape[0]):
        wc_ref[0, ci] = wc[ci:ci + 1, :]

    pn = _dot(xb, wn_ref[...])
    rows = pn.shape[0]
    dk, hk_n = HEAD_DIM, NSA_KV_HEADS
    group = lambda j: pn[:, D_NSA + j * D_KV:D_NSA + (j + 1) * D_KV]

    for half in range(rows // Q_BLOCK):
        q_tr = jnp.transpose(pn[half * Q_BLOCK:(half + 1) * Q_BLOCK, :D_NSA])
        for hk in range(hk_n):
            base = hk * NSA_GROUP * dk
            heads = jnp.concatenate([q_tr[base + g * dk:base + (g + 1) * dk, :] for g in range(NSA_GROUP)], axis=1)
            qt_ref[0, hk, half] = jnp.concatenate([heads * (dk ** -0.5 * LOG2E), qx_ref[hk]], axis=0).astype(BF16)

    pos = pl.program_id(1) * rows + lax.broadcasted_iota(jnp.int32, (rows, dk), 0)
    col = lax.broadcasted_iota(jnp.int32, (rows, dk), 1)
    pos_cols = _position_features(pos // SEL_BLOCK, pos % SEL_BLOCK, col)
    kt = KEY_TILE
    ones_rows = (lax.broadcasted_iota(jnp.int32, (V_ROWS - dk, kt), 0) == 0).astype(F32)
    kc, vc, ks, vs, kw, vw = (group(j) for j in range(6))
    vs_tr, vw_tr = jnp.transpose(vs), jnp.transpose(vw)
    gl_tr = jnp.transpose(pn[:, D_NSA + 6 * D_KV:])
    n_gate = 3 * NSA_GROUP
    for hk in range(hk_n):
        sl = slice(hk * dk, (hk + 1) * dk)
        kc_ref[0, hk] = kc[:, sl]
        vc_ref[0, hk] = vc[:, sl]
        ks_ref[0, hk] = jnp.concatenate([ks[:, sl], pos_cols], axis=1).astype(BF16)
        kw_ref[0, hk] = jnp.concatenate([kw[:, sl], pos_cols], axis=1).astype(BF16)
        for u in range(rows // kt):
            vst_ref[0, hk, u] = jnp.concatenate([vs_tr[sl, u * kt:(u + 1) * kt], ones_rows], axis=0).astype(BF16)
            vwt_ref[0, hk, u] = jnp.concatenate([vw_tr[sl, u * kt:(u + 1) * kt], ones_rows], axis=0).astype(BF16)
        glt_ref[0, hk] = gl_tr[hk * n_gate:(hk + 1) * n_gate, :]


def _inproj(x2, batch, g, w_r, w_n, q_extra, rwkv_consts):
    m, d = x2.shape
    t = m // batch
    tm = INPROJ_ROWS
    nt = t // tm
    hk, dk = NSA_KV_HEADS, HEAD_DIM
    dr, c = D_RWKV, RWKV_CHUNK
    const = lambda a: pl.BlockSpec(a.shape, lambda b, i: (0, 0))
    keys = pl.BlockSpec((1, hk, tm, 2 * dk), lambda b, i: (b, 0, i, 0))
    vals = pl.BlockSpec((1, hk, tm // KEY_TILE, V_ROWS, KEY_TILE), lambda b, i: (b, 0, i, 0, 0))
    cmp_in = pl.BlockSpec((1, hk, tm, dk), lambda b, i: (b, 0, i, 0))
    keys_shape = jax.ShapeDtypeStruct((batch, hk, t, 2 * dk), BF16)
    vals_shape = jax.ShapeDtypeStruct((batch, hk, t // KEY_TILE, V_ROWS, KEY_TILE), BF16)
    cmp_shape = jax.ShapeDtypeStruct((batch, hk, t, dk), F32)
    return pl.pallas_call(
        _inproj_kernel,
        grid=(batch, nt),
        in_specs=[pl.BlockSpec((tm, d), lambda b, i: (b * nt + i, 0)), const(g), const(w_r), const(w_n),
                  pl.BlockSpec(q_extra.shape, lambda b, i: (0, 0, 0))] + [const(a) for a in rwkv_consts],
        out_specs=[pl.BlockSpec((5, 1, tm, dr), lambda b, i: (0, b, i, 0)),
                   pl.BlockSpec((2, 1, tm, dr), lambda b, i: (0, b, i, 0)),
                   pl.BlockSpec((1, tm // c, 1, dr), lambda b, i: (b, i, 0, 0)),
                   pl.BlockSpec((1, hk, tm // Q_BLOCK, 2 * dk, QG), lambda b, i: (b, 0, i, 0, 0)),
                   keys, keys, vals, vals, cmp_in, cmp_in,
                   pl.BlockSpec((1, hk, 3 * NSA_GROUP, tm), lambda b, i: (b, 0, 0, i))],
        out_shape=[jax.ShapeDtypeStruct((5, batch, t, dr), BF16),
                   jax.ShapeDtypeStruct((2, batch, t, dr), F32),
                   jax.ShapeDtypeStruct((batch, t // c, 1, dr), F32),
                   jax.ShapeDtypeStruct((batch, hk, t // Q_BLOCK, 2 * dk, QG), BF16),
                   keys_shape, keys_shape, vals_shape, vals_shape, cmp_shape, cmp_shape,
                   jax.ShapeDtypeStruct((batch, hk, 3 * NSA_GROUP, t), F32)],
        scratch_shapes=[pltpu.VMEM((8, w_r.shape[1]), F32)],
        compiler_params=_params("arbitrary", "arbitrary"),
        name="inproj",
    )(x2, g, w_r, w_n, q_extra, *rwkv_consts)


def _rwkv_operands(p, p_prev, mu, w0, w2, a0, a2, g2, k_k, k_a, r_k, bd, tri):
    c = RWKV_CHUNK
    ps = p + mu * (p_prev - p)
    d = D_RWKV
    r, k, v = ps[:, 0:d], ps[:, d:2 * d], ps[:, 2 * d:3 * d]
    dw = ps[:, 3 * d:3 * d + LORA_W]
    da = ps[:, 3 * d + LORA_W:3 * d + LORA_W + LORA_A]
    dg = ps[:, 3 * d + LORA_W + LORA_A:]

    logw = -DECAY_LOG2_SCALE * jax.nn.sigmoid(w0 + _dot(jnp.tanh(dw).astype(BF16), w2))
    a = jax.nn.sigmoid(a0 + _dot(da.astype(BF16), a2))
    g = _dot(jax.nn.sigmoid(dg).astype(BF16), g2)

    kk = k * k_k
    kk = kk / jnp.maximum(jnp.sqrt(_dot_exact_rhs(kk * kk, bd, PREP_SPLIT)), 1e-12)
    kp = k * (1.0 + (a - 1.0) * k_a)
    bonus = _dot_exact_rhs(r * kp * r_k, bd, PREP_SPLIT) * v

    n_chunks = p.shape[0] // c
    cum = jnp.concatenate([_dot_exact_lhs(tri, logw[ci * c:(ci + 1) * c], PREP_SPLIT) for ci in range(n_chunks)],
                          axis=0)
    e_pos, e_neg = jnp.exp2(cum), jnp.exp2(-cum)
    wc = jnp.concatenate([e_pos[(ci + 1) * c - 1:(ci + 1) * c, :] for ci in range(n_chunks)], axis=0)
    ops = (r * e_pos, kp * e_neg, kk * a * e_neg, -kk * jnp.exp2(cum - logw), v)
    return tuple(o.astype(BF16) for o in ops), wc, bonus, g


def _rwkv_scan_kernel(ops_ref, aux_ref, wc_ref, lnw_ref, lnb_ref, y_ref, s_ref):
    @pl.when(pl.program_id(0) == 0)
    def _():
        s_ref[...] = jnp.zeros_like(s_ref)

    _rwkv_step(ops_ref, aux_ref, wc_ref, lnw_ref, lnb_ref, y_ref, s_ref)


def _rwkv_step(ops_ref, aux_ref, wc_ref, lnw_ref, lnb_ref, y_ref, s_ref):
    _, nb, span, d = ops_ref.shape
    c = RWKV_CHUNK
    nh, dk = RWKV_HEADS, HEAD_DIM

    row = lax.broadcasted_iota(jnp.int32, (1, c, c), 1)
    col = lax.broadcasted_iota(jnp.int32, (1, c, c), 2)
    strict = col < row
    incl = col <= row
    eye = (row == col).astype(F32)

    def heads(x):
        return jnp.stack([x[b, :, h * dk:(h + 1) * dk] for b in range(nb) for h in range(nh)], axis=0)

    def independent(j):
        rt, kt, bt, at, vb = (heads(ops_ref[idx, :, j * c:(j + 1) * c, :]) for idx in range(5))
        ar = jnp.concatenate([at, rt], axis=1)
        bk = jnp.concatenate([bt, kt], axis=1)
        amat = _bdot("nik,njk->nij", ar, bk)
        n_ab = jnp.where(strict, amat[:, :c, :c], 0.0)
        a_ak = jnp.where(strict, amat[:, :c, c:], 0.0).astype(BF16)
        a_rb = jnp.where(incl, amat[:, c:, :c], 0.0).astype(BF16)
        a_rk = jnp.where(incl, amat[:, c:, c:], 0.0).astype(BF16)
        def same_block(size):
            return (row // size) == (col // size)

        def mm16(a, b):
            a_hi, a_lo = _split(a, 2)
            b_hi, b_lo = _split(b, 2)
            mm = lambda p, q: _bdot("nij,njk->nik", p, q)
            return mm(a_hi, b_hi) + mm(a_hi, b_lo) + mm(a_lo, b_hi)

        pw = jnp.where(same_block(TRI_BASE), n_ab, 0.0)
        inv = eye + pw
        for _ in range(int(np.log2(TRI_BASE)) - 1):
            pw = mm16(pw, pw)
            inv = inv + mm16(pw, inv)
        size = TRI_BASE
        while size < c:
            lower = jnp.where(same_block(2 * size) & jnp.logical_not(same_block(size)), n_ab, 0.0)
            inv = inv + mm16(mm16(inv, lower), inv)
            size *= 2
        return ar, bk, vb, _bdot("nij,njv->niv", a_ak, vb), a_rb, a_rk, inv.astype(BF16)

    parts = [independent(j) for j in range(span // c)]
    s = s_ref[...]
    for j, (ar, bk, vb, akv, a_rb, a_rk, inv) in enumerate(parts):
        ar_s = _bdot("nik,nvk->niv", ar, s.astype(BF16))
        ub = _bdot("nij,njv->niv", inv, (ar_s[:, :c] + akv).astype(BF16)).astype(BF16)
        y = ar_s[:, c:] + _bdot("nij,njv->niv", a_rb, ub) + _bdot("nij,njv->niv", a_rk, vb)
        uv_t = jnp.swapaxes(jnp.concatenate([ub, vb], axis=1), 1, 2)
        s = (s + _bdot("nvi,nik->nvk", uv_t, bk)) * heads(wc_ref[:, j])

        mean = jnp.mean(y, axis=-1, keepdims=True)
        var = jnp.mean(jnp.square(y - mean), axis=-1, keepdims=True)
        yn = (y - mean) * lax.rsqrt(var + GN_EPS)
        rsl = slice(j * c, (j + 1) * c)
        for b in range(nb):
            wide = jnp.concatenate([yn[b * nh + h] for h in range(nh)], axis=-1)
            y_ref[b, rsl, :] = ((wide * lnw_ref[...] + lnb_ref[...] + aux_ref[0, b, rsl, :])
                                * aux_ref[1, b, rsl, :]).astype(y_ref.dtype)
    s_ref[...] = s


def _rwkv_constants():
    head = np.arange(D_RWKV) // HEAD_DIM
    bd = jnp.asarray(head[:, None] == head[None, :], BF16)
    tri = jnp.asarray(np.tril(np.ones((RWKV_CHUNK, RWKV_CHUNK))), BF16)
    return bd, tri


def _rwkv_scan(ops, aux, wc, lnw, lnb):
    _, batch, t, d = ops.shape
    c = RWKV_CHUNK
    const1 = lambda a: pl.BlockSpec(a.shape, lambda i: (0, 0))
    span = RWKV_SCAN_CHUNKS * c
    return pl.pallas_call(
        _rwkv_scan_kernel,
        grid=(t // span,),
        in_specs=[pl.BlockSpec((5, batch, span, d), lambda i: (0, 0, i, 0)),
                  pl.BlockSpec((2, batch, span, d), lambda i: (0, 0, i, 0)),
                  pl.BlockSpec((batch, RWKV_SCAN_CHUNKS, 1, d), lambda i: (0, i, 0, 0)),
                  const1(lnw), const1(lnb)],
        out_specs=pl.BlockSpec((batch, span, d), lambda i: (0, i, 0)),
        out_shape=jax.ShapeDtypeStruct((batch, t, d), BF16),
        scratch_shapes=[pltpu.VMEM((batch * RWKV_HEADS, HEAD_DIM, HEAD_DIM), F32)],
        compiler_params=_params("arbitrary"),
        name="rwkv_scan",
    )(ops, aux, wc, lnw, lnb)


def _compress_kernel(zk_ref, zv_ref, pek_ref, pev_ref, k1_ref, k2_ref, v1_ref, v2_ref, ko_ref, vo_ref):
    def one(z_ref, pe_ref, w1_ref, w2_ref):
        n16 = z_ref.shape[2] // CMP_STRIDE
        dk = HEAD_DIM
        first = second = None
        for l in range(CMP_STRIDE):
            z = z_ref[0, 0, pl.ds(l, n16, stride=CMP_STRIDE), :]
            lo = _dot((z + pe_ref[l:l + 1, :]).astype(BF16), w1_ref[l * dk:(l + 1) * dk, :])
            u = CMP_STRIDE + l
            hi = _dot((z + pe_ref[u:u + 1, :]).astype(BF16), w1_ref[u * dk:(u + 1) * dk, :])
            first = lo if first is None else first + lo
            second = hi if second is None else second + hi
        hidden = first + pltpu.roll(second, n16 - 1, axis=0)
        return _dot(jax.nn.gelu(hidden).astype(BF16), w2_ref[...])

    kc = one(zk_ref, pek_ref, k1_ref, k2_ref)
    blk = lax.broadcasted_iota(jnp.int32, kc.shape, 0)
    col = lax.broadcasted_iota(jnp.int32, kc.shape, 1)
    per = SEL_BLOCK // CMP_STRIDE
    hi = blk // per
    lo = CMP_STRIDE * (blk % per) + (CMP_BLOCK - 1)
    feat = _position_features(hi, lo, col)
    ko_ref[0] = jnp.concatenate([kc, feat], axis=1).astype(BF16)
    vc = one(zv_ref, pev_ref, v1_ref, v2_ref)
    vc_t = jnp.transpose(jnp.concatenate([vc, jnp.zeros_like(vc)], axis=1))
    vo_ref[0] = vc_t[:HEAD_DIM].astype(BF16)


def _compress(zk, zv, pek, pev, k1, k2, v1, v2):
    batch, hk, t, dk = zk.shape
    n16 = t // CMP_STRIDE
    zspec = pl.BlockSpec((1, 1, t, dk), lambda b, h: (b, h, 0, 0))
    full = lambda a: pl.BlockSpec(a.shape, lambda b, h: (0,) * a.ndim)
    return pl.pallas_call(
        _compress_kernel,
        grid=(batch, hk),
        in_specs=[zspec, zspec, full(pek), full(pev), full(k1), full(k2), full(v1), full(v2)],
        out_specs=[pl.BlockSpec((1, n16, 2 * dk), lambda b, h: (b * hk + h, 0, 0)),
                   pl.BlockSpec((1, dk, n16), lambda b, h: (b * hk + h, 0, 0))],
        out_shape=[jax.ShapeDtypeStruct((batch * hk, n16, 2 * dk), BF16),
                   jax.ShapeDtypeStruct((batch * hk, dk, n16), BF16)],
        compiler_params=_params("arbitrary", "arbitrary"),
        name="nsa_compress",
    )(zk, zv, pek, pev, k1, k2, v1, v2)


def _alibi_query_rows(slopes):
    per_lane = jnp.repeat(slopes.reshape(NSA_KV_HEADS, NSA_GROUP), Q_BLOCK, axis=1)
    rows = []
    for term in LOG2E_TERMS:
        rows += [(SEL_BLOCK * term) * per_lane, term * per_lane]
    rows = jnp.stack(rows, axis=1)
    return jnp.pad(rows, ((0, 0), (0, HEAD_DIM - rows.shape[1]), (0, 0)))


def _position_features(hi, lo, col):
    return jnp.where(col < 2 * len(LOG2E_TERMS), jnp.where(col % 2 == 0, hi, lo), 0).astype(F32)


def _cmp_select_kernel(qt_ref, kc_ref, vct_ref, ovt_ref, oct_ref, selt_ref, act_ref, imp_ref):
    nblk = qt_ref.shape[2]
    width = nblk * Q_BLOCK
    ncp = kc_ref.shape[1]

    def attend(rows):
        cmp_end = lax.broadcasted_iota(jnp.int32, (rows, Q_BLOCK), 0) * CMP_STRIDE + (CMP_BLOCK - 1)
        for u in range(nblk):
            q0 = (pl.program_id(2) * nblk + u) * Q_BLOCK
            s = _dot(kc_ref[0, 0:rows, :], qt_ref[0, 0, u])
            ok = cmp_end <= q0 + lax.broadcasted_iota(jnp.int32, (rows, Q_BLOCK), 1)
            any_ok = (q0 + lax.broadcasted_iota(jnp.int32, (1, Q_BLOCK), 1) >= CMP_BLOCK - 1).astype(F32)
            p_sum = jnp.zeros((rows, Q_BLOCK), F32)
            probs = []
            for g in range(NSA_GROUP):
                sg = jnp.where(ok, s[:, g * Q_BLOCK:(g + 1) * Q_BLOCK], NEG)
                e = jnp.exp2(sg - jnp.max(sg, axis=0, keepdims=True))
                p = e * (any_ok / jnp.sum(e, axis=0, keepdims=True))
                p_sum = p_sum + p
                probs.append(p.astype(BF16))
            oct_ref[0, 0, u] = _dot(vct_ref[0, :, 0:rows], jnp.concatenate(probs, axis=1))
            imp_ref[:, u * Q_BLOCK:(u + 1) * Q_BLOCK] = _dot_exact_lhs(ovt_ref[:, 0:rows], p_sum)

    chunk = min(CMP_ROW_CHUNK, ncp)
    n_chunks = ncp // chunk
    last_t = (pl.program_id(2) + 1) * width - 1
    needed = jnp.maximum((last_t - (CMP_BLOCK - 1)) // CMP_STRIDE + 1, 1)
    needed_chunks = jnp.minimum((needed + chunk - 1) // chunk, n_chunks)
    for nck in range(1, n_chunks + 1):
        pl.when(needed_chunks == nck)(lambda nck=nck: attend(nck * chunk))

    imp = imp_ref[...]
    blk = lax.broadcasted_iota(jnp.int32, (LANES, width), 0)
    cur = (pl.program_id(2) * width + lax.broadcasted_iota(jnp.int32, (LANES, width), 1)) // SEL_BLOCK
    valid = blk <= cur
    forced = (blk == 0) | (blk == cur) | (blk == cur - 1)
    x = jnp.where(forced, -jnp.inf, jnp.where(valid, imp, NEG))
    blk_f = blk.astype(F32)
    for _ in range(SEL_TOPK - 3):
        m = jnp.max(x, axis=0, keepdims=True)
        first = jnp.min(jnp.where(x == m, blk_f, float(LANES)), axis=0, keepdims=True)
        x = jnp.where(blk_f == first, -jnp.inf, x)
    sel = ((x == -jnp.inf) & valid).astype(F32)
    ones = jnp.ones((8, Q_BLOCK), BF16)
    for u in range(nblk):
        sel_u = sel[:, u * Q_BLOCK:(u + 1) * Q_BLOCK]
        selt_ref[0, 0, u] = sel_u
        act_ref[0, 0, u] = lax.dot_general(ones, sel_u.astype(BF16), (((1,), (1,)), ((), ())),
                                           preferred_element_type=F32)


def _cmp_select(q_t, kc_aug, vc_t, ov_t):
    batch, hk, nq, dk2, qg = q_t.shape
    dk = dk2 // 2
    ncp = kc_aug.shape[1]
    nblk = CMP_Q_BLOCKS
    return pl.pallas_call(
        _cmp_select_kernel,
        grid=(batch, hk, nq // nblk),
        in_specs=[pl.BlockSpec((1, 1, nblk, dk2, qg), lambda b, h, i: (b, h, i, 0, 0)),
                  pl.BlockSpec((1, ncp, 2 * dk), lambda b, h, i: (b * hk + h, 0, 0)),
                  pl.BlockSpec((1, dk, ncp), lambda b, h, i: (b * hk + h, 0, 0)),
                  pl.BlockSpec(ov_t.shape, lambda b, h, i: (0, 0))],
        out_specs=[pl.BlockSpec((1, 1, nblk, dk, qg), lambda b, h, i: (b, h, i, 0, 0)),
                   pl.BlockSpec((1, 1, nblk, LANES, Q_BLOCK), lambda b, h, i: (b, h, i, 0, 0)),
                   pl.BlockSpec((1, 1, nblk, 8, LANES), lambda b, h, i: (b, h, i, 0, 0))],
        out_shape=[jax.ShapeDtypeStruct((batch, hk, nq, dk, qg), F32),
                   jax.ShapeDtypeStruct((batch, hk, nq, LANES, Q_BLOCK), F32),
                   jax.ShapeDtypeStruct((batch, hk, nq, 8, LANES), F32)],
        scratch_shapes=[pltpu.VMEM((LANES, nblk * Q_BLOCK), F32)],
        compiler_params=_params("arbitrary", "arbitrary", "arbitrary"),
        name="nsa_cmp_select",
    )(q_t, kc_aug, vc_t, ov_t)


def _sel_win_kernel(ids_ref, cnt_ref, qt_ref, selt_ref, ks_ref, vst_ref, kw_ref, vwt_ref,
                    oct_ref, glt_ref, yt_ref):
    b, hk, i = pl.program_id(0), pl.program_id(1), pl.program_id(2)
    nblk = qt_ref.shape[2]
    kt = KEY_TILE
    last_tile = ks_ref.shape[2] // kt - 1
    lane_minus_row = (lax.broadcasted_iota(jnp.int32, (kt, Q_BLOCK), 1)
                      - lax.broadcasted_iota(jnp.int32, (kt, Q_BLOCK), 0))
    start = (jnp.full((1, QG), SOFTMAX_FLOOR, F32), jnp.zeros((V_ROWS, QG), F32))
    result = lambda state: state[1][:HEAD_DIM] / state[1][HEAD_DIM:HEAD_DIM + 1]

    def block(u):
        blk = i * nblk + u
        step = (b * pl.num_programs(1) + hk) * (pl.num_programs(2) * nblk) + blk
        q0 = blk * Q_BLOCK
        q_aug = qt_ref[0, 0, u]
        count = cnt_ref[step]

        def all_scores(tiles):
            s_all = _dot(jnp.concatenate([k_tile for k_tile, _, _ in tiles], axis=0), q_aug)
            return [s_all[n * kt:(n + 1) * kt] for n in range(len(tiles))]

        def attend(state, tiles, scores=None):
            m_old, acc_old = state
            scores = all_scores(tiles) if scores is None else scores
            probs, maxes = [[] for _ in tiles], []
            for g in range(NSA_GROUP):
                gsl = slice(g * Q_BLOCK, (g + 1) * Q_BLOCK)
                masked = [jnp.where(mask, s[:, gsl], NEG) for s, (_, _, mask) in zip(scores, tiles)]
                top = jnp.max(masked[0].reshape(kt // 8, 8, Q_BLOCK), axis=0)
                for sg in masked[1:]:
                    top = jnp.maximum(top, jnp.max(sg.reshape(kt // 8, 8, Q_BLOCK), axis=0))
                mg = jnp.maximum(m_old[:, gsl], jnp.max(top, axis=0, keepdims=True))
                for n, sg in enumerate(masked):
                    probs[n].append(jnp.exp2(sg - mg).astype(BF16))
                maxes.append(mg)
            m_new = jnp.concatenate(maxes, axis=1)
            p_all = jnp.concatenate([jnp.concatenate(p, axis=1) for p in probs], axis=0)
            vt_all = jnp.concatenate([vt_tile for _, vt_tile, _ in tiles], axis=1)
            return m_new, jnp.exp2(m_old - m_new) * acc_old + _dot(vt_all, p_all)

        def sel_tile(n):
            j = jnp.minimum(ids_ref[step * MAX_KEY_TILES + jnp.minimum(n, MAX_KEY_TILES - 1)], last_tile)
            k0 = pl.multiple_of(j * kt, kt)
            per = kt // SEL_BLOCK
            picked = jnp.concatenate(
                [jnp.broadcast_to(selt_ref[0, 0, u, pl.ds(j * per + r, 1), :], (SEL_BLOCK, Q_BLOCK))
                 for r in range(per)], axis=0)
            causal_from = jnp.where(n < count, k0 - q0, 1 << 30)
            mask = (picked > 0.5) & (lane_minus_row >= causal_from)
            return ks_ref[0, 0, pl.ds(k0, kt), :], vst_ref[0, 0, j], mask

        last = (q0 + Q_BLOCK - 1) // kt

        def win_tile(r):
            j = last - r
            jc = jnp.maximum(j, 0)
            k0 = pl.multiple_of(jc * kt, kt)
            dist = lane_minus_row + jnp.where(j >= 0, q0 - k0, -(1 << 30))
            mask = (dist >= 0) & (dist < WINDOW)
            return kw_ref[0, 0, pl.ds(k0, kt), :], vwt_ref[0, 0, jc], mask

        win_tiles = [win_tile(r) for r in range(WIN_TILES)]
        sel_tiles = [sel_tile(n) for n in range(SEL_FIRST)]
        scores = all_scores(win_tiles + sel_tiles)
        state_w = attend(start, win_tiles, scores[:WIN_TILES])
        state_s = attend(start, sel_tiles, scores[WIN_TILES:])
        extra_steps = (jnp.maximum(count - SEL_FIRST, 0) + SEL_GROUP - 1) // SEL_GROUP
        more = lambda n, st: attend(st, [sel_tile(SEL_FIRST + SEL_GROUP * n + r) for r in range(SEL_GROUP)])
        return state_w, state_s, extra_steps, more

    blocks = [block(u) for u in range(nblk)]
    for u, (state_w, state_s, extra_steps, more) in enumerate(blocks):
        state_s = lax.fori_loop(0, extra_steps, more, state_s)
        o_s, o_w = result(state_s), result(state_w)
        lanes = slice(u * Q_BLOCK, (u + 1) * Q_BLOCK)
        gates = jax.nn.sigmoid(glt_ref[0, 0, :, lanes])
        o_c = oct_ref[0, 0, u]
        outs = []
        for g in range(NSA_GROUP):
            gsl = slice(g * Q_BLOCK, (g + 1) * Q_BLOCK)
            outs.append(gates[3 * g:3 * g + 1] * o_c[:, gsl] + gates[3 * g + 1:3 * g + 2] * o_s[:, gsl]
                        + gates[3 * g + 2:3 * g + 3] * o_w[:, gsl])
        yt_ref[0, :, lanes] = jnp.concatenate(outs, axis=0).astype(yt_ref.dtype)


def _sel_win(tile_ids, tile_cnt, q_t, sel_t, ks, vs_t, kw, vw_t, oc_t, gl_t):
    batch, hk, nq, dk, qg = oc_t.shape
    t = nq * Q_BLOCK
    nkt = t // KEY_TILE
    grp = qg // Q_BLOCK
    nblk = SEL_Q_BLOCKS
    qspec = pl.BlockSpec((1, 1, nblk, 2 * dk, qg), lambda b, h, i, *_: (b, h, i, 0, 0))
    ospec = pl.BlockSpec((1, 1, nblk, dk, qg), lambda b, h, i, *_: (b, h, i, 0, 0))
    kspec = pl.BlockSpec((1, 1, t, 2 * dk), lambda b, h, i, *_: (b, h, 0, 0))
    vspec = pl.BlockSpec((1, 1, nkt, V_ROWS, KEY_TILE), lambda b, h, i, *_: (b, h, 0, 0, 0))
    grid_spec = pltpu.PrefetchScalarGridSpec(
        num_scalar_prefetch=2,
        grid=(batch, hk, nq // nblk),
        in_specs=[qspec,
                  pl.BlockSpec((1, 1, nblk, LANES, Q_BLOCK), lambda b, h, i, *_: (b, h, i, 0, 0)),
                  kspec, vspec, kspec, vspec, ospec,
                  pl.BlockSpec((1, 1, 3 * grp, nblk * Q_BLOCK), lambda b, h, i, *_: (b, h, 0, i))],
        out_specs=pl.BlockSpec((1, grp * dk, nblk * Q_BLOCK), lambda b, h, i, *_: (b, h, i)),
    )
    return pl.pallas_call(
        _sel_win_kernel,
        grid_spec=grid_spec,
        out_shape=jax.ShapeDtypeStruct((batch, hk * grp * dk, t), BF16),
        compiler_params=_params("arbitrary", "arbitrary", "arbitrary"),
        name="nsa_sel_win",
    )(tile_ids, tile_cnt, q_t, sel_t, ks, vs_t, kw, vw_t, oc_t, gl_t)


def _ffn_kernel(x_ref, yr_ref, ynt_ref, wo1_ref, wo2_ref, g2_ref, wg_ref, wu_ref, wd_ref, gf_ref, o_ref):
    h1 = (x_ref[...] + _dot(yr_ref[...].astype(BF16), wo1_ref[...])
          + _dot_tn(ynt_ref[0].astype(BF16), wo2_ref[...]))
    hn = _rms(h1, g2_ref[...]).astype(BF16)
    gate = _dot(hn, wg_ref[...])
    up = _dot(hn, wu_ref[...])
    act = gate * jax.nn.sigmoid(gate) * up
    o_ref[...] = _rms(h1 + _dot(act.astype(BF16), wd_ref[...]), gf_ref[...])


def _ffn(x2, yr, yn_t, wo1, wo2, g2, wg, wu, wd, gf, tm=512):
    m, d = x2.shape
    _, dn, t = yn_t.shape
    per_seq = t // tm
    row = lambda n: pl.BlockSpec((tm, n), lambda i: (i, 0))
    const = lambda a: pl.BlockSpec(a.shape, lambda i: (0, 0), pipeline_mode=pl.Buffered(1))
    return pl.pallas_call(
        _ffn_kernel,
        grid=(m // tm,),
        in_specs=[row(d), row(yr.shape[1]),
                  pl.BlockSpec((1, dn, tm), lambda i: (i // per_seq, 0, i % per_seq)),
                  const(wo1), const(wo2), const(g2), const(wg), const(wu), const(wd), const(gf)],
        out_specs=row(d),
        out_shape=jax.ShapeDtypeStruct((m, d), F32),
        compiler_params=_params("arbitrary"),
        name="outproj_ffn",
    )(x2, yr, yn_t, wo1, wo2, g2, wg, wu, wd, gf)


def _overlap_matrix_t(t):
    n16 = t // CMP_STRIDE
    n_cmp = (t - CMP_BLOCK) // CMP_STRIDE + 1
    n_sel = t // SEL_BLOCK
    cmp_start = np.arange(n_cmp) * CMP_STRIDE
    sel_start = np.arange(n_sel) * SEL_BLOCK
    ov = np.clip(np.minimum(cmp_start[:, None] + CMP_BLOCK, sel_start[None, :] + SEL_BLOCK)
                 - np.maximum(cmp_start[:, None], sel_start[None, :]), 0, None) / CMP_STRIDE
    full = np.zeros((LANES, n16), np.float32)
    full[:n_sel, :n_cmp] = ov.T
    return jnp.asarray(full, BF16)


def kernel(x, norm1_g, w_in, mu_shift, rwkv_w0, rwkv_w2, rwkv_a0, rwkv_a2, rwkv_g2, rwkv_k_k, rwkv_k_a,
           rwkv_r_k, rwkv_lnx_w, rwkv_lnx_b, nsa_pe_k, nsa_pe_v, nsa_cmp_k_w1, nsa_cmp_k_w2, nsa_cmp_v_w1,
           nsa_cmp_v_w2, w_out, norm2_g, ffn_w_gate, ffn_w_up, ffn_w_down, norm_f_g):
    batch, t, d_model = x.shape
    assert w_in.shape[0] == 1, "the final RMSNorm is fused into the (single) layer's FFN kernel"
    assert t % INPROJ_ROWS == 0 and t // SEL_BLOCK <= LANES
    hk = NSA_KV_HEADS
    nq = t // Q_BLOCK
    assert nq % CMP_Q_BLOCKS == 0 and nq % SEL_Q_BLOCKS == 0 and t % (RWKV_SCAN_CHUNKS * RWKV_CHUNK) == 0
    slopes = 2.0 ** (-8.0 * jnp.arange(1, NSA_Q_HEADS + 1, dtype=F32) / NSA_Q_HEADS)
    nsa_pad = _round_up(N_NSA_COLS, LANES)
    row = lambda a: a.reshape(1, -1)
    i = 0

    h = x.reshape(batch * t, d_model)
    w_r = w_in[i][:, :N_RWKV_COLS].astype(BF16)
    w_n = jnp.pad(w_in[i][:, N_RWKV_COLS:], ((0, 0), (0, nsa_pad - N_NSA_COLS))).astype(BF16)
    rwkv_consts = (row(mu_shift[i]), row(rwkv_w0[i]), rwkv_w2[i].astype(BF16), row(rwkv_a0[i]),
                   rwkv_a2[i].astype(BF16), rwkv_g2[i].astype(BF16), row(rwkv_k_k[i]), row(rwkv_k_a[i]),
                   row(rwkv_r_k[i])) + _rwkv_constants()
    ops, aux, wc, q_t, ks, kw, vs_t, vw_t, zk, zv, gl_t = _inproj(h, batch, row(norm1_g[i]), w_r, w_n,
                                                                  _alibi_query_rows(slopes), rwkv_consts)

    y_rwkv = _rwkv_scan(ops, aux, wc, row(rwkv_lnx_w[i]), row(rwkv_lnx_b[i])).reshape(batch * t, D_RWKV)

    kc_aug, vc_t = _compress(zk, zv, nsa_pe_k[i], nsa_pe_v[i],
                             nsa_cmp_k_w1[i].astype(BF16), nsa_cmp_k_w2[i].astype(BF16),
                             nsa_cmp_v_w1[i].astype(BF16), nsa_cmp_v_w2[i].astype(BF16))
    oc_t, sel_t, picked = _cmp_select(q_t, kc_aug, vc_t, _overlap_matrix_t(t))

    blocks_per_tile = KEY_TILE // SEL_BLOCK
    active = (picked[:, :, :, 0, :] > 0).reshape(batch, hk, nq, MAX_KEY_TILES, blocks_per_tile).any(axis=-1)
    tile_ids = jnp.argsort(jnp.logical_not(active), axis=-1, stable=True).astype(jnp.int32).reshape(-1)
    tile_cnt = active.sum(axis=-1).astype(jnp.int32).reshape(-1)
    y_nsa_t = _sel_win(tile_ids, tile_cnt, q_t, sel_t, ks, vs_t, kw, vw_t, oc_t, gl_t)

    out = _ffn(h, y_rwkv, y_nsa_t, w_out[i][:D_RWKV].astype(BF16), w_out[i][D_RWKV:].astype(BF16),
               row(norm2_g[i]), ffn_w_gate[i].astype(BF16), ffn_w_up[i].astype(BF16),
               ffn_w_down[i].astype(BF16), row(norm_f_g))
    return out.reshape(batch, t, d_model)
```
